```python
import jax, jax.numpy as jnp
from jax import lax
import numpy as np

D_MODEL = 1024
BATCH = 16
SEQ = 256
DEPTH = 4
DEC_BATCH = 2
DEC_SEQ = 4096
PAST_LEN = 256

GRID_W = 64
ROPE_THETA = 10000.0
Q_BLOCK = 128
EPS = 1e-6

D_MIX = D_MODEL
GROUP_W = D_MIX // 4

RW_HEADS = 4
RW_HEAD_DIM = GROUP_W // RW_HEADS
RW_DECAY_RANK = 64
RW_ICLR_RANK = 32
RW_GATE_RANK = 128
RW_DECAY_SCALE = 0.6065306597

MLA_HEADS = 4
MLA_NOPE_DIM = 64
MLA_ROPE_DIM = 32
MLA_V_DIM = GROUP_W // MLA_HEADS
MLA_QK_DIM = MLA_NOPE_DIM + MLA_ROPE_DIM
MLA_Q_RANK = 192
MLA_KV_RANK = 128

GQA_HEADS = 4
GQA_KV_HEADS = 2
GQA_HEAD_DIM = GROUP_W // GQA_HEADS
GQA_GROUP = GQA_HEADS // GQA_KV_HEADS

CONV_W = 3

IN_SPLITS = (
    GROUP_W, GROUP_W, GROUP_W,
    RW_DECAY_RANK, RW_DECAY_RANK,
    RW_ICLR_RANK, RW_ICLR_RANK,
    RW_GATE_RANK,
    MLA_Q_RANK, MLA_KV_RANK, MLA_ROPE_DIM,
    GQA_HEADS * GQA_HEAD_DIM, GQA_KV_HEADS * GQA_HEAD_DIM, GQA_KV_HEADS * GQA_HEAD_DIM,
    GROUP_W, GROUP_W, GROUP_W,
)
D_IN = 2720

PEER_HEADS = 8
PEER_N_KEYS = 128
PEER_N_EXPERTS = PEER_N_KEYS * PEER_N_KEYS
PEER_QDIM = 256
PEER_HALF = PEER_QDIM // 2
PEER_TOPK = 16
PEER_BLOCK = 128

kernel_name = 'hybrid_diffusion_trunk_step'


def rms_norm(x, g):
    xf = x.astype(jnp.float32)
    y = xf * lax.rsqrt(jnp.mean(xf * xf, axis=-1, keepdims=True) + EPS)
    return (y * g.astype(jnp.float32)).astype(x.dtype)


def heads(x, n):
    return x.reshape(x.shape[:-1] + (n, x.shape[-1] // n))


def split_cols(x, sizes):
    offs = np.cumsum(np.array(sizes))[:-1]
    return jnp.split(x, [int(o) for o in offs], axis=-1)


def grid_rope(n_tok, rot_dim):
    rows = n_tok // GRID_W
    row = jnp.repeat(jnp.arange(rows, dtype=jnp.float32), GRID_W)
    col = jnp.tile(jnp.arange(GRID_W, dtype=jnp.float32), rows)
    n_freq = rot_dim // 4
    freqs = ROPE_THETA ** (-jnp.arange(n_freq, dtype=jnp.float32) / n_freq)
    ang = jnp.concatenate([row[:, None] * freqs, col[:, None] * freqs], axis=-1)
    return jnp.cos(ang), jnp.sin(ang)


def apply_rope(x, cs):
    cos, sin = cs[0][None, :, None, :], cs[1][None, :, None, :]
    half = x.shape[-1] // 2
    x1 = x[..., :half].astype(jnp.float32)
    x2 = x[..., half:].astype(jnp.float32)
    return jnp.concatenate([x1 * cos - x2 * sin, x1 * sin + x2 * cos], axis=-1).astype(x.dtype)


def attention(q, k, v, scale):
    b, tq, hk, g, d = q.shape
    nb = tq // Q_BLOCK
    q_blocks = jnp.moveaxis(q.reshape(b, nb, Q_BLOCK, hk, g, d), 1, 0)

    def one_block(qb):
        s = jnp.einsum('bqhgd,bkhd->bhgqk', qb, k).astype(jnp.float32) * scale
        pr = jax.nn.softmax(s, axis=-1).astype(v.dtype)
        return jnp.einsum('bhgqk,bkhe->bqhge', pr, v)

    o = lax.map(one_block, q_blocks)
    return jnp.moveaxis(o, 0, 1).reshape(b, tq, hk, g, v.shape[-1])


def wkv7_scan(r, w, k, v, kk, a, s0, reverse):
    def step(s, inp):
        r_t, w_t, k_t, v_t, kk_t, a_t = inp
        s_kk = jnp.einsum('bhvk,bhk->bhv', s, kk_t)
        s = (s * w_t[:, :, None, :] - s_kk[..., None] * (kk_t * a_t)[:, :, None, :]
             + v_t[..., :, None] * k_t[..., None, :])
        return s, jnp.einsum('bhvk,bhk->bhv', s, r_t)

    xs = tuple(jnp.moveaxis(t, 1, 0) for t in (r, w, k, v, kk, a))
    s_fin, ys = lax.scan(step, s0, xs, reverse=reverse)
    return jnp.moveaxis(ys, 0, 1), s_fin


def rwkv7_mixer(r, k, v, wd, ad, gd, p, s_ctx):
    b, t, _ = r.shape
    kkf = heads(k * p['rw_kk'], RW_HEADS).astype(jnp.float32)
    kk = (kkf * lax.rsqrt(jnp.sum(kkf * kkf, axis=-1, keepdims=True) + EPS)).astype(k.dtype)
    rh, kh, vh = heads(r, RW_HEADS), heads(k, RW_HEADS), heads(v, RW_HEADS)
    ys, finals = [], []
    for d in range(2):
        w = jnp.exp(-RW_DECAY_SCALE * jax.nn.sigmoid(p['rw_w0'][d] + jnp.tanh(wd[d]) @ p['rw_wu'][d]))
        a = jax.nn.sigmoid(p['rw_a0'][d] + ad[d] @ p['rw_au'][d])
        k_d = k * (1 + (a - 1) * p['rw_ka'])
        if s_ctx is None:
            s_init = jnp.zeros((b, RW_HEADS, RW_HEAD_DIM, RW_HEAD_DIM), r.dtype)
        else:
            s_init = s_ctx[:, d].astype(r.dtype)
        y, s_fin = wkv7_scan(rh, heads(w, RW_HEADS), heads(k_d, RW_HEADS), vh, kk,
                             heads(a, RW_HEADS), s_init, reverse=(d == 1))
        ys.append(y)
        finals.append(s_fin)
    y = rms_norm(ys[0] + ys[1], p['rw_gn'].reshape(RW_HEADS, RW_HEAD_DIM))
    bonus = jnp.sum(rh * kh * p['rw_rk'], axis=-1, keepdims=True) * vh
    g = jax.nn.sigmoid(gd) @ p['rw_gu']
    out = (y + bonus).reshape(b, t, GROUP_W) * g
    return out, jnp.stack(finals, axis=1)


def mla_expand(ckv, k_rope, p):
    kv = heads(ckv @ p['mla_wukv'], MLA_HEADS)
    k_nope, v = kv[..., :MLA_NOPE_DIM], kv[..., MLA_NOPE_DIM:]
    k_r = jnp.broadcast_to(k_rope[:, :, None, :], k_nope.shape[:-1] + (MLA_ROPE_DIM,))
    k = rms_norm(jnp.concatenate([k_nope, k_r], axis=-1), p['mla_kn_g'])
    return k, v


def rope_tail(x, cs, d):
    return jnp.concatenate([x[..., :-d], apply_rope(x[..., -d:], cs)], axis=-1)


def mla_mixer(q_c, ckv_raw, k_rope, p, rope, ctx):
    q = heads(rms_norm(q_c, p['mla_qa_g']) @ p['mla_wuq'], MLA_HEADS)
    q = rms_norm(q, p['mla_qn_g'])
    ckv = rms_norm(ckv_raw, p['mla_kva_g'])
    k, v = mla_expand(ckv, k_rope, p)
    if ctx is not None:
        q = rope_tail(q, rope, MLA_ROPE_DIM)
        k = rope_tail(k, rope, MLA_ROPE_DIM)
        k_ctx, v_ctx = mla_expand(ctx[0], ctx[1], p)
        k = jnp.concatenate([k_ctx, k], axis=1)
        v = jnp.concatenate([v_ctx, v], axis=1)
    o = attention(q[:, :, :, None, :], k, v, MLA_QK_DIM ** -0.5)
    return o.reshape(q.shape[:2] + (GROUP_W,)), ckv


def gqa_mixer(q_g, k_g, v_g, p, rope, ctx):
    q = rms_norm(heads(q_g, GQA_HEADS), p['gqa_qn_g'])
    k = rms_norm(heads(k_g, GQA_KV_HEADS), p['gqa_kn_g'])
    v = heads(v_g, GQA_KV_HEADS)
    k_keep, v_keep = k, v
    if ctx is not None:
        q = apply_rope(q, rope)
        k = jnp.concatenate([ctx[0], apply_rope(k, rope)], axis=1)
        v = jnp.concatenate([ctx[1], v], axis=1)
    b, t = q.shape[:2]
    o = attention(q.reshape(b, t, GQA_KV_HEADS, GQA_GROUP, GQA_HEAD_DIM), k, v, GQA_HEAD_DIM ** -0.5)
    return o.reshape(b, t, GROUP_W), k_keep, v_keep


def short_conv_mixer(x_in, b_g, c_g, p):
    u = c_g * x_in
    up = jnp.pad(u, ((0, 0), (1, 1), (0, 0)))
    w = p['conv_w']
    y = up[:, :-2] * w[0] + up[:, 1:-1] * w[1] + up[:, 2:] * w[2] + p['conv_b']
    return b_g * y


def peer_ffn(h, p):
    b, t, d = h.shape
    xb = h.reshape(-1, PEER_BLOCK, d)
    U, V = p['peer_u'], p['peer_v']

    def block(x):
        q = (x @ p['peer_wq']).reshape(PEER_BLOCK, PEER_HEADS, 2, PEER_HALF)
        s1 = jnp.einsum('nhd,hkd->nhk', q[:, :, 0], p['peer_k1']).astype(jnp.float32)
        s2 = jnp.einsum('nhd,hkd->nhk', q[:, :, 1], p['peer_k2']).astype(jnp.float32)
        v1, i1 = lax.top_k(s1, PEER_TOPK)
        v2, i2 = lax.top_k(s2, PEER_TOPK)
        cand = (v1[..., :, None] + v2[..., None, :]).reshape(PEER_BLOCK, PEER_HEADS, PEER_TOPK * PEER_TOPK)
        cand_idx = (i1[..., :, None] * PEER_N_KEYS + i2[..., None, :]).reshape(PEER_BLOCK, PEER_HEADS, PEER_TOPK * PEER_TOPK)
        best, pos = lax.top_k(cand, PEER_TOPK)
        idx = jnp.take_along_axis(cand_idx, pos, axis=-1)
        g = jax.nn.softmax(best, axis=-1).astype(x.dtype)
        act = jax.nn.gelu(jnp.einsum('nhkd,nd->nhk', U[idx], x))
        return jnp.einsum('nhk,nhkd->nd', g * act, V[idx])

    return lax.map(block, xb).reshape(b, t, d)


def trunk_layer(x, mod, p, rope_mla, rope_gqa, ctx):
    shift1, scale1, gate1, shift2, scale2, gate2 = jnp.split(mod, 6, axis=-1)
    h = rms_norm(x, p['norm1_g']) * (1 + scale1) + shift1
    (r, k, v, wd_f, wd_b, ad_f, ad_b, gd, q_c, ckv_raw, k_rope,
     q_g, k_g, v_g, x_in, b_g, c_g) = split_cols(h @ p['w_in'], IN_SPLITS)
    ctx_state = None if ctx is None else ctx[0]
    ctx_mla = None if ctx is None else (ctx[1], ctx[2])
    ctx_gqa = None if ctx is None else (ctx[3], ctx[4])
    o_a, s_a = rwkv7_mixer(r, k, v, (wd_f, wd_b), (ad_f, ad_b), gd, p, ctx_state)
    o_b, ckv = mla_mixer(q_c, ckv_raw, k_rope, p, rope_mla, ctx_mla)
    o_c, k_c, v_c = gqa_mixer(q_g, k_g, v_g, p, rope_gqa, ctx_gqa)
    o_d = short_conv_mixer(x_in, b_g, c_g, p)
    x = x + gate1 * (jnp.concatenate([o_a, o_b, o_c, o_d], axis=-1) @ p['w_out'])
    h2 = rms_norm(x, p['norm2_g']) * (1 + scale2) + shift2
    x = x + gate2 * peer_ffn(h2, p)
    return x, (s_a, ckv, k_rope, k_c, v_c)


def setup_inputs(seed: int = 0) -> dict:
    key = jax.random.key(seed)
    ks = iter(jax.random.split(key, 48))

    def nrm(shape, scale):
        return jax.random.normal(next(ks), shape, jnp.float32) * scale

    def gain(shape, base=1.0):
        return base + nrm(shape, 0.05)

    return {
        'x_prompt': nrm((BATCH, SEQ, D_MODEL), 1.0),
        'x_sample': nrm((DEC_BATCH, DEC_SEQ, D_MODEL), 1.0),
        'state_rwkv': nrm((DEC_BATCH, DEPTH, 2, RW_HEADS, RW_HEAD_DIM, RW_HEAD_DIM), 0.5),
        'cache_mla_ckv': nrm((DEC_BATCH, DEPTH, PAST_LEN, MLA_KV_RANK), 1.0),
        'cache_mla_krope': nrm((DEC_BATCH, DEPTH, PAST_LEN, MLA_ROPE_DIM), 1.0),
        'cache_gqa_k': nrm((DEC_BATCH, DEPTH, PAST_LEN, GQA_KV_HEADS, GQA_HEAD_DIM), 1.0),
        'cache_gqa_v': nrm((DEC_BATCH, DEPTH, PAST_LEN, GQA_KV_HEADS, GQA_HEAD_DIM), 1.0),
        'c': nrm((DEC_BATCH, D_MODEL), 1.0),
        'c_ctx': nrm((D_MODEL,), 1.0),
        'norm1_g': gain((DEPTH, D_MODEL)),
        'norm2_g': gain((DEPTH, D_MODEL)),
        'w_mod': nrm((DEPTH, D_MODEL, 6 * D_MODEL), 0.5 * D_MODEL ** -0.5),
        'b_mod': nrm((DEPTH, 6 * D_MODEL), 0.02),
        'w_in': nrm((DEPTH, D_MODEL, D_IN), D_MODEL ** -0.5),
        'w_out': nrm((DEPTH, D_MIX, D_MODEL), D_MIX ** -0.5),
        'rw_w0': nrm((DEPTH, 2, GROUP_W), 0.5),
        'rw_wu': nrm((DEPTH, 2, RW_DECAY_RANK, GROUP_W), RW_DECAY_RANK ** -0.5),
        'rw_a0': nrm((DEPTH, 2, GROUP_W), 0.5),
        'rw_au': nrm((DEPTH, 2, RW_ICLR_RANK, GROUP_W), 0.5 * RW_ICLR_RANK ** -0.5),
        'rw_gu': nrm((DEPTH, RW_GATE_RANK, GROUP_W), RW_GATE_RANK ** -0.5),
        'rw_kk': gain((DEPTH, GROUP_W), 0.85),
        'rw_ka': gain((DEPTH, GROUP_W)),
        'rw_rk': nrm((DEPTH, RW_HEADS, RW_HEAD_DIM), 0.1),
        'rw_gn': gain((DEPTH, GROUP_W)),
        'mla_qa_g': gain((DEPTH, MLA_Q_RANK)),
        'mla_wuq': nrm((DEPTH, MLA_Q_RANK, MLA_HEADS * MLA_QK_DIM), MLA_Q_RANK ** -0.5),
        'mla_kva_g': gain((DEPTH, MLA_KV_RANK)),
        'mla_wukv': nrm((DEPTH, MLA_KV_RANK, MLA_HEADS * (MLA_NOPE_DIM + MLA_V_DIM)), MLA_KV_RANK ** -0.5),
        'mla_qn_g': gain((DEPTH, MLA_QK_DIM)),
        'mla_kn_g': gain((DEPTH, MLA_QK_DIM)),
        'gqa_qn_g': gain((DEPTH, GQA_HEAD_DIM)),
        'gqa_kn_g': gain((DEPTH, GQA_HEAD_DIM)),
        'conv_w': nrm((DEPTH, CONV_W, GROUP_W), 0.5),
        'conv_b': nrm((DEPTH, GROUP_W), 0.02),
        'peer_wq': nrm((DEPTH, D_MODEL, PEER_HEADS * PEER_QDIM), D_MODEL ** -0.5),
        'peer_k1': nrm((DEPTH, PEER_HEADS, PEER_N_KEYS, PEER_HALF), PEER_HALF ** -0.5),
        'peer_k2': nrm((DEPTH, PEER_HEADS, PEER_N_KEYS, PEER_HALF), PEER_HALF ** -0.5),
        'peer_u': nrm((DEPTH, PEER_N_EXPERTS, D_MODEL), D_MODEL ** -0.5),
        'peer_v': nrm((DEPTH, PEER_N_EXPERTS, D_MODEL), 0.5),
    }


def reference(x_prompt, x_sample, state_rwkv, cache_mla_ckv, cache_mla_krope, cache_gqa_k, cache_gqa_v,
              c, c_ctx, norm1_g, norm2_g, w_mod, b_mod, w_in, w_out,
              rw_w0, rw_wu, rw_a0, rw_au, rw_gu, rw_kk, rw_ka, rw_rk, rw_gn,
              mla_qa_g, mla_wuq, mla_kva_g, mla_wukv, mla_qn_g, mla_kn_g,
              gqa_qn_g, gqa_kn_g, conv_w, conv_b,
              peer_wq, peer_k1, peer_k2, peer_u, peer_v):
    n_lat = x_sample.shape[1]
    rope_mla = grid_rope(n_lat, MLA_ROPE_DIM)
    rope_gqa = grid_rope(n_lat, GQA_HEAD_DIM)
    y_prompt, y_sample = x_prompt, x_sample
    st_a, st_ckv, st_kr, st_k, st_v = [], [], [], [], []
    for l in range(DEPTH):
        p = {
            'norm1_g': norm1_g[l], 'norm2_g': norm2_g[l], 'w_in': w_in[l], 'w_out': w_out[l],
            'rw_w0': rw_w0[l], 'rw_wu': rw_wu[l], 'rw_a0': rw_a0[l], 'rw_au': rw_au[l],
            'rw_gu': rw_gu[l], 'rw_kk': rw_kk[l], 'rw_ka': rw_ka[l], 'rw_rk': rw_rk[l], 'rw_gn': rw_gn[l],
            'mla_qa_g': mla_qa_g[l], 'mla_wuq': mla_wuq[l], 'mla_kva_g': mla_kva_g[l],
            'mla_wukv': mla_wukv[l], 'mla_qn_g': mla_qn_g[l], 'mla_kn_g': mla_kn_g[l],
            'gqa_qn_g': gqa_qn_g[l], 'gqa_kn_g': gqa_kn_g[l], 'conv_w': conv_w[l], 'conv_b': conv_b[l],
            'peer_wq': peer_wq[l], 'peer_k1': peer_k1[l], 'peer_k2': peer_k2[l],
            'peer_u': peer_u[l], 'peer_v': peer_v[l],
        }
        mod_ctx = jax.nn.silu(c_ctx) @ w_mod[l] + b_mod[l]
        y_prompt, (s_a, ckv, kr, k_c, v_c) = trunk_layer(y_prompt, mod_ctx, p, None, None, None)
        st_a.append(s_a)
        st_ckv.append(ckv)
        st_kr.append(kr)
        st_k.append(k_c)
        st_v.append(v_c)
        mod_lat = (jax.nn.silu(c) @ w_mod[l] + b_mod[l])[:, None, :]
        ctx = (state_rwkv[:, l], cache_mla_ckv[:, l], cache_mla_krope[:, l], cache_gqa_k[:, l], cache_gqa_v[:, l])
        y_sample, _ = trunk_layer(y_sample, mod_lat, p, rope_mla, rope_gqa, ctx)
    new_state_rwkv = jnp.stack(st_a, axis=1)
    new_cache_mla_ckv = jnp.stack(st_ckv, axis=1)
    new_cache_mla_krope = jnp.stack(st_kr, axis=1)
    new_cache_gqa_k = jnp.stack(st_k, axis=1)
    new_cache_gqa_v = jnp.stack(st_v, axis=1)
    return (y_prompt, y_sample, new_state_rwkv, new_cache_mla_ckv, new_cache_mla_krope, new_cache_gqa_k, new_cache_gqa_v)
```

```python
import functools

import numpy as np
import jax
import jax.numpy as jnp
from jax import lax
from jax.experimental import pallas as pl
from jax.experimental.pallas import tpu as pltpu

F32 = jnp.float32
BF16 = jnp.bfloat16
HI = lax.Precision.HIGHEST

D_MODEL = 1024
N_CTX_SEQ = 16
CTX_LEN = 256
N_LAT_SEQ = 2
LAT_LEN = 4096
GRID_W = 64
ROPE_THETA = 10000.0
EPS = 1e-6
GROUP_W = 256
HEAD_DIM = 64
RW_DECAY_SCALE = 0.6065306597
MLA_QK_DIM = 96
MLA_Q_RANK = 192
PEER_HEADS = 8
PEER_KEYS = 128
PEER_TOPK = 16
N_EXPERTS = PEER_KEYS * PEER_KEYS

N_CTX_TOK = N_CTX_SEQ * CTX_LEN
N_LAT_TOK = N_LAT_SEQ * LAT_LEN
N_TOK = N_CTX_TOK + N_LAT_TOK
COND_GROUP = 4096

A_W, B_W, C_W, D_W = 1152, 640, 512, 768
IN_PAD = A_W + B_W + C_W + D_W

SCAN_CHUNK = 128
TILE = 256
IN_TILE = 512
PEER_TN = 512
PEER_TE = 1024
TOPK_TN = 128
ATT_TQ = 256

VMEM_LIMIT = 56 * 1024 * 1024


def _cparams(sem):
    return pltpu.CompilerParams(dimension_semantics=sem, vmem_limit_bytes=VMEM_LIMIT)


def _dot(a, b, precision=None):
    return jnp.dot(a, b, preferred_element_type=F32, precision=precision)


def _dot_nt(a, b, precision=None):
    return lax.dot_general(a, b, (((1,), (1,)), ((), ())), preferred_element_type=F32,
                           precision=precision)


def _dot_tn(a, b, precision=None):
    return lax.dot_general(a, b, (((0,), (0,)), ((), ())), preferred_element_type=F32,
                           precision=precision)


def _rms_rows(x, g):
    return x * lax.rsqrt(jnp.mean(x * x, axis=-1, keepdims=True) + EPS) * g


def _group_matrix(n, group, value):
    r = lax.broadcasted_iota(jnp.int32, (n, n), 0) // group
    c = lax.broadcasted_iota(jnp.int32, (n, n), 1) // group
    return jnp.where(r == c, value, 0.0).astype(F32)


def _mod_body(c_ref, w_ref, b_ref, o_ref):
    c = c_ref[...]
    s = c * jax.nn.sigmoid(c)
    o_ref[...] = _dot(s.astype(BF16), w_ref[...].astype(BF16)) + b_ref[...]


def _modulation(cvecs, w_mod, b_mod):
    depth = w_mod.shape[0]
    tn = 1536
    return pl.pallas_call(
        _mod_body,
        grid=(depth, 6 * D_MODEL // tn),
        in_specs=[pl.BlockSpec((8, D_MODEL), lambda l, j: (0, 0)),
                  pl.BlockSpec((None, D_MODEL, tn), lambda l, j: (l, 0, j)),
                  pl.BlockSpec((None, 1, tn), lambda l, j: (l, 0, j))],
        out_specs=pl.BlockSpec((None, 8, tn), lambda l, j: (l, 0, j)),
        out_shape=jax.ShapeDtypeStruct((depth, 8, 6 * D_MODEL), F32),
        compiler_params=_cparams(("arbitrary", "arbitrary")),
        name="adaln_mod",
    )(cvecs, w_mod, b_mod.reshape(depth, 1, 6 * D_MODEL))


def _in_body(x_ref, mod_ref, g_ref, w_ref, a_ref, b_ref, c_ref, d_ref):
    mod = mod_ref[...]
    shift = mod[:, 0:D_MODEL]
    scale = mod[:, D_MODEL:2 * D_MODEL]
    h = _rms_rows(x_ref[...], g_ref[...]) * (1.0 + scale) + shift
    y = _dot(h.astype(BF16), w_ref[...])
    a_ref[...] = y[:, 0:A_W]
    b_ref[...] = y[:, A_W:A_W + B_W]
    c_ref[...] = y[:, A_W + B_W:A_W + B_W + C_W]
    d_ref[...] = y[:, A_W + B_W + C_W:IN_PAD]


def _in_proj(x, mod, norm_g, w_pad):
    n = x.shape[0]
    tpg = COND_GROUP // IN_TILE
    row = lambda i: (i, 0)
    return pl.pallas_call(
        _in_body,
        grid=(n // IN_TILE,),
        in_specs=[pl.BlockSpec((IN_TILE, D_MODEL), row),
                  pl.BlockSpec((None, 1, 6 * D_MODEL), lambda i: (i // tpg, 0, 0)),
                  pl.BlockSpec((1, D_MODEL), lambda i: (0, 0)),
                  pl.BlockSpec((D_MODEL, IN_PAD), lambda i: (0, 0))],
        out_specs=[pl.BlockSpec((IN_TILE, A_W), row), pl.BlockSpec((IN_TILE, B_W), row),
                   pl.BlockSpec((IN_TILE, C_W), row), pl.BlockSpec((IN_TILE, D_W), row)],
        out_shape=[jax.ShapeDtypeStruct((n, A_W), F32), jax.ShapeDtypeStruct((n, B_W), F32),
                   jax.ShapeDtypeStruct((n, C_W), F32), jax.ShapeDtypeStruct((n, D_W), F32)],
        compiler_params=_cparams(("arbitrary",)),
        name="in_proj",
    )(x, mod, norm_g, w_pad)


def _chunk_cumsum(x, reverse):
    c = x.shape[0]
    rows = lax.broadcasted_iota(jnp.int32, x.shape, 0)
    sh = 1
    while sh < c:
        if reverse:
            x = x + jnp.where(rows < c - sh, pltpu.roll(x, c - sh, 0), 0.0)
        else:
            x = x + jnp.where(rows >= sh, pltpu.roll(x, sh, 0), 0.0)
        sh *= 2
    return x


def _neumann_inverse(l_mat):
    c = l_mat.shape[0]
    r = lax.broadcasted_iota(jnp.int32, (c, c), 0)
    q = lax.broadcasted_iota(jnp.int32, (c, c), 1)
    inv = jnp.where(r == q, 1.0, 0.0).astype(F32) + l_mat
    power = l_mat
    span = 2
    while span < c:
        pb = power.astype(BF16)
        power = _dot(pb, pb)
        inv = inv + _dot(inv.astype(BF16), power.astype(BF16))
        span *= 2
    return inv


def _scan_body(fb_ref, bb_ref, first_ref, last_ref, seq_ref,
               af_ref, ab_ref, s0_ref, w0_ref, wu_ref, a0_ref, au_ref, kkp_ref, ka_ref,
               yf_ref, yb_ref, sfin_ref, s_scr):
    step = pl.program_id(0)

    @pl.when(first_ref[step] == 1)
    def _():
        s_scr[...] = s0_ref[...]

    c = SCAN_CHUNK
    row = lax.broadcasted_iota(jnp.int32, (c, c), 0)
    col = lax.broadcasted_iota(jnp.int32, (c, c), 1)
    lane_c = lax.broadcasted_iota(jnp.int32, (c, 128), 1) < HEAD_DIM
    lane_2c = lax.broadcasted_iota(jnp.int32, (2 * c, 128), 1) < HEAD_DIM
    same_head = _group_matrix(128, HEAD_DIM, 1.0)
    kkp = kkp_ref[...]
    ka = ka_ref[...]

    for d in range(2):
        x_ref = af_ref if d == 0 else ab_ref
        y_ref = yf_ref if d == 0 else yb_ref
        if d == 0:
            strict, incl = col < row, col <= row
        else:
            strict, incl = col > row, col >= row
        r = x_ref[:, 0:256]
        k = x_ref[:, 256:512]
        v = x_ref[:, 512:768]
        wd = x_ref[:, 768:896]
        ad = x_ref[:, 896:1024]
        wlog = -RW_DECAY_SCALE * jax.nn.sigmoid(
            w0_ref[d:d + 1, :] + _dot(jnp.tanh(wd).astype(BF16), wu_ref[d]))
        a = jax.nn.sigmoid(a0_ref[d:d + 1, :] + _dot(ad.astype(BF16), au_ref[d]))
        kd = k * (1.0 + (a - 1.0) * ka)
        kkf = k * kkp
        cum = _chunk_cumsum(wlog, reverse=(d == 1))
        tot = cum[c - 1:c, :] if d == 0 else cum[0:1, :]
        p_inc = jnp.exp(cum)
        p_exc = jnp.exp(cum - wlog)
        p_inv = jnp.exp(-cum)
        p_tot = jnp.exp(tot)

        for p in range(2):
            sl = slice(128 * p, 128 * p + 128)
            kk_raw = kkf[:, sl]
            ss = _dot(kk_raw * kk_raw, same_head, precision=HI)
            kk = kk_raw * lax.rsqrt(ss + EPS)
            a_p = -kk * p_exc[:, sl]
            r_p = r[:, sl] * p_inc[:, sl]
            b_i = kk * a[:, sl] * p_inv[:, sl]
            k_i = kd[:, sl] * p_inv[:, sl]
            v_p = v[:, sl]
            vb = v_p.astype(BF16)
            lhs = jnp.concatenate([a_p, r_p], axis=0)
            rhs = jnp.concatenate([b_i, k_i], axis=0).astype(BF16)
            grams = (_dot_nt(jnp.where(lane_2c, lhs, 0.0).astype(BF16), rhs),
                     _dot_nt(jnp.where(lane_2c, 0.0, lhs).astype(BF16), rhs))
            s_pair = s_scr[d, p]
            sb = s_pair.astype(BF16)
            a_s = _dot_nt(a_p.astype(BF16), sb)
            r_s = _dot_nt(r_p.astype(BF16), sb)

            invs, dms, ets, fts = [], [], [], []
            for g in grams:
                invs.append(_neumann_inverse(jnp.where(strict, g[0:c, 0:c], 0.0)).astype(BF16))
                dms.append(jnp.where(strict, g[0:c, c:2 * c], 0.0).astype(BF16))
                ets.append(jnp.where(incl, g[c:2 * c, 0:c], 0.0).astype(BF16))
                fts.append(jnp.where(incl, g[c:2 * c, c:2 * c], 0.0).astype(BF16))

            u0 = a_s + jnp.where(lane_c, _dot(dms[0], vb), _dot(dms[1], vb))
            u0b = u0.astype(BF16)
            z = jnp.where(lane_c, _dot(invs[0], u0b), _dot(invs[1], u0b))
            zb = z.astype(BF16)
            y = r_s + jnp.where(lane_c, _dot(ets[0], zb) + _dot(fts[0], vb),
                                _dot(ets[1], zb) + _dot(fts[1], vb))
            y_ref[:, sl] = y
            pt = p_tot[:, sl]
            upd = _dot_tn(zb, (b_i * pt).astype(BF16)) + _dot_tn(vb, (k_i * pt).astype(BF16))
            s_scr[d, p] = s_pair * pt + same_head * upd

    @pl.when(last_ref[step] == 1)
    def _():
        sfin_ref[...] = s_scr[...]


def _scan_tables():
    c = SCAN_CHUNK
    fb, bb, first, last, seq = [], [], [], [], []
    base = 0
    sid = 0
    for nseq, length in ((N_CTX_SEQ, CTX_LEN), (N_LAT_SEQ, LAT_LEN)):
        nc = length // c
        for _ in range(nseq):
            for j in range(nc):
                fb.append(base + j)
                bb.append(base + nc - 1 - j)
                first.append(int(j == 0))
                last.append(int(j == nc - 1))
                seq.append(sid)
            base += nc
            sid += 1
    return tuple(np.asarray(t, np.int32) for t in (fb, bb, first, last, seq))


def _rwkv_scan(a_grp, s0_pairs, w0, wu_pad, a0, au_pad, kkp, ka):
    tables = _scan_tables()
    n_steps = tables[0].shape[0]
    n_seq = s0_pairs.shape[0]
    n = a_grp.shape[0]
    c = SCAN_CHUNK
    const2 = lambda s, fb, bb, fi, la, sq: (0, 0)
    const3 = lambda s, fb, bb, fi, la, sq: (0, 0, 0)
    grid_spec = pltpu.PrefetchScalarGridSpec(
        num_scalar_prefetch=5,
        grid=(n_steps,),
        in_specs=[pl.BlockSpec((c, A_W), lambda s, fb, bb, fi, la, sq: (fb[s], 0)),
                  pl.BlockSpec((c, A_W), lambda s, fb, bb, fi, la, sq: (bb[s], 0)),
                  pl.BlockSpec((None, 2, 2, 128, 128), lambda s, fb, bb, fi, la, sq: (sq[s], 0, 0, 0, 0)),
                  pl.BlockSpec((2, GROUP_W), const2),
                  pl.BlockSpec((2, 128, GROUP_W), const3),
                  pl.BlockSpec((2, GROUP_W), const2),
                  pl.BlockSpec((2, 128, GROUP_W), const3),
                  pl.BlockSpec((1, GROUP_W), const2),
                  pl.BlockSpec((1, GROUP_W), const2)],
        out_specs=[pl.BlockSpec((c, GROUP_W), lambda s, fb, bb, fi, la, sq: (fb[s], 0)),
                   pl.BlockSpec((c, GROUP_W), lambda s, fb, bb, fi, la, sq: (bb[s], 0)),
                   pl.BlockSpec((None, 2, 2, 128, 128), lambda s, fb, bb, fi, la, sq: (sq[s], 0, 0, 0, 0))],
        scratch_shapes=[pltpu.VMEM((2, 2, 128, 128), F32)],
    )
    return pl.pallas_call(
        _scan_body,
        grid_spec=grid_spec,
        out_shape=[jax.ShapeDtypeStruct((n, GROUP_W), F32), jax.ShapeDtypeStruct((n, GROUP_W), F32),
                   jax.ShapeDtypeStruct((n_seq, 2, 2, 128, 128), F32)],
        compiler_params=_cparams(("arbitrary",)),
        name="rwkv_scan",
    )(*[jnp.asarray(t) for t in tables], a_grp, a_grp, s0_pairs, w0, wu_pad, a0, au_pad, kkp, ka)


MLA_SCALE = MLA_QK_DIM ** -0.5
GQA_SCALE = HEAD_DIM ** -0.5


def _mla_keys(ckv_b, rope_slot, rope_slot_sw, wk_ref, kng_ref, cosm, sinm, k_out_ref):
    g = kng_ref[0:1, :]
    g_sw = kng_ref[1:2, :]
    for h in range(4):
        nope = _dot(ckv_b, wk_ref[h])
        kr = nope + rope_slot
        rs = lax.rsqrt(jnp.sum(kr * kr, axis=-1, keepdims=True) * (1.0 / MLA_QK_DIM) + EPS)
        if cosm is None:
            k_out_ref[h] = (kr * rs * g).astype(BF16)
        else:
            ks = nope + rope_slot_sw
            k_out_ref[h] = ((kr * rs * g) * cosm + (ks * rs * g_sw) * sinm).astype(BF16)


def _prep_body(b_ref, c_ref, cosm_ref, sinm_ref, cosg_ref, sing_ref,
               qag_ref, wqq_ref, qng_ref, kvag_ref, wk_ref, wv_ref, kng_ref, gqg_ref, gkg_ref,
               qm_ref, km_ref, vm_ref, ckvn_ref, qg_ref, kg_ref, vg_ref, kgn_ref):
    cosm = cosm_ref[...]
    sinm = sinm_ref[...]
    qc = b_ref[:, 0:256]
    qn = qc * lax.rsqrt(jnp.sum(qc * qc, axis=-1, keepdims=True) * (1.0 / MLA_Q_RANK) + EPS) * qag_ref[...]
    qnb = qn.astype(BF16)
    g = qng_ref[0:1, :]
    g_sw = qng_ref[1:2, :]
    for h in range(4):
        qr = _dot(qnb, wqq_ref[h])
        qs = _dot(qnb, wqq_ref[4 + h])
        rs = lax.rsqrt(jnp.sum(qr * qr, axis=-1, keepdims=True) * (1.0 / MLA_QK_DIM) + EPS)
        qm_ref[h] = (((qr * rs * g) * cosm + (qs * rs * g_sw) * sinm) * MLA_SCALE).astype(BF16)
    ckv = _rms_rows(b_ref[:, 256:384], kvag_ref[...])
    ckvn_ref[...] = ckv
    cb = ckv.astype(BF16)
    _mla_keys(cb, b_ref[:, 384:512], b_ref[:, 512:640], wk_ref, kng_ref, cosm, sinm, km_ref)
    for p in range(2):
        vm_ref[p] = _dot(cb, wv_ref[p]).astype(BF16)

    tm = c_ref.shape[0]
    avg = _group_matrix(128, HEAD_DIM, 1.0 / HEAD_DIM)
    lane = lax.broadcasted_iota(jnp.int32, (tm, 128), 1)
    first_half = (lane % HEAD_DIM) < (HEAD_DIM // 2)
    low = lane < HEAD_DIM
    cosg = cosg_ref[...]
    sing = sing_ref[...]

    def rotate(xn):
        swapped = jnp.where(first_half, pltpu.roll(xn, 128 - HEAD_DIM // 2, 1), pltpu.roll(xn, HEAD_DIM // 2, 1))
        return xn * cosg + swapped * sing

    for blk in range(2):
        x = c_ref[:, 128 * blk:128 * blk + 128]
        xn = x * lax.rsqrt(_dot(x * x, avg, precision=HI) + EPS) * gqg_ref[...]
        xr = rotate(xn) * GQA_SCALE
        qg_ref[blk] = jnp.where(low, xr, 0.0).astype(BF16)
        qg_ref[blk + 2] = jnp.where(low, 0.0, xr).astype(BF16)
    xk = c_ref[:, 256:384]
    kn = xk * lax.rsqrt(_dot(xk * xk, avg, precision=HI) + EPS) * gkg_ref[...]
    kgn_ref[...] = kn
    kg_ref[...] = rotate(kn).astype(BF16)
    vg_ref[...] = c_ref[:, 384:512].astype(BF16)


def _attn_prep(b_grp, c_grp, tabs, lw):
    n = b_grp.shape[0]
    n_ctx_tiles = N_CTX_TOK // TILE
    lat_tiles = LAT_LEN // TILE
    row = lambda i: (i, 0)
    hrow = lambda i: (0, i, 0)
    tab = lambda i: (jnp.where(i < n_ctx_tiles, 0, 1 + (i - n_ctx_tiles) % lat_tiles), 0)
    c2 = lambda i: (0, 0)
    c3 = lambda i: (0, 0, 0)
    return pl.pallas_call(
        _prep_body,
        grid=(n // TILE,),
        in_specs=[pl.BlockSpec((TILE, B_W), row), pl.BlockSpec((TILE, C_W), row),
                  pl.BlockSpec((TILE, 128), tab), pl.BlockSpec((TILE, 128), tab),
                  pl.BlockSpec((TILE, 128), tab), pl.BlockSpec((TILE, 128), tab),
                  pl.BlockSpec((1, 256), c2), pl.BlockSpec((8, 256, 128), c3), pl.BlockSpec((2, 128), c2),
                  pl.BlockSpec((1, 128), c2), pl.BlockSpec((4, 128, 128), c3), pl.BlockSpec((2, 128, 128), c3),
                  pl.BlockSpec((2, 128), c2), pl.BlockSpec((1, 128), c2), pl.BlockSpec((1, 128), c2)],
        out_specs=[pl.BlockSpec((4, TILE, 128), hrow), pl.BlockSpec((4, TILE, 128), hrow),
                   pl.BlockSpec((2, TILE, 128), hrow), pl.BlockSpec((TILE, 128), row),
                   pl.BlockSpec((4, TILE, 128), hrow), pl.BlockSpec((TILE, 128), row),
                   pl.BlockSpec((TILE, 128), row), pl.BlockSpec((TILE, 128), row)],
        out_shape=[jax.ShapeDtypeStruct((4, n, 128), BF16), jax.ShapeDtypeStruct((4, n, 128), BF16),
                   jax.ShapeDtypeStruct((2, n, 128), BF16), jax.ShapeDtypeStruct((n, 128), F32),
                   jax.ShapeDtypeStruct((4, n, 128), BF16), jax.ShapeDtypeStruct((n, 128), BF16),
                   jax.ShapeDtypeStruct((n, 128), BF16), jax.ShapeDtypeStruct((n, 128), F32)],
        compiler_params=_cparams(("arbitrary",)),
        name="attn_prep",
    )(b_grp, c_grp, tabs["cosm"], tabs["sinm"], tabs["cosg"], tabs["sing"],
      lw["qa_g"], lw["wqq"], lw["qn_g2"], lw["kva_g"], lw["wk"], lw["wv"], lw["kn_g2"], lw["gq_g"], lw["gk_g"])


def _ctxkv_body(ckv_ref, krp_ref, wk_ref, wv_ref, kng_ref, k_ref, v_ref):
    cb = ckv_ref[...].astype(BF16)
    _mla_keys(cb, krp_ref[...], None, wk_ref, kng_ref, None, None, k_ref)
    for p in range(2):
        v_ref[p] = _dot(cb, wv_ref[p]).astype(BF16)


def _mla_ctx_kv(cache_ckv, cache_krope_placed, wk, wv, kn_g2):
    nb, depth, plen, _ = cache_ckv.shape
    return pl.pallas_call(
        _ctxkv_body,
        grid=(depth, nb),
        in_specs=[pl.BlockSpec((None, None, plen, 128), lambda l, b: (b, l, 0, 0)),
                  pl.BlockSpec((None, None, plen, 128), lambda l, b: (b, l, 0, 0)),
                  pl.BlockSpec((None, 4, 128, 128), lambda l, b: (l, 0, 0, 0)),
                  pl.BlockSpec((None, 2, 128, 128), lambda l, b: (l, 0, 0, 0)),
                  pl.BlockSpec((None, 2, 128), lambda l, b: (l, 0, 0))],
        out_specs=[pl.BlockSpec((None, 4, None, plen, 128), lambda l, b: (l, 0, b, 0, 0)),
                   pl.BlockSpec((None, 2, None, plen, 128), lambda l, b: (l, 0, b, 0, 0))],
        out_shape=[jax.ShapeDtypeStruct((depth, 4, nb, plen, 128), BF16),
                   jax.ShapeDtypeStruct((depth, 2, nb, plen, 128), BF16)],
        compiler_params=_cparams(("arbitrary", "arbitrary")),
        name="mla_ctx_kv",
    )(cache_ckv, cache_krope_placed, wk, wv, kn_g2)


def _attn_body(*refs, has_ctx):
    if has_ctx:
        qa_ref, qb_ref, ka_ref, kb_ref, v_ref, kca_ref, kcb_ref, vc_ref, o_ref = refs
    else:
        qa_ref, qb_ref, ka_ref, kb_ref, v_ref, o_ref = refs
        kca_ref = kcb_ref = vc_ref = None

    def head(q_ref, k_ref, kc_ref):
        q = q_ref[...]
        s = _dot_nt(q, k_ref[...])
        m = jnp.max(s, axis=-1, keepdims=True)
        if has_ctx:
            sc = _dot_nt(q, kc_ref[...].astype(BF16))
            m = jnp.maximum(m, jnp.max(sc, axis=-1, keepdims=True))
            pc = jnp.exp(sc - m)
        e = jnp.exp(s - m)
        den = jnp.sum(e, axis=-1, keepdims=True)
        acc = _dot(e.astype(BF16), v_ref[...])
        if has_ctx:
            den = den + jnp.sum(pc, axis=-1, keepdims=True)
            acc = acc + _dot(pc.astype(BF16), vc_ref[...].astype(BF16))
        return acc / den

    oa = head(qa_ref, ka_ref, kca_ref)
    ob = head(qb_ref, kb_ref, kcb_ref)
    lane = lax.broadcasted_iota(jnp.int32, oa.shape, 1)
    o_ref[...] = jnp.where(lane < HEAD_DIM, oa, ob).astype(o_ref.dtype)


def _attention(q, ka, kb, v, head_a, head_b, k_head_a, k_head_b, v_idx, n_seq, seq_len, tok0, ctx=None):
    tq = min(ATT_TQ, seq_len)
    nq = seq_len // tq
    seq0 = tok0 // seq_len
    q0 = tok0 // tq

    def qmap(hsel):
        return lambda b, p, i: (hsel(p), q0 + b * nq + i, 0)

    def kmap(hsel, arr):
        if arr.ndim == 3:
            return lambda b, p, i: (hsel(p), seq0 + b, 0)
        return lambda b, p, i: (seq0 + b, 0)

    def kspec(arr, hsel):
        if arr.ndim == 3:
            return pl.BlockSpec((None, seq_len, 128), kmap(hsel, arr))
        return pl.BlockSpec((seq_len, 128), kmap(hsel, arr))

    in_specs = [pl.BlockSpec((None, tq, 128), qmap(head_a)), pl.BlockSpec((None, tq, 128), qmap(head_b)),
                kspec(ka, k_head_a), kspec(kb, k_head_b), kspec(v, v_idx)]
    args = [q, q, ka, kb, v]
    if ctx is not None:
        kca, kcb, vc = ctx
        past = vc.shape[-2]

        def cspec(arr, hsel):
            if arr.ndim == 4:
                return pl.BlockSpec((None, None, past, 128), lambda b, p, i: (hsel(p), b, 0, 0))
            return pl.BlockSpec((None, past, 128), lambda b, p, i: (b, 0, 0))

        in_specs += [cspec(kca, k_head_a), cspec(kcb, k_head_b), cspec(vc, v_idx)]
        args += [kca, kcb, vc]
    return pl.pallas_call(
        functools.partial(_attn_body, has_ctx=ctx is not None),
        grid=(n_seq, 2, nq),
        in_specs=in_specs,
        out_specs=pl.BlockSpec((tq, 128), lambda b, p, i: (b * nq + i, p)),
        out_shape=jax.ShapeDtypeStruct((n_seq * seq_len, 256), BF16),
        compiler_params=_cparams(("arbitrary", "arbitrary", "arbitrary")),
        name="attention_ctx" if ctx is None else "attention_lat",
    )(*args)


def _out_body(x_ref, mod_ref, a_ref, yf_ref, yb_ref, obc_ref, obl_ref, occ_ref, ocl_ref,
              d_ref, dprev_ref, dnext_ref, wo_ref, gu_ref, rk_ref, gn_ref, cw_ref, cb_ref, n2_ref, wq_ref,
              x1_ref, h2_ref, q_ref, *, n_ctx_tiles):
    i = pl.program_id(0)
    is_ctx = i < n_ctx_tiles
    tm = x_ref.shape[0]
    mod = mod_ref[...]
    gate1 = mod[:, 2 * D_MODEL:3 * D_MODEL]
    shift2 = mod[:, 3 * D_MODEL:4 * D_MODEL]
    scale2 = mod[:, 4 * D_MODEL:5 * D_MODEL]

    r = a_ref[:, 0:256]
    k = a_ref[:, 256:512]
    v = a_ref[:, 512:768]
    gd = a_ref[:, 1024:1152]
    y = yf_ref[...] + yb_ref[...]
    avg = _group_matrix(GROUP_W, HEAD_DIM, 1.0 / HEAD_DIM)
    ones = _group_matrix(GROUP_W, HEAD_DIM, 1.0)
    yn = y * lax.rsqrt(_dot(y * y, avg, precision=HI) + EPS) * gn_ref[...]
    bonus = _dot(r * k * rk_ref[...], ones, precision=HI) * v
    gate = _dot(jax.nn.sigmoid(gd).astype(BF16), gu_ref[...])
    o_a = (yn + bonus) * gate

    o_b = jnp.where(is_ctx, obc_ref[...], obl_ref[...])
    o_c = jnp.where(is_ctx, occ_ref[...], ocl_ref[...])

    u = d_ref[:, 512:768] * d_ref[:, 0:256]
    u_prev = dprev_ref[:, 512:768] * dprev_ref[:, 0:256]
    u_next = dnext_ref[:, 512:768] * dnext_ref[:, 0:256]
    rows = lax.broadcasted_iota(jnp.int32, (tm, GROUP_W), 0)
    up = jnp.where(rows == 0, u_prev, pltpu.roll(u, 1, 0))
    un = jnp.where(rows == tm - 1, u_next, pltpu.roll(u, tm - 1, 0))
    conv = up * cw_ref[0:1, :] + u * cw_ref[1:2, :] + un * cw_ref[2:3, :] + cb_ref[...]
    o_d = d_ref[:, 256:512] * conv

    mix_in = jnp.concatenate([o_a.astype(BF16), o_b, o_c, o_d.astype(BF16)], axis=1)
    x1 = x_ref[...] + gate1 * _dot(mix_in, wo_ref[...])
    x1_ref[...] = x1
    h2 = (_rms_rows(x1, n2_ref[...]) * (1.0 + scale2) + shift2).astype(BF16)
    h2_ref[...] = h2
    q_ref[...] = _dot(h2, wq_ref[...])


def _out_proj(x, mod, a_grp, yf, yb, ob_ctx, ob_lat, oc_ctx, oc_lat, d_grp, d_prev, d_next, lw):
    n = x.shape[0]
    n_tiles = n // TILE
    n_ctx_tiles = N_CTX_TOK // TILE
    tpg = COND_GROUP // TILE
    row = lambda i: (i, 0)
    ctx_row = lambda i: (jnp.minimum(i, n_ctx_tiles - 1), 0)
    lat_row = lambda i: (jnp.maximum(i - n_ctx_tiles, 0), 0)
    halo = lambda i: (i, 0, 0)
    c2 = lambda i: (0, 0)
    nq = PEER_HEADS * 2 * PEER_KEYS
    return pl.pallas_call(
        functools.partial(_out_body, n_ctx_tiles=n_ctx_tiles),
        grid=(n_tiles,),
        in_specs=[pl.BlockSpec((TILE, D_MODEL), row),
                  pl.BlockSpec((None, 1, 6 * D_MODEL), lambda i: (i // tpg, 0, 0)),
                  pl.BlockSpec((TILE, A_W), row),
                  pl.BlockSpec((TILE, GROUP_W), row), pl.BlockSpec((TILE, GROUP_W), row),
                  pl.BlockSpec((TILE, GROUP_W), ctx_row), pl.BlockSpec((TILE, GROUP_W), lat_row),
                  pl.BlockSpec((TILE, GROUP_W), ctx_row), pl.BlockSpec((TILE, GROUP_W), lat_row),
                  pl.BlockSpec((TILE, D_W), row),
                  pl.BlockSpec((None, 1, D_W), halo), pl.BlockSpec((None, 1, D_W), halo),
                  pl.BlockSpec((D_MODEL, D_MODEL), c2), pl.BlockSpec((128, GROUP_W), c2),
                  pl.BlockSpec((1, GROUP_W), c2), pl.BlockSpec((1, GROUP_W), c2),
                  pl.BlockSpec((3, GROUP_W), c2), pl.BlockSpec((1, GROUP_W), c2),
                  pl.BlockSpec((1, D_MODEL), c2), pl.BlockSpec((D_MODEL, nq), c2)],
        out_specs=[pl.BlockSpec((TILE, D_MODEL), row), pl.BlockSpec((TILE, D_MODEL), row),
                   pl.BlockSpec((TILE, nq), row)],
        out_shape=[jax.ShapeDtypeStruct((n, D_MODEL), F32), jax.ShapeDtypeStruct((n, D_MODEL), BF16),
                   jax.ShapeDtypeStruct((n, nq), F32)],
        compiler_params=_cparams(("arbitrary",)),
        name="out_proj",
    )(x, mod, a_grp, yf, yb, ob_ctx, ob_lat, oc_ctx, oc_lat, d_grp, d_prev, d_next,
      lw["w_out"], lw["gu"], lw["rk"], lw["gn"], lw["conv_w"], lw["conv_b"], lw["norm2_g"], lw["wq"])


def _top16(x):
    nrow, tn = x.shape
    rows = lax.broadcasted_iota(jnp.int32, (nrow, tn), 0)
    slot = lax.broadcasted_iota(jnp.int32, (PEER_TOPK, tn), 0)

    def body(k, carry):
        cur, vals = carry
        m = jnp.max(cur, axis=0, keepdims=True)
        first = jnp.min(jnp.where(cur == m, rows, nrow), axis=0, keepdims=True)
        cur = jnp.where(rows == first, -jnp.inf, cur)
        vals = jnp.where(slot == k, m, vals)
        return cur, vals

    _, vals = lax.fori_loop(0, PEER_TOPK, body, (x, jnp.zeros((PEER_TOPK, tn), F32)))
    return vals


def _topk_body(q_ref, k1_ref, k2_ref, s1_ref, s2_ref, e1_ref, e2_ref, tau_ref):
    for h in range(PEER_HEADS):
        qa = q_ref[:, (2 * h) * PEER_KEYS:(2 * h + 1) * PEER_KEYS].astype(BF16)
        qb = q_ref[:, (2 * h + 1) * PEER_KEYS:(2 * h + 2) * PEER_KEYS].astype(BF16)
        s1 = _dot_nt(k1_ref[h], qa)
        s2 = _dot_nt(k2_ref[h], qb)
        v1 = _top16(s1)
        v2 = _top16(s2)
        cand = jnp.concatenate([v1[a:a + 1, :] + v2 for a in range(PEER_TOPK)], axis=0)
        best = _top16(cand)
        zsum = jnp.sum(jnp.exp(best - best[0:1, :]), axis=0, keepdims=True)
        s1_ref[h] = s1
        s2_ref[h] = s2
        e1_ref[h] = jnp.exp(s1 - v1[0:1, :])
        e2_ref[h] = jnp.exp(s2 - v2[0:1, :]) / zsum
        tau_ref[h] = best[PEER_TOPK - 1:PEER_TOPK, :]


def _peer_topk(q, k1, k2):
    n = q.shape[0]
    tn = TOPK_TN
    big = pl.BlockSpec((PEER_HEADS, PEER_KEYS, tn), lambda i: (0, 0, i))
    big_shape = jax.ShapeDtypeStruct((PEER_HEADS, PEER_KEYS, n), F32)
    c3 = lambda i: (0, 0, 0)
    return pl.pallas_call(
        _topk_body,
        grid=(n // tn,),
        in_specs=[pl.BlockSpec((tn, PEER_HEADS * 2 * PEER_KEYS), lambda i: (i, 0)),
                  pl.BlockSpec((PEER_HEADS, PEER_KEYS, PEER_KEYS), c3),
                  pl.BlockSpec((PEER_HEADS, PEER_KEYS, PEER_KEYS), c3)],
        out_specs=[big, big, big, big, pl.BlockSpec((PEER_HEADS, 1, tn), lambda i: (0, 0, i))],
        out_shape=[big_shape, big_shape, big_shape, big_shape,
                   jax.ShapeDtypeStruct((PEER_HEADS, 1, n), F32)],
        compiler_params=_cparams(("arbitrary",)),
        name="peer_topk",
    )(q, k1, k2)


GELU_C0 = 0.7978845608028654
GELU_C1 = 0.044715


def _dense_body(h2_ref, u_ref, vt_ref, s1_ref, s2_ref, e1_ref, e2_ref, tau_ref, x1_ref, mod_ref,
                o_ref, hs, gs, acc):
    j = pl.program_id(1)

    @pl.when(j == 0)
    def _():
        acc[...] = jnp.zeros_like(acc)

    hs[...] = _dot_nt(u_ref[...], h2_ref[...])
    tn = hs.shape[1]

    def slab(ii, carry):
        off = pl.multiple_of(ii * PEER_KEYS, PEER_KEYS)
        w = jnp.zeros((PEER_KEYS, tn), F32)
        for h in range(PEER_HEADS):
            cand = s1_ref[h, pl.ds(ii, 1), :] + s2_ref[h]
            w = w + jnp.where(cand >= tau_ref[h], e1_ref[h, pl.ds(ii, 1), :] * e2_ref[h], 0.0)
        x = hs[pl.ds(off, PEER_KEYS), :]
        act = 0.5 * x * (1.0 + jnp.tanh(GELU_C0 * (x + GELU_C1 * (x * x * x))))
        gs[pl.ds(off, PEER_KEYS), :] = (w * act).astype(BF16)
        return carry

    lax.fori_loop(0, PEER_TE // PEER_KEYS, slab, 0)
    acc[...] += _dot(vt_ref[...], gs[...])

    @pl.when(j == pl.num_programs(1) - 1)
    def _():
        gate2 = mod_ref[:, 5 * D_MODEL:6 * D_MODEL]
        o_ref[...] = x1_ref[...] + gate2 * acc[...].T


def _peer_dense(h2, u_b, vt_b, s1, s2, e1, e2, tau, x1, mod):
    n = h2.shape[0]
    tn, te = PEER_TN, PEER_TE
    rows_per_step = te // PEER_KEYS
    tpg = COND_GROUP // tn
    tok = lambda i, j: (i, 0)
    key_rows = pl.BlockSpec((PEER_HEADS, rows_per_step, tn), lambda i, j: (0, j, i))
    key_all = pl.BlockSpec((PEER_HEADS, PEER_KEYS, tn), lambda i, j: (0, 0, i))
    return pl.pallas_call(
        _dense_body,
        grid=(n // tn, N_EXPERTS // te),
        in_specs=[pl.BlockSpec((tn, D_MODEL), tok),
                  pl.BlockSpec((te, D_MODEL), lambda i, j: (j, 0)),
                  pl.BlockSpec((D_MODEL, te), lambda i, j: (0, j)),
                  key_rows, key_all, key_rows, key_all,
                  pl.BlockSpec((PEER_HEADS, 1, tn), lambda i, j: (0, 0, i)),
                  pl.BlockSpec((tn, D_MODEL), tok),
                  pl.BlockSpec((None, 1, 6 * D_MODEL), lambda i, j: (i // tpg, 0, 0))],
        out_specs=pl.BlockSpec((tn, D_MODEL), tok),
        out_shape=jax.ShapeDtypeStruct((n, D_MODEL), F32),
        scratch_shapes=[pltpu.VMEM((te, tn), F32), pltpu.VMEM((te, tn), BF16), pltpu.VMEM((D_MODEL, tn), F32)],
        compiler_params=_cparams(("arbitrary", "arbitrary")),
        name="peer_dense",
    )(h2, u_b, vt_b, s1, s2, e1, e2, tau, x1, mod)


def _in_proj_columns():
    src = np.full((IN_PAD,), -1, np.int64)

    def put(dst, start, width):
        src[dst:dst + width] = np.arange(start, start + width)

    put(0, 0, 768)
    put(768, 768, 128)
    put(896, 896, 64)
    put(1024, 960, 128)
    b0 = A_W
    put(b0, 1088, 192)
    put(b0 + 256, 1280, 128)
    put(b0 + 384 + 64, 1408, 32)
    put(b0 + 512 + 64, 1408 + 16, 16)
    put(b0 + 512 + 80, 1408, 16)
    c0 = A_W + B_W
    for slot, head in enumerate((0, 2, 1, 3)):
        put(c0 + 64 * slot, 1440 + 64 * head, 64)
    put(c0 + 256, 1696, 256)
    put(A_W + B_W + C_W, 1952, 768)
    return src


def _swap_tail(w):
    return jnp.concatenate([w[..., :64], w[..., 80:96], w[..., 64:80]], axis=-1)


def _pad_last(w, width):
    return jnp.pad(w, [(0, 0)] * (w.ndim - 1) + [(0, width - w.shape[-1])])


def _layer_weights(p):
    depth = p["w_in"].shape[0]
    src = _in_proj_columns()
    w_in = jnp.take(p["w_in"], jnp.asarray(np.maximum(src, 0)), axis=2)
    w_in = jnp.where(jnp.asarray(src >= 0)[None, None, :], w_in, 0.0).astype(BF16)

    wuq = p["mla_wuq"].reshape(depth, MLA_Q_RANK, 4, MLA_QK_DIM).transpose(0, 2, 1, 3)
    wq_plain = jnp.pad(wuq, ((0, 0), (0, 0), (0, 256 - MLA_Q_RANK), (0, 128 - MLA_QK_DIM)))
    wq_swap = jnp.pad(_swap_tail(wuq), ((0, 0), (0, 0), (0, 256 - MLA_Q_RANK), (0, 128 - MLA_QK_DIM)))
    wukv = p["mla_wukv"].reshape(depth, 128, 4, 128)
    wk = _pad_last(wukv[..., :64].transpose(0, 2, 1, 3), 128)
    wv_heads = wukv[..., 64:].transpose(0, 2, 1, 3)
    wv = jnp.concatenate([wv_heads[:, 0::2], wv_heads[:, 1::2]], axis=-1)

    def gain2(g):
        return jnp.stack([_pad_last(g, 128), _pad_last(_swap_tail(g), 128)], axis=1)

    w_out = p["w_out"]
    oc = w_out[:, 512:768].reshape(depth, 4, 64, D_MODEL)[:, jnp.asarray([0, 2, 1, 3])].reshape(depth, 256, D_MODEL)
    w_out = jnp.concatenate([w_out[:, :512], oc, w_out[:, 768:]], axis=1).astype(BF16)

    wu_pad = jnp.zeros((depth, 2, 128, GROUP_W), F32)
    wu_pad = wu_pad.at[:, 0, 0:64].set(p["rw_wu"][:, 0]).at[:, 1, 64:128].set(p["rw_wu"][:, 1])
    au_pad = jnp.zeros((depth, 2, 128, GROUP_W), F32)
    au_pad = au_pad.at[:, 0, 0:32].set(p["rw_au"][:, 0]).at[:, 1, 32:64].set(p["rw_au"][:, 1])

    return {
        "w_in": w_in,
        "norm1_g": p["norm1_g"][:, None, :],
        "norm2_g": p["norm2_g"][:, None, :],
        "w_out": w_out,
        "w0": p["rw_w0"], "a0": p["rw_a0"],
        "wu": wu_pad.astype(BF16), "au": au_pad.astype(BF16),
        "kkp": p["rw_kk"][:, None, :], "ka": p["rw_ka"][:, None, :],
        "gu": p["rw_gu"].astype(BF16),
        "rk": p["rw_rk"].reshape(depth, 1, GROUP_W),
        "gn": p["rw_gn"][:, None, :],
        "qa_g": _pad_last(p["mla_qa_g"], 256)[:, None, :],
        "wqq": jnp.concatenate([wq_plain, wq_swap], axis=1).astype(BF16),
        "qn_g2": gain2(p["mla_qn_g"]),
        "kva_g": p["mla_kva_g"][:, None, :],
        "wk": wk.astype(BF16), "wv": wv.astype(BF16),
        "kn_g2": gain2(p["mla_kn_g"]),
        "gq_g": jnp.tile(p["gqa_qn_g"], (1, 2))[:, None, :],
        "gk_g": jnp.tile(p["gqa_kn_g"], (1, 2))[:, None, :],
        "conv_w": p["conv_w"], "conv_b": p["conv_b"][:, None, :],
        "wq": p["peer_wq"].astype(BF16),
        "k1": p["peer_k1"].astype(BF16), "k2": p["peer_k2"].astype(BF16),
        "u": p["peer_u"].astype(BF16),
        "vt": jnp.swapaxes(p["peer_v"], 1, 2).astype(BF16),
    }


def _rope_tables():
    t = jnp.arange(LAT_LEN, dtype=F32)
    grid_row = jnp.floor(t / GRID_W)
    grid_col = t - grid_row * GRID_W

    def angles(rot_dim):
        n_freq = rot_dim // 4
        freqs = ROPE_THETA ** (-jnp.arange(n_freq, dtype=F32) / n_freq)
        ang = jnp.concatenate([grid_row[:, None] * freqs, grid_col[:, None] * freqs], axis=-1)
        return jnp.cos(ang), jnp.sin(ang)

    cm, sm = angles(32)
    ones64 = jnp.ones((LAT_LEN, 64), F32)
    zeros64 = jnp.zeros((LAT_LEN, 64), F32)
    cosm = jnp.concatenate([ones64, cm, cm, ones64[:, :32]], axis=-1)
    sinm = jnp.concatenate([zeros64, -sm, sm, zeros64[:, :32]], axis=-1)
    cg, sg = angles(64)
    cosg = jnp.tile(jnp.concatenate([cg, cg], axis=-1), (1, 2))
    sing = jnp.tile(jnp.concatenate([-sg, sg], axis=-1), (1, 2))
    ident = jnp.ones((TILE, 128), F32)
    zero = jnp.zeros((TILE, 128), F32)
    return {"cosm": jnp.concatenate([ident, cosm]), "sinm": jnp.concatenate([zero, sinm]),
            "cosg": jnp.concatenate([ident, cosg]), "sing": jnp.concatenate([zero, sing])}


def _states_to_pairs(s):
    lead = s.shape[:-3]
    s = s.reshape(lead + (2, 2, 64, 64))
    z = jnp.zeros(lead + (2, 64, 64), s.dtype)
    top = jnp.concatenate([s[..., 0, :, :], z], axis=-1)
    bot = jnp.concatenate([z, s[..., 1, :, :]], axis=-1)
    return jnp.concatenate([top, bot], axis=-2)


def _pairs_to_states(sp):
    lead = sp.shape[:-3]
    a = sp[..., 0:64, 0:64]
    b = sp[..., 64:128, 64:128]
    return jnp.stack([a, b], axis=-3).reshape(lead + (4, 64, 64))


def _conv_halos(d_grp):
    n_tiles = d_grp.shape[0] // TILE
    tiles = d_grp.reshape(n_tiles, TILE, D_W)
    first_rows = tiles[:, 0, :]
    last_rows = tiles[:, TILE - 1, :]
    zero = jnp.zeros((1, D_W), d_grp.dtype)
    prev = jnp.concatenate([zero, last_rows[:-1]], axis=0)
    nxt = jnp.concatenate([first_rows[1:], zero], axis=0)
    idx = np.arange(n_tiles)
    n_ctx_tiles = N_CTX_TOK // TILE
    per_seq = LAT_LEN // TILE
    lat_pos = (idx - n_ctx_tiles) % per_seq
    seq_start = np.where(idx < n_ctx_tiles, True, lat_pos == 0)
    seq_end = np.where(idx < n_ctx_tiles, True, lat_pos == per_seq - 1)
    prev = jnp.where(jnp.asarray(seq_start)[:, None], 0.0, prev)
    nxt = jnp.where(jnp.asarray(seq_end)[:, None], 0.0, nxt)
    return prev[:, None, :], nxt[:, None, :]


def kernel(x_prompt, x_sample, state_rwkv, cache_mla_ckv, cache_mla_krope, cache_gqa_k, cache_gqa_v, c, c_ctx, norm1_g, norm2_g, w_mod, b_mod, w_in, w_out, rw_w0, rw_wu, rw_a0, rw_au, rw_gu, rw_kk, rw_ka, rw_rk, rw_gn, mla_qa_g, mla_wuq, mla_kva_g, mla_wukv, mla_qn_g, mla_kn_g, gqa_qn_g, gqa_kn_g, conv_w, conv_b, peer_wq, peer_k1, peer_k2, peer_u, peer_v):
    depth = w_in.shape[0]
    params = dict(norm1_g=norm1_g, norm2_g=norm2_g, w_in=w_in, w_out=w_out, rw_w0=rw_w0, rw_wu=rw_wu,
                  rw_a0=rw_a0, rw_au=rw_au, rw_gu=rw_gu, rw_kk=rw_kk, rw_ka=rw_ka, rw_rk=rw_rk, rw_gn=rw_gn,
                  mla_qa_g=mla_qa_g, mla_wuq=mla_wuq, mla_kva_g=mla_kva_g, mla_wukv=mla_wukv,
                  mla_qn_g=mla_qn_g, mla_kn_g=mla_kn_g, gqa_qn_g=gqa_qn_g, gqa_kn_g=gqa_kn_g,
                  conv_w=conv_w, conv_b=conv_b, peer_wq=peer_wq, peer_k1=peer_k1, peer_k2=peer_k2,
                  peer_u=peer_u, peer_v=peer_v)
    lw_all = _layer_weights(params)
    tabs = _rope_tables()

    cvecs = jnp.concatenate([c_ctx[None, :], c, jnp.zeros((8 - 1 - N_LAT_SEQ, D_MODEL), F32)], axis=0)
    mods = _modulation(cvecs, w_mod, b_mod).reshape(depth, 8, 1, 6 * D_MODEL)

    x = jnp.concatenate([x_prompt.reshape(N_CTX_TOK, D_MODEL), x_sample.reshape(N_LAT_TOK, D_MODEL)], axis=0)

    s0_lat = _states_to_pairs(state_rwkv)
    s0_all = jnp.concatenate([jnp.zeros((N_CTX_SEQ,) + s0_lat.shape[1:], F32), s0_lat], axis=0)

    krope_placed = jnp.pad(cache_mla_krope, ((0, 0), (0, 0), (0, 0), (64, 32)))
    kctx_m, vctx_m = _mla_ctx_kv(cache_mla_ckv, krope_placed, lw_all["wk"], lw_all["wv"], lw_all["kn_g2"])
    past = cache_gqa_k.shape[2]
    kctx_g = cache_gqa_k.reshape(N_LAT_SEQ, depth, past, 128)
    vctx_g = cache_gqa_v.reshape(N_LAT_SEQ, depth, past, 128)

    ident = lambda p: p
    st_a, st_ckv, st_kr, st_k, st_v = [], [], [], [], []
    for l in range(depth):
        lw = {name: w[l] for name, w in lw_all.items()}
        mod = mods[l]
        a_grp, b_grp, c_grp, d_grp = _in_proj(x, mod, lw["norm1_g"], lw["w_in"])

        yf, yb, s_fin = _rwkv_scan(a_grp, s0_all[:, l], lw["w0"], lw["wu"], lw["a0"], lw["au"], lw["kkp"], lw["ka"])

        qm, km, vm, ckvn, qg, kg, vg, kgn = _attn_prep(b_grp, c_grp, tabs, lw)
        ob_ctx = _attention(qm, km, km, vm, lambda p: 2 * p, lambda p: 2 * p + 1, lambda p: 2 * p,
                            lambda p: 2 * p + 1, ident, N_CTX_SEQ, CTX_LEN, 0)
        ob_lat = _attention(qm, km, km, vm, lambda p: 2 * p, lambda p: 2 * p + 1, lambda p: 2 * p,
                            lambda p: 2 * p + 1, ident, N_LAT_SEQ, LAT_LEN, N_CTX_TOK,
                            ctx=(kctx_m[l], kctx_m[l], vctx_m[l]))
        oc_ctx = _attention(qg, kg, kg, vg, ident, lambda p: p + 2, ident, ident, ident,
                            N_CTX_SEQ, CTX_LEN, 0)
        oc_lat = _attention(qg, kg, kg, vg, ident, lambda p: p + 2, ident, ident, ident,
                            N_LAT_SEQ, LAT_LEN, N_CTX_TOK,
                            ctx=(kctx_g[:, l], kctx_g[:, l], vctx_g[:, l]))

        d_prev, d_next = _conv_halos(d_grp)
        x1, h2, q = _out_proj(x, mod, a_grp, yf, yb, ob_ctx, ob_lat, oc_ctx, oc_lat, d_grp, d_prev, d_next, lw)

        s1, s2, e1, e2, tau = _peer_topk(q, lw["k1"], lw["k2"])
        x = _peer_dense(h2, lw["u"], lw["vt"], s1, s2, e1, e2, tau, x1, mod)

        st_a.append(_pairs_to_states(s_fin[:N_CTX_SEQ]))
        st_ckv.append(ckvn[:N_CTX_TOK].reshape(N_CTX_SEQ, CTX_LEN, 128))
        st_kr.append(b_grp[:N_CTX_TOK, 448:480].reshape(N_CTX_SEQ, CTX_LEN, 32))
        st_k.append(kgn[:N_CTX_TOK].reshape(N_CTX_SEQ, CTX_LEN, 2, HEAD_DIM))
        st_v.append(c_grp[:N_CTX_TOK, 384:512].reshape(N_CTX_SEQ, CTX_LEN, 2, HEAD_DIM))

    y_prompt = x[:N_CTX_TOK].reshape(N_CTX_SEQ, CTX_LEN, D_MODEL)
    y_sample = x[N_CTX_TOK:].reshape(N_LAT_SEQ, LAT_LEN, D_MODEL)
    return (y_prompt, y_sample, jnp.stack(st_a, axis=1), jnp.stack(st_ckv, axis=1), jnp.stack(st_kr, axis=1),
            jnp.stack(st_k, axis=1), jnp.stack(st_v, axis=1))
```

```python
import functools

import numpy as np
import jax
import jax.numpy as jnp
from jax import lax
from jax.experimental import pallas as pl
from jax.experimental.pallas import tpu as pltpu

F32 = jnp.float32
BF16 = jnp.bfloat16
HI = lax.Precision.HIGHEST

D_MODEL = 1024
N_CTX_SEQ = 16
CTX_LEN = 256
N_LAT_SEQ = 2
LAT_LEN = 4096
GRID_W = 64
ROPE_THETA = 10000.0
EPS = 1e-6
GROUP_W = 256
HEAD_DIM = 64
RW_DECAY_SCALE = 0.6065306597
MLA_QK_DIM = 96
MLA_Q_RANK = 192
PEER_HEADS = 8
PEER_KEYS = 128
PEER_TOPK = 16
N_EXPERTS = PEER_KEYS * PEER_KEYS

N_CTX_TOK = N_CTX_SEQ * CTX_LEN
N_LAT_TOK = N_LAT_SEQ * LAT_LEN
N_TOK = N_CTX_TOK + N_LAT_TOK
COND_GROUP = 4096

A_W, B_W, C_W, D_W = 1152, 640, 512, 768
IN_PAD = A_W + B_W + C_W + D_W

SCAN_CHUNK = 128
TILE = 256
IN_TILE = 512
PEER_TN = 512
PEER_TE = 1024
TOPK_TN = 128
ATT_TQ = 256

VMEM_LIMIT = 56 * 1024 * 1024


def _cparams(sem):
    return pltpu.CompilerParams(dimension_semantics=sem, vmem_limit_bytes=VMEM_LIMIT)


def _dot(a, b, precision=None):
    return jnp.dot(a, b, preferred_element_type=F32, precision=precision)


def _dot_nt(a, b, precision=None):
    return lax.dot_general(a, b, (((1,), (1,)), ((), ())), preferred_element_type=F32,
                           precision=precision)


def _dot_tn(a, b, precision=None):
    return lax.dot_general(a, b, (((0,), (0,)), ((), ())), preferred_element_type=F32,
                           precision=precision)


def _rms_rows(x, g):
    return x * lax.rsqrt(jnp.mean(x * x, axis=-1, keepdims=True) + EPS) * g


def _group_matrix(n, group, value):
    r = lax.broadcasted_iota(jnp.int32, (n, n), 0) // group
    c = lax.broadcasted_iota(jnp.int32, (n, n), 1) // group
    return jnp.where(r == c, value, 0.0).astype(F32)


def _mod_body(c_ref, w_ref, b_ref, o_ref):
    c = c_ref[...]
    s = c * jax.nn.sigmoid(c)
    o_ref[...] = _dot(s.astype(BF16), w_ref[...].astype(BF16)) + b_ref[...]


def _modulation(cvecs, w_mod, b_mod):
    depth = w_mod.shape[0]
    tn = 1536
    return pl.pallas_call(
        _mod_body,
        grid=(depth, 6 * D_MODEL // tn),
        in_specs=[pl.BlockSpec((8, D_MODEL), lambda l, j: (0, 0)),
                  pl.BlockSpec((None, D_MODEL, tn), lambda l, j: (l, 0, j)),
                  pl.BlockSpec((None, 1, tn), lambda l, j: (l, 0, j))],
        out_specs=pl.BlockSpec((None, 8, tn), lambda l, j: (l, 0, j)),
        out_shape=jax.ShapeDtypeStruct((depth, 8, 6 * D_MODEL), F32),
        compiler_params=_cparams(("arbitrary", "arbitrary")),
        name="adaln_mod",
    )(cvecs, w_mod, b_mod.reshape(depth, 1, 6 * D_MODEL))


def _in_body(x_ref, mod_ref, g_ref, w_ref, a_ref, b_ref, c_ref, d_ref):
    mod = mod_ref[...]
    shift = mod[:, 0:D_MODEL]
    scale = mod[:, D_MODEL:2 * D_MODEL]
    h = _rms_rows(x_ref[...], g_ref[...]) * (1.0 + scale) + shift
    y = _dot(h.astype(BF16), w_ref[...])
    a_ref[...] = y[:, 0:A_W]
    b_ref[...] = y[:, A_W:A_W + B_W]
    c_ref[...] = y[:, A_W + B_W:A_W + B_W + C_W]
    d_ref[...] = y[:, A_W + B_W + C_W:IN_PAD]


def _in_proj(x, mod, norm_g, w_pad):
    n = x.shape[0]
    tpg = COND_GROUP // IN_TILE
    row = lambda i: (i, 0)
    return pl.pallas_call(
        _in_body,
        grid=(n // IN_TILE,),
        in_specs=[pl.BlockSpec((IN_TILE, D_MODEL), row),
                  pl.BlockSpec((None, 1, 6 * D_MODEL), lambda i: (i // tpg, 0, 0)),
                  pl.BlockSpec((1, D_MODEL), lambda i: (0, 0)),
                  pl.BlockSpec((D_MODEL, IN_PAD), lambda i: (0, 0))],
        out_specs=[pl.BlockSpec((IN_TILE, A_W), row), pl.BlockSpec((IN_TILE, B_W), row),
                   pl.BlockSpec((IN_TILE, C_W), row), pl.BlockSpec((IN_TILE, D_W), row)],
        out_shape=[jax.ShapeDtypeStruct((n, A_W), F32), jax.ShapeDtypeStruct((n, B_W), F32),
                   jax.ShapeDtypeStruct((n, C_W), F32), jax.ShapeDtypeStruct((n, D_W), F32)],
        compiler_params=_cparams(("arbitrary",)),
        name="in_proj",
    )(x, mod, norm_g, w_pad)


def _chunk_cumsum(x, reverse):
    c = x.shape[0]
    rows = lax.broadcasted_iota(jnp.int32, x.shape, 0)
    sh = 1
    while sh < c:
        if reverse:
            x = x + jnp.where(rows < c - sh, pltpu.roll(x, c - sh, 0), 0.0)
        else:
            x = x + jnp.where(rows >= sh, pltpu.roll(x, sh, 0), 0.0)
        sh *= 2
    return x


def _scan_body(fb_ref, bb_ref, first_ref, last_ref, seq_ref,
               af_ref, ab_ref, s0_ref, w0_ref, wu_ref, a0_ref, au_ref, kkp_ref, ka_ref,
               yf_ref, yb_ref, sfin_ref, s_scr):
    step = pl.program_id(0)

    @pl.when(first_ref[step] == 1)
    def _():
        s_scr[...] = s0_ref[...]

    c = SCAN_CHUNK
    row = lax.broadcasted_iota(jnp.int32, (c, c), 0)
    col = lax.broadcasted_iota(jnp.int32, (c, c), 1)
    lane_c = lax.broadcasted_iota(jnp.int32, (c, 128), 1) < HEAD_DIM
    lane_2c = lax.broadcasted_iota(jnp.int32, (2 * c, 128), 1) < HEAD_DIM
    same_head = _group_matrix(128, HEAD_DIM, 1.0)
    kkp = kkp_ref[...]
    ka = ka_ref[...]

    def per_head(lo, hi):
        return jnp.where(lane_c, lo, hi)

    groups = []
    for d in range(2):
        x_ref = af_ref if d == 0 else ab_ref
        if d == 0:
            strict, incl = col < row, col <= row
        else:
            strict, incl = col > row, col >= row
        r = x_ref[:, 0:256]
        k = x_ref[:, 256:512]
        v = x_ref[:, 512:768]
        wd = x_ref[:, 768:896]
        ad = x_ref[:, 896:1024]
        wlog = -RW_DECAY_SCALE * jax.nn.sigmoid(
            w0_ref[d:d + 1, :] + _dot(jnp.tanh(wd).astype(BF16), wu_ref[d]))
        a = jax.nn.sigmoid(a0_ref[d:d + 1, :] + _dot(ad.astype(BF16), au_ref[d]))
        kd = k * (1.0 + (a - 1.0) * ka)
        kkf = k * kkp
        cum = _chunk_cumsum(wlog, reverse=(d == 1))
        tot = cum[c - 1:c, :] if d == 0 else cum[0:1, :]
        p_inc = jnp.exp(cum)
        p_exc = jnp.exp(cum - wlog)
        p_inv = jnp.exp(-cum)
        p_tot = jnp.exp(tot)
        for p in range(2):
            sl = slice(128 * p, 128 * p + 128)
            groups.append(dict(d=d, p=p, sl=sl, strict=strict, incl=incl, kk_raw=kkf[:, sl],
                               a=a[:, sl], kd=kd[:, sl], r=r[:, sl], vb=v[:, sl].astype(BF16),
                               p_inc=p_inc[:, sl], p_exc=p_exc[:, sl], p_inv=p_inv[:, sl], pt=p_tot[:, sl]))

    for g in groups:
        g["ss"] = _dot(g["kk_raw"] * g["kk_raw"], same_head, precision=HI)
    for g in groups:
        kk = g["kk_raw"] * lax.rsqrt(g["ss"] + EPS)
        g["a_p"] = (-kk * g["p_exc"]).astype(BF16)
        g["r_p"] = (g["r"] * g["p_inc"]).astype(BF16)
        g["b_i"] = kk * g["a"] * g["p_inv"]
        g["k_i"] = g["kd"] * g["p_inv"]
        g["lhs"] = jnp.concatenate([g["a_p"], g["r_p"]], axis=0)
        g["rhs"] = jnp.concatenate([g["b_i"], g["k_i"]], axis=0).astype(BF16)
        g["sb"] = s_scr[g["d"], g["p"]].astype(BF16)
    zero_b = jnp.zeros((), BF16)
    for g in groups:
        g["gram"] = (_dot_nt(jnp.where(lane_2c, g["lhs"], zero_b), g["rhs"]),
                     _dot_nt(jnp.where(lane_2c, zero_b, g["lhs"]), g["rhs"]))
        g["a_s"] = _dot_nt(g["a_p"], g["sb"])
        g["r_s"] = _dot_nt(g["r_p"], g["sb"])
    for g in groups:
        st, inc = g["strict"], g["incl"]
        g["pow"] = [jnp.where(st, gm[0:c, 0:c], 0.0).astype(BF16) for gm in g["gram"]]
        g["dm"] = [jnp.where(st, gm[0:c, c:2 * c], 0.0).astype(BF16) for gm in g["gram"]]
        g["et"] = [jnp.where(inc, gm[c:2 * c, 0:c], 0.0).astype(BF16) for gm in g["gram"]]
        g["ft"] = [jnp.where(inc, gm[c:2 * c, c:2 * c], 0.0).astype(BF16) for gm in g["gram"]]
    for g in groups:
        g["z"] = g["a_s"] + per_head(_dot(g["dm"][0], g["vb"]), _dot(g["dm"][1], g["vb"]))
    span = 1
    while span < c:
        for g in groups:
            zb = g["z"].astype(BF16)
            g["z"] = g["z"] + per_head(_dot(g["pow"][0], zb), _dot(g["pow"][1], zb))
        span *= 2
        if span < c:
            for g in groups:
                g["pow"] = [_dot(m, m).astype(BF16) for m in g["pow"]]
    for g in groups:
        zb = g["z"].astype(BF16)
        g["zb"] = zb
        y = g["r_s"] + per_head(_dot(g["et"][0], zb) + _dot(g["ft"][0], g["vb"]),
                                _dot(g["et"][1], zb) + _dot(g["ft"][1], g["vb"]))
        y_ref = yf_ref if g["d"] == 0 else yb_ref
        y_ref[:, g["sl"]] = y
    for g in groups:
        pt = g["pt"]
        upd = (_dot_tn(g["zb"], (g["b_i"] * pt).astype(BF16))
               + _dot_tn(g["vb"], (g["k_i"] * pt).astype(BF16)))
        s_scr[g["d"], g["p"]] = s_scr[g["d"], g["p"]] * pt + same_head * upd

    @pl.when(last_ref[step] == 1)
    def _():
        sfin_ref[...] = s_scr[...]


def _scan_tables():
    c = SCAN_CHUNK
    fb, bb, first, last, seq = [], [], [], [], []
    base = 0
    sid = 0
    for nseq, length in ((N_CTX_SEQ, CTX_LEN), (N_LAT_SEQ, LAT_LEN)):
        nc = length // c
        for _ in range(nseq):
            for j in range(nc):
                fb.append(base + j)
                bb.append(base + nc - 1 - j)
                first.append(int(j == 0))
                last.append(int(j == nc - 1))
                seq.append(sid)
            base += nc
            sid += 1
    return tuple(np.asarray(t, np.int32) for t in (fb, bb, first, last, seq))


def _rwkv_scan(a_grp, s0_pairs, w0, wu_pad, a0, au_pad, kkp, ka):
    tables = _scan_tables()
    n_steps = tables[0].shape[0]
    n_seq = s0_pairs.shape[0]
    n = a_grp.shape[0]
    c = SCAN_CHUNK
    const2 = lambda s, fb, bb, fi, la, sq: (0, 0)
    const3 = lambda s, fb, bb, fi, la, sq: (0, 0, 0)
    grid_spec = pltpu.PrefetchScalarGridSpec(
        num_scalar_prefetch=5,
        grid=(n_steps,),
        in_specs=[pl.BlockSpec((c, A_W), lambda s, fb, bb, fi, la, sq: (fb[s], 0)),
                  pl.BlockSpec((c, A_W), lambda s, fb, bb, fi, la, sq: (bb[s], 0)),
                  pl.BlockSpec((None, 2, 2, 128, 128), lambda s, fb, bb, fi, la, sq: (sq[s], 0, 0, 0, 0)),
                  pl.BlockSpec((2, GROUP_W), const2),
                  pl.BlockSpec((2, 128, GROUP_W), const3),
                  pl.BlockSpec((2, GROUP_W), const2),
                  pl.BlockSpec((2, 128, GROUP_W), const3),
                  pl.BlockSpec((1, GROUP_W), const2),
                  pl.BlockSpec((1, GROUP_W), const2)],
        out_specs=[pl.BlockSpec((c, GROUP_W), lambda s, fb, bb, fi, la, sq: (fb[s], 0)),
                   pl.BlockSpec((c, GROUP_W), lambda s, fb, bb, fi, la, sq: (bb[s], 0)),
                   pl.BlockSpec((None, 2, 2, 128, 128), lambda s, fb, bb, fi, la, sq: (sq[s], 0, 0, 0, 0))],
        scratch_shapes=[pltpu.VMEM((2, 2, 128, 128), F32)],
    )
    return pl.pallas_call(
        _scan_body,
        grid_spec=grid_spec,
        out_shape=[jax.ShapeDtypeStruct((n, GROUP_W), F32), jax.ShapeDtypeStruct((n, GROUP_W), F32),
                   jax.ShapeDtypeStruct((n_seq, 2, 2, 128, 128), F32)],
        compiler_params=_cparams(("arbitrary",)),
        name="rwkv_scan",
    )(*[jnp.asarray(t) for t in tables], a_grp, a_grp, s0_pairs, w0, wu_pad, a0, au_pad, kkp, ka)


MLA_SCALE = MLA_QK_DIM ** -0.5
GQA_SCALE = HEAD_DIM ** -0.5


def _mla_keys(ckv_b, rope_slot, rope_slot_sw, wk_ref, kng_ref, cosm, sinm, k_out_ref):
    g = kng_ref[0:1, :]
    g_sw = kng_ref[1:2, :]
    for h in range(4):
        nope = _dot(ckv_b, wk_ref[h])
        kr = nope + rope_slot
        rs = lax.rsqrt(jnp.sum(kr * kr, axis=-1, keepdims=True) * (1.0 / MLA_QK_DIM) + EPS)
        if cosm is None:
            k_out_ref[h] = (kr * rs * g).astype(BF16)
        else:
            ks = nope + rope_slot_sw
            k_out_ref[h] = ((kr * rs * g) * cosm + (ks * rs * g_sw) * sinm).astype(BF16)


def _prep_body(b_ref, c_ref, cosm_ref, sinm_ref, cosg_ref, sing_ref,
               qag_ref, wqq_ref, qng_ref, kvag_ref, wk_ref, wv_ref, kng_ref, gqg_ref, gkg_ref,
               qm_ref, km_ref, vm_ref, ckvn_ref, qg_ref, kg_ref, vg_ref, kgn_ref):
    cosm = cosm_ref[...]
    sinm = sinm_ref[...]
    qc = b_ref[:, 0:256]
    qn = qc * lax.rsqrt(jnp.sum(qc * qc, axis=-1, keepdims=True) * (1.0 / MLA_Q_RANK) + EPS) * qag_ref[...]
    qnb = qn.astype(BF16)
    g = qng_ref[0:1, :]
    g_sw = qng_ref[1:2, :]
    for h in range(4):
        qr = _dot(qnb, wqq_ref[h])
        qs = _dot(qnb, wqq_ref[4 + h])
        rs = lax.rsqrt(jnp.sum(qr * qr, axis=-1, keepdims=True) * (1.0 / MLA_QK_DIM) + EPS)
        qm_ref[h] = (((qr * rs * g) * cosm + (qs * rs * g_sw) * sinm) * MLA_SCALE).astype(BF16)
    ckv = _rms_rows(b_ref[:, 256:384], kvag_ref[...])
    ckvn_ref[...] = ckv
    cb = ckv.astype(BF16)
    _mla_keys(cb, b_ref[:, 384:512], b_ref[:, 512:640], wk_ref, kng_ref, cosm, sinm, km_ref)
    for p in range(2):
        vm_ref[p] = _dot(cb, wv_ref[p]).astype(BF16)

    tm = c_ref.shape[0]
    avg = _group_matrix(128, HEAD_DIM, 1.0 / HEAD_DIM)
    lane = lax.broadcasted_iota(jnp.int32, (tm, 128), 1)
    first_half = (lane % HEAD_DIM) < (HEAD_DIM // 2)
    low = lane < HEAD_DIM
    cosg = cosg_ref[...]
    sing = sing_ref[...]

    def rotate(xn):
        swapped = jnp.where(first_half, pltpu.roll(xn, 128 - HEAD_DIM // 2, 1), pltpu.roll(xn, HEAD_DIM // 2, 1))
        return xn * cosg + swapped * sing

    for blk in range(2):
        x = c_ref[:, 128 * blk:128 * blk + 128]
        xn = x * lax.rsqrt(_dot(x * x, avg, precision=HI) + EPS) * gqg_ref[...]
        xr = rotate(xn) * GQA_SCALE
        qg_ref[blk] = jnp.where(low, xr, 0.0).astype(BF16)
        qg_ref[blk + 2] = jnp.where(low, 0.0, xr).astype(BF16)
    xk = c_ref[:, 256:384]
    kn = xk * lax.rsqrt(_dot(xk * xk, avg, precision=HI) + EPS) * gkg_ref[...]
    kgn_ref[...] = kn
    kg_ref[...] = rotate(kn).astype(BF16)
    vg_ref[...] = c_ref[:, 384:512].astype(BF16)


def _attn_prep(b_grp, c_grp, tabs, lw):
    n = b_grp.shape[0]
    n_ctx_tiles = N_CTX_TOK // TILE
    lat_tiles = LAT_LEN // TILE
    row = lambda i: (i, 0)
    hrow = lambda i: (0, i, 0)
    tab = lambda i: (jnp.where(i < n_ctx_tiles, 0, 1 + (i - n_ctx_tiles) % lat_tiles), 0)
    c2 = lambda i: (0, 0)
    c3 = lambda i: (0, 0, 0)
    return pl.pallas_call(
        _prep_body,
        grid=(n // TILE,),
        in_specs=[pl.BlockSpec((TILE, B_W), row), pl.BlockSpec((TILE, C_W), row),
                  pl.BlockSpec((TILE, 128), tab), pl.BlockSpec((TILE, 128), tab),
                  pl.BlockSpec((TILE, 128), tab), pl.BlockSpec((TILE, 128), tab),
                  pl.BlockSpec((1, 256), c2), pl.BlockSpec((8, 256, 128), c3), pl.BlockSpec((2, 128), c2),
                  pl.BlockSpec((1, 128), c2), pl.BlockSpec((4, 128, 128), c3), pl.BlockSpec((2, 128, 128), c3),
                  pl.BlockSpec((2, 128), c2), pl.BlockSpec((1, 128), c2), pl.BlockSpec((1, 128), c2)],
        out_specs=[pl.BlockSpec((4, TILE, 128), hrow), pl.BlockSpec((4, TILE, 128), hrow),
                   pl.BlockSpec((2, TILE, 128), hrow), pl.BlockSpec((TILE, 128), row),
                   pl.BlockSpec((4, TILE, 128), hrow), pl.BlockSpec((TILE, 128), row),
                   pl.BlockSpec((TILE, 128), row), pl.BlockSpec((TILE, 128), row)],
        out_shape=[jax.ShapeDtypeStruct((4, n, 128), BF16), jax.ShapeDtypeStruct((4, n, 128), BF16),
                   jax.ShapeDtypeStruct((2, n, 128), BF16), jax.ShapeDtypeStruct((n, 128), F32),
                   jax.ShapeDtypeStruct((4, n, 128), BF16), jax.ShapeDtypeStruct((n, 128), BF16),
                   jax.ShapeDtypeStruct((n, 128), BF16), jax.ShapeDtypeStruct((n, 128), F32)],
        compiler_params=_cparams(("arbitrary",)),
        name="attn_prep",
    )(b_grp, c_grp, tabs["cosm"], tabs["sinm"], tabs["cosg"], tabs["sing"],
      lw["qa_g"], lw["wqq"], lw["qn_g2"], lw["kva_g"], lw["wk"], lw["wv"], lw["kn_g2"], lw["gq_g"], lw["gk_g"])


def _ctxkv_body(ckv_ref, krp_ref, wk_ref, wv_ref, kng_ref, k_ref, v_ref):
    cb = ckv_ref[...].astype(BF16)
    _mla_keys(cb, krp_ref[...], None, wk_ref, kng_ref, None, None, k_ref)
    for p in range(2):
        v_ref[p] = _dot(cb, wv_ref[p]).astype(BF16)


def _mla_ctx_kv(cache_ckv, cache_krope_placed, wk, wv, kn_g2):
    nb, depth, plen, _ = cache_ckv.shape
    return pl.pallas_call(
        _ctxkv_body,
        grid=(depth, nb),
        in_specs=[pl.BlockSpec((None, None, plen, 128), lambda l, b: (b, l, 0, 0)),
                  pl.BlockSpec((None, None, plen, 128), lambda l, b: (b, l, 0, 0)),
                  pl.BlockSpec((None, 4, 128, 128), lambda l, b: (l, 0, 0, 0)),
                  pl.BlockSpec((None, 2, 128, 128), lambda l, b: (l, 0, 0, 0)),
                  pl.BlockSpec((None, 2, 128), lambda l, b: (l, 0, 0))],
        out_specs=[pl.BlockSpec((None, 4, None, plen, 128), lambda l, b: (l, 0, b, 0, 0)),
                   pl.BlockSpec((None, 2, None, plen, 128), lambda l, b: (l, 0, b, 0, 0))],
        out_shape=[jax.ShapeDtypeStruct((depth, 4, nb, plen, 128), BF16),
                   jax.ShapeDtypeStruct((depth, 2, nb, plen, 128), BF16)],
        compiler_params=_cparams(("arbitrary", "arbitrary")),
        name="mla_ctx_kv",
    )(cache_ckv, cache_krope_placed, wk, wv, kn_g2)


def _attn_body(*refs, has_ctx):
    if has_ctx:
        qa_ref, qb_ref, ka_ref, kb_ref, v_ref, kca_ref, kcb_ref, vc_ref, o_ref = refs
    else:
        qa_ref, qb_ref, ka_ref, kb_ref, v_ref, o_ref = refs
        kca_ref = kcb_ref = vc_ref = None

    def head(q_ref, k_ref, kc_ref):
        q = q_ref[...]
        s = _dot_nt(q, k_ref[...])
        m = jnp.max(s, axis=-1, keepdims=True)
        if has_ctx:
            sc = _dot_nt(q, kc_ref[...].astype(BF16))
            m = jnp.maximum(m, jnp.max(sc, axis=-1, keepdims=True))
            pc = jnp.exp(sc - m)
        e = jnp.exp(s - m)
        den = jnp.sum(e, axis=-1, keepdims=True)
        acc = _dot(e.astype(BF16), v_ref[...])
        if has_ctx:
            den = den + jnp.sum(pc, axis=-1, keepdims=True)
            acc = acc + _dot(pc.astype(BF16), vc_ref[...].astype(BF16))
        return acc / den

    oa = head(qa_ref, ka_ref, kca_ref)
    ob = head(qb_ref, kb_ref, kcb_ref)
    lane = lax.broadcasted_iota(jnp.int32, oa.shape, 1)
    o_ref[...] = jnp.where(lane < HEAD_DIM, oa, ob).astype(o_ref.dtype)


def _attention(q, ka, kb, v, head_a, head_b, k_head_a, k_head_b, v_idx, n_seq, seq_len, tok0, ctx=None):
    tq = min(ATT_TQ, seq_len)
    nq = seq_len // tq
    seq0 = tok0 // seq_len
    q0 = tok0 // tq

    def qmap(hsel):
        return lambda b, p, i: (hsel(p), q0 + b * nq + i, 0)

    def kmap(hsel, arr):
        if arr.ndim == 3:
            return lambda b, p, i: (hsel(p), seq0 + b, 0)
        return lambda b, p, i: (seq0 + b, 0)

    def kspec(arr, hsel):
        if arr.ndim == 3:
            return pl.BlockSpec((None, seq_len, 128), kmap(hsel, arr))
        return pl.BlockSpec((seq_len, 128), kmap(hsel, arr))

    in_specs = [pl.BlockSpec((None, tq, 128), qmap(head_a)), pl.BlockSpec((None, tq, 128), qmap(head_b)),
                kspec(ka, k_head_a), kspec(kb, k_head_b), kspec(v, v_idx)]
    args = [q, q, ka, kb, v]
    if ctx is not None:
        kca, kcb, vc = ctx
        past = vc.shape[-2]

        def cspec(arr, hsel):
            if arr.ndim == 4:
                return pl.BlockSpec((None, None, past, 128), lambda b, p, i: (hsel(p), b, 0, 0))
            return pl.BlockSpec((None, past, 128), lambda b, p, i: (b, 0, 0))

        in_specs += [cspec(kca, k_head_a), cspec(kcb, k_head_b), cspec(vc, v_idx)]
        args += [kca, kcb, vc]
    return pl.pallas_call(
        functools.partial(_attn_body, has_ctx=ctx is not None),
        grid=(n_seq, 2, nq),
        in_specs=in_specs,
        out_specs=pl.BlockSpec((tq, 128), lambda b, p, i: (b * nq + i, p)),
        out_shape=jax.ShapeDtypeStruct((n_seq * seq_len, 256), BF16),
        compiler_params=_cparams(("arbitrary", "arbitrary", "arbitrary")),
        name="attention_ctx" if ctx is None else "attention_lat",
    )(*args)


def _out_body(x_ref, mod_ref, a_ref, yf_ref, yb_ref, obc_ref, obl_ref, occ_ref, ocl_ref,
              d_ref, dprev_ref, dnext_ref, wo_ref, gu_ref, rk_ref, gn_ref, cw_ref, cb_ref, n2_ref, wq_ref,
              x1_ref, h2_ref, q_ref, *, n_ctx_tiles):
    i = pl.program_id(0)
    is_ctx = i < n_ctx_tiles
    tm = x_ref.shape[0]
    mod = mod_ref[...]
    gate1 = mod[:, 2 * D_MODEL:3 * D_MODEL]
    shift2 = mod[:, 3 * D_MODEL:4 * D_MODEL]
    scale2 = mod[:, 4 * D_MODEL:5 * D_MODEL]

    r = a_ref[:, 0:256]
    k = a_ref[:, 256:512]
    v = a_ref[:, 512:768]
    gd = a_ref[:, 1024:1152]
    y = yf_ref[...] + yb_ref[...]
    avg = _group_matrix(GROUP_W, HEAD_DIM, 1.0 / HEAD_DIM)
    ones = _group_matrix(GROUP_W, HEAD_DIM, 1.0)
    yn = y * lax.rsqrt(_dot(y * y, avg, precision=HI) + EPS) * gn_ref[...]
    bonus = _dot(r * k * rk_ref[...], ones, precision=HI) * v
    gate = _dot(jax.nn.sigmoid(gd).astype(BF16), gu_ref[...])
    o_a = (yn + bonus) * gate

    o_b = jnp.where(is_ctx, obc_ref[...], obl_ref[...])
    o_c = jnp.where(is_ctx, occ_ref[...], ocl_ref[...])

    u = d_ref[:, 512:768] * d_ref[:, 0:256]
    u_prev = dprev_ref[:, 512:768] * dprev_ref[:, 0:256]
    u_next = dnext_ref[:, 512:768] * dnext_ref[:, 0:256]
    rows = lax.broadcasted_iota(jnp.int32, (tm, GROUP_W), 0)
    up = jnp.where(rows == 0, u_prev, pltpu.roll(u, 1, 0))
    un = jnp.where(rows == tm - 1, u_next, pltpu.roll(u, tm - 1, 0))
    conv = up * cw_ref[0:1, :] + u * cw_ref[1:2, :] + un * cw_ref[2:3, :] + cb_ref[...]
    o_d = d_ref[:, 256:512] * conv

    mix_in = jnp.concatenate([o_a.astype(BF16), o_b, o_c, o_d.astype(BF16)], axis=1)
    x1 = x_ref[...] + gate1 * _dot(mix_in, wo_ref[...])
    x1_ref[...] = x1
    h2 = (_rms_rows(x1, n2_ref[...]) * (1.0 + scale2) + shift2).astype(BF16)
    h2_ref[...] = h2
    q_ref[...] = _dot(h2, wq_ref[...])


def _out_proj(x, mod, a_grp, yf, yb, ob_ctx, ob_lat, oc_ctx, oc_lat, d_grp, d_prev, d_next, lw):
    n = x.shape[0]
    n_tiles = n // TILE
    n_ctx_tiles = N_CTX_TOK // TILE
    tpg = COND_GROUP // TILE
    row = lambda i: (i, 0)
    ctx_row = lambda i: (jnp.minimum(i, n_ctx_tiles - 1), 0)
    lat_row = lambda i: (jnp.maximum(i - n_ctx_tiles, 0), 0)
    halo = lambda i: (i, 0, 0)
    c2 = lambda i: (0, 0)
    nq = PEER_HEADS * 2 * PEER_KEYS
    return pl.pallas_call(
        functools.partial(_out_body, n_ctx_tiles=n_ctx_tiles),
        grid=(n_tiles,),
        in_specs=[pl.BlockSpec((TILE, D_MODEL), row),
                  pl.BlockSpec((None, 1, 6 * D_MODEL), lambda i: (i // tpg, 0, 0)),
                  pl.BlockSpec((TILE, A_W), row),
                  pl.BlockSpec((TILE, GROUP_W), row), pl.BlockSpec((TILE, GROUP_W), row),
                  pl.BlockSpec((TILE, GROUP_W), ctx_row), pl.BlockSpec((TILE, GROUP_W), lat_row),
                  pl.BlockSpec((TILE, GROUP_W), ctx_row), pl.BlockSpec((TILE, GROUP_W), lat_row),
                  pl.BlockSpec((TILE, D_W), row),
                  pl.BlockSpec((None, 1, D_W), halo), pl.BlockSpec((None, 1, D_W), halo),
                  pl.BlockSpec((D_MODEL, D_MODEL), c2), pl.BlockSpec((128, GROUP_W), c2),
                  pl.BlockSpec((1, GROUP_W), c2), pl.BlockSpec((1, GROUP_W), c2),
                  pl.BlockSpec((3, GROUP_W), c2), pl.BlockSpec((1, GROUP_W), c2),
                  pl.BlockSpec((1, D_MODEL), c2), pl.BlockSpec((D_MODEL, nq), c2)],
        out_specs=[pl.BlockSpec((TILE, D_MODEL), row), pl.BlockSpec((TILE, D_MODEL), row),
                   pl.BlockSpec((TILE, nq), row)],
        out_shape=[jax.ShapeDtypeStruct((n, D_MODEL), F32), jax.ShapeDtypeStruct((n, D_MODEL), BF16),
                   jax.ShapeDtypeStruct((n, nq), F32)],
        compiler_params=_cparams(("arbitrary",)),
        name="out_proj",
    )(x, mod, a_grp, yf, yb, ob_ctx, ob_lat, oc_ctx, oc_lat, d_grp, d_prev, d_next,
      lw["w_out"], lw["gu"], lw["rk"], lw["gn"], lw["conv_w"], lw["conv_b"], lw["norm2_g"], lw["wq"])


def _exchange(a, b):
    if a is None:
        return b, None
    if b is None:
        return a, None
    return jnp.maximum(a, b), jnp.minimum(a, b)


def _sort16_desc(xs):
    xs = list(xs)
    k = 2
    while k <= 16:
        j = k // 2
        while j >= 1:
            for i in range(16):
                partner = i ^ j
                if partner > i:
                    hi, lo = _exchange(xs[i], xs[partner])
                    xs[i], xs[partner] = (hi, lo) if (i & k) == 0 else (lo, hi)
            j //= 2
        k *= 2
    return xs


def _bitonic_merge_desc(xs):
    xs = list(xs)
    j = 8
    while j >= 1:
        for i in range(16):
            partner = i ^ j
            if partner > i:
                xs[i], xs[partner] = _exchange(xs[i], xs[partner])
        j //= 2
    return xs


def _top16_of_rows(xs):
    ys = _sort16_desc(xs)
    for shift in (4, 2, 1):
        zs = [None if y is None else pltpu.roll(y, shift, 0) for y in ys]
        ts = [_exchange(ys[i], zs[15 - i])[0] for i in range(16)]
        ys = _bitonic_merge_desc(ts)
    return ys


def _topk_body(q_ref, k1_ref, k2_ref, thr_ref, s2_ref, e1_ref, e2_ref):
    tn = q_ref.shape[0]
    sub = lax.broadcasted_iota(jnp.int32, (8, tn), 0)

    def spread(vals):
        out = vals[7]
        for s in range(6, -1, -1):
            out = jnp.where(sub == s, vals[s], out)
        return out

    for h in range(PEER_HEADS):
        qa = q_ref[:, (2 * h) * PEER_KEYS:(2 * h + 1) * PEER_KEYS].astype(BF16)
        qb = q_ref[:, (2 * h + 1) * PEER_KEYS:(2 * h + 2) * PEER_KEYS].astype(BF16)
        s1 = _dot_nt(k1_ref[h], qa)
        s2 = _dot_nt(k2_ref[h], qb)
        v1 = _top16_of_rows([s1[8 * i:8 * i + 8, :] for i in range(16)])
        v2 = _top16_of_rows([s2[8 * i:8 * i + 8, :] for i in range(16)])
        v2_lo, v2_hi, v1_hi = spread(v2[0:8]), spread(v2[8:16]), spread(v1[8:16])
        cands = ([v1[0] + v2_lo, v1[0] + v2_hi] + [v1[a] + v2_lo for a in range(1, 8)]
                 + [v1_hi + v2[0]] + [None] * 6)
        best = _top16_of_rows(cands)
        tau = best[PEER_TOPK - 1][0:1, :]
        zsum = jnp.exp(best[0] - best[0])
        for kth in range(1, PEER_TOPK):
            zsum = zsum + jnp.exp(best[kth] - best[0])
        thr = jnp.full(s1.shape, jnp.inf, F32)
        for b in range(PEER_TOPK):
            vb = v2[b][0:1, :]
            thr = jnp.where(s1 + vb >= tau, vb, thr)
        thr_ref[h] = thr
        s2_ref[h] = s2
        e1_ref[h] = jnp.exp(s1 - v1[0][0:1, :])
        e2_ref[h] = jnp.exp(s2 - v2[0][0:1, :]) * (0.5 / zsum[0:1, :])


def _peer_topk(q, k1, k2):
    n = q.shape[0]
    tn = TOPK_TN
    big = pl.BlockSpec((PEER_HEADS, PEER_KEYS, tn), lambda i: (0, 0, i))
    big_shape = jax.ShapeDtypeStruct((PEER_HEADS, PEER_KEYS, n), F32)
    c3 = lambda i: (0, 0, 0)
    return pl.pallas_call(
        _topk_body,
        grid=(n // tn,),
        in_specs=[pl.BlockSpec((tn, PEER_HEADS * 2 * PEER_KEYS), lambda i: (i, 0)),
                  pl.BlockSpec((PEER_HEADS, PEER_KEYS, PEER_KEYS), c3),
                  pl.BlockSpec((PEER_HEADS, PEER_KEYS, PEER_KEYS), c3)],
        out_specs=[big, big, big, big],
        out_shape=[big_shape, big_shape, big_shape, big_shape],
        compiler_params=_cparams(("arbitrary",)),
        name="peer_topk",
    )(q, k1, k2)


GELU_C0 = 0.7978845608028654
GELU_C1 = 0.044715


PEER_PIECE = 256


def _dense_body(h2_ref, u_ref, vt_ref, thr_ref, s2_ref, e1_ref, e2_ref, x1_ref, mod_ref,
                o_ref, acc, *piece_scratch):
    j = pl.program_id(1)

    @pl.when(j == 0)
    def _():
        acc[...] = jnp.zeros_like(acc)

    n_pieces = len(piece_scratch) // 2
    hs, gs = piece_scratch[:n_pieces], piece_scratch[n_pieces:]
    tn = hs[0].shape[1]
    h2 = h2_ref[...]

    def pre_activations(p):
        rows = slice(p * PEER_PIECE, (p + 1) * PEER_PIECE)
        hs[p][...] = _dot_nt(u_ref[rows, :], h2)

    pre_activations(0)
    for p in range(n_pieces):
        if p + 1 < n_pieces:
            pre_activations(p + 1)
        halves = PEER_PIECE // PEER_KEYS
        for sub in range(2 * (tn // 128)):
            keys = slice((sub % 2) * 64, (sub % 2) * 64 + 64)
            cols = slice((sub // 2) * 128, (sub // 2) * 128 + 128)
            n_grp = 64 // 8
            w = [[None] * n_grp for _ in range(halves)]
            for h in range(PEER_HEADS):
                s2t = s2_ref[h, keys, cols]
                e2t = e2_ref[h, keys, cols]
                for half in range(halves):
                    ii = p * halves + half
                    thr8 = jnp.broadcast_to(thr_ref[h, ii:ii + 1, cols], (8, 128))
                    e18 = jnp.broadcast_to(e1_ref[h, ii:ii + 1, cols], (8, 128))
                    for g in range(n_grp):
                        grp = slice(8 * g, 8 * g + 8)
                        term = jnp.where(s2t[grp] >= thr8, e18 * e2t[grp], 0.0)
                        w[half][g] = term if w[half][g] is None else w[half][g] + term
            for half in range(halves):
                rows = slice(half * PEER_KEYS + keys.start, half * PEER_KEYS + keys.stop)
                x = hs[p][rows, cols]
                inner = x * (GELU_C0 + (GELU_C0 * GELU_C1) * (x * x))
                wh = jnp.concatenate(w[half], axis=0)
                gs[p][rows, cols] = (wh * x * (1.0 + jnp.tanh(inner))).astype(BF16)
        prows = slice(p * PEER_PIECE, (p + 1) * PEER_PIECE)
        acc[...] += _dot(vt_ref[:, prows], gs[p][...])

    @pl.when(j == pl.num_programs(1) - 1)
    def _():
        gate2 = mod_ref[:, 5 * D_MODEL:6 * D_MODEL]
        o_ref[...] = x1_ref[...] + gate2 * acc[...].T


def _peer_dense(h2, u_b, vt_b, thr, s2, e1, e2, x1, mod):
    n = h2.shape[0]
    tn, te = PEER_TN, PEER_TE
    rows_per_step = te // PEER_KEYS
    tpg = COND_GROUP // tn
    tok = lambda i, j: (i, 0)
    key_rows = pl.BlockSpec((PEER_HEADS, rows_per_step, tn), lambda i, j: (0, j, i))
    key_all = pl.BlockSpec((PEER_HEADS, PEER_KEYS, tn), lambda i, j: (0, 0, i))
    return pl.pallas_call(
        _dense_body,
        grid=(n // tn, N_EXPERTS // te),
        in_specs=[pl.BlockSpec((tn, D_MODEL), tok),
                  pl.BlockSpec((te, D_MODEL), lambda i, j: (j, 0)),
                  pl.BlockSpec((D_MODEL, te), lambda i, j: (0, j)),
                  key_rows, key_all, key_rows, key_all,
                  pl.BlockSpec((tn, D_MODEL), tok),
                  pl.BlockSpec((None, 1, 6 * D_MODEL), lambda i, j: (i // tpg, 0, 0))],
        out_specs=pl.BlockSpec((tn, D_MODEL), tok),
        out_shape=jax.ShapeDtypeStruct((n, D_MODEL), F32),
        scratch_shapes=([pltpu.VMEM((D_MODEL, tn), F32)]
                        + [pltpu.VMEM((PEER_PIECE, tn), F32)] * (te // PEER_PIECE)
                        + [pltpu.VMEM((PEER_PIECE, tn), BF16)] * (te // PEER_PIECE)),
        compiler_params=_cparams(("arbitrary", "arbitrary")),
        name="peer_dense",
    )(h2, u_b, vt_b, thr, s2, e1, e2, x1, mod)


def _in_proj_columns():
    src = np.full((IN_PAD,), -1, np.int64)

    def put(dst, start, width):
        src[dst:dst + width] = np.arange(start, start + width)

    put(0, 0, 768)
    put(768, 768, 128)
    put(896, 896, 64)
    put(1024, 960, 128)
    b0 = A_W
    put(b0, 1088, 192)
    put(b0 + 256, 1280, 128)
    put(b0 + 384 + 64, 1408, 32)
    put(b0 + 512 + 64, 1408 + 16, 16)
    put(b0 + 512 + 80, 1408, 16)
    c0 = A_W + B_W
    for slot, head in enumerate((0, 2, 1, 3)):
        put(c0 + 64 * slot, 1440 + 64 * head, 64)
    put(c0 + 256, 1696, 256)
    put(A_W + B_W + C_W, 1952, 768)
    return src


def _swap_tail(w):
    return jnp.concatenate([w[..., :64], w[..., 80:96], w[..., 64:80]], axis=-1)


def _pad_last(w, width):
    return jnp.pad(w, [(0, 0)] * (w.ndim - 1) + [(0, width - w.shape[-1])])


def _layer_weights(p):
    depth = p["w_in"].shape[0]
    src = _in_proj_columns()
    w_in = jnp.take(p["w_in"], jnp.asarray(np.maximum(src, 0)), axis=2)
    w_in = jnp.where(jnp.asarray(src >= 0)[None, None, :], w_in, 0.0).astype(BF16)

    wuq = p["mla_wuq"].reshape(depth, MLA_Q_RANK, 4, MLA_QK_DIM).transpose(0, 2, 1, 3)
    wq_plain = jnp.pad(wuq, ((0, 0), (0, 0), (0, 256 - MLA_Q_RANK), (0, 128 - MLA_QK_DIM)))
    wq_swap = jnp.pad(_swap_tail(wuq), ((0, 0), (0, 0), (0, 256 - MLA_Q_RANK), (0, 128 - MLA_QK_DIM)))
    wukv = p["mla_wukv"].reshape(depth, 128, 4, 128)
    wk = _pad_last(wukv[..., :64].transpose(0, 2, 1, 3), 128)
    wv_heads = wukv[..., 64:].transpose(0, 2, 1, 3)
    wv = jnp.concatenate([wv_heads[:, 0::2], wv_heads[:, 1::2]], axis=-1)

    def gain2(g):
        return jnp.stack([_pad_last(g, 128), _pad_last(_swap_tail(g), 128)], axis=1)

    w_out = p["w_out"]
    oc = w_out[:, 512:768].reshape(depth, 4, 64, D_MODEL)[:, jnp.asarray([0, 2, 1, 3])].reshape(depth, 256, D_MODEL)
    w_out = jnp.concatenate([w_out[:, :512], oc, w_out[:, 768:]], axis=1).astype(BF16)

    wu_pad = jnp.zeros((depth, 2, 128, GROUP_W), F32)
    wu_pad = wu_pad.at[:, 0, 0:64].set(p["rw_wu"][:, 0]).at[:, 1, 64:128].set(p["rw_wu"][:, 1])
    au_pad = jnp.zeros((depth, 2, 128, GROUP_W), F32)
    au_pad = au_pad.at[:, 0, 0:32].set(p["rw_au"][:, 0]).at[:, 1, 32:64].set(p["rw_au"][:, 1])

    return {
        "w_in": w_in,
        "norm1_g": p["norm1_g"][:, None, :],
        "norm2_g": p["norm2_g"][:, None, :],
        "w_out": w_out,
        "w0": p["rw_w0"], "a0": p["rw_a0"],
        "wu": wu_pad.astype(BF16), "au": au_pad.astype(BF16),
        "kkp": p["rw_kk"][:, None, :], "ka": p["rw_ka"][:, None, :],
        "gu": p["rw_gu"].astype(BF16),
        "rk": p["rw_rk"].reshape(depth, 1, GROUP_W),
        "gn": p["rw_gn"][:, None, :],
        "qa_g": _pad_last(p["mla_qa_g"], 256)[:, None, :],
        "wqq": jnp.concatenate([wq_plain, wq_swap], axis=1).astype(BF16),
        "qn_g2": gain2(p["mla_qn_g"]),
        "kva_g": p["mla_kva_g"][:, None, :],
        "wk": wk.astype(BF16), "wv": wv.astype(BF16),
        "kn_g2": gain2(p["mla_kn_g"]),
        "gq_g": jnp.tile(p["gqa_qn_g"], (1, 2))[:, None, :],
        "gk_g": jnp.tile(p["gqa_kn_g"], (1, 2))[:, None, :],
        "conv_w": p["conv_w"], "conv_b": p["conv_b"][:, None, :],
        "wq": p["peer_wq"].astype(BF16),
        "k1": p["peer_k1"].astype(BF16), "k2": p["peer_k2"].astype(BF16),
        "u": p["peer_u"].astype(BF16),
        "vt": jnp.swapaxes(p["peer_v"], 1, 2).astype(BF16),
    }


def _rope_tables():
    t = jnp.arange(LAT_LEN, dtype=F32)
    grid_row = jnp.floor(t / GRID_W)
    grid_col = t - grid_row * GRID_W

    def angles(rot_dim):
        n_freq = rot_dim // 4
        freqs = ROPE_THETA ** (-jnp.arange(n_freq, dtype=F32) / n_freq)
        ang = jnp.concatenate([grid_row[:, None] * freqs, grid_col[:, None] * freqs], axis=-1)
        return jnp.cos(ang), jnp.sin(ang)

    cm, sm = angles(32)
    ones64 = jnp.ones((LAT_LEN, 64), F32)
    zeros64 = jnp.zeros((LAT_LEN, 64), F32)
    cosm = jnp.concatenate([ones64, cm, cm, ones64[:, :32]], axis=-1)
    sinm = jnp.concatenate([zeros64, -sm, sm, zeros64[:, :32]], axis=-1)
    cg, sg = angles(64)
    cosg = jnp.tile(jnp.concatenate([cg, cg], axis=-1), (1, 2))
    sing = jnp.tile(jnp.concatenate([-sg, sg], axis=-1), (1, 2))
    ident = jnp.ones((TILE, 128), F32)
    zero = jnp.zeros((TILE, 128), F32)
    return {"cosm": jnp.concatenate([ident, cosm]), "sinm": jnp.concatenate([zero, sinm]),
            "cosg": jnp.concatenate([ident, cosg]), "sing": jnp.concatenate([zero, sing])}


def _states_to_pairs(s):
    lead = s.shape[:-3]
    s = s.reshape(lead + (2, 2, 64, 64))
    z = jnp.zeros(lead + (2, 64, 64), s.dtype)
    top = jnp.concatenate([s[..., 0, :, :], z], axis=-1)
    bot = jnp.concatenate([z, s[..., 1, :, :]], axis=-1)
    return jnp.concatenate([top, bot], axis=-2)


def _pairs_to_states(sp):
    lead = sp.shape[:-3]
    a = sp[..., 0:64, 0:64]
    b = sp[..., 64:128, 64:128]
    return jnp.stack([a, b], axis=-3).reshape(lead + (4, 64, 64))


def _conv_halos(d_grp):
    n_tiles = d_grp.shape[0] // TILE
    tiles = d_grp.reshape(n_tiles, TILE, D_W)
    first_rows = tiles[:, 0, :]
    last_rows = tiles[:, TILE - 1, :]
    zero = jnp.zeros((1, D_W), d_grp.dtype)
    prev = jnp.concatenate([zero, last_rows[:-1]], axis=0)
    nxt = jnp.concatenate([first_rows[1:], zero], axis=0)
    idx = np.arange(n_tiles)
    n_ctx_tiles = N_CTX_TOK // TILE
    per_seq = LAT_LEN // TILE
    lat_pos = (idx - n_ctx_tiles) % per_seq
    seq_start = np.where(idx < n_ctx_tiles, True, lat_pos == 0)
    seq_end = np.where(idx < n_ctx_tiles, True, lat_pos == per_seq - 1)
    prev = jnp.where(jnp.asarray(seq_start)[:, None], 0.0, prev)
    nxt = jnp.where(jnp.asarray(seq_end)[:, None], 0.0, nxt)
    return prev[:, None, :], nxt[:, None, :]


def kernel(x_prompt, x_sample, state_rwkv, cache_mla_ckv, cache_mla_krope, cache_gqa_k, cache_gqa_v, c, c_ctx, norm1_g, norm2_g, w_mod, b_mod, w_in, w_out, rw_w0, rw_wu, rw_a0, rw_au, rw_gu, rw_kk, rw_ka, rw_rk, rw_gn, mla_qa_g, mla_wuq, mla_kva_g, mla_wukv, mla_qn_g, mla_kn_g, gqa_qn_g, gqa_kn_g, conv_w, conv_b, peer_wq, peer_k1, peer_k2, peer_u, peer_v):
    depth = w_in.shape[0]
    params = dict(norm1_g=norm1_g, norm2_g=norm2_g, w_in=w_in, w_out=w_out, rw_w0=rw_w0, rw_wu=rw_wu,
                  rw_a0=rw_a0, rw_au=rw_au, rw_gu=rw_gu, rw_kk=rw_kk, rw_ka=rw_ka, rw_rk=rw_rk, rw_gn=rw_gn,
                  mla_qa_g=mla_qa_g, mla_wuq=mla_wuq, mla_kva_g=mla_kva_g, mla_wukv=mla_wukv,
                  mla_qn_g=mla_qn_g, mla_kn_g=mla_kn_g, gqa_qn_g=gqa_qn_g, gqa_kn_g=gqa_kn_g,
                  conv_w=conv_w, conv_b=conv_b, peer_wq=peer_wq, peer_k1=peer_k1, peer_k2=peer_k2,
                  peer_u=peer_u, peer_v=peer_v)
    lw_all = _layer_weights(params)
    tabs = _rope_tables()

    cvecs = jnp.concatenate([c_ctx[None, :], c, jnp.zeros((8 - 1 - N_LAT_SEQ, D_MODEL), F32)], axis=0)
    mods = _modulation(cvecs, w_mod, b_mod).reshape(depth, 8, 1, 6 * D_MODEL)

    x = jnp.concatenate([x_prompt.reshape(N_CTX_TOK, D_MODEL), x_sample.reshape(N_LAT_TOK, D_MODEL)], axis=0)

    s0_lat = _states_to_pairs(state_rwkv)
    s0_all = jnp.concatenate([jnp.zeros((N_CTX_SEQ,) + s0_lat.shape[1:], F32), s0_lat], axis=0)

    krope_placed = jnp.pad(cache_mla_krope, ((0, 0), (0, 0), (0, 0), (64, 32)))
    kctx_m, vctx_m = _mla_ctx_kv(cache_mla_ckv, krope_placed, lw_all["wk"], lw_all["wv"], lw_all["kn_g2"])
    past = cache_gqa_k.shape[2]
    kctx_g = cache_gqa_k.reshape(N_LAT_SEQ, depth, past, 128)
    vctx_g = cache_gqa_v.reshape(N_LAT_SEQ, depth, past, 128)

    ident = lambda p: p
    st_a, st_ckv, st_kr, st_k, st_v = [], [], [], [], []
    for l in range(depth):
        lw = {name: w[l] for name, w in lw_all.items()}
        mod = mods[l]
        a_grp, b_grp, c_grp, d_grp = _in_proj(x, mod, lw["norm1_g"], lw["w_in"])

        yf, yb, s_fin = _rwkv_scan(a_grp, s0_all[:, l], lw["w0"], lw["wu"], lw["a0"], lw["au"], lw["kkp"], lw["ka"])

        qm, km, vm, ckvn, qg, kg, vg, kgn = _attn_prep(b_grp, c_grp, tabs, lw)
        ob_ctx = _attention(qm, km, km, vm, lambda p: 2 * p, lambda p: 2 * p + 1, lambda p: 2 * p,
                            lambda p: 2 * p + 1, ident, N_CTX_SEQ, CTX_LEN, 0)
        ob_lat = _attention(qm, km, km, vm, lambda p: 2 * p, lambda p: 2 * p + 1, lambda p: 2 * p,
                            lambda p: 2 * p + 1, ident, N_LAT_SEQ, LAT_LEN, N_CTX_TOK,
                            ctx=(kctx_m[l], kctx_m[l], vctx_m[l]))
        oc_ctx = _attention(qg, kg, kg, vg, ident, lambda p: p + 2, ident, ident, ident,
                            N_CTX_SEQ, CTX_LEN, 0)
        oc_lat = _attention(qg, kg, kg, vg, ident, lambda p: p + 2, ident, ident, ident,
                            N_LAT_SEQ, LAT_LEN, N_CTX_TOK,
                            ctx=(kctx_g[:, l], kctx_g[:, l], vctx_g[:, l]))

        d_prev, d_next = _conv_halos(d_grp)
        x1, h2, q = _out_proj(x, mod, a_grp, yf, yb, ob_ctx, ob_lat, oc_ctx, oc_lat, d_grp, d_prev, d_next, lw)

        thr, s2, e1, e2 = _peer_topk(q, lw["k1"], lw["k2"])
        x = _peer_dense(h2, lw["u"], lw["vt"], thr, s2, e1, e2, x1, mod)

        st_a.append(_pairs_to_states(s_fin[:N_CTX_SEQ]))
        st_ckv.append(ckvn[:N_CTX_TOK].reshape(N_CTX_SEQ, CTX_LEN, 128))
        st_kr.append(b_grp[:N_CTX_TOK, 448:480].reshape(N_CTX_SEQ, CTX_LEN, 32))
        st_k.append(kgn[:N_CTX_TOK].reshape(N_CTX_SEQ, CTX_LEN, 2, HEAD_DIM))
        st_v.append(c_grp[:N_CTX_TOK, 384:512].reshape(N_CTX_SEQ, CTX_LEN, 2, HEAD_DIM))

    y_prompt = x[:N_CTX_TOK].reshape(N_CTX_SEQ, CTX_LEN, D_MODEL)
    y_sample = x[N_CTX_TOK:].reshape(N_LAT_SEQ, LAT_LEN, D_MODEL)
    return (y_prompt, y_sample, jnp.stack(st_a, axis=1), jnp.stack(st_ckv, axis=1), jnp.stack(st_kr, axis=1),
            jnp.stack(st_k, axis=1), jnp.stack(st_v, axis=1))
```

```python
import functools

import numpy as np
import jax
import jax.numpy as jnp
from jax import lax
from jax.experimental import pallas as pl
from jax.experimental.pallas import tpu as pltpu

F32 = jnp.float32
BF16 = jnp.bfloat16
HI = lax.Precision.HIGHEST

D_MODEL = 1024
N_CTX_SEQ = 16
CTX_LEN = 256
N_LAT_SEQ = 2
LAT_LEN = 4096
GRID_W = 64
ROPE_THETA = 10000.0
EPS = 1e-6
GROUP_W = 256
HEAD_DIM = 64
RW_DECAY_SCALE = 0.6065306597
MLA_QK_DIM = 96
MLA_Q_RANK = 192
PEER_HEADS = 8
PEER_KEYS = 128
PEER_TOPK = 16
N_EXPERTS = PEER_KEYS * PEER_KEYS

N_CTX_TOK = N_CTX_SEQ * CTX_LEN
N_LAT_TOK = N_LAT_SEQ * LAT_LEN
N_TOK = N_CTX_TOK + N_LAT_TOK
COND_GROUP = 4096

A_W, B_W, C_W, D_W = 1152, 640, 512, 768
IN_PAD = A_W + B_W + C_W + D_W

SCAN_CHUNK = 128
TILE = 256
IN_TILE = 512
PEER_TN = 512
PEER_TE = 2048
TOPK_TN = 128
ATT_TQ = 256
ATT_CHUNK = 512

VMEM_LIMIT = 56 * 1024 * 1024


def _cparams(sem):
    return pltpu.CompilerParams(dimension_semantics=sem, vmem_limit_bytes=VMEM_LIMIT)


def _dot(a, b, precision=None):
    return jnp.dot(a, b, preferred_element_type=F32, precision=precision)


def _dot_nt(a, b, precision=None):
    return lax.dot_general(a, b, (((1,), (1,)), ((), ())), preferred_element_type=F32,
                           precision=precision)


def _dot_tn(a, b, precision=None):
    return lax.dot_general(a, b, (((0,), (0,)), ((), ())), preferred_element_type=F32,
                           precision=precision)


def _rms_rows(x, g):
    return x * lax.rsqrt(jnp.mean(x * x, axis=-1, keepdims=True) + EPS) * g


def _group_matrix(n, group, value):
    r = lax.broadcasted_iota(jnp.int32, (n, n), 0) // group
    c = lax.broadcasted_iota(jnp.int32, (n, n), 1) // group
    return jnp.where(r == c, value, 0.0).astype(F32)


def _mod_body(c_ref, w_ref, b_ref, o_ref):
    c = c_ref[...]
    s = c * jax.nn.sigmoid(c)
    o_ref[...] = _dot(s.astype(BF16), w_ref[...].astype(BF16)) + b_ref[...]


def _modulation(cvecs, w_mod, b_mod):
    depth = w_mod.shape[0]
    tn = 1536
    return pl.pallas_call(
        _mod_body,
        grid=(depth, 6 * D_MODEL // tn),
        in_specs=[pl.BlockSpec((8, D_MODEL), lambda l, j: (0, 0)),
                  pl.BlockSpec((None, D_MODEL, tn), lambda l, j: (l, 0, j)),
                  pl.BlockSpec((None, 1, tn), lambda l, j: (l, 0, j))],
        out_specs=pl.BlockSpec((None, 8, tn), lambda l, j: (l, 0, j)),
        out_shape=jax.ShapeDtypeStruct((depth, 8, 6 * D_MODEL), F32),
        compiler_params=_cparams(("arbitrary", "arbitrary")),
        name="adaln_mod",
    )(cvecs, w_mod, b_mod.reshape(depth, 1, 6 * D_MODEL))


def _in_body(x_ref, mod_ref, g_ref, w_ref, a_ref, b_ref, c_ref, d_ref):
    mod = mod_ref[...]
    shift = mod[:, 0:D_MODEL]
    scale = mod[:, D_MODEL:2 * D_MODEL]
    h = _rms_rows(x_ref[...], g_ref[...]) * (1.0 + scale) + shift
    y = _dot(h.astype(BF16), w_ref[...])
    a_ref[...] = y[:, 0:A_W]
    b_ref[...] = y[:, A_W:A_W + B_W]
    c_ref[...] = y[:, A_W + B_W:A_W + B_W + C_W]
    d_ref[...] = y[:, A_W + B_W + C_W:IN_PAD]


def _in_proj(x, mod, norm_g, w_pad):
    n = x.shape[0]
    tpg = COND_GROUP // IN_TILE
    row = lambda i: (i, 0)
    return pl.pallas_call(
        _in_body,
        grid=(n // IN_TILE,),
        in_specs=[pl.BlockSpec((IN_TILE, D_MODEL), row),
                  pl.BlockSpec((None, 1, 6 * D_MODEL), lambda i: (i // tpg, 0, 0)),
                  pl.BlockSpec((1, D_MODEL), lambda i: (0, 0)),
                  pl.BlockSpec((D_MODEL, IN_PAD), lambda i: (0, 0))],
        out_specs=[pl.BlockSpec((IN_TILE, A_W), row), pl.BlockSpec((IN_TILE, B_W), row),
                   pl.BlockSpec((IN_TILE, C_W), row), pl.BlockSpec((IN_TILE, D_W), row)],
        out_shape=[jax.ShapeDtypeStruct((n, A_W), F32), jax.ShapeDtypeStruct((n, B_W), F32),
                   jax.ShapeDtypeStruct((n, C_W), F32), jax.ShapeDtypeStruct((n, D_W), F32)],
        compiler_params=_cparams(("arbitrary",)),
        name="in_proj",
    )(x, mod, norm_g, w_pad)


def _chunk_cumsum(x, reverse):
    c = x.shape[0]
    rows = lax.broadcasted_iota(jnp.int32, x.shape, 0)
    sh = 1
    while sh < c:
        if reverse:
            x = x + jnp.where(rows < c - sh, pltpu.roll(x, c - sh, 0), 0.0)
        else:
            x = x + jnp.where(rows >= sh, pltpu.roll(x, sh, 0), 0.0)
        sh *= 2
    return x


def _scan_body(fb_ref, bb_ref, first_ref, last_ref, seq_ref,
               af_ref, ab_ref, s0_ref, w0_ref, wu_ref, a0_ref, au_ref, kkp_ref, ka_ref,
               yf_ref, yb_ref, sfin_ref, s_scr):
    step = pl.program_id(0)

    @pl.when(first_ref[step] == 1)
    def _():
        s_scr[...] = s0_ref[...]

    c = SCAN_CHUNK
    row = lax.broadcasted_iota(jnp.int32, (c, c), 0)
    col = lax.broadcasted_iota(jnp.int32, (c, c), 1)
    lane_c = lax.broadcasted_iota(jnp.int32, (c, 128), 1) < HEAD_DIM
    lane_2c = lax.broadcasted_iota(jnp.int32, (2 * c, 128), 1) < HEAD_DIM
    same_head = _group_matrix(128, HEAD_DIM, 1.0)
    kkp = kkp_ref[...]
    ka = ka_ref[...]

    def per_head(lo, hi):
        return jnp.where(lane_c, lo, hi)

    groups = []
    for d in range(2):
        x_ref = af_ref if d == 0 else ab_ref
        if d == 0:
            strict, incl = col < row, col <= row
        else:
            strict, incl = col > row, col >= row
        r = x_ref[:, 0:256]
        k = x_ref[:, 256:512]
        v = x_ref[:, 512:768]
        wd = x_ref[:, 768:896]
        ad = x_ref[:, 896:1024]
        wlog = -RW_DECAY_SCALE * jax.nn.sigmoid(
            w0_ref[d:d + 1, :] + _dot(jnp.tanh(wd).astype(BF16), wu_ref[d]))
        a = jax.nn.sigmoid(a0_ref[d:d + 1, :] + _dot(ad.astype(BF16), au_ref[d]))
        kd = k * (1.0 + (a - 1.0) * ka)
        kkf = k * kkp
        cum = _chunk_cumsum(wlog, reverse=(d == 1))
        tot = cum[c - 1:c, :] if d == 0 else cum[0:1, :]
        p_inc = jnp.exp(cum)
        p_exc = jnp.exp(cum - wlog)
        p_inv = jnp.exp(-cum)
        p_tot = jnp.exp(tot)
        for p in range(2):
            sl = slice(128 * p, 128 * p + 128)
            groups.append(dict(d=d, p=p, sl=sl, strict=strict, incl=incl, kk_raw=kkf[:, sl],
                               a=a[:, sl], kd=kd[:, sl], r=r[:, sl], vb=v[:, sl].astype(BF16),
                               p_inc=p_inc[:, sl], p_exc=p_exc[:, sl], p_inv=p_inv[:, sl], pt=p_tot[:, sl]))

    for g in groups:
        g["ss"] = _dot(g["kk_raw"] * g["kk_raw"], same_head, precision=HI)
    for g in groups:
        kk = g["kk_raw"] * lax.rsqrt(g["ss"] + EPS)
        g["a_p"] = (-kk * g["p_exc"]).astype(BF16)
        g["r_p"] = (g["r"] * g["p_inc"]).astype(BF16)
        g["b_i"] = kk * g["a"] * g["p_inv"]
        g["k_i"] = g["kd"] * g["p_inv"]
        g["lhs"] = jnp.concatenate([g["a_p"], g["r_p"]], axis=0)
        g["rhs"] = jnp.concatenate([g["b_i"], g["k_i"]], axis=0).astype(BF16)
        g["sb"] = s_scr[g["d"], g["p"]].astype(BF16)
    zero_b = jnp.zeros((), BF16)
    for g in groups:
        g["gram"] = (_dot_nt(jnp.where(lane_2c, g["lhs"], zero_b), g["rhs"]),
                     _dot_nt(jnp.where(lane_2c, zero_b, g["lhs"]), g["rhs"]))
        g["a_s"] = _dot_nt(g["a_p"], g["sb"])
        g["r_s"] = _dot_nt(g["r_p"], g["sb"])
    for g in groups:
        st, inc = g["strict"], g["incl"]
        g["pow"] = [jnp.where(st, gm[0:c, 0:c], 0.0).astype(BF16) for gm in g["gram"]]
        g["dm"] = [jnp.where(st, gm[0:c, c:2 * c], 0.0).astype(BF16) for gm in g["gram"]]
        g["et"] = [jnp.where(inc, gm[c:2 * c, 0:c], 0.0).astype(BF16) for gm in g["gram"]]
        g["ft"] = [jnp.where(inc, gm[c:2 * c, c:2 * c], 0.0).astype(BF16) for gm in g["gram"]]
    for g in groups:
        g["z"] = g["a_s"] + per_head(_dot(g["dm"][0], g["vb"]), _dot(g["dm"][1], g["vb"]))
    span = 1
    while span < c:
        for g in groups:
            zb = g["z"].astype(BF16)
            g["z"] = g["z"] + per_head(_dot(g["pow"][0], zb), _dot(g["pow"][1], zb))
        span *= 2
        if span < c:
            for g in groups:
                g["pow"] = [_dot(m, m).astype(BF16) for m in g["pow"]]
    for g in groups:
        zb = g["z"].astype(BF16)
        g["zb"] = zb
        y = g["r_s"] + per_head(_dot(g["et"][0], zb) + _dot(g["ft"][0], g["vb"]),
                                _dot(g["et"][1], zb) + _dot(g["ft"][1], g["vb"]))
        y_ref = yf_ref if g["d"] == 0 else yb_ref
        y_ref[:, g["sl"]] = y
    for g in groups:
        pt = g["pt"]
        upd = (_dot_tn(g["zb"], (g["b_i"] * pt).astype(BF16))
               + _dot_tn(g["vb"], (g["k_i"] * pt).astype(BF16)))
        s_scr[g["d"], g["p"]] = s_scr[g["d"], g["p"]] * pt + same_head * upd

    @pl.when(last_ref[step] == 1)
    def _():
        sfin_ref[...] = s_scr[...]


def _scan_tables():
    c = SCAN_CHUNK
    fb, bb, first, last, seq = [], [], [], [], []
    base = 0
    sid = 0
    for nseq, length in ((N_CTX_SEQ, CTX_LEN), (N_LAT_SEQ, LAT_LEN)):
        nc = length // c
        for _ in range(nseq):
            for j in range(nc):
                fb.append(base + j)
                bb.append(base + nc - 1 - j)
                first.append(int(j == 0))
                last.append(int(j == nc - 1))
                seq.append(sid)
            base += nc
            sid += 1
    return tuple(np.asarray(t, np.int32) for t in (fb, bb, first, last, seq))


def _rwkv_scan(a_grp, s0_pairs, w0, wu_pad, a0, au_pad, kkp, ka):
    tables = _scan_tables()
    n_steps = tables[0].shape[0]
    n_seq = s0_pairs.shape[0]
    n = a_grp.shape[0]
    c = SCAN_CHUNK
    const2 = lambda s, fb, bb, fi, la, sq: (0, 0)
    const3 = lambda s, fb, bb, fi, la, sq: (0, 0, 0)
    grid_spec = pltpu.PrefetchScalarGridSpec(
        num_scalar_prefetch=5,
        grid=(n_steps,),
        in_specs=[pl.BlockSpec((c, A_W), lambda s, fb, bb, fi, la, sq: (fb[s], 0)),
                  pl.BlockSpec((c, A_W), lambda s, fb, bb, fi, la, sq: (bb[s], 0)),
                  pl.BlockSpec((None, 2, 2, 128, 128), lambda s, fb, bb, fi, la, sq: (sq[s], 0, 0, 0, 0)),
                  pl.BlockSpec((2, GROUP_W), const2),
                  pl.BlockSpec((2, 128, GROUP_W), const3),
                  pl.BlockSpec((2, GROUP_W), const2),
                  pl.BlockSpec((2, 128, GROUP_W), const3),
                  pl.BlockSpec((1, GROUP_W), const2),
                  pl.BlockSpec((1, GROUP_W), const2)],
        out_specs=[pl.BlockSpec((c, GROUP_W), lambda s, fb, bb, fi, la, sq: (fb[s], 0)),
                   pl.BlockSpec((c, GROUP_W), lambda s, fb, bb, fi, la, sq: (bb[s], 0)),
                   pl.BlockSpec((None, 2, 2, 128, 128), lambda s, fb, bb, fi, la, sq: (sq[s], 0, 0, 0, 0))],
        scratch_shapes=[pltpu.VMEM((2, 2, 128, 128), F32)],
    )
    return pl.pallas_call(
        _scan_body,
        grid_spec=grid_spec,
        out_shape=[jax.ShapeDtypeStruct((n, GROUP_W), F32), jax.ShapeDtypeStruct((n, GROUP_W), F32),
                   jax.ShapeDtypeStruct((n_seq, 2, 2, 128, 128), F32)],
        compiler_params=_cparams(("arbitrary",)),
        name="rwkv_scan",
    )(*[jnp.asarray(t) for t in tables], a_grp, a_grp, s0_pairs, w0, wu_pad, a0, au_pad, kkp, ka)


MLA_SCALE = MLA_QK_DIM ** -0.5
GQA_SCALE = HEAD_DIM ** -0.5


def _mla_keys(ckv_b, rope_slot, rope_slot_sw, wk_ref, kng_ref, cosm, sinm, k_out_ref):
    g = kng_ref[0:1, :]
    g_sw = kng_ref[1:2, :]
    for h in range(4):
        nope = _dot(ckv_b, wk_ref[h])
        kr = nope + rope_slot
        rs = lax.rsqrt(jnp.sum(kr * kr, axis=-1, keepdims=True) * (1.0 / MLA_QK_DIM) + EPS)
        if cosm is None:
            k_out_ref[h] = (kr * rs * g).astype(BF16)
        else:
            ks = nope + rope_slot_sw
            k_out_ref[h] = ((kr * rs * g) * cosm + (ks * rs * g_sw) * sinm).astype(BF16)


def _prep_body(b_ref, c_ref, cosm_ref, sinm_ref, cosg_ref, sing_ref,
               qag_ref, wqq_ref, qng_ref, kvag_ref, wk_ref, wv_ref, kng_ref, gqg_ref, gkg_ref,
               qm_ref, km_ref, vm_ref, ckvn_ref, qg_ref, kg_ref, vg_ref, kgn_ref):
    cosm = cosm_ref[...]
    sinm = sinm_ref[...]
    qc = b_ref[:, 0:256]
    qn = qc * lax.rsqrt(jnp.sum(qc * qc, axis=-1, keepdims=True) * (1.0 / MLA_Q_RANK) + EPS) * qag_ref[...]
    qnb = qn.astype(BF16)
    g = qng_ref[0:1, :]
    g_sw = qng_ref[1:2, :]
    for h in range(4):
        qr = _dot(qnb, wqq_ref[h])
        qs = _dot(qnb, wqq_ref[4 + h])
        rs = lax.rsqrt(jnp.sum(qr * qr, axis=-1, keepdims=True) * (1.0 / MLA_QK_DIM) + EPS)
        qm_ref[h] = (((qr * rs * g) * cosm + (qs * rs * g_sw) * sinm) * MLA_SCALE).astype(BF16)
    ckv = _rms_rows(b_ref[:, 256:384], kvag_ref[...])
    ckvn_ref[...] = ckv
    cb = ckv.astype(BF16)
    _mla_keys(cb, b_ref[:, 384:512], b_ref[:, 512:640], wk_ref, kng_ref, cosm, sinm, km_ref)
    for p in range(2):
        vm_ref[p] = _dot_nt(wv_ref[p], cb).astype(BF16)

    tm = c_ref.shape[0]
    avg = _group_matrix(128, HEAD_DIM, 1.0 / HEAD_DIM)
    lane = lax.broadcasted_iota(jnp.int32, (tm, 128), 1)
    first_half = (lane % HEAD_DIM) < (HEAD_DIM // 2)
    low = lane < HEAD_DIM
    cosg = cosg_ref[...]
    sing = sing_ref[...]

    def rotate(xn):
        swapped = jnp.where(first_half, pltpu.roll(xn, 128 - HEAD_DIM // 2, 1), pltpu.roll(xn, HEAD_DIM // 2, 1))
        return xn * cosg + swapped * sing

    for blk in range(2):
        x = c_ref[:, 128 * blk:128 * blk + 128]
        xn = x * lax.rsqrt(_dot(x * x, avg, precision=HI) + EPS) * gqg_ref[...]
        xr = rotate(xn) * GQA_SCALE
        qg_ref[blk] = jnp.where(low, xr, 0.0).astype(BF16)
        qg_ref[blk + 2] = jnp.where(low, 0.0, xr).astype(BF16)
    xk = c_ref[:, 256:384]
    kn = xk * lax.rsqrt(_dot(xk * xk, avg, precision=HI) + EPS) * gkg_ref[...]
    kgn_ref[...] = kn
    kg_ref[...] = rotate(kn).astype(BF16)
    vg_ref[...] = c_ref[:, 384:512].T.astype(BF16)


def _attn_prep(b_grp, c_grp, tabs, lw):
    n = b_grp.shape[0]
    n_ctx_tiles = N_CTX_TOK // TILE
    lat_tiles = LAT_LEN // TILE
    row = lambda i: (i, 0)
    hrow = lambda i: (0, i, 0)
    tab = lambda i: (jnp.where(i < n_ctx_tiles, 0, 1 + (i - n_ctx_tiles) % lat_tiles), 0)
    c2 = lambda i: (0, 0)
    c3 = lambda i: (0, 0, 0)
    return pl.pallas_call(
        _prep_body,
        grid=(n // TILE,),
        in_specs=[pl.BlockSpec((TILE, B_W), row), pl.BlockSpec((TILE, C_W), row),
                  pl.BlockSpec((TILE, 128), tab), pl.BlockSpec((TILE, 128), tab),
                  pl.BlockSpec((TILE, 128), tab), pl.BlockSpec((TILE, 128), tab),
                  pl.BlockSpec((1, 256), c2), pl.BlockSpec((8, 256, 128), c3), pl.BlockSpec((2, 128), c2),
                  pl.BlockSpec((1, 128), c2), pl.BlockSpec((4, 128, 128), c3), pl.BlockSpec((2, 128, 128), c3),
                  pl.BlockSpec((2, 128), c2), pl.BlockSpec((1, 128), c2), pl.BlockSpec((1, 128), c2)],
        out_specs=[pl.BlockSpec((4, TILE, 128), hrow), pl.BlockSpec((4, TILE, 128), hrow),
                   pl.BlockSpec((2, 128, TILE), lambda i: (0, 0, i)), pl.BlockSpec((TILE, 128), row),
                   pl.BlockSpec((4, TILE, 128), hrow), pl.BlockSpec((TILE, 128), row),
                   pl.BlockSpec((128, TILE), lambda i: (0, i)), pl.BlockSpec((TILE, 128), row)],
        out_shape=[jax.ShapeDtypeStruct((4, n, 128), BF16), jax.ShapeDtypeStruct((4, n, 128), BF16),
                   jax.ShapeDtypeStruct((2, 128, n), BF16), jax.ShapeDtypeStruct((n, 128), F32),
                   jax.ShapeDtypeStruct((4, n, 128), BF16), jax.ShapeDtypeStruct((n, 128), BF16),
                   jax.ShapeDtypeStruct((128, n), BF16), jax.ShapeDtypeStruct((n, 128), F32)],
        compiler_params=_cparams(("arbitrary",)),
        name="attn_prep",
    )(b_grp, c_grp, tabs["cosm"], tabs["sinm"], tabs["cosg"], tabs["sing"],
      lw["qa_g"], lw["wqq"], lw["qn_g2"], lw["kva_g"], lw["wk"], lw["wv"], lw["kn_g2"], lw["gq_g"], lw["gk_g"])


def _ctxkv_body(ckv_ref, krp_ref, wk_ref, wv_ref, kng_ref, k_ref, v_ref):
    cb = ckv_ref[...].astype(BF16)
    _mla_keys(cb, krp_ref[...], None, wk_ref, kng_ref, None, None, k_ref)
    for p in range(2):
        v_ref[p] = _dot_nt(wv_ref[p], cb).astype(BF16)


def _mla_ctx_kv(cache_ckv, cache_krope_placed, wk, wv, kn_g2):
    nb, depth, plen, _ = cache_ckv.shape
    return pl.pallas_call(
        _ctxkv_body,
        grid=(depth, nb),
        in_specs=[pl.BlockSpec((None, None, plen, 128), lambda l, b: (b, l, 0, 0)),
                  pl.BlockSpec((None, None, plen, 128), lambda l, b: (b, l, 0, 0)),
                  pl.BlockSpec((None, 4, 128, 128), lambda l, b: (l, 0, 0, 0)),
                  pl.BlockSpec((None, 2, 128, 128), lambda l, b: (l, 0, 0, 0)),
                  pl.BlockSpec((None, 2, 128), lambda l, b: (l, 0, 0))],
        out_specs=[pl.BlockSpec((None, 4, None, plen, 128), lambda l, b: (l, 0, b, 0, 0)),
                   pl.BlockSpec((None, 2, None, 128, plen), lambda l, b: (l, 0, b, 0, 0))],
        out_shape=[jax.ShapeDtypeStruct((depth, 4, nb, plen, 128), BF16),
                   jax.ShapeDtypeStruct((depth, 2, nb, 128, plen), BF16)],
        compiler_params=_cparams(("arbitrary", "arbitrary")),
        name="mla_ctx_kv",
    )(cache_ckv, cache_krope_placed, wk, wv, kn_g2)


def _attn_body(*refs, has_ctx):
    if has_ctx:
        qa_ref, qb_ref, ka_ref, kb_ref, vt_ref, kca_ref, kcb_ref, vct_ref, o_ref, sa_scr, sb_scr = refs
    else:
        qa_ref, qb_ref, ka_ref, kb_ref, vt_ref, o_ref, sa_scr, sb_scr = refs
        kca_ref = kcb_ref = vct_ref = None
    tk = ka_ref.shape[0]
    chunk = min(ATT_CHUNK, tk)
    q = (qa_ref[...], qb_ref[...])
    k_refs = (ka_ref, kb_ref)
    kc_refs = (kca_ref, kcb_ref)
    s_scr = (sa_scr, sb_scr)
    vrows = (slice(0, HEAD_DIM), slice(HEAD_DIM, 2 * HEAD_DIM))
    spans = [("new", c * chunk, chunk) for c in range(tk // chunk)]
    if has_ctx:
        spans = [("ctx", 0, kca_ref.shape[0])] + spans

    def scratch_rows(kind, start, size):
        base = tk if kind == "ctx" else 0
        return slice(base + start, base + start + size)

    m = [None, None]
    for kind, start, size in spans:
        for hd in range(2):
            if kind == "ctx":
                keys = kc_refs[hd][...].astype(BF16)
            else:
                keys = k_refs[hd][start:start + size, :]
            s = _dot_nt(keys, q[hd])
            s_scr[hd][scratch_rows(kind, start, size), :] = s
            cm = jnp.max(s, axis=0, keepdims=True)
            m[hd] = cm if m[hd] is None else jnp.maximum(m[hd], cm)

    den = [None, None]
    acc = [None, None]
    for kind, start, size in spans:
        for hd in range(2):
            e = jnp.exp(s_scr[hd][scratch_rows(kind, start, size), :] - m[hd])
            if kind == "ctx":
                vals = vct_ref[vrows[hd], :].astype(BF16)
            else:
                vals = vt_ref[vrows[hd], start:start + size]
            part = _dot(vals, e.astype(BF16))
            rs = jnp.sum(e, axis=0, keepdims=True)
            den[hd] = rs if den[hd] is None else den[hd] + rs
            acc[hd] = part if acc[hd] is None else acc[hd] + part
    out_t = jnp.concatenate([acc[0] / den[0], acc[1] / den[1]], axis=0)
    o_ref[...] = out_t.T.astype(o_ref.dtype)


def _attention(q, ka, kb, v, head_a, head_b, k_head_a, k_head_b, v_idx, n_seq, seq_len, tok0, ctx=None):
    tq = min(ATT_TQ, seq_len)
    nq = seq_len // tq
    seq0 = tok0 // seq_len
    q0 = tok0 // tq

    def qmap(hsel):
        return lambda b, p, i: (hsel(p), q0 + b * nq + i, 0)

    def kmap(hsel, arr):
        if arr.ndim == 3:
            return lambda b, p, i: (hsel(p), seq0 + b, 0)
        return lambda b, p, i: (seq0 + b, 0)

    def kspec(arr, hsel):
        if arr.ndim == 3:
            return pl.BlockSpec((None, seq_len, 128), kmap(hsel, arr))
        return pl.BlockSpec((seq_len, 128), kmap(hsel, arr))

    if v.ndim == 3:
        vspec = pl.BlockSpec((None, 128, seq_len), lambda b, p, i: (v_idx(p), 0, seq0 + b))
    else:
        vspec = pl.BlockSpec((128, seq_len), lambda b, p, i: (0, seq0 + b))
    in_specs = [pl.BlockSpec((None, tq, 128), qmap(head_a)), pl.BlockSpec((None, tq, 128), qmap(head_b)),
                kspec(ka, k_head_a), kspec(kb, k_head_b), vspec]
    args = [q, q, ka, kb, v]
    if ctx is not None:
        kca, kcb, vc = ctx
        past = vc.shape[-1]

        def cspec(arr, hsel, shape):
            if arr.ndim == 4:
                return pl.BlockSpec((None, None) + shape, lambda b, p, i: (hsel(p), b, 0, 0))
            return pl.BlockSpec((None,) + shape, lambda b, p, i: (b, 0, 0))

        in_specs += [cspec(kca, k_head_a, (past, 128)), cspec(kcb, k_head_b, (past, 128)),
                     cspec(vc, v_idx, (128, past))]
        args += [kca, kcb, vc]
    return pl.pallas_call(
        functools.partial(_attn_body, has_ctx=ctx is not None),
        grid=(n_seq, 2, nq),
        in_specs=in_specs,
        out_specs=pl.BlockSpec((tq, 128), lambda b, p, i: (b * nq + i, p)),
        out_shape=jax.ShapeDtypeStruct((n_seq * seq_len, 256), BF16),
        scratch_shapes=[pltpu.VMEM((seq_len + (0 if ctx is None else ctx[2].shape[-1]), tq), F32)] * 2,
        compiler_params=_cparams(("arbitrary", "arbitrary", "arbitrary")),
        name="attention_ctx" if ctx is None else "attention_lat",
    )(*args)


def _out_body(x_ref, mod_ref, a_ref, yf_ref, yb_ref, obc_ref, obl_ref, occ_ref, ocl_ref,
              d_ref, dprev_ref, dnext_ref, wo_ref, gu_ref, rk_ref, gn_ref, cw_ref, cb_ref, n2_ref, wq_ref,
              x1_ref, h2_ref, q_ref, *, n_ctx_tiles):
    i = pl.program_id(0)
    is_ctx = i < n_ctx_tiles
    tm = x_ref.shape[0]
    mod = mod_ref[...]
    gate1 = mod[:, 2 * D_MODEL:3 * D_MODEL]
    shift2 = mod[:, 3 * D_MODEL:4 * D_MODEL]
    scale2 = mod[:, 4 * D_MODEL:5 * D_MODEL]

    r = a_ref[:, 0:256]
    k = a_ref[:, 256:512]
    v = a_ref[:, 512:768]
    gd = a_ref[:, 1024:1152]
    y = yf_ref[...] + yb_ref[...]
    avg = _group_matrix(GROUP_W, HEAD_DIM, 1.0 / HEAD_DIM)
    ones = _group_matrix(GROUP_W, HEAD_DIM, 1.0)
    yn = y * lax.rsqrt(_dot(y * y, avg, precision=HI) + EPS) * gn_ref[...]
    bonus = _dot(r * k * rk_ref[...], ones, precision=HI) * v
    gate = _dot(jax.nn.sigmoid(gd).astype(BF16), gu_ref[...])
    o_a = (yn + bonus) * gate

    o_b = jnp.where(is_ctx, obc_ref[...], obl_ref[...])
    o_c = jnp.where(is_ctx, occ_ref[...], ocl_ref[...])

    u = d_ref[:, 512:768] * d_ref[:, 0:256]
    u_prev = dprev_ref[:, 512:768] * dprev_ref[:, 0:256]
    u_next = dnext_ref[:, 512:768] * dnext_ref[:, 0:256]
    rows = lax.broadcasted_iota(jnp.int32, (tm, GROUP_W), 0)
    up = jnp.where(rows == 0, u_prev, pltpu.roll(u, 1, 0))
    un = jnp.where(rows == tm - 1, u_next, pltpu.roll(u, tm - 1, 0))
    conv = up * cw_ref[0:1, :] + u * cw_ref[1:2, :] + un * cw_ref[2:3, :] + cb_ref[...]
    o_d = d_ref[:, 256:512] * conv

    mix_in = jnp.concatenate([o_a.astype(BF16), o_b, o_c, o_d.astype(BF16)], axis=1)
    x1 = x_ref[...] + gate1 * _dot(mix_in, wo_ref[...])
    x1_ref[...] = x1
    h2 = (_rms_rows(x1, n2_ref[...]) * (1.0 + scale2) + shift2).astype(BF16)
    h2_ref[...] = h2
    q_ref[...] = _dot(h2, wq_ref[...])


def _out_proj(x, mod, a_grp, yf, yb, ob_ctx, ob_lat, oc_ctx, oc_lat, d_grp, d_prev, d_next, lw):
    n = x.shape[0]
    n_tiles = n // TILE
    n_ctx_tiles = N_CTX_TOK // TILE
    tpg = COND_GROUP // TILE
    row = lambda i: (i, 0)
    ctx_row = lambda i: (jnp.minimum(i, n_ctx_tiles - 1), 0)
    lat_row = lambda i: (jnp.maximum(i - n_ctx_tiles, 0), 0)
    halo = lambda i: (i, 0, 0)
    c2 = lambda i: (0, 0)
    nq = PEER_HEADS * 2 * PEER_KEYS
    return pl.pallas_call(
        functools.partial(_out_body, n_ctx_tiles=n_ctx_tiles),
        grid=(n_tiles,),
        in_specs=[pl.BlockSpec((TILE, D_MODEL), row),
                  pl.BlockSpec((None, 1, 6 * D_MODEL), lambda i: (i // tpg, 0, 0)),
                  pl.BlockSpec((TILE, A_W), row),
                  pl.BlockSpec((TILE, GROUP_W), row), pl.BlockSpec((TILE, GROUP_W), row),
                  pl.BlockSpec((TILE, GROUP_W), ctx_row), pl.BlockSpec((TILE, GROUP_W), lat_row),
                  pl.BlockSpec((TILE, GROUP_W), ctx_row), pl.BlockSpec((TILE, GROUP_W), lat_row),
                  pl.BlockSpec((TILE, D_W), row),
                  pl.BlockSpec((None, 1, D_W), halo), pl.BlockSpec((None, 1, D_W), halo),
                  pl.BlockSpec((D_MODEL, D_MODEL), c2), pl.BlockSpec((128, GROUP_W), c2),
                  pl.BlockSpec((1, GROUP_W), c2), pl.BlockSpec((1, GROUP_W), c2),
                  pl.BlockSpec((3, GROUP_W), c2), pl.BlockSpec((1, GROUP_W), c2),
                  pl.BlockSpec((1, D_MODEL), c2), pl.BlockSpec((D_MODEL, nq), c2)],
        out_specs=[pl.BlockSpec((TILE, D_MODEL), row), pl.BlockSpec((TILE, D_MODEL), row),
                   pl.BlockSpec((TILE, nq), row)],
        out_shape=[jax.ShapeDtypeStruct((n, D_MODEL), F32), jax.ShapeDtypeStruct((n, D_MODEL), BF16),
                   jax.ShapeDtypeStruct((n, nq), F32)],
        compiler_params=_cparams(("arbitrary",)),
        name="out_proj",
    )(x, mod, a_grp, yf, yb, ob_ctx, ob_lat, oc_ctx, oc_lat, d_grp, d_prev, d_next,
      lw["w_out"], lw["gu"], lw["rk"], lw["gn"], lw["conv_w"], lw["conv_b"], lw["norm2_g"], lw["wq"])


def _exchange(a, b):
    if a is None:
        return b, None
    if b is None:
        return a, None
    return jnp.maximum(a, b), jnp.minimum(a, b)


def _sort16_desc(xs):
    xs = list(xs)
    k = 2
    while k <= 16:
        j = k // 2
        while j >= 1:
            for i in range(16):
                partner = i ^ j
                if partner > i:
                    hi, lo = _exchange(xs[i], xs[partner])
                    xs[i], xs[partner] = (hi, lo) if (i & k) == 0 else (lo, hi)
            j //= 2
        k *= 2
    return xs


def _bitonic_merge_desc(xs):
    xs = list(xs)
    j = 8
    while j >= 1:
        for i in range(16):
            partner = i ^ j
            if partner > i:
                xs[i], xs[partner] = _exchange(xs[i], xs[partner])
        j //= 2
    return xs


def _top16_of_rows(xs):
    ys = _sort16_desc(xs)
    for shift in (4, 2, 1):
        zs = [None if y is None else pltpu.roll(y, shift, 0) for y in ys]
        ts = [_exchange(ys[i], zs[15 - i])[0] for i in range(16)]
        ys = _bitonic_merge_desc(ts)
    return ys


def _topk_body(q_ref, k1_ref, k2_ref, thr_ref, s2_ref, e1_ref, e2_ref):
    tn = q_ref.shape[0]
    sub = lax.broadcasted_iota(jnp.int32, (8, tn), 0)

    def spread(vals):
        out = vals[7]
        for s in range(6, -1, -1):
            out = jnp.where(sub == s, vals[s], out)
        return out

    for h in range(PEER_HEADS):
        qa = q_ref[:, (2 * h) * PEER_KEYS:(2 * h + 1) * PEER_KEYS].astype(BF16)
        qb = q_ref[:, (2 * h + 1) * PEER_KEYS:(2 * h + 2) * PEER_KEYS].astype(BF16)
        s1 = _dot_nt(k1_ref[h], qa)
        s2 = _dot_nt(k2_ref[h], qb)
        v1 = _top16_of_rows([s1[8 * i:8 * i + 8, :] for i in range(16)])
        v2 = _top16_of_rows([s2[8 * i:8 * i + 8, :] for i in range(16)])
        v2_lo, v2_hi, v1_hi = spread(v2[0:8]), spread(v2[8:16]), spread(v1[8:16])
        cands = ([v1[0] + v2_lo, v1[0] + v2_hi] + [v1[a] + v2_lo for a in range(1, 8)]
                 + [v1_hi + v2[0]] + [None] * 6)
        best = _top16_of_rows(cands)
        tau = best[PEER_TOPK - 1][0:1, :]
        zsum = jnp.exp(best[0] - best[0])
        for kth in range(1, PEER_TOPK):
            zsum = zsum + jnp.exp(best[kth] - best[0])
        thr = jnp.full(s1.shape, jnp.inf, F32)
        for b in range(PEER_TOPK):
            vb = v2[b][0:1, :]
            thr = jnp.where(s1 + vb >= tau, vb, thr)
        thr_ref[h] = thr
        s2_ref[h] = s2
        e1_ref[h] = jnp.exp(s1 - v1[0][0:1, :])
        e2_ref[h] = jnp.exp(s2 - v2[0][0:1, :]) * (0.5 / zsum[0:1, :])


def _peer_topk(q, k1, k2):
    n = q.shape[0]
    tn = TOPK_TN
    big = pl.BlockSpec((PEER_HEADS, PEER_KEYS, tn), lambda i: (0, 0, i))
    big_shape = jax.ShapeDtypeStruct((PEER_HEADS, PEER_KEYS, n), F32)
    c3 = lambda i: (0, 0, 0)
    return pl.pallas_call(
        _topk_body,
        grid=(n // tn,),
        in_specs=[pl.BlockSpec((tn, PEER_HEADS * 2 * PEER_KEYS), lambda i: (i, 0)),
                  pl.BlockSpec((PEER_HEADS, PEER_KEYS, PEER_KEYS), c3),
                  pl.BlockSpec((PEER_HEADS, PEER_KEYS, PEER_KEYS), c3)],
        out_specs=[big, big, big, big],
        out_shape=[big_shape, big_shape, big_shape, big_shape],
        compiler_params=_cparams(("arbitrary",)),
        name="peer_topk",
    )(q, k1, k2)


GELU_C0 = 0.7978845608028654
GELU_C1 = 0.044715


PEER_PIECE = 256


def _dense_body(h2_ref, u_ref, vt_ref, thr_ref, s2_ref, e1_ref, e2_ref, x1_ref, mod_ref,
                o_ref, acc, *piece_scratch):
    j = pl.program_id(1)

    @pl.when(j == 0)
    def _():
        acc[...] = jnp.zeros_like(acc)

    n_pieces = len(piece_scratch) // 2
    hs, gs = piece_scratch[:n_pieces], piece_scratch[n_pieces:]
    tn = hs[0].shape[1]
    h2 = h2_ref[...]

    def pre_activations(p):
        rows = slice(p * PEER_PIECE, (p + 1) * PEER_PIECE)
        hs[p][...] = _dot_nt(u_ref[rows, :], h2)

    pre_activations(0)
    for p in range(n_pieces):
        if p + 1 < n_pieces:
            pre_activations(p + 1)
        halves = PEER_PIECE // PEER_KEYS
        for sub in range(2 * (tn // 128)):
            keys = slice((sub % 2) * 64, (sub % 2) * 64 + 64)
            cols = slice((sub // 2) * 128, (sub // 2) * 128 + 128)
            n_grp = 64 // 8
            w = [[None] * n_grp for _ in range(halves)]
            for h in range(PEER_HEADS):
                s2t = s2_ref[h, keys, cols]
                e2t = e2_ref[h, keys, cols]
                for half in range(halves):
                    ii = p * halves + half
                    thr8 = jnp.broadcast_to(thr_ref[h, ii:ii + 1, cols], (8, 128))
                    e18 = jnp.broadcast_to(e1_ref[h, ii:ii + 1, cols], (8, 128))
                    for g in range(n_grp):
                        grp = slice(8 * g, 8 * g + 8)
                        term = jnp.where(s2t[grp] >= thr8, e18 * e2t[grp], 0.0)
                        w[half][g] = term if w[half][g] is None else w[half][g] + term
            for half in range(halves):
                rows = slice(half * PEER_KEYS + keys.start, half * PEER_KEYS + keys.stop)
                x = hs[p][rows, cols]
                inner = x * (GELU_C0 + (GELU_C0 * GELU_C1) * (x * x))
                wh = jnp.concatenate(w[half], axis=0)
                gs[p][rows, cols] = (wh * x * (1.0 + jnp.tanh(inner))).astype(BF16)
        prows = slice(p * PEER_PIECE, (p + 1) * PEER_PIECE)
        acc[...] += _dot(vt_ref[:, prows], gs[p][...])

    @pl.when(j == pl.num_programs(1) - 1)
    def _():
        gate2 = mod_ref[:, 5 * D_MODEL:6 * D_MODEL]
        o_ref[...] = x1_ref[...] + gate2 * acc[...].T


def _peer_dense(h2, u_b, vt_b, thr, s2, e1, e2, x1, mod):
    n = h2.shape[0]
    tn, te = PEER_TN, PEER_TE
    rows_per_step = te // PEER_KEYS
    tpg = COND_GROUP // tn
    tok = lambda i, j: (i, 0)
    key_rows = pl.BlockSpec((PEER_HEADS, rows_per_step, tn), lambda i, j: (0, j, i))
    key_all = pl.BlockSpec((PEER_HEADS, PEER_KEYS, tn), lambda i, j: (0, 0, i))
    return pl.pallas_call(
        _dense_body,
        grid=(n // tn, N_EXPERTS // te),
        in_specs=[pl.BlockSpec((tn, D_MODEL), tok),
                  pl.BlockSpec((te, D_MODEL), lambda i, j: (j, 0)),
                  pl.BlockSpec((D_MODEL, te), lambda i, j: (0, j)),
                  key_rows, key_all, key_rows, key_all,
                  pl.BlockSpec((tn, D_MODEL), tok),
                  pl.BlockSpec((None, 1, 6 * D_MODEL), lambda i, j: (i // tpg, 0, 0))],
        out_specs=pl.BlockSpec((tn, D_MODEL), tok),
        out_shape=jax.ShapeDtypeStruct((n, D_MODEL), F32),
        scratch_shapes=([pltpu.VMEM((D_MODEL, tn), F32)]
                        + [pltpu.VMEM((PEER_PIECE, tn), F32)] * (te // PEER_PIECE)
                        + [pltpu.VMEM((PEER_PIECE, tn), BF16)] * (te // PEER_PIECE)),
        compiler_params=_cparams(("arbitrary", "arbitrary")),
        name="peer_dense",
    )(h2, u_b, vt_b, thr, s2, e1, e2, x1, mod)


def _in_proj_columns():
    src = np.full((IN_PAD,), -1, np.int64)

    def put(dst, start, width):
        src[dst:dst + width] = np.arange(start, start + width)

    put(0, 0, 768)
    put(768, 768, 128)
    put(896, 896, 64)
    put(1024, 960, 128)
    b0 = A_W
    put(b0, 1088, 192)
    put(b0 + 256, 1280, 128)
    put(b0 + 384 + 64, 1408, 32)
    put(b0 + 512 + 64, 1408 + 16, 16)
    put(b0 + 512 + 80, 1408, 16)
    c0 = A_W + B_W
    for slot, head in enumerate((0, 2, 1, 3)):
        put(c0 + 64 * slot, 1440 + 64 * head, 64)
    put(c0 + 256, 1696, 256)
    put(A_W + B_W + C_W, 1952, 768)
    return src


def _swap_tail(w):
    return jnp.concatenate([w[..., :64], w[..., 80:96], w[..., 64:80]], axis=-1)


def _pad_last(w, width):
    return jnp.pad(w, [(0, 0)] * (w.ndim - 1) + [(0, width - w.shape[-1])])


def _layer_weights(p):
    depth = p["w_in"].shape[0]
    src = _in_proj_columns()
    w_in = jnp.take(p["w_in"], jnp.asarray(np.maximum(src, 0)), axis=2)
    w_in = jnp.where(jnp.asarray(src >= 0)[None, None, :], w_in, 0.0).astype(BF16)

    wuq = p["mla_wuq"].reshape(depth, MLA_Q_RANK, 4, MLA_QK_DIM).transpose(0, 2, 1, 3)
    wq_plain = jnp.pad(wuq, ((0, 0), (0, 0), (0, 256 - MLA_Q_RANK), (0, 128 - MLA_QK_DIM)))
    wq_swap = jnp.pad(_swap_tail(wuq), ((0, 0), (0, 0), (0, 256 - MLA_Q_RANK), (0, 128 - MLA_QK_DIM)))
    wukv = p["mla_wukv"].reshape(depth, 128, 4, 128)
    wk = _pad_last(wukv[..., :64].transpose(0, 2, 1, 3), 128)
    wv_heads = wukv[..., 64:].transpose(0, 2, 1, 3)
    wv = jnp.concatenate([wv_heads[:, 0::2], wv_heads[:, 1::2]], axis=-1)

    def gain2(g):
        return jnp.stack([_pad_last(g, 128), _pad_last(_swap_tail(g), 128)], axis=1)

    w_out = p["w_out"]
    oc = w_out[:, 512:768].reshape(depth, 4, 64, D_MODEL)[:, jnp.asarray([0, 2, 1, 3])].reshape(depth, 256, D_MODEL)
    w_out = jnp.concatenate([w_out[:, :512], oc, w_out[:, 768:]], axis=1).astype(BF16)

    wu_pad = jnp.zeros((depth, 2, 128, GROUP_W), F32)
    wu_pad = wu_pad.at[:, 0, 0:64].set(p["rw_wu"][:, 0]).at[:, 1, 64:128].set(p["rw_wu"][:, 1])
    au_pad = jnp.zeros((depth, 2, 128, GROUP_W), F32)
    au_pad = au_pad.at[:, 0, 0:32].set(p["rw_au"][:, 0]).at[:, 1, 32:64].set(p["rw_au"][:, 1])

    return {
        "w_in": w_in,
        "norm1_g": p["norm1_g"][:, None, :],
        "norm2_g": p["norm2_g"][:, None, :],
        "w_out": w_out,
        "w0": p["rw_w0"], "a0": p["rw_a0"],
        "wu": wu_pad.astype(BF16), "au": au_pad.astype(BF16),
        "kkp": p["rw_kk"][:, None, :], "ka": p["rw_ka"][:, None, :],
        "gu": p["rw_gu"].astype(BF16),
        "rk": p["rw_rk"].reshape(depth, 1, GROUP_W),
        "gn": p["rw_gn"][:, None, :],
        "qa_g": _pad_last(p["mla_qa_g"], 256)[:, None, :],
        "wqq": jnp.concatenate([wq_plain, wq_swap], axis=1).astype(BF16),
        "qn_g2": gain2(p["mla_qn_g"]),
        "kva_g": p["mla_kva_g"][:, None, :],
        "wk": wk.astype(BF16), "wv": jnp.swapaxes(wv, -1, -2).astype(BF16),
        "kn_g2": gain2(p["mla_kn_g"]),
        "gq_g": jnp.tile(p["gqa_qn_g"], (1, 2))[:, None, :],
        "gk_g": jnp.tile(p["gqa_kn_g"], (1, 2))[:, None, :],
        "conv_w": p["conv_w"], "conv_b": p["conv_b"][:, None, :],
        "wq": p["peer_wq"].astype(BF16),
        "k1": p["peer_k1"].astype(BF16), "k2": p["peer_k2"].astype(BF16),
        "u": p["peer_u"].astype(BF16),
        "vt": jnp.swapaxes(p["peer_v"], 1, 2).astype(BF16),
    }


def _rope_tables():
    t = jnp.arange(LAT_LEN, dtype=F32)
    grid_row = jnp.floor(t / GRID_W)
    grid_col = t - grid_row * GRID_W

    def angles(rot_dim):
        n_freq = rot_dim // 4
        freqs = ROPE_THETA ** (-jnp.arange(n_freq, dtype=F32) / n_freq)
        ang = jnp.concatenate([grid_row[:, None] * freqs, grid_col[:, None] * freqs], axis=-1)
        return jnp.cos(ang), jnp.sin(ang)

    cm, sm = angles(32)
    ones64 = jnp.ones((LAT_LEN, 64), F32)
    zeros64 = jnp.zeros((LAT_LEN, 64), F32)
    cosm = jnp.concatenate([ones64, cm, cm, ones64[:, :32]], axis=-1)
    sinm = jnp.concatenate([zeros64, -sm, sm, zeros64[:, :32]], axis=-1)
    cg, sg = angles(64)
    cosg = jnp.tile(jnp.concatenate([cg, cg], axis=-1), (1, 2))
    sing = jnp.tile(jnp.concatenate([-sg, sg], axis=-1), (1, 2))
    ident = jnp.ones((TILE, 128), F32)
    zero = jnp.zeros((TILE, 128), F32)
    return {"cosm": jnp.concatenate([ident, cosm]), "sinm": jnp.concatenate([zero, sinm]),
            "cosg": jnp.concatenate([ident, cosg]), "sing": jnp.concatenate([zero, sing])}


def _states_to_pairs(s):
    lead = s.shape[:-3]
    s = s.reshape(lead + (2, 2, 64, 64))
    z = jnp.zeros(lead + (2, 64, 64), s.dtype)
    top = jnp.concatenate([s[..., 0, :, :], z], axis=-1)
    bot = jnp.concatenate([z, s[..., 1, :, :]], axis=-1)
    return jnp.concatenate([top, bot], axis=-2)


def _pairs_to_states(sp):
    lead = sp.shape[:-3]
    a = sp[..., 0:64, 0:64]
    b = sp[..., 64:128, 64:128]
    return jnp.stack([a, b], axis=-3).reshape(lead + (4, 64, 64))


def _conv_halos(d_grp):
    n_tiles = d_grp.shape[0] // TILE
    tiles = d_grp.reshape(n_tiles, TILE, D_W)
    first_rows = tiles[:, 0, :]
    last_rows = tiles[:, TILE - 1, :]
    zero = jnp.zeros((1, D_W), d_grp.dtype)
    prev = jnp.concatenate([zero, last_rows[:-1]], axis=0)
    nxt = jnp.concatenate([first_rows[1:], zero], axis=0)
    idx = np.arange(n_tiles)
    n_ctx_tiles = N_CTX_TOK // TILE
    per_seq = LAT_LEN // TILE
    lat_pos = (idx - n_ctx_tiles) % per_seq
    seq_start = np.where(idx < n_ctx_tiles, True, lat_pos == 0)
    seq_end = np.where(idx < n_ctx_tiles, True, lat_pos == per_seq - 1)
    prev = jnp.where(jnp.asarray(seq_start)[:, None], 0.0, prev)
    nxt = jnp.where(jnp.asarray(seq_end)[:, None], 0.0, nxt)
    return prev[:, None, :], nxt[:, None, :]


def kernel(x_prompt, x_sample, state_rwkv, cache_mla_ckv, cache_mla_krope, cache_gqa_k, cache_gqa_v, c, c_ctx, norm1_g, norm2_g, w_mod, b_mod, w_in, w_out, rw_w0, rw_wu, rw_a0, rw_au, rw_gu, rw_kk, rw_ka, rw_rk, rw_gn, mla_qa_g, mla_wuq, mla_kva_g, mla_wukv, mla_qn_g, mla_kn_g, gqa_qn_g, gqa_kn_g, conv_w, conv_b, peer_wq, peer_k1, peer_k2, peer_u, peer_v):
    depth = w_in.shape[0]
    params = dict(norm1_g=norm1_g, norm2_g=norm2_g, w_in=w_in, w_out=w_out, rw_w0=rw_w0, rw_wu=rw_wu,
                  rw_a0=rw_a0, rw_au=rw_au, rw_gu=rw_gu, rw_kk=rw_kk, rw_ka=rw_ka, rw_rk=rw_rk, rw_gn=rw_gn,
                  mla_qa_g=mla_qa_g, mla_wuq=mla_wuq, mla_kva_g=mla_kva_g, mla_wukv=mla_wukv,
                  mla_qn_g=mla_qn_g, mla_kn_g=mla_kn_g, gqa_qn_g=gqa_qn_g, gqa_kn_g=gqa_kn_g,
                  conv_w=conv_w, conv_b=conv_b, peer_wq=peer_wq, peer_k1=peer_k1, peer_k2=peer_k2,
                  peer_u=peer_u, peer_v=peer_v)
    lw_all = _layer_weights(params)
    tabs = _rope_tables()

    cvecs = jnp.concatenate([c_ctx[None, :], c, jnp.zeros((8 - 1 - N_LAT_SEQ, D_MODEL), F32)], axis=0)
    mods = _modulation(cvecs, w_mod, b_mod).reshape(depth, 8, 1, 6 * D_MODEL)

    x = jnp.concatenate([x_prompt.reshape(N_CTX_TOK, D_MODEL), x_sample.reshape(N_LAT_TOK, D_MODEL)], axis=0)

    s0_lat = _states_to_pairs(state_rwkv)
    s0_all = jnp.concatenate([jnp.zeros((N_CTX_SEQ,) + s0_lat.shape[1:], F32), s0_lat], axis=0)

    krope_placed = jnp.pad(cache_mla_krope, ((0, 0), (0, 0), (0, 0), (64, 32)))
    kctx_m, vctx_m = _mla_ctx_kv(cache_mla_ckv, krope_placed, lw_all["wk"], lw_all["wv"], lw_all["kn_g2"])
    past = cache_gqa_k.shape[2]
    kctx_g = cache_gqa_k.reshape(N_LAT_SEQ, depth, past, 128)
    vctx_g = jnp.swapaxes(cache_gqa_v.reshape(N_LAT_SEQ, depth, past, 128), -1, -2)

    ident = lambda p: p
    st_a, st_ckv, st_kr, st_k, st_v = [], [], [], [], []
    for l in range(depth):
        lw = {name: w[l] for name, w in lw_all.items()}
        mod = mods[l]
        a_grp, b_grp, c_grp, d_grp = _in_proj(x, mod, lw["norm1_g"], lw["w_in"])

        yf, yb, s_fin = _rwkv_scan(a_grp, s0_all[:, l], lw["w0"], lw["wu"], lw["a0"], lw["au"], lw["kkp"], lw["ka"])

        qm, km, vm, ckvn, qg, kg, vg, kgn = _attn_prep(b_grp, c_grp, tabs, lw)
        ob_ctx = _attention(qm, km, km, vm, lambda p: 2 * p, lambda p: 2 * p + 1, lambda p: 2 * p,
                            lambda p: 2 * p + 1, ident, N_CTX_SEQ, CTX_LEN, 0)
        ob_lat = _attention(qm, km, km, vm, lambda p: 2 * p, lambda p: 2 * p + 1, lambda p: 2 * p,
                            lambda p: 2 * p + 1, ident, N_LAT_SEQ, LAT_LEN, N_CTX_TOK,
                            ctx=(kctx_m[l], kctx_m[l], vctx_m[l]))
        oc_ctx = _attention(qg, kg, kg, vg, ident, lambda p: p + 2, ident, ident, ident,
                            N_CTX_SEQ, CTX_LEN, 0)
        oc_lat = _attention(qg, kg, kg, vg, ident, lambda p: p + 2, ident, ident, ident,
                            N_LAT_SEQ, LAT_LEN, N_CTX_TOK,
                            ctx=(kctx_g[:, l], kctx_g[:, l], vctx_g[:, l]))

        d_prev, d_next = _conv_halos(d_grp)
        x1, h2, q = _out_proj(x, mod, a_grp, yf, yb, ob_ctx, ob_lat, oc_ctx, oc_lat, d_grp, d_prev, d_next, lw)

        thr, s2, e1, e2 = _peer_topk(q, lw["k1"], lw["k2"])
        x = _peer_dense(h2, lw["u"], lw["vt"], thr, s2, e1, e2, x1, mod)

        st_a.append(_pairs_to_states(s_fin[:N_CTX_SEQ]))
        st_ckv.append(ckvn[:N_CTX_TOK].reshape(N_CTX_SEQ, CTX_LEN, 128))
        st_kr.append(b_grp[:N_CTX_TOK, 448:480].reshape(N_CTX_SEQ, CTX_LEN, 32))
        st_k.append(kgn[:N_CTX_TOK].reshape(N_CTX_SEQ, CTX_LEN, 2, HEAD_DIM))
        st_v.append(c_grp[:N_CTX_TOK, 384:512].reshape(N_CTX_SEQ, CTX_LEN, 2, HEAD_DIM))

    y_prompt = x[:N_CTX_TOK].reshape(N_CTX_SEQ, CTX_LEN, D_MODEL)
    y_sample = x[N_CTX_TOK:].reshape(N_LAT_SEQ, LAT_LEN, D_MODEL)
    return (y_prompt, y_sample, jnp.stack(st_a, axis=1), jnp.stack(st_ckv, axis=1), jnp.stack(st_kr, axis=1),
            jnp.stack(st_k, axis=1), jnp.stack(st_v, axis=1))
```

```python
import functools

import numpy as np
import jax
import jax.numpy as jnp
from jax import lax
from jax.experimental import pallas as pl
from jax.experimental.pallas import tpu as pltpu

F32 = jnp.float32
BF16 = jnp.bfloat16
HI = lax.Precision.HIGHEST

D_MODEL = 1024
N_CTX_SEQ = 16
CTX_LEN = 256
N_LAT_SEQ = 2
LAT_LEN = 4096
GRID_W = 64
ROPE_THETA = 10000.0
EPS = 1e-6
GROUP_W = 256
HEAD_DIM = 64
RW_DECAY_SCALE = 0.6065306597
MLA_QK_DIM = 96
MLA_Q_RANK = 192
PEER_HEADS = 8
PEER_KEYS = 128
PEER_TOPK = 16
N_EXPERTS = PEER_KEYS * PEER_KEYS

N_CTX_TOK = N_CTX_SEQ * CTX_LEN
N_LAT_TOK = N_LAT_SEQ * LAT_LEN
N_TOK = N_CTX_TOK + N_LAT_TOK
COND_GROUP = 4096

A_W, B_W, C_W, D_W = 1152, 640, 512, 768
IN_PAD = A_W + B_W + C_W + D_W

SCAN_CHUNK = 128
TILE = 256
IN_TILE = 512
PEER_TN = 512
PEER_TE = 2048
TOPK_TN = 128
ATT_TQ = 256
ATT_CHUNK = 512

VMEM_LIMIT = 56 * 1024 * 1024


def _cparams(sem):
    return pltpu.CompilerParams(dimension_semantics=sem, vmem_limit_bytes=VMEM_LIMIT)


def _dot(a, b, precision=None):
    return jnp.dot(a, b, preferred_element_type=F32, precision=precision)


def _dot_nt(a, b, precision=None):
    return lax.dot_general(a, b, (((1,), (1,)), ((), ())), preferred_element_type=F32,
                           precision=precision)


def _dot_tn(a, b, precision=None):
    return lax.dot_general(a, b, (((0,), (0,)), ((), ())), preferred_element_type=F32,
                           precision=precision)


def _rms_rows(x, g):
    return x * lax.rsqrt(jnp.mean(x * x, axis=-1, keepdims=True) + EPS) * g


def _group_matrix(n, group, value):
    r = lax.broadcasted_iota(jnp.int32, (n, n), 0) // group
    c = lax.broadcasted_iota(jnp.int32, (n, n), 1) // group
    return jnp.where(r == c, value, 0.0).astype(F32)


def _mod_body(c_ref, w_ref, b_ref, o_ref):
    c = c_ref[...]
    s = c * jax.nn.sigmoid(c)
    o_ref[...] = _dot(s.astype(BF16), w_ref[...].astype(BF16)) + b_ref[...]


def _modulation(cvecs, w_mod, b_mod):
    depth = w_mod.shape[0]
    tn = 1536
    return pl.pallas_call(
        _mod_body,
        grid=(depth, 6 * D_MODEL // tn),
        in_specs=[pl.BlockSpec((8, D_MODEL), lambda l, j: (0, 0)),
                  pl.BlockSpec((None, D_MODEL, tn), lambda l, j: (l, 0, j)),
                  pl.BlockSpec((None, 1, tn), lambda l, j: (l, 0, j))],
        out_specs=pl.BlockSpec((None, 8, tn), lambda l, j: (l, 0, j)),
        out_shape=jax.ShapeDtypeStruct((depth, 8, 6 * D_MODEL), F32),
        compiler_params=_cparams(("arbitrary", "arbitrary")),
        name="adaln_mod",
    )(cvecs, w_mod, b_mod.reshape(depth, 1, 6 * D_MODEL))


def _in_body(x_ref, mod_ref, g_ref, w_ref, a_ref, b_ref, c_ref, d_ref):
    mod = mod_ref[...]
    shift = mod[:, 0:D_MODEL]
    scale = mod[:, D_MODEL:2 * D_MODEL]
    h = _rms_rows(x_ref[...], g_ref[...]) * (1.0 + scale) + shift
    y = _dot(h.astype(BF16), w_ref[...])
    a_ref[...] = y[:, 0:A_W]
    b_ref[...] = y[:, A_W:A_W + B_W]
    c_ref[...] = y[:, A_W + B_W:A_W + B_W + C_W]
    d_ref[...] = y[:, A_W + B_W + C_W:IN_PAD]


def _in_proj(x, mod, norm_g, w_pad_all, layer):
    n = x.shape[0]
    tpg = COND_GROUP // IN_TILE
    row = lambda i: (i, 0)
    return pl.pallas_call(
        _in_body,
        grid=(n // IN_TILE,),
        in_specs=[pl.BlockSpec((IN_TILE, D_MODEL), row),
                  pl.BlockSpec((None, 1, 6 * D_MODEL), lambda i: (i // tpg, 0, 0)),
                  pl.BlockSpec((1, D_MODEL), lambda i: (0, 0)),
                  pl.BlockSpec((None, D_MODEL, IN_PAD), lambda i: (layer, 0, 0))],
        out_specs=[pl.BlockSpec((IN_TILE, A_W), row), pl.BlockSpec((IN_TILE, B_W), row),
                   pl.BlockSpec((IN_TILE, C_W), row), pl.BlockSpec((IN_TILE, D_W), row)],
        out_shape=[jax.ShapeDtypeStruct((n, A_W), F32), jax.ShapeDtypeStruct((n, B_W), F32),
                   jax.ShapeDtypeStruct((n, C_W), F32), jax.ShapeDtypeStruct((n, D_W), F32)],
        compiler_params=_cparams(("arbitrary",)),
        name="in_proj",
    )(x, mod, norm_g, w_pad_all)


def _chunk_cumsum(x, reverse):
    c = x.shape[0]
    rows = lax.broadcasted_iota(jnp.int32, x.shape, 0)
    sh = 1
    while sh < c:
        if reverse:
            x = x + jnp.where(rows < c - sh, pltpu.roll(x, c - sh, 0), 0.0)
        else:
            x = x + jnp.where(rows >= sh, pltpu.roll(x, sh, 0), 0.0)
        sh *= 2
    return x


def _scan_body(fb_ref, bb_ref, first_ref, last_ref, seq_ref,
               af_ref, ab_ref, s0_ref, w0_ref, wu_ref, a0_ref, au_ref, kkp_ref, ka_ref,
               yf_ref, yb_ref, sfin_ref, s_scr):
    step = pl.program_id(0)

    @pl.when(first_ref[step] == 1)
    def _():
        s_scr[...] = s0_ref[...]

    c = SCAN_CHUNK
    row = lax.broadcasted_iota(jnp.int32, (c, c), 0)
    col = lax.broadcasted_iota(jnp.int32, (c, c), 1)
    lane_c = lax.broadcasted_iota(jnp.int32, (c, 128), 1) < HEAD_DIM
    lane_2c = lax.broadcasted_iota(jnp.int32, (2 * c, 128), 1) < HEAD_DIM
    same_head = _group_matrix(128, HEAD_DIM, 1.0)
    kkp = kkp_ref[...]
    ka = ka_ref[...]

    def per_head(lo, hi):
        return jnp.where(lane_c, lo, hi)

    groups = []
    for d in range(2):
        x_ref = af_ref if d == 0 else ab_ref
        if d == 0:
            strict, incl = col < row, col <= row
        else:
            strict, incl = col > row, col >= row
        r = x_ref[:, 0:256]
        k = x_ref[:, 256:512]
        v = x_ref[:, 512:768]
        wd = x_ref[:, 768:896]
        ad = x_ref[:, 896:1024]
        wlog = -RW_DECAY_SCALE * jax.nn.sigmoid(
            w0_ref[d:d + 1, :] + _dot(jnp.tanh(wd).astype(BF16), wu_ref[d]))
        a = jax.nn.sigmoid(a0_ref[d:d + 1, :] + _dot(ad.astype(BF16), au_ref[d]))
        kd = k * (1.0 + (a - 1.0) * ka)
        kkf = k * kkp
        cum = _chunk_cumsum(wlog, reverse=(d == 1))
        tot = cum[c - 1:c, :] if d == 0 else cum[0:1, :]
        p_inc = jnp.exp(cum)
        p_exc = jnp.exp(cum - wlog)
        p_inv = jnp.exp(-cum)
        p_tot = jnp.exp(tot)
        for p in range(2):
            sl = slice(128 * p, 128 * p + 128)
            groups.append(dict(d=d, p=p, sl=sl, strict=strict, incl=incl, kk_raw=kkf[:, sl],
                               a=a[:, sl], kd=kd[:, sl], r=r[:, sl], vb=v[:, sl].astype(BF16),
                               p_inc=p_inc[:, sl], p_exc=p_exc[:, sl], p_inv=p_inv[:, sl], pt=p_tot[:, sl]))

    for g in groups:
        g["ss"] = _dot(g["kk_raw"] * g["kk_raw"], same_head, precision=HI)
    for g in groups:
        kk = g["kk_raw"] * lax.rsqrt(g["ss"] + EPS)
        g["a_p"] = (-kk * g["p_exc"]).astype(BF16)
        g["r_p"] = (g["r"] * g["p_inc"]).astype(BF16)
        g["b_i"] = kk * g["a"] * g["p_inv"]
        g["k_i"] = g["kd"] * g["p_inv"]
        g["lhs"] = jnp.concatenate([g["a_p"], g["r_p"]], axis=0)
        g["rhs"] = jnp.concatenate([g["b_i"], g["k_i"]], axis=0).astype(BF16)
        g["sb"] = s_scr[g["d"], g["p"]].astype(BF16)
    zero_b = jnp.zeros((), BF16)
    for g in groups:
        g["gram"] = (_dot_nt(jnp.where(lane_2c, g["lhs"], zero_b), g["rhs"]),
                     _dot_nt(jnp.where(lane_2c, zero_b, g["lhs"]), g["rhs"]))
        g["a_s"] = _dot_nt(g["a_p"], g["sb"])
        g["r_s"] = _dot_nt(g["r_p"], g["sb"])
    for g in groups:
        st, inc = g["strict"], g["incl"]
        g["pow"] = [jnp.where(st, gm[0:c, 0:c], 0.0).astype(BF16) for gm in g["gram"]]
        g["dm"] = [jnp.where(st, gm[0:c, c:2 * c], 0.0).astype(BF16) for gm in g["gram"]]
        g["et"] = [jnp.where(inc, gm[c:2 * c, 0:c], 0.0).astype(BF16) for gm in g["gram"]]
        g["ft"] = [jnp.where(inc, gm[c:2 * c, c:2 * c], 0.0).astype(BF16) for gm in g["gram"]]
    for g in groups:
        g["z"] = g["a_s"] + per_head(_dot(g["dm"][0], g["vb"]), _dot(g["dm"][1], g["vb"]))
    span = 1
    while span < c:
        for g in groups:
            zb = g["z"].astype(BF16)
            g["z"] = g["z"] + per_head(_dot(g["pow"][0], zb), _dot(g["pow"][1], zb))
        span *= 2
        if span < c:
            for g in groups:
                g["pow"] = [_dot(m, m).astype(BF16) for m in g["pow"]]
    for g in groups:
        zb = g["z"].astype(BF16)
        g["zb"] = zb
        y = g["r_s"] + per_head(_dot(g["et"][0], zb) + _dot(g["ft"][0], g["vb"]),
                                _dot(g["et"][1], zb) + _dot(g["ft"][1], g["vb"]))
        y_ref = yf_ref if g["d"] == 0 else yb_ref
        y_ref[:, g["sl"]] = y
    for g in groups:
        pt = g["pt"]
        upd = (_dot_tn(g["zb"], (g["b_i"] * pt).astype(BF16))
               + _dot_tn(g["vb"], (g["k_i"] * pt).astype(BF16)))
        s_scr[g["d"], g["p"]] = s_scr[g["d"], g["p"]] * pt + same_head * upd

    @pl.when(last_ref[step] == 1)
    def _():
        sfin_ref[...] = s_scr[...]


def _scan_tables():
    c = SCAN_CHUNK
    fb, bb, first, last, seq = [], [], [], [], []
    base = 0
    sid = 0
    for nseq, length in ((N_CTX_SEQ, CTX_LEN), (N_LAT_SEQ, LAT_LEN)):
        nc = length // c
        for _ in range(nseq):
            for j in range(nc):
                fb.append(base + j)
                bb.append(base + nc - 1 - j)
                first.append(int(j == 0))
                last.append(int(j == nc - 1))
                seq.append(sid)
            base += nc
            sid += 1
    return tuple(np.asarray(t, np.int32) for t in (fb, bb, first, last, seq))


def _rwkv_scan(a_grp, s0_pairs, w0, wu_pad, a0, au_pad, kkp, ka):
    tables = _scan_tables()
    n_steps = tables[0].shape[0]
    n_seq = s0_pairs.shape[0]
    n = a_grp.shape[0]
    c = SCAN_CHUNK
    const2 = lambda s, fb, bb, fi, la, sq: (0, 0)
    const3 = lambda s, fb, bb, fi, la, sq: (0, 0, 0)
    grid_spec = pltpu.PrefetchScalarGridSpec(
        num_scalar_prefetch=5,
        grid=(n_steps,),
        in_specs=[pl.BlockSpec((c, A_W), lambda s, fb, bb, fi, la, sq: (fb[s], 0)),
                  pl.BlockSpec((c, A_W), lambda s, fb, bb, fi, la, sq: (bb[s], 0)),
                  pl.BlockSpec((None, 2, 2, 128, 128), lambda s, fb, bb, fi, la, sq: (sq[s], 0, 0, 0, 0)),
                  pl.BlockSpec((2, GROUP_W), const2),
                  pl.BlockSpec((2, 128, GROUP_W), const3),
                  pl.BlockSpec((2, GROUP_W), const2),
                  pl.BlockSpec((2, 128, GROUP_W), const3),
                  pl.BlockSpec((1, GROUP_W), const2),
                  pl.BlockSpec((1, GROUP_W), const2)],
        out_specs=[pl.BlockSpec((c, GROUP_W), lambda s, fb, bb, fi, la, sq: (fb[s], 0)),
                   pl.BlockSpec((c, GROUP_W), lambda s, fb, bb, fi, la, sq: (bb[s], 0)),
                   pl.BlockSpec((None, 2, 2, 128, 128), lambda s, fb, bb, fi, la, sq: (sq[s], 0, 0, 0, 0))],
        scratch_shapes=[pltpu.VMEM((2, 2, 128, 128), F32)],
    )
    return pl.pallas_call(
        _scan_body,
        grid_spec=grid_spec,
        out_shape=[jax.ShapeDtypeStruct((n, GROUP_W), F32), jax.ShapeDtypeStruct((n, GROUP_W), F32),
                   jax.ShapeDtypeStruct((n_seq, 2, 2, 128, 128), F32)],
        compiler_params=_cparams(("arbitrary",)),
        name="rwkv_scan",
    )(*[jnp.asarray(t) for t in tables], a_grp, a_grp, s0_pairs, w0, wu_pad, a0, au_pad, kkp, ka)


MLA_SCALE = MLA_QK_DIM ** -0.5
GQA_SCALE = HEAD_DIM ** -0.5


def _mla_keys(ckv_b, rope_slot, rope_slot_sw, wk_ref, kng_ref, cosm, sinm, k_out_ref):
    g = kng_ref[0:1, :]
    g_sw = kng_ref[1:2, :]
    for h in range(4):
        nope = _dot(ckv_b, wk_ref[h])
        kr = nope + rope_slot
        rs = lax.rsqrt(jnp.sum(kr * kr, axis=-1, keepdims=True) * (1.0 / MLA_QK_DIM) + EPS)
        if cosm is None:
            k_out_ref[h] = (kr * rs * g).astype(BF16)
        else:
            ks = nope + rope_slot_sw
            k_out_ref[h] = ((kr * rs * g) * cosm + (ks * rs * g_sw) * sinm).astype(BF16)


def _prep_body(b_ref, c_ref, cosm_ref, sinm_ref, cosg_ref, sing_ref,
               qag_ref, wqq_ref, qng_ref, kvag_ref, wk_ref, wv_ref, kng_ref, gqg_ref, gkg_ref,
               qm_ref, km_ref, vm_ref, ckvn_ref, qg_ref, kg_ref, vg_ref, kgn_ref):
    cosm = cosm_ref[...]
    sinm = sinm_ref[...]
    qc = b_ref[:, 0:256]
    qn = qc * lax.rsqrt(jnp.sum(qc * qc, axis=-1, keepdims=True) * (1.0 / MLA_Q_RANK) + EPS) * qag_ref[...]
    qnb = qn.astype(BF16)
    g = qng_ref[0:1, :]
    g_sw = qng_ref[1:2, :]
    for h in range(4):
        qr = _dot(qnb, wqq_ref[h])
        qs = _dot(qnb, wqq_ref[4 + h])
        rs = lax.rsqrt(jnp.sum(qr * qr, axis=-1, keepdims=True) * (1.0 / MLA_QK_DIM) + EPS)
        qm_ref[h] = (((qr * rs * g) * cosm + (qs * rs * g_sw) * sinm) * MLA_SCALE).astype(BF16)
    ckv = _rms_rows(b_ref[:, 256:384], kvag_ref[...])
    ckvn_ref[...] = ckv
    cb = ckv.astype(BF16)
    _mla_keys(cb, b_ref[:, 384:512], b_ref[:, 512:640], wk_ref, kng_ref, cosm, sinm, km_ref)
    for p in range(2):
        vm_ref[p] = _dot_nt(wv_ref[p], cb).astype(BF16)

    tm = c_ref.shape[0]
    avg = _group_matrix(128, HEAD_DIM, 1.0 / HEAD_DIM)
    lane = lax.broadcasted_iota(jnp.int32, (tm, 128), 1)
    first_half = (lane % HEAD_DIM) < (HEAD_DIM // 2)
    low = lane < HEAD_DIM
    cosg = cosg_ref[...]
    sing = sing_ref[...]

    def rotate(xn):
        swapped = jnp.where(first_half, pltpu.roll(xn, 128 - HEAD_DIM // 2, 1), pltpu.roll(xn, HEAD_DIM // 2, 1))
        return xn * cosg + swapped * sing

    for blk in range(2):
        x = c_ref[:, 128 * blk:128 * blk + 128]
        xn = x * lax.rsqrt(_dot(x * x, avg, precision=HI) + EPS) * gqg_ref[...]
        xr = rotate(xn) * GQA_SCALE
        qg_ref[blk] = jnp.where(low, xr, 0.0).astype(BF16)
        qg_ref[blk + 2] = jnp.where(low, 0.0, xr).astype(BF16)
    xk = c_ref[:, 256:384]
    kn = xk * lax.rsqrt(_dot(xk * xk, avg, precision=HI) + EPS) * gkg_ref[...]
    kgn_ref[...] = kn
    kg_ref[...] = rotate(kn).astype(BF16)
    vg_ref[...] = c_ref[:, 384:512].T.astype(BF16)


def _attn_prep(b_grp, c_grp, tabs, lw):
    n = b_grp.shape[0]
    n_ctx_tiles = N_CTX_TOK // TILE
    lat_tiles = LAT_LEN // TILE
    row = lambda i: (i, 0)
    hrow = lambda i: (0, i, 0)
    tab = lambda i: (jnp.where(i < n_ctx_tiles, 0, 1 + (i - n_ctx_tiles) % lat_tiles), 0)
    c2 = lambda i: (0, 0)
    c3 = lambda i: (0, 0, 0)
    return pl.pallas_call(
        _prep_body,
        grid=(n // TILE,),
        in_specs=[pl.BlockSpec((TILE, B_W), row), pl.BlockSpec((TILE, C_W), row),
                  pl.BlockSpec((TILE, 128), tab), pl.BlockSpec((TILE, 128), tab),
                  pl.BlockSpec((TILE, 128), tab), pl.BlockSpec((TILE, 128), tab),
                  pl.BlockSpec((1, 256), c2), pl.BlockSpec((8, 256, 128), c3), pl.BlockSpec((2, 128), c2),
                  pl.BlockSpec((1, 128), c2), pl.BlockSpec((4, 128, 128), c3), pl.BlockSpec((2, 128, 128), c3),
                  pl.BlockSpec((2, 128), c2), pl.BlockSpec((1, 128), c2), pl.BlockSpec((1, 128), c2)],
        out_specs=[pl.BlockSpec((4, TILE, 128), hrow), pl.BlockSpec((4, TILE, 128), hrow),
                   pl.BlockSpec((2, 128, TILE), lambda i: (0, 0, i)), pl.BlockSpec((TILE, 128), row),
                   pl.BlockSpec((4, TILE, 128), hrow), pl.BlockSpec((TILE, 128), row),
                   pl.BlockSpec((128, TILE), lambda i: (0, i)), pl.BlockSpec((TILE, 128), row)],
        out_shape=[jax.ShapeDtypeStruct((4, n, 128), BF16), jax.ShapeDtypeStruct((4, n, 128), BF16),
                   jax.ShapeDtypeStruct((2, 128, n), BF16), jax.ShapeDtypeStruct((n, 128), F32),
                   jax.ShapeDtypeStruct((4, n, 128), BF16), jax.ShapeDtypeStruct((n, 128), BF16),
                   jax.ShapeDtypeStruct((128, n), BF16), jax.ShapeDtypeStruct((n, 128), F32)],
        compiler_params=_cparams(("arbitrary",)),
        name="attn_prep",
    )(b_grp, c_grp, tabs["cosm"], tabs["sinm"], tabs["cosg"], tabs["sing"],
      lw["qa_g"], lw["wqq"], lw["qn_g2"], lw["kva_g"], lw["wk"], lw["wv"], lw["kn_g2"], lw["gq_g"], lw["gk_g"])


def _ctxkv_body(ckv_ref, krp_ref, wk_ref, wv_ref, kng_ref, k_ref, v_ref):
    cb = ckv_ref[...].astype(BF16)
    _mla_keys(cb, krp_ref[...], None, wk_ref, kng_ref, None, None, k_ref)
    for p in range(2):
        v_ref[p] = _dot_nt(wv_ref[p], cb).astype(BF16)


def _mla_ctx_kv(cache_ckv, cache_krope_placed, wk, wv, kn_g2):
    nb, depth, plen, _ = cache_ckv.shape
    return pl.pallas_call(
        _ctxkv_body,
        grid=(depth, nb),
        in_specs=[pl.BlockSpec((None, None, plen, 128), lambda l, b: (b, l, 0, 0)),
                  pl.BlockSpec((None, None, plen, 128), lambda l, b: (b, l, 0, 0)),
                  pl.BlockSpec((None, 4, 128, 128), lambda l, b: (l, 0, 0, 0)),
                  pl.BlockSpec((None, 2, 128, 128), lambda l, b: (l, 0, 0, 0)),
                  pl.BlockSpec((None, 2, 128), lambda l, b: (l, 0, 0))],
        out_specs=[pl.BlockSpec((None, 4, None, plen, 128), lambda l, b: (l, 0, b, 0, 0)),
                   pl.BlockSpec((None, 2, None, 128, plen), lambda l, b: (l, 0, b, 0, 0))],
        out_shape=[jax.ShapeDtypeStruct((depth, 4, nb, plen, 128), BF16),
                   jax.ShapeDtypeStruct((depth, 2, nb, 128, plen), BF16)],
        compiler_params=_cparams(("arbitrary", "arbitrary")),
        name="mla_ctx_kv",
    )(cache_ckv, cache_krope_placed, wk, wv, kn_g2)


def _attn_body(*refs, has_ctx):
    if has_ctx:
        qa_ref, qb_ref, ka_ref, kb_ref, vt_ref, kca_ref, kcb_ref, vct_ref, o_ref, sa_scr, sb_scr = refs
    else:
        qa_ref, qb_ref, ka_ref, kb_ref, vt_ref, o_ref, sa_scr, sb_scr = refs
        kca_ref = kcb_ref = vct_ref = None
    tk = ka_ref.shape[0]
    chunk = min(ATT_CHUNK, tk)
    q = (qa_ref[...], qb_ref[...])
    k_refs = (ka_ref, kb_ref)
    kc_refs = (kca_ref, kcb_ref)
    s_scr = (sa_scr, sb_scr)
    vrows = (slice(0, HEAD_DIM), slice(HEAD_DIM, 2 * HEAD_DIM))
    spans = [("new", c * chunk, chunk) for c in range(tk // chunk)]
    if has_ctx:
        spans = [("ctx", 0, kca_ref.shape[0])] + spans

    def scratch_rows(kind, start, size):
        base = tk if kind == "ctx" else 0
        return slice(base + start, base + start + size)

    m = [None, None]
    for kind, start, size in spans:
        for hd in range(2):
            if kind == "ctx":
                keys = kc_refs[hd][...].astype(BF16)
            else:
                keys = k_refs[hd][start:start + size, :]
            s = _dot_nt(keys, q[hd])
            s_scr[hd][scratch_rows(kind, start, size), :] = s
            cm = jnp.max(s, axis=0, keepdims=True)
            m[hd] = cm if m[hd] is None else jnp.maximum(m[hd], cm)

    den = [None, None]
    acc = [None, None]
    for kind, start, size in spans:
        for hd in range(2):
            e = jnp.exp(s_scr[hd][scratch_rows(kind, start, size), :] - m[hd])
            if kind == "ctx":
                vals = vct_ref[vrows[hd], :].astype(BF16)
            else:
                vals = vt_ref[vrows[hd], start:start + size]
            part = _dot(vals, e.astype(BF16))
            rs = jnp.sum(e, axis=0, keepdims=True)
            den[hd] = rs if den[hd] is None else den[hd] + rs
            acc[hd] = part if acc[hd] is None else acc[hd] + part
    out_t = jnp.concatenate([acc[0] / den[0], acc[1] / den[1]], axis=0)
    o_ref[...] = out_t.T.astype(o_ref.dtype)


def _attention(q, ka, kb, v, head_a, head_b, k_head_a, k_head_b, v_idx, n_seq, seq_len, tok0, ctx=None):
    tq = min(ATT_TQ, seq_len)
    nq = seq_len // tq
    seq0 = tok0 // seq_len
    q0 = tok0 // tq

    def qmap(hsel):
        return lambda b, p, i: (hsel(p), q0 + b * nq + i, 0)

    def kmap(hsel, arr):
        if arr.ndim == 3:
            return lambda b, p, i: (hsel(p), seq0 + b, 0)
        return lambda b, p, i: (seq0 + b, 0)

    def kspec(arr, hsel):
        if arr.ndim == 3:
            return pl.BlockSpec((None, seq_len, 128), kmap(hsel, arr))
        return pl.BlockSpec((seq_len, 128), kmap(hsel, arr))

    if v.ndim == 3:
        vspec = pl.BlockSpec((None, 128, seq_len), lambda b, p, i: (v_idx(p), 0, seq0 + b))
    else:
        vspec = pl.BlockSpec((128, seq_len), lambda b, p, i: (0, seq0 + b))
    in_specs = [pl.BlockSpec((None, tq, 128), qmap(head_a)), pl.BlockSpec((None, tq, 128), qmap(head_b)),
                kspec(ka, k_head_a), kspec(kb, k_head_b), vspec]
    args = [q, q, ka, kb, v]
    if ctx is not None:
        kca, kcb, vc = ctx
        past = vc.shape[-1]

        def cspec(arr, hsel, shape):
            if arr.ndim == 4:
                return pl.BlockSpec((None, None) + shape, lambda b, p, i: (hsel(p), b, 0, 0))
            return pl.BlockSpec((None,) + shape, lambda b, p, i: (b, 0, 0))

        in_specs += [cspec(kca, k_head_a, (past, 128)), cspec(kcb, k_head_b, (past, 128)),
                     cspec(vc, v_idx, (128, past))]
        args += [kca, kcb, vc]
    return pl.pallas_call(
        functools.partial(_attn_body, has_ctx=ctx is not None),
        grid=(n_seq, 2, nq),
        in_specs=in_specs,
        out_specs=pl.BlockSpec((tq, 128), lambda b, p, i: (b * nq + i, p)),
        out_shape=jax.ShapeDtypeStruct((n_seq * seq_len, 256), BF16),
        scratch_shapes=[pltpu.VMEM((seq_len + (0 if ctx is None else ctx[2].shape[-1]), tq), F32)] * 2,
        compiler_params=_cparams(("arbitrary", "arbitrary", "arbitrary")),
        name="attention_ctx" if ctx is None else "attention_lat",
    )(*args)


def _out_body(x_ref, mod_ref, a_ref, yf_ref, yb_ref, obc_ref, obl_ref, occ_ref, ocl_ref,
              d_ref, dprev_ref, dnext_ref, wo_ref, gu_ref, rk_ref, gn_ref, cw_ref, cb_ref, n2_ref, wq_ref,
              x1_ref, h2_ref, q_ref, *, n_ctx_tiles):
    i = pl.program_id(0)
    is_ctx = i < n_ctx_tiles
    tm = x_ref.shape[0]
    mod = mod_ref[...]
    gate1 = mod[:, 2 * D_MODEL:3 * D_MODEL]
    shift2 = mod[:, 3 * D_MODEL:4 * D_MODEL]
    scale2 = mod[:, 4 * D_MODEL:5 * D_MODEL]

    r = a_ref[:, 0:256]
    k = a_ref[:, 256:512]
    v = a_ref[:, 512:768]
    gd = a_ref[:, 1024:1152]
    y = yf_ref[...] + yb_ref[...]
    avg = _group_matrix(GROUP_W, HEAD_DIM, 1.0 / HEAD_DIM)
    ones = _group_matrix(GROUP_W, HEAD_DIM, 1.0)
    yn = y * lax.rsqrt(_dot(y * y, avg, precision=HI) + EPS) * gn_ref[...]
    bonus = _dot(r * k * rk_ref[...], ones, precision=HI) * v
    gate = _dot(jax.nn.sigmoid(gd).astype(BF16), gu_ref[...])
    o_a = (yn + bonus) * gate

    o_b = jnp.where(is_ctx, obc_ref[...], obl_ref[...])
    o_c = jnp.where(is_ctx, occ_ref[...], ocl_ref[...])

    u = d_ref[:, 512:768] * d_ref[:, 0:256]
    u_prev = dprev_ref[:, 512:768] * dprev_ref[:, 0:256]
    u_next = dnext_ref[:, 512:768] * dnext_ref[:, 0:256]
    rows = lax.broadcasted_iota(jnp.int32, (tm, GROUP_W), 0)
    up = jnp.where(rows == 0, u_prev, pltpu.roll(u, 1, 0))
    un = jnp.where(rows == tm - 1, u_next, pltpu.roll(u, tm - 1, 0))
    conv = up * cw_ref[0:1, :] + u * cw_ref[1:2, :] + un * cw_ref[2:3, :] + cb_ref[...]
    o_d = d_ref[:, 256:512] * conv

    mix_in = jnp.concatenate([o_a.astype(BF16), o_b, o_c, o_d.astype(BF16)], axis=1)
    x1 = x_ref[...] + gate1 * _dot(mix_in, wo_ref[...])
    x1_ref[...] = x1
    h2 = (_rms_rows(x1, n2_ref[...]) * (1.0 + scale2) + shift2).astype(BF16)
    h2_ref[...] = h2
    q_ref[...] = _dot(h2, wq_ref[...])


def _out_proj(x, mod, a_grp, yf, yb, ob_ctx, ob_lat, oc_ctx, oc_lat, d_grp, d_prev, d_next, lw, w_out_all, wq_all,
              layer):
    n = x.shape[0]
    n_tiles = n // TILE
    n_ctx_tiles = N_CTX_TOK // TILE
    tpg = COND_GROUP // TILE
    row = lambda i: (i, 0)
    ctx_row = lambda i: (jnp.minimum(i, n_ctx_tiles - 1), 0)
    lat_row = lambda i: (jnp.maximum(i - n_ctx_tiles, 0), 0)
    halo = lambda i: (i, 0, 0)
    c2 = lambda i: (0, 0)
    nq = PEER_HEADS * 2 * PEER_KEYS
    return pl.pallas_call(
        functools.partial(_out_body, n_ctx_tiles=n_ctx_tiles),
        grid=(n_tiles,),
        in_specs=[pl.BlockSpec((TILE, D_MODEL), row),
                  pl.BlockSpec((None, 1, 6 * D_MODEL), lambda i: (i // tpg, 0, 0)),
                  pl.BlockSpec((TILE, A_W), row),
                  pl.BlockSpec((TILE, GROUP_W), row), pl.BlockSpec((TILE, GROUP_W), row),
                  pl.BlockSpec((TILE, GROUP_W), ctx_row), pl.BlockSpec((TILE, GROUP_W), lat_row),
                  pl.BlockSpec((TILE, GROUP_W), ctx_row), pl.BlockSpec((TILE, GROUP_W), lat_row),
                  pl.BlockSpec((TILE, D_W), row),
                  pl.BlockSpec((None, 1, D_W), halo), pl.BlockSpec((None, 1, D_W), halo),
                  pl.BlockSpec((None, D_MODEL, D_MODEL), lambda i: (layer, 0, 0)), pl.BlockSpec((128, GROUP_W), c2),
                  pl.BlockSpec((1, GROUP_W), c2), pl.BlockSpec((1, GROUP_W), c2),
                  pl.BlockSpec((3, GROUP_W), c2), pl.BlockSpec((1, GROUP_W), c2),
                  pl.BlockSpec((1, D_MODEL), c2), pl.BlockSpec((None, D_MODEL, nq), lambda i: (layer, 0, 0))],
        out_specs=[pl.BlockSpec((TILE, D_MODEL), row), pl.BlockSpec((TILE, D_MODEL), row),
                   pl.BlockSpec((TILE, nq), row)],
        out_shape=[jax.ShapeDtypeStruct((n, D_MODEL), F32), jax.ShapeDtypeStruct((n, D_MODEL), BF16),
                   jax.ShapeDtypeStruct((n, nq), F32)],
        compiler_params=_cparams(("arbitrary",)),
        name="out_proj",
    )(x, mod, a_grp, yf, yb, ob_ctx, ob_lat, oc_ctx, oc_lat, d_grp, d_prev, d_next,
      w_out_all, lw["gu"], lw["rk"], lw["gn"], lw["conv_w"], lw["conv_b"], lw["norm2_g"], wq_all)


def _exchange(a, b):
    if a is None:
        return b, None
    if b is None:
        return a, None
    return jnp.maximum(a, b), jnp.minimum(a, b)


def _sort16_desc(xs):
    xs = list(xs)
    k = 2
    while k <= 16:
        j = k // 2
        while j >= 1:
            for i in range(16):
                partner = i ^ j
                if partner > i:
                    hi, lo = _exchange(xs[i], xs[partner])
                    xs[i], xs[partner] = (hi, lo) if (i & k) == 0 else (lo, hi)
            j //= 2
        k *= 2
    return xs


def _bitonic_merge_desc(xs):
    xs = list(xs)
    j = 8
    while j >= 1:
        for i in range(16):
            partner = i ^ j
            if partner > i:
                xs[i], xs[partner] = _exchange(xs[i], xs[partner])
        j //= 2
    return xs


def _top16_of_rows(xs):
    ys = _sort16_desc(xs)
    for shift in (4, 2, 1):
        zs = [None if y is None else pltpu.roll(y, shift, 0) for y in ys]
        ts = [_exchange(ys[i], zs[15 - i])[0] for i in range(16)]
        ys = _bitonic_merge_desc(ts)
    return ys


def _topk_body(q_ref, k1_ref, k2_ref, thr_ref, s2_ref, e1_ref, e2_ref):
    tn = q_ref.shape[0]
    sub = lax.broadcasted_iota(jnp.int32, (8, tn), 0)

    def spread(vals):
        out = vals[7]
        for s in range(6, -1, -1):
            out = jnp.where(sub == s, vals[s], out)
        return out

    for h in range(PEER_HEADS):
        qa = q_ref[:, (2 * h) * PEER_KEYS:(2 * h + 1) * PEER_KEYS].astype(BF16)
        qb = q_ref[:, (2 * h + 1) * PEER_KEYS:(2 * h + 2) * PEER_KEYS].astype(BF16)
        s1 = _dot_nt(k1_ref[h], qa)
        s2 = _dot_nt(k2_ref[h], qb)
        v1 = _top16_of_rows([s1[8 * i:8 * i + 8, :] for i in range(16)])
        v2 = _top16_of_rows([s2[8 * i:8 * i + 8, :] for i in range(16)])
        v2_lo, v2_hi, v1_hi = spread(v2[0:8]), spread(v2[8:16]), spread(v1[8:16])
        cands = ([v1[0] + v2_lo, v1[0] + v2_hi] + [v1[a] + v2_lo for a in range(1, 8)]
                 + [v1_hi + v2[0]] + [None] * 6)
        best = _top16_of_rows(cands)
        tau = best[PEER_TOPK - 1][0:1, :]
        zsum = jnp.exp(best[0] - best[0])
        for kth in range(1, PEER_TOPK):
            zsum = zsum + jnp.exp(best[kth] - best[0])
        thr = jnp.full(s1.shape, jnp.inf, F32)
        for b in range(PEER_TOPK):
            vb = v2[b][0:1, :]
            thr = jnp.where(s1 + vb >= tau, vb, thr)
        thr_ref[h] = thr
        s2_ref[h] = s2
        e1_ref[h] = jnp.exp(s1 - v1[0][0:1, :])
        e2_ref[h] = jnp.exp(s2 - v2[0][0:1, :]) * (0.5 / zsum[0:1, :])


def _peer_topk(q, k1, k2):
    n = q.shape[0]
    tn = TOPK_TN
    big = pl.BlockSpec((PEER_HEADS, PEER_KEYS, tn), lambda i: (0, 0, i))
    big_shape = jax.ShapeDtypeStruct((PEER_HEADS, PEER_KEYS, n), F32)
    c3 = lambda i: (0, 0, 0)
    return pl.pallas_call(
        _topk_body,
        grid=(n // tn,),
        in_specs=[pl.BlockSpec((tn, PEER_HEADS * 2 * PEER_KEYS), lambda i: (i, 0)),
                  pl.BlockSpec((PEER_HEADS, PEER_KEYS, PEER_KEYS), c3),
                  pl.BlockSpec((PEER_HEADS, PEER_KEYS, PEER_KEYS), c3)],
        out_specs=[big, big, big, big],
        out_shape=[big_shape, big_shape, big_shape, big_shape],
        compiler_params=_cparams(("arbitrary",)),
        name="peer_topk",
    )(q, k1, k2)


GELU_C0 = 0.7978845608028654
GELU_C1 = 0.044715


PEER_PIECE = 256
PEER_VGROUP = 256


def _dense_body(h2_ref, u_ref, vt_ref, thr_ref, s2_ref, e1_ref, e2_ref, x1_ref, mod_ref,
                o_ref, acc, *piece_scratch):
    j = pl.program_id(1)

    @pl.when(j == 0)
    def _():
        acc[...] = jnp.zeros_like(acc)

    n_pieces = u_ref.shape[0] // PEER_PIECE
    hs, gs = piece_scratch[:n_pieces], piece_scratch[n_pieces:]
    per_group = PEER_VGROUP // PEER_PIECE
    tn = hs[0].shape[1]
    h2 = h2_ref[...]

    def pre_activations(p):
        rows = slice(p * PEER_PIECE, (p + 1) * PEER_PIECE)
        hs[p][...] = _dot_nt(u_ref[rows, :], h2)

    pre_activations(0)
    for p in range(n_pieces):
        if p + 1 < n_pieces:
            pre_activations(p + 1)
        halves = PEER_PIECE // PEER_KEYS
        for sub in range(2 * (tn // 128)):
            keys = slice((sub % 2) * 64, (sub % 2) * 64 + 64)
            cols = slice((sub // 2) * 128, (sub // 2) * 128 + 128)
            n_grp = 64 // 8
            w = [[None] * n_grp for _ in range(halves)]
            for h in range(PEER_HEADS):
                s2t = s2_ref[h, keys, cols]
                e2t = e2_ref[h, keys, cols]
                for half in range(halves):
                    ii = p * halves + half
                    thr8 = jnp.broadcast_to(thr_ref[h, ii:ii + 1, cols], (8, 128))
                    e18 = jnp.broadcast_to(e1_ref[h, ii:ii + 1, cols], (8, 128))
                    for g in range(n_grp):
                        grp = slice(8 * g, 8 * g + 8)
                        term = jnp.where(s2t[grp] >= thr8, e18 * e2t[grp], 0.0)
                        w[half][g] = term if w[half][g] is None else w[half][g] + term
            for half in range(halves):
                rows = slice(half * PEER_KEYS + keys.start, half * PEER_KEYS + keys.stop)
                x = hs[p][rows, cols]
                inner = x * (GELU_C0 + (GELU_C0 * GELU_C1) * (x * x))
                wh = jnp.concatenate(w[half], axis=0)
                grows = slice((p % per_group) * PEER_PIECE + rows.start, (p % per_group) * PEER_PIECE + rows.stop)
                gs[p // per_group][grows, cols] = (wh * x * (1.0 + jnp.tanh(inner))).astype(BF16)
        if (p + 1) % per_group == 0:
            g = p // per_group
            vrows = slice(g * PEER_VGROUP, (g + 1) * PEER_VGROUP)
            acc[...] += _dot(vt_ref[:, vrows], gs[g][...])

    @pl.when(j == pl.num_programs(1) - 1)
    def _():
        gate2 = mod_ref[:, 5 * D_MODEL:6 * D_MODEL]
        o_ref[...] = x1_ref[...] + gate2 * acc[...].T


def _peer_dense(h2, u_all, vt_all, layer, thr, s2, e1, e2, x1, mod):
    n = h2.shape[0]
    tn, te = PEER_TN, PEER_TE
    rows_per_step = te // PEER_KEYS
    tpg = COND_GROUP // tn
    tok = lambda i, j: (i, 0)
    key_rows = pl.BlockSpec((PEER_HEADS, rows_per_step, tn), lambda i, j: (0, j, i))
    key_all = pl.BlockSpec((PEER_HEADS, PEER_KEYS, tn), lambda i, j: (0, 0, i))
    return pl.pallas_call(
        _dense_body,
        grid=(n // tn, N_EXPERTS // te),
        in_specs=[pl.BlockSpec((tn, D_MODEL), tok),
                  pl.BlockSpec((None, te, D_MODEL), lambda i, j: (layer, j, 0)),
                  pl.BlockSpec((None, D_MODEL, te), lambda i, j: (layer, 0, j)),
                  key_rows, key_all, key_rows, key_all,
                  pl.BlockSpec((tn, D_MODEL), tok),
                  pl.BlockSpec((None, 1, 6 * D_MODEL), lambda i, j: (i // tpg, 0, 0))],
        out_specs=pl.BlockSpec((tn, D_MODEL), tok),
        out_shape=jax.ShapeDtypeStruct((n, D_MODEL), F32),
        scratch_shapes=([pltpu.VMEM((D_MODEL, tn), F32)]
                        + [pltpu.VMEM((PEER_PIECE, tn), F32)] * (te // PEER_PIECE)
                        + [pltpu.VMEM((PEER_VGROUP, tn), BF16)] * (te // PEER_VGROUP)),
        compiler_params=_cparams(("arbitrary", "arbitrary")),
        name="peer_dense",
    )(h2, u_all, vt_all, thr, s2, e1, e2, x1, mod)


def _in_proj_columns():
    src = np.full((IN_PAD,), -1, np.int64)

    def put(dst, start, width):
        src[dst:dst + width] = np.arange(start, start + width)

    put(0, 0, 768)
    put(768, 768, 128)
    put(896, 896, 64)
    put(1024, 960, 128)
    b0 = A_W
    put(b0, 1088, 192)
    put(b0 + 256, 1280, 128)
    put(b0 + 384 + 64, 1408, 32)
    put(b0 + 512 + 64, 1408 + 16, 16)
    put(b0 + 512 + 80, 1408, 16)
    c0 = A_W + B_W
    for slot, head in enumerate((0, 2, 1, 3)):
        put(c0 + 64 * slot, 1440 + 64 * head, 64)
    put(c0 + 256, 1696, 256)
    put(A_W + B_W + C_W, 1952, 768)
    return src


def _swap_tail(w):
    return jnp.concatenate([w[..., :64], w[..., 80:96], w[..., 64:80]], axis=-1)


def _pad_last(w, width):
    return jnp.pad(w, [(0, 0)] * (w.ndim - 1) + [(0, width - w.shape[-1])])


STACKED_WEIGHTS = ("w_in", "w_out", "wq", "u", "vt")


def _layer_weights(p):
    depth = p["w_in"].shape[0]
    src = _in_proj_columns()
    runs, start = [], 0
    for i in range(1, IN_PAD + 1):
        if i == IN_PAD or (src[i] != src[i - 1] + 1 if src[i - 1] >= 0 else src[i] >= 0):
            runs.append((start, i))
            start = i
    w_bf = p["w_in"].astype(BF16)
    pieces = [w_bf[:, :, src[a]:src[a] + (b - a)] if src[a] >= 0
              else jnp.zeros((depth, D_MODEL, b - a), BF16) for a, b in runs]
    w_in = jnp.concatenate(pieces, axis=2)

    wuq = p["mla_wuq"].reshape(depth, MLA_Q_RANK, 4, MLA_QK_DIM).transpose(0, 2, 1, 3)
    wq_plain = jnp.pad(wuq, ((0, 0), (0, 0), (0, 256 - MLA_Q_RANK), (0, 128 - MLA_QK_DIM)))
    wq_swap = jnp.pad(_swap_tail(wuq), ((0, 0), (0, 0), (0, 256 - MLA_Q_RANK), (0, 128 - MLA_QK_DIM)))
    wukv = p["mla_wukv"].reshape(depth, 128, 4, 128)
    wk = _pad_last(wukv[..., :64].transpose(0, 2, 1, 3), 128)
    wv_heads = wukv[..., 64:].transpose(0, 2, 1, 3)
    wv = jnp.concatenate([wv_heads[:, 0::2], wv_heads[:, 1::2]], axis=-1)

    def gain2(g):
        return jnp.stack([_pad_last(g, 128), _pad_last(_swap_tail(g), 128)], axis=1)

    w_out = p["w_out"]
    oc = w_out[:, 512:768].reshape(depth, 4, 64, D_MODEL)[:, jnp.asarray([0, 2, 1, 3])].reshape(depth, 256, D_MODEL)
    w_out = jnp.concatenate([w_out[:, :512], oc, w_out[:, 768:]], axis=1).astype(BF16)

    wu_pad = jnp.zeros((depth, 2, 128, GROUP_W), F32)
    wu_pad = wu_pad.at[:, 0, 0:64].set(p["rw_wu"][:, 0]).at[:, 1, 64:128].set(p["rw_wu"][:, 1])
    au_pad = jnp.zeros((depth, 2, 128, GROUP_W), F32)
    au_pad = au_pad.at[:, 0, 0:32].set(p["rw_au"][:, 0]).at[:, 1, 32:64].set(p["rw_au"][:, 1])

    return {
        "w_in": w_in,
        "norm1_g": p["norm1_g"][:, None, :],
        "norm2_g": p["norm2_g"][:, None, :],
        "w_out": w_out,
        "w0": p["rw_w0"], "a0": p["rw_a0"],
        "wu": wu_pad.astype(BF16), "au": au_pad.astype(BF16),
        "kkp": p["rw_kk"][:, None, :], "ka": p["rw_ka"][:, None, :],
        "gu": p["rw_gu"].astype(BF16),
        "rk": p["rw_rk"].reshape(depth, 1, GROUP_W),
        "gn": p["rw_gn"][:, None, :],
        "qa_g": _pad_last(p["mla_qa_g"], 256)[:, None, :],
        "wqq": jnp.concatenate([wq_plain, wq_swap], axis=1).astype(BF16),
        "qn_g2": gain2(p["mla_qn_g"]),
        "kva_g": p["mla_kva_g"][:, None, :],
        "wk": wk.astype(BF16), "wv": jnp.swapaxes(wv, -1, -2).astype(BF16),
        "kn_g2": gain2(p["mla_kn_g"]),
        "gq_g": jnp.tile(p["gqa_qn_g"], (1, 2))[:, None, :],
        "gk_g": jnp.tile(p["gqa_kn_g"], (1, 2))[:, None, :],
        "conv_w": p["conv_w"], "conv_b": p["conv_b"][:, None, :],
        "wq": p["peer_wq"].astype(BF16),
        "k1": p["peer_k1"].astype(BF16), "k2": p["peer_k2"].astype(BF16),
        "u": p["peer_u"].astype(BF16),
        "vt": jnp.swapaxes(p["peer_v"], 1, 2).astype(BF16),
    }


def _rope_tables():
    t = jnp.arange(LAT_LEN, dtype=F32)
    grid_row = jnp.floor(t / GRID_W)
    grid_col = t - grid_row * GRID_W

    def angles(rot_dim):
        n_freq = rot_dim // 4
        freqs = ROPE_THETA ** (-jnp.arange(n_freq, dtype=F32) / n_freq)
        ang = jnp.concatenate([grid_row[:, None] * freqs, grid_col[:, None] * freqs], axis=-1)
        return jnp.cos(ang), jnp.sin(ang)

    cm, sm = angles(32)
    ones64 = jnp.ones((LAT_LEN, 64), F32)
    zeros64 = jnp.zeros((LAT_LEN, 64), F32)
    cosm = jnp.concatenate([ones64, cm, cm, ones64[:, :32]], axis=-1)
    sinm = jnp.concatenate([zeros64, -sm, sm, zeros64[:, :32]], axis=-1)
    cg, sg = angles(64)
    cosg = jnp.tile(jnp.concatenate([cg, cg], axis=-1), (1, 2))
    sing = jnp.tile(jnp.concatenate([-sg, sg], axis=-1), (1, 2))
    ident = jnp.ones((TILE, 128), F32)
    zero = jnp.zeros((TILE, 128), F32)
    return {"cosm": jnp.concatenate([ident, cosm]), "sinm": jnp.concatenate([zero, sinm]),
            "cosg": jnp.concatenate([ident, cosg]), "sing": jnp.concatenate([zero, sing])}


def _states_to_pairs(s):
    lead = s.shape[:-3]
    s = s.reshape(lead + (2, 2, 64, 64))
    z = jnp.zeros(lead + (2, 64, 64), s.dtype)
    top = jnp.concatenate([s[..., 0, :, :], z], axis=-1)
    bot = jnp.concatenate([z, s[..., 1, :, :]], axis=-1)
    return jnp.concatenate([top, bot], axis=-2)


def _pairs_to_states(sp):
    lead = sp.shape[:-3]
    a = sp[..., 0:64, 0:64]
    b = sp[..., 64:128, 64:128]
    return jnp.stack([a, b], axis=-3).reshape(lead + (4, 64, 64))


def _conv_halos(d_grp):
    n_tiles = d_grp.shape[0] // TILE
    tiles = d_grp.reshape(n_tiles, TILE, D_W)
    first_rows = tiles[:, 0, :]
    last_rows = tiles[:, TILE - 1, :]
    zero = jnp.zeros((1, D_W), d_grp.dtype)
    prev = jnp.concatenate([zero, last_rows[:-1]], axis=0)
    nxt = jnp.concatenate([first_rows[1:], zero], axis=0)
    idx = np.arange(n_tiles)
    n_ctx_tiles = N_CTX_TOK // TILE
    per_seq = LAT_LEN // TILE
    lat_pos = (idx - n_ctx_tiles) % per_seq
    seq_start = np.where(idx < n_ctx_tiles, True, lat_pos == 0)
    seq_end = np.where(idx < n_ctx_tiles, True, lat_pos == per_seq - 1)
    prev = jnp.where(jnp.asarray(seq_start)[:, None], 0.0, prev)
    nxt = jnp.where(jnp.asarray(seq_end)[:, None], 0.0, nxt)
    return prev[:, None, :], nxt[:, None, :]


def kernel(x_prompt, x_sample, state_rwkv, cache_mla_ckv, cache_mla_krope, cache_gqa_k, cache_gqa_v, c, c_ctx, norm1_g, norm2_g, w_mod, b_mod, w_in, w_out, rw_w0, rw_wu, rw_a0, rw_au, rw_gu, rw_kk, rw_ka, rw_rk, rw_gn, mla_qa_g, mla_wuq, mla_kva_g, mla_wukv, mla_qn_g, mla_kn_g, gqa_qn_g, gqa_kn_g, conv_w, conv_b, peer_wq, peer_k1, peer_k2, peer_u, peer_v):
    depth = w_in.shape[0]
    params = dict(norm1_g=norm1_g, norm2_g=norm2_g, w_in=w_in, w_out=w_out, rw_w0=rw_w0, rw_wu=rw_wu,
                  rw_a0=rw_a0, rw_au=rw_au, rw_gu=rw_gu, rw_kk=rw_kk, rw_ka=rw_ka, rw_rk=rw_rk, rw_gn=rw_gn,
                  mla_qa_g=mla_qa_g, mla_wuq=mla_wuq, mla_kva_g=mla_kva_g, mla_wukv=mla_wukv,
                  mla_qn_g=mla_qn_g, mla_kn_g=mla_kn_g, gqa_qn_g=gqa_qn_g, gqa_kn_g=gqa_kn_g,
                  conv_w=conv_w, conv_b=conv_b, peer_wq=peer_wq, peer_k1=peer_k1, peer_k2=peer_k2,
                  peer_u=peer_u, peer_v=peer_v)
    lw_all = _layer_weights(params)
    tabs = _rope_tables()

    cvecs = jnp.concatenate([c_ctx[None, :], c, jnp.zeros((8 - 1 - N_LAT_SEQ, D_MODEL), F32)], axis=0)
    mods = _modulation(cvecs, w_mod, b_mod).reshape(depth, 8, 1, 6 * D_MODEL)

    x = jnp.concatenate([x_prompt.reshape(N_CTX_TOK, D_MODEL), x_sample.reshape(N_LAT_TOK, D_MODEL)], axis=0)

    s0_lat = _states_to_pairs(state_rwkv)
    s0_all = jnp.concatenate([jnp.zeros((N_CTX_SEQ,) + s0_lat.shape[1:], F32), s0_lat], axis=0)

    krope_placed = jnp.pad(cache_mla_krope, ((0, 0), (0, 0), (0, 0), (64, 32)))
    kctx_m, vctx_m = _mla_ctx_kv(cache_mla_ckv, krope_placed, lw_all["wk"], lw_all["wv"], lw_all["kn_g2"])
    past = cache_gqa_k.shape[2]
    kctx_g = cache_gqa_k.reshape(N_LAT_SEQ, depth, past, 128)
    vctx_g = jnp.swapaxes(cache_gqa_v.reshape(N_LAT_SEQ, depth, past, 128), -1, -2)

    ident = lambda p: p
    st_a, st_ckv, st_kr, st_k, st_v = [], [], [], [], []
    for l in range(depth):
        lw = {name: w[l] for name, w in lw_all.items() if name not in STACKED_WEIGHTS}
        mod = mods[l]
        a_grp, b_grp, c_grp, d_grp = _in_proj(x, mod, lw["norm1_g"], lw_all["w_in"], l)

        yf, yb, s_fin = _rwkv_scan(a_grp, s0_all[:, l], lw["w0"], lw["wu"], lw["a0"], lw["au"], lw["kkp"], lw["ka"])

        qm, km, vm, ckvn, qg, kg, vg, kgn = _attn_prep(b_grp, c_grp, tabs, lw)
        ob_ctx = _attention(qm, km, km, vm, lambda p: 2 * p, lambda p: 2 * p + 1, lambda p: 2 * p,
                            lambda p: 2 * p + 1, ident, N_CTX_SEQ, CTX_LEN, 0)
        ob_lat = _attention(qm, km, km, vm, lambda p: 2 * p, lambda p: 2 * p + 1, lambda p: 2 * p,
                            lambda p: 2 * p + 1, ident, N_LAT_SEQ, LAT_LEN, N_CTX_TOK,
                            ctx=(kctx_m[l], kctx_m[l], vctx_m[l]))
        oc_ctx = _attention(qg, kg, kg, vg, ident, lambda p: p + 2, ident, ident, ident,
                            N_CTX_SEQ, CTX_LEN, 0)
        oc_lat = _attention(qg, kg, kg, vg, ident, lambda p: p + 2, ident, ident, ident,
                            N_LAT_SEQ, LAT_LEN, N_CTX_TOK,
                            ctx=(kctx_g[:, l], kctx_g[:, l], vctx_g[:, l]))

        d_prev, d_next = _conv_halos(d_grp)
        x1, h2, q = _out_proj(x, mod, a_grp, yf, yb, ob_ctx, ob_lat, oc_ctx, oc_lat, d_grp, d_prev, d_next, lw,
                              lw_all["w_out"], lw_all["wq"], l)

        thr, s2, e1, e2 = _peer_topk(q, lw["k1"], lw["k2"])
        x = _peer_dense(h2, lw_all["u"], lw_all["vt"], l, thr, s2, e1, e2, x1, mod)

        st_a.append(_pairs_to_states(s_fin[:N_CTX_SEQ]))
        st_ckv.append(ckvn[:N_CTX_TOK].reshape(N_CTX_SEQ, CTX_LEN, 128))
        st_kr.append(b_grp[:N_CTX_TOK, 448:480].reshape(N_CTX_SEQ, CTX_LEN, 32))
        st_k.append(kgn[:N_CTX_TOK].reshape(N_CTX_SEQ, CTX_LEN, 2, HEAD_DIM))
        st_v.append(c_grp[:N_CTX_TOK, 384:512].reshape(N_CTX_SEQ, CTX_LEN, 2, HEAD_DIM))

    y_prompt = x[:N_CTX_TOK].reshape(N_CTX_SEQ, CTX_LEN, D_MODEL)
    y_sample = x[N_CTX_TOK:].reshape(N_LAT_SEQ, LAT_LEN, D_MODEL)
    return (y_prompt, y_sample, jnp.stack(st_a, axis=1), jnp.stack(st_ckv, axis=1), jnp.stack(st_kr, axis=1),
            jnp.stack(st_k, axis=1), jnp.stack(st_v, axis=1))
```

```python
import functools

import numpy as np
import jax
import jax.numpy as jnp
from jax import lax
from jax.experimental import pallas as pl
from jax.experimental.pallas import tpu as pltpu

F32 = jnp.float32
BF16 = jnp.bfloat16
HI = lax.Precision.HIGHEST

D_MODEL = 1024
N_CTX_SEQ = 16
CTX_LEN = 256
N_LAT_SEQ = 2
LAT_LEN = 4096
GRID_W = 64
ROPE_THETA = 10000.0
EPS = 1e-6
GROUP_W = 256
HEAD_DIM = 64
RW_DECAY_SCALE = 0.6065306597
MLA_QK_DIM = 96
MLA_Q_RANK = 192
PEER_HEADS = 8
PEER_KEYS = 128
PEER_TOPK = 16
N_EXPERTS = PEER_KEYS * PEER_KEYS

N_CTX_TOK = N_CTX_SEQ * CTX_LEN
N_LAT_TOK = N_LAT_SEQ * LAT_LEN
N_TOK = N_CTX_TOK + N_LAT_TOK
COND_GROUP = 4096

A_W, B_W, C_W, D_W = 1152, 640, 512, 768
IN_PAD = A_W + B_W + C_W + D_W

SCAN_CHUNK = 128
TILE = 256
IN_TILE = 512
PEER_TN = 512
PEER_TE = 2048
TOPK_TN = 128
ATT_TQ = 256
ATT_CHUNK = 512

VMEM_LIMIT = 56 * 1024 * 1024


def _cparams(sem):
    return pltpu.CompilerParams(dimension_semantics=sem, vmem_limit_bytes=VMEM_LIMIT)


def _dot(a, b, precision=None):
    return jnp.dot(a, b, preferred_element_type=F32, precision=precision)


def _dot_nt(a, b, precision=None):
    return lax.dot_general(a, b, (((1,), (1,)), ((), ())), preferred_element_type=F32,
                           precision=precision)


def _dot_tn(a, b, precision=None):
    return lax.dot_general(a, b, (((0,), (0,)), ((), ())), preferred_element_type=F32,
                           precision=precision)


def _rms_rows(x, g):
    return x * lax.rsqrt(jnp.mean(x * x, axis=-1, keepdims=True) + EPS) * g


def _group_matrix(n, group, value):
    r = lax.broadcasted_iota(jnp.int32, (n, n), 0) // group
    c = lax.broadcasted_iota(jnp.int32, (n, n), 1) // group
    return jnp.where(r == c, value, 0.0).astype(F32)


def _mod_body(c_ref, w_ref, b_ref, o_ref):
    c = c_ref[...]
    s = c * jax.nn.sigmoid(c)
    o_ref[...] = _dot(s.astype(BF16), w_ref[...].astype(BF16)) + b_ref[...]


def _modulation(cvecs, w_mod, b_mod):
    depth = w_mod.shape[0]
    tn = 1536
    return pl.pallas_call(
        _mod_body,
        grid=(depth, 6 * D_MODEL // tn),
        in_specs=[pl.BlockSpec((8, D_MODEL), lambda l, j: (0, 0)),
                  pl.BlockSpec((None, D_MODEL, tn), lambda l, j: (l, 0, j)),
                  pl.BlockSpec((None, 1, tn), lambda l, j: (l, 0, j))],
        out_specs=pl.BlockSpec((None, 8, tn), lambda l, j: (l, 0, j)),
        out_shape=jax.ShapeDtypeStruct((depth, 8, 6 * D_MODEL), F32),
        compiler_params=_cparams(("arbitrary", "arbitrary")),
        name="adaln_mod",
    )(cvecs, w_mod, b_mod.reshape(depth, 1, 6 * D_MODEL))


def _in_body(x_ref, mod_ref, g_ref, w_ref, a_ref, b_ref, c_ref, d_ref):
    mod = mod_ref[...]
    shift = mod[:, 0:D_MODEL]
    scale = mod[:, D_MODEL:2 * D_MODEL]
    h = _rms_rows(x_ref[...], g_ref[...]) * (1.0 + scale) + shift
    y = _dot(h.astype(BF16), w_ref[...])
    a_ref[...] = y[:, 0:A_W]
    b_ref[...] = y[:, A_W:A_W + B_W]
    c_ref[...] = y[:, A_W + B_W:A_W + B_W + C_W]
    d_ref[...] = y[:, A_W + B_W + C_W:IN_PAD]


def _in_proj(x, mod, norm_g, w_pad_all, layer):
    n = x.shape[0]
    tpg = COND_GROUP // IN_TILE
    row = lambda i: (i, 0)
    return pl.pallas_call(
        _in_body,
        grid=(n // IN_TILE,),
        in_specs=[pl.BlockSpec((IN_TILE, D_MODEL), row),
                  pl.BlockSpec((None, 1, 6 * D_MODEL), lambda i: (i // tpg, 0, 0)),
                  pl.BlockSpec((1, D_MODEL), lambda i: (0, 0)),
                  pl.BlockSpec((None, D_MODEL, IN_PAD), lambda i: (layer, 0, 0))],
        out_specs=[pl.BlockSpec((IN_TILE, A_W), row), pl.BlockSpec((IN_TILE, B_W), row),
                   pl.BlockSpec((IN_TILE, C_W), row), pl.BlockSpec((IN_TILE, D_W), row)],
        out_shape=[jax.ShapeDtypeStruct((n, A_W), F32), jax.ShapeDtypeStruct((n, B_W), F32),
                   jax.ShapeDtypeStruct((n, C_W), F32), jax.ShapeDtypeStruct((n, D_W), F32)],
        compiler_params=_cparams(("arbitrary",)),
        name="in_proj",
    )(x, mod, norm_g, w_pad_all)


def _chunk_cumsum(x, reverse):
    c = x.shape[0]
    rows = lax.broadcasted_iota(jnp.int32, x.shape, 0)
    sh = 1
    while sh < c:
        if reverse:
            x = x + jnp.where(rows < c - sh, pltpu.roll(x, c - sh, 0), 0.0)
        else:
            x = x + jnp.where(rows >= sh, pltpu.roll(x, sh, 0), 0.0)
        sh *= 2
    return x


def _scan_body(fb_ref, bb_ref, first_ref, last_ref, seq_ref,
               af_ref, ab_ref, s0_ref, w0_ref, wu_ref, a0_ref, au_ref, kkp_ref, ka_ref,
               yf_ref, yb_ref, sfin_ref, s_scr):
    step = pl.program_id(0)

    @pl.when(first_ref[step] == 1)
    def _():
        s_scr[...] = s0_ref[...]

    c = SCAN_CHUNK
    row = lax.broadcasted_iota(jnp.int32, (c, c), 0)
    col = lax.broadcasted_iota(jnp.int32, (c, c), 1)
    lane_c = lax.broadcasted_iota(jnp.int32, (c, 128), 1) < HEAD_DIM
    lane_2c = lax.broadcasted_iota(jnp.int32, (2 * c, 128), 1) < HEAD_DIM
    same_head = _group_matrix(128, HEAD_DIM, 1.0)
    kkp = kkp_ref[...]
    ka = ka_ref[...]

    def per_head(lo, hi):
        return jnp.where(lane_c, lo, hi)

    groups = []
    for d in range(2):
        x_ref = af_ref if d == 0 else ab_ref
        if d == 0:
            strict, incl = col < row, col <= row
        else:
            strict, incl = col > row, col >= row
        r = x_ref[:, 0:256]
        k = x_ref[:, 256:512]
        v = x_ref[:, 512:768]
        wd = x_ref[:, 768:896]
        ad = x_ref[:, 896:1024]
        wlog = -RW_DECAY_SCALE * jax.nn.sigmoid(
            w0_ref[d:d + 1, :] + _dot(jnp.tanh(wd).astype(BF16), wu_ref[d]))
        a = jax.nn.sigmoid(a0_ref[d:d + 1, :] + _dot(ad.astype(BF16), au_ref[d]))
        kd = k * (1.0 + (a - 1.0) * ka)
        kkf = k * kkp
        cum = _chunk_cumsum(wlog, reverse=(d == 1))
        tot = cum[c - 1:c, :] if d == 0 else cum[0:1, :]
        p_inc = jnp.exp(cum)
        p_exc = jnp.exp(cum - wlog)
        p_inv = jnp.exp(-cum)
        p_tot = jnp.exp(tot)
        for p in range(2):
            sl = slice(128 * p, 128 * p + 128)
            groups.append(dict(d=d, p=p, sl=sl, strict=strict, incl=incl, kk_raw=kkf[:, sl],
                               a=a[:, sl], kd=kd[:, sl], r=r[:, sl], vb=v[:, sl].astype(BF16),
                               p_inc=p_inc[:, sl], p_exc=p_exc[:, sl], p_inv=p_inv[:, sl], pt=p_tot[:, sl]))

    for g in groups:
        g["ss"] = _dot(g["kk_raw"] * g["kk_raw"], same_head, precision=HI)
    for g in groups:
        kk = g["kk_raw"] * lax.rsqrt(g["ss"] + EPS)
        g["a_p"] = (-kk * g["p_exc"]).astype(BF16)
        g["r_p"] = (g["r"] * g["p_inc"]).astype(BF16)
        g["b_i"] = kk * g["a"] * g["p_inv"]
        g["k_i"] = g["kd"] * g["p_inv"]
        g["lhs"] = jnp.concatenate([g["a_p"], g["r_p"]], axis=0)
        g["rhs"] = jnp.concatenate([g["b_i"], g["k_i"]], axis=0).astype(BF16)
        g["sb"] = s_scr[g["d"], g["p"]].astype(BF16)
    zero_b = jnp.zeros((), BF16)
    for g in groups:
        g["gram"] = (_dot_nt(jnp.where(lane_2c, g["lhs"], zero_b), g["rhs"]),
                     _dot_nt(jnp.where(lane_2c, zero_b, g["lhs"]), g["rhs"]))
        g["a_s"] = _dot_nt(g["a_p"], g["sb"])
        g["r_s"] = _dot_nt(g["r_p"], g["sb"])
    for g in groups:
        st, inc = g["strict"], g["incl"]
        g["pow"] = [jnp.where(st, gm[0:c, 0:c], 0.0).astype(BF16) for gm in g["gram"]]
        g["dm"] = [jnp.where(st, gm[0:c, c:2 * c], 0.0).astype(BF16) for gm in g["gram"]]
        g["et"] = [jnp.where(inc, gm[c:2 * c, 0:c], 0.0).astype(BF16) for gm in g["gram"]]
        g["ft"] = [jnp.where(inc, gm[c:2 * c, c:2 * c], 0.0).astype(BF16) for gm in g["gram"]]
    for g in groups:
        g["z"] = g["a_s"] + per_head(_dot(g["dm"][0], g["vb"]), _dot(g["dm"][1], g["vb"]))
    span = 1
    while span < c:
        for g in groups:
            zb = g["z"].astype(BF16)
            g["z"] = g["z"] + per_head(_dot(g["pow"][0], zb), _dot(g["pow"][1], zb))
        span *= 2
        if span < c:
            for g in groups:
                g["pow"] = [_dot(m, m).astype(BF16) for m in g["pow"]]
    for g in groups:
        zb = g["z"].astype(BF16)
        g["zb"] = zb
        y = g["r_s"] + per_head(_dot(g["et"][0], zb) + _dot(g["ft"][0], g["vb"]),
                                _dot(g["et"][1], zb) + _dot(g["ft"][1], g["vb"]))
        y_ref = yf_ref if g["d"] == 0 else yb_ref
        y_ref[:, g["sl"]] = y
    for g in groups:
        pt = g["pt"]
        upd = (_dot_tn(g["zb"], (g["b_i"] * pt).astype(BF16))
               + _dot_tn(g["vb"], (g["k_i"] * pt).astype(BF16)))
        s_scr[g["d"], g["p"]] = s_scr[g["d"], g["p"]] * pt + same_head * upd

    @pl.when(last_ref[step] == 1)
    def _():
        sfin_ref[...] = s_scr[...]


def _scan_tables():
    c = SCAN_CHUNK
    fb, bb, first, last, seq = [], [], [], [], []
    base = 0
    sid = 0
    for nseq, length in ((N_CTX_SEQ, CTX_LEN), (N_LAT_SEQ, LAT_LEN)):
        nc = length // c
        for _ in range(nseq):
            for j in range(nc):
                fb.append(base + j)
                bb.append(base + nc - 1 - j)
                first.append(int(j == 0))
                last.append(int(j == nc - 1))
                seq.append(sid)
            base += nc
            sid += 1
    return tuple(np.asarray(t, np.int32) for t in (fb, bb, first, last, seq))


def _rwkv_scan(a_grp, s0_pairs, w0, wu_pad, a0, au_pad, kkp, ka):
    tables = _scan_tables()
    n_steps = tables[0].shape[0]
    n_seq = s0_pairs.shape[0]
    n = a_grp.shape[0]
    c = SCAN_CHUNK
    const2 = lambda s, fb, bb, fi, la, sq: (0, 0)
    const3 = lambda s, fb, bb, fi, la, sq: (0, 0, 0)
    grid_spec = pltpu.PrefetchScalarGridSpec(
        num_scalar_prefetch=5,
        grid=(n_steps,),
        in_specs=[pl.BlockSpec((c, A_W), lambda s, fb, bb, fi, la, sq: (fb[s], 0)),
                  pl.BlockSpec((c, A_W), lambda s, fb, bb, fi, la, sq: (bb[s], 0)),
                  pl.BlockSpec((None, 2, 2, 128, 128), lambda s, fb, bb, fi, la, sq: (sq[s], 0, 0, 0, 0)),
                  pl.BlockSpec((2, GROUP_W), const2),
                  pl.BlockSpec((2, 128, GROUP_W), const3),
                  pl.BlockSpec((2, GROUP_W), const2),
                  pl.BlockSpec((2, 128, GROUP_W), const3),
                  pl.BlockSpec((1, GROUP_W), const2),
                  pl.BlockSpec((1, GROUP_W), const2)],
        out_specs=[pl.BlockSpec((c, GROUP_W), lambda s, fb, bb, fi, la, sq: (fb[s], 0)),
                   pl.BlockSpec((c, GROUP_W), lambda s, fb, bb, fi, la, sq: (bb[s], 0)),
                   pl.BlockSpec((None, 2, 2, 128, 128), lambda s, fb, bb, fi, la, sq: (sq[s], 0, 0, 0, 0))],
        scratch_shapes=[pltpu.VMEM((2, 2, 128, 128), F32)],
    )
    return pl.pallas_call(
        _scan_body,
        grid_spec=grid_spec,
        out_shape=[jax.ShapeDtypeStruct((n, GROUP_W), F32), jax.ShapeDtypeStruct((n, GROUP_W), F32),
                   jax.ShapeDtypeStruct((n_seq, 2, 2, 128, 128), F32)],
        compiler_params=_cparams(("arbitrary",)),
        name="rwkv_scan",
    )(*[jnp.asarray(t) for t in tables], a_grp, a_grp, s0_pairs, w0, wu_pad, a0, au_pad, kkp, ka)


MLA_SCALE = MLA_QK_DIM ** -0.5
GQA_SCALE = HEAD_DIM ** -0.5


def _mla_keys(ckv_b, rope_slot, rope_slot_sw, wk_ref, kng_ref, cosm, sinm, k_out_ref):
    g = kng_ref[0:1, :]
    g_sw = kng_ref[1:2, :]
    for h in range(4):
        nope = _dot(ckv_b, wk_ref[h])
        kr = nope + rope_slot
        rs = lax.rsqrt(jnp.sum(kr * kr, axis=-1, keepdims=True) * (1.0 / MLA_QK_DIM) + EPS)
        if cosm is None:
            k_out_ref[h] = (kr * rs * g).astype(BF16)
        else:
            ks = nope + rope_slot_sw
            k_out_ref[h] = ((kr * rs * g) * cosm + (ks * rs * g_sw) * sinm).astype(BF16)


def _prep_body(b_ref, c_ref, cosm_ref, sinm_ref, cosg_ref, sing_ref,
               qag_ref, wqq_ref, qng_ref, kvag_ref, wk_ref, wv_ref, kng_ref, gqg_ref, gkg_ref,
               qm_ref, km_ref, vm_ref, ckvn_ref, qg_ref, kg_ref, vg_ref, kgn_ref):
    cosm = cosm_ref[...]
    sinm = sinm_ref[...]
    qc = b_ref[:, 0:256]
    qn = qc * lax.rsqrt(jnp.sum(qc * qc, axis=-1, keepdims=True) * (1.0 / MLA_Q_RANK) + EPS) * qag_ref[...]
    qnb = qn.astype(BF16)
    g = qng_ref[0:1, :]
    g_sw = qng_ref[1:2, :]
    for h in range(4):
        qr = _dot(qnb, wqq_ref[h])
        qs = _dot(qnb, wqq_ref[4 + h])
        rs = lax.rsqrt(jnp.sum(qr * qr, axis=-1, keepdims=True) * (1.0 / MLA_QK_DIM) + EPS)
        qm_ref[h] = (((qr * rs * g) * cosm + (qs * rs * g_sw) * sinm) * MLA_SCALE).astype(BF16)
    ckv = _rms_rows(b_ref[:, 256:384], kvag_ref[...])
    ckvn_ref[...] = ckv
    cb = ckv.astype(BF16)
    _mla_keys(cb, b_ref[:, 384:512], b_ref[:, 512:640], wk_ref, kng_ref, cosm, sinm, km_ref)
    for p in range(2):
        vm_ref[p] = _dot_nt(wv_ref[p], cb).astype(BF16)

    tm = c_ref.shape[0]
    avg = _group_matrix(128, HEAD_DIM, 1.0 / HEAD_DIM)
    lane = lax.broadcasted_iota(jnp.int32, (tm, 128), 1)
    first_half = (lane % HEAD_DIM) < (HEAD_DIM // 2)
    low = lane < HEAD_DIM
    cosg = cosg_ref[...]
    sing = sing_ref[...]

    def rotate(xn):
        swapped = jnp.where(first_half, pltpu.roll(xn, 128 - HEAD_DIM // 2, 1), pltpu.roll(xn, HEAD_DIM // 2, 1))
        return xn * cosg + swapped * sing

    for blk in range(2):
        x = c_ref[:, 128 * blk:128 * blk + 128]
        xn = x * lax.rsqrt(_dot(x * x, avg, precision=HI) + EPS) * gqg_ref[...]
        xr = rotate(xn) * GQA_SCALE
        qg_ref[blk] = jnp.where(low, xr, 0.0).astype(BF16)
        qg_ref[blk + 2] = jnp.where(low, 0.0, xr).astype(BF16)
    xk = c_ref[:, 256:384]
    kn = xk * lax.rsqrt(_dot(xk * xk, avg, precision=HI) + EPS) * gkg_ref[...]
    kgn_ref[...] = kn
    kg_ref[...] = rotate(kn).astype(BF16)
    vg_ref[...] = c_ref[:, 384:512].T.astype(BF16)


def _attn_prep(b_grp, c_grp, tabs, lw):
    n = b_grp.shape[0]
    n_ctx_tiles = N_CTX_TOK // TILE
    lat_tiles = LAT_LEN // TILE
    row = lambda i: (i, 0)
    hrow = lambda i: (0, i, 0)
    tab = lambda i: (jnp.where(i < n_ctx_tiles, 0, 1 + (i - n_ctx_tiles) % lat_tiles), 0)
    c2 = lambda i: (0, 0)
    c3 = lambda i: (0, 0, 0)
    return pl.pallas_call(
        _prep_body,
        grid=(n // TILE,),
        in_specs=[pl.BlockSpec((TILE, B_W), row), pl.BlockSpec((TILE, C_W), row),
                  pl.BlockSpec((TILE, 128), tab), pl.BlockSpec((TILE, 128), tab),
                  pl.BlockSpec((TILE, 128), tab), pl.BlockSpec((TILE, 128), tab),
                  pl.BlockSpec((1, 256), c2), pl.BlockSpec((8, 256, 128), c3), pl.BlockSpec((2, 128), c2),
                  pl.BlockSpec((1, 128), c2), pl.BlockSpec((4, 128, 128), c3), pl.BlockSpec((2, 128, 128), c3),
                  pl.BlockSpec((2, 128), c2), pl.BlockSpec((1, 128), c2), pl.BlockSpec((1, 128), c2)],
        out_specs=[pl.BlockSpec((4, TILE, 128), hrow), pl.BlockSpec((4, TILE, 128), hrow),
                   pl.BlockSpec((2, 128, TILE), lambda i: (0, 0, i)), pl.BlockSpec((TILE, 128), row),
                   pl.BlockSpec((4, TILE, 128), hrow), pl.BlockSpec((TILE, 128), row),
                   pl.BlockSpec((128, TILE), lambda i: (0, i)), pl.BlockSpec((TILE, 128), row)],
        out_shape=[jax.ShapeDtypeStruct((4, n, 128), BF16), jax.ShapeDtypeStruct((4, n, 128), BF16),
                   jax.ShapeDtypeStruct((2, 128, n), BF16), jax.ShapeDtypeStruct((n, 128), F32),
                   jax.ShapeDtypeStruct((4, n, 128), BF16), jax.ShapeDtypeStruct((n, 128), BF16),
                   jax.ShapeDtypeStruct((128, n), BF16), jax.ShapeDtypeStruct((n, 128), F32)],
        compiler_params=_cparams(("arbitrary",)),
        name="attn_prep",
    )(b_grp, c_grp, tabs["cosm"], tabs["sinm"], tabs["cosg"], tabs["sing"],
      lw["qa_g"], lw["wqq"], lw["qn_g2"], lw["kva_g"], lw["wk"], lw["wv"], lw["kn_g2"], lw["gq_g"], lw["gk_g"])


def _ctxkv_body(ckv_ref, krp_ref, wk_ref, wv_ref, kng_ref, k_ref, v_ref):
    cb = ckv_ref[...].astype(BF16)
    _mla_keys(cb, krp_ref[...], None, wk_ref, kng_ref, None, None, k_ref)
    for p in range(2):
        v_ref[p] = _dot_nt(wv_ref[p], cb).astype(BF16)


def _mla_ctx_kv(cache_ckv, cache_krope_placed, wk, wv, kn_g2):
    nb, depth, plen, _ = cache_ckv.shape
    return pl.pallas_call(
        _ctxkv_body,
        grid=(depth, nb),
        in_specs=[pl.BlockSpec((None, None, plen, 128), lambda l, b: (b, l, 0, 0)),
                  pl.BlockSpec((None, None, plen, 128), lambda l, b: (b, l, 0, 0)),
                  pl.BlockSpec((None, 4, 128, 128), lambda l, b: (l, 0, 0, 0)),
                  pl.BlockSpec((None, 2, 128, 128), lambda l, b: (l, 0, 0, 0)),
                  pl.BlockSpec((None, 2, 128), lambda l, b: (l, 0, 0))],
        out_specs=[pl.BlockSpec((None, 4, None, plen, 128), lambda l, b: (l, 0, b, 0, 0)),
                   pl.BlockSpec((None, 2, None, 128, plen), lambda l, b: (l, 0, b, 0, 0))],
        out_shape=[jax.ShapeDtypeStruct((depth, 4, nb, plen, 128), BF16),
                   jax.ShapeDtypeStruct((depth, 2, nb, 128, plen), BF16)],
        compiler_params=_cparams(("arbitrary", "arbitrary")),
        name="mla_ctx_kv",
    )(cache_ckv, cache_krope_placed, wk, wv, kn_g2)


def _attn_body(*refs, has_ctx):
    if has_ctx:
        qa_ref, qb_ref, ka_ref, kb_ref, vt_ref, kca_ref, kcb_ref, vct_ref, o_ref, sa_scr, sb_scr = refs
    else:
        qa_ref, qb_ref, ka_ref, kb_ref, vt_ref, o_ref, sa_scr, sb_scr = refs
        kca_ref = kcb_ref = vct_ref = None
    tk = ka_ref.shape[0]
    chunk = min(ATT_CHUNK, tk)
    q = (qa_ref[...], qb_ref[...])
    k_refs = (ka_ref, kb_ref)
    kc_refs = (kca_ref, kcb_ref)
    s_scr = (sa_scr, sb_scr)
    vrows = (slice(0, HEAD_DIM), slice(HEAD_DIM, 2 * HEAD_DIM))
    spans = [("new", c * chunk, chunk) for c in range(tk // chunk)]
    if has_ctx:
        spans = [("ctx", 0, kca_ref.shape[0])] + spans

    def scratch_rows(kind, start, size):
        base = tk if kind == "ctx" else 0
        return slice(base + start, base + start + size)

    m = [None, None]
    for kind, start, size in spans:
        for hd in range(2):
            if kind == "ctx":
                keys = kc_refs[hd][...].astype(BF16)
            else:
                keys = k_refs[hd][start:start + size, :]
            s = _dot_nt(keys, q[hd])
            s_scr[hd][scratch_rows(kind, start, size), :] = s
            cm = jnp.max(s, axis=0, keepdims=True)
            m[hd] = cm if m[hd] is None else jnp.maximum(m[hd], cm)

    den = [None, None]
    acc = [None, None]
    for kind, start, size in spans:
        for hd in range(2):
            e = jnp.exp(s_scr[hd][scratch_rows(kind, start, size), :] - m[hd])
            if kind == "ctx":
                vals = vct_ref[vrows[hd], :].astype(BF16)
            else:
                vals = vt_ref[vrows[hd], start:start + size]
            part = _dot(vals, e.astype(BF16))
            rs = jnp.sum(e, axis=0, keepdims=True)
            den[hd] = rs if den[hd] is None else den[hd] + rs
            acc[hd] = part if acc[hd] is None else acc[hd] + part
    out_t = jnp.concatenate([acc[0] / den[0], acc[1] / den[1]], axis=0)
    o_ref[...] = out_t.T.astype(o_ref.dtype)


def _attention(q, ka, kb, v, head_a, head_b, k_head_a, k_head_b, v_idx, n_seq, seq_len, tok0, ctx=None):
    tq = min(ATT_TQ, seq_len)
    nq = seq_len // tq
    seq0 = tok0 // seq_len
    q0 = tok0 // tq

    def qmap(hsel):
        return lambda b, p, i: (hsel(p), q0 + b * nq + i, 0)

    def kmap(hsel, arr):
        if arr.ndim == 3:
            return lambda b, p, i: (hsel(p), seq0 + b, 0)
        return lambda b, p, i: (seq0 + b, 0)

    def kspec(arr, hsel):
        if arr.ndim == 3:
            return pl.BlockSpec((None, seq_len, 128), kmap(hsel, arr))
        return pl.BlockSpec((seq_len, 128), kmap(hsel, arr))

    if v.ndim == 3:
        vspec = pl.BlockSpec((None, 128, seq_len), lambda b, p, i: (v_idx(p), 0, seq0 + b))
    else:
        vspec = pl.BlockSpec((128, seq_len), lambda b, p, i: (0, seq0 + b))
    in_specs = [pl.BlockSpec((None, tq, 128), qmap(head_a)), pl.BlockSpec((None, tq, 128), qmap(head_b)),
                kspec(ka, k_head_a), kspec(kb, k_head_b), vspec]
    args = [q, q, ka, kb, v]
    if ctx is not None:
        kca, kcb, vc = ctx
        past = vc.shape[-1]

        def cspec(arr, hsel, shape):
            if arr.ndim == 4:
                return pl.BlockSpec((None, None) + shape, lambda b, p, i: (hsel(p), b, 0, 0))
            return pl.BlockSpec((None,) + shape, lambda b, p, i: (b, 0, 0))

        in_specs += [cspec(kca, k_head_a, (past, 128)), cspec(kcb, k_head_b, (past, 128)),
                     cspec(vc, v_idx, (128, past))]
        args += [kca, kcb, vc]
    return pl.pallas_call(
        functools.partial(_attn_body, has_ctx=ctx is not None),
        grid=(n_seq, 2, nq),
        in_specs=in_specs,
        out_specs=pl.BlockSpec((tq, 128), lambda b, p, i: (b * nq + i, p)),
        out_shape=jax.ShapeDtypeStruct((n_seq * seq_len, 256), BF16),
        scratch_shapes=[pltpu.VMEM((seq_len + (0 if ctx is None else ctx[2].shape[-1]), tq), F32)] * 2,
        compiler_params=_cparams(("arbitrary", "arbitrary", "arbitrary")),
        name="attention_ctx" if ctx is None else "attention_lat",
    )(*args)


def _out_body(x_ref, mod_ref, a_ref, yf_ref, yb_ref, obc_ref, obl_ref, occ_ref, ocl_ref,
              d_ref, dprev_ref, dnext_ref, wo_ref, gu_ref, rk_ref, gn_ref, cw_ref, cb_ref, n2_ref, wq_ref,
              x1_ref, h2_ref, q_ref, *, n_ctx_tiles):
    i = pl.program_id(0)
    is_ctx = i < n_ctx_tiles
    tm = x_ref.shape[0]
    mod = mod_ref[...]
    gate1 = mod[:, 2 * D_MODEL:3 * D_MODEL]
    shift2 = mod[:, 3 * D_MODEL:4 * D_MODEL]
    scale2 = mod[:, 4 * D_MODEL:5 * D_MODEL]

    r = a_ref[:, 0:256]
    k = a_ref[:, 256:512]
    v = a_ref[:, 512:768]
    gd = a_ref[:, 1024:1152]
    y = yf_ref[...] + yb_ref[...]
    avg = _group_matrix(GROUP_W, HEAD_DIM, 1.0 / HEAD_DIM)
    ones = _group_matrix(GROUP_W, HEAD_DIM, 1.0)
    yn = y * lax.rsqrt(_dot(y * y, avg, precision=HI) + EPS) * gn_ref[...]
    bonus = _dot(r * k * rk_ref[...], ones, precision=HI) * v
    gate = _dot(jax.nn.sigmoid(gd).astype(BF16), gu_ref[...])
    o_a = (yn + bonus) * gate

    o_b = jnp.where(is_ctx, obc_ref[...], obl_ref[...])
    o_c = jnp.where(is_ctx, occ_ref[...], ocl_ref[...])

    u = d_ref[:, 512:768] * d_ref[:, 0:256]
    u_prev = dprev_ref[:, 512:768] * dprev_ref[:, 0:256]
    u_next = dnext_ref[:, 512:768] * dnext_ref[:, 0:256]
    rows = lax.broadcasted_iota(jnp.int32, (tm, GROUP_W), 0)
    up = jnp.where(rows == 0, u_prev, pltpu.roll(u, 1, 0))
    un = jnp.where(rows == tm - 1, u_next, pltpu.roll(u, tm - 1, 0))
    conv = up * cw_ref[0:1, :] + u * cw_ref[1:2, :] + un * cw_ref[2:3, :] + cb_ref[...]
    o_d = d_ref[:, 256:512] * conv

    mix_in = jnp.concatenate([o_a.astype(BF16), o_b, o_c, o_d.astype(BF16)], axis=1)
    x1 = x_ref[...] + gate1 * _dot(mix_in, wo_ref[...])
    x1_ref[...] = x1
    h2 = (_rms_rows(x1, n2_ref[...]) * (1.0 + scale2) + shift2).astype(BF16)
    h2_ref[...] = h2
    q_ref[...] = _dot(h2, wq_ref[...])


def _out_proj(x, mod, a_grp, yf, yb, ob_ctx, ob_lat, oc_ctx, oc_lat, d_grp, d_prev, d_next, lw, w_out_all, wq_all,
              layer):
    n = x.shape[0]
    n_tiles = n // TILE
    n_ctx_tiles = N_CTX_TOK // TILE
    tpg = COND_GROUP // TILE
    row = lambda i: (i, 0)
    ctx_row = lambda i: (jnp.minimum(i, n_ctx_tiles - 1), 0)
    lat_row = lambda i: (jnp.maximum(i - n_ctx_tiles, 0), 0)
    halo = lambda i: (i, 0, 0)
    c2 = lambda i: (0, 0)
    nq = PEER_HEADS * 2 * PEER_KEYS
    return pl.pallas_call(
        functools.partial(_out_body, n_ctx_tiles=n_ctx_tiles),
        grid=(n_tiles,),
        in_specs=[pl.BlockSpec((TILE, D_MODEL), row),
                  pl.BlockSpec((None, 1, 6 * D_MODEL), lambda i: (i // tpg, 0, 0)),
                  pl.BlockSpec((TILE, A_W), row),
                  pl.BlockSpec((TILE, GROUP_W), row), pl.BlockSpec((TILE, GROUP_W), row),
                  pl.BlockSpec((TILE, GROUP_W), ctx_row), pl.BlockSpec((TILE, GROUP_W), lat_row),
                  pl.BlockSpec((TILE, GROUP_W), ctx_row), pl.BlockSpec((TILE, GROUP_W), lat_row),
                  pl.BlockSpec((TILE, D_W), row),
                  pl.BlockSpec((None, 1, D_W), halo), pl.BlockSpec((None, 1, D_W), halo),
                  pl.BlockSpec((None, D_MODEL, D_MODEL), lambda i: (layer, 0, 0)), pl.BlockSpec((128, GROUP_W), c2),
                  pl.BlockSpec((1, GROUP_W), c2), pl.BlockSpec((1, GROUP_W), c2),
                  pl.BlockSpec((3, GROUP_W), c2), pl.BlockSpec((1, GROUP_W), c2),
                  pl.BlockSpec((1, D_MODEL), c2), pl.BlockSpec((None, D_MODEL, nq), lambda i: (layer, 0, 0))],
        out_specs=[pl.BlockSpec((TILE, D_MODEL), row), pl.BlockSpec((TILE, D_MODEL), row),
                   pl.BlockSpec((TILE, nq), row)],
        out_shape=[jax.ShapeDtypeStruct((n, D_MODEL), F32), jax.ShapeDtypeStruct((n, D_MODEL), BF16),
                   jax.ShapeDtypeStruct((n, nq), F32)],
        compiler_params=_cparams(("arbitrary",)),
        name="out_proj",
    )(x, mod, a_grp, yf, yb, ob_ctx, ob_lat, oc_ctx, oc_lat, d_grp, d_prev, d_next,
      w_out_all, lw["gu"], lw["rk"], lw["gn"], lw["conv_w"], lw["conv_b"], lw["norm2_g"], wq_all)


def _exchange(a, b):
    if a is None:
        return b, None
    if b is None:
        return a, None
    return jnp.maximum(a, b), jnp.minimum(a, b)


def _sort16_desc(xs):
    xs = list(xs)
    k = 2
    while k <= 16:
        j = k // 2
        while j >= 1:
            for i in range(16):
                partner = i ^ j
                if partner > i:
                    hi, lo = _exchange(xs[i], xs[partner])
                    xs[i], xs[partner] = (hi, lo) if (i & k) == 0 else (lo, hi)
            j //= 2
        k *= 2
    return xs


def _bitonic_merge_desc(xs):
    xs = list(xs)
    j = 8
    while j >= 1:
        for i in range(16):
            partner = i ^ j
            if partner > i:
                xs[i], xs[partner] = _exchange(xs[i], xs[partner])
        j //= 2
    return xs


def _top16_of_rows(xs):
    ys = _sort16_desc(xs)
    for shift in (4, 2, 1):
        zs = [None if y is None else pltpu.roll(y, shift, 0) for y in ys]
        ts = [_exchange(ys[i], zs[15 - i])[0] for i in range(16)]
        ys = _bitonic_merge_desc(ts)
    return ys


def _topk_body(q_ref, k1_ref, k2_ref, thr_ref, s2_ref, e1_ref, e2_ref):
    tn = q_ref.shape[0]
    sub = lax.broadcasted_iota(jnp.int32, (8, tn), 0)

    def spread(vals):
        out = vals[7]
        for s in range(6, -1, -1):
            out = jnp.where(sub == s, vals[s], out)
        return out

    for h in range(PEER_HEADS):
        qa = q_ref[:, (2 * h) * PEER_KEYS:(2 * h + 1) * PEER_KEYS].astype(BF16)
        qb = q_ref[:, (2 * h + 1) * PEER_KEYS:(2 * h + 2) * PEER_KEYS].astype(BF16)
        s1 = _dot_nt(k1_ref[h], qa)
        s2 = _dot_nt(k2_ref[h], qb)
        v1 = _top16_of_rows([s1[8 * i:8 * i + 8, :] for i in range(16)])
        v2 = _top16_of_rows([s2[8 * i:8 * i + 8, :] for i in range(16)])
        v2_lo, v2_hi, v1_hi = spread(v2[0:8]), spread(v2[8:16]), spread(v1[8:16])
        cands = ([v1[0] + v2_lo, v1[0] + v2_hi] + [v1[a] + v2_lo for a in range(1, 8)]
                 + [v1_hi + v2[0]] + [None] * 6)
        best = _top16_of_rows(cands)
        tau = best[PEER_TOPK - 1][0:1, :]
        zsum = jnp.exp(best[0] - best[0])
        for kth in range(1, PEER_TOPK):
            zsum = zsum + jnp.exp(best[kth] - best[0])
        thr = jnp.full(s1.shape, jnp.inf, F32)
        for b in range(PEER_TOPK):
            vb = v2[b][0:1, :]
            thr = jnp.where(s1 + vb >= tau, vb, thr)
        thr_ref[h] = thr
        s2_ref[h] = s2
        e1_ref[h] = jnp.exp(s1 - v1[0][0:1, :])
        e2_ref[h] = jnp.exp(s2 - v2[0][0:1, :]) * (0.5 / zsum[0:1, :])


def _rotated_block(i):
    per_tile = PEER_TN // TOPK_TN
    return (i // per_tile) * per_tile + (i + 1) % per_tile


def _peer_topk(q, k1, k2):
    n = q.shape[0]
    tn = TOPK_TN
    big = pl.BlockSpec((PEER_HEADS, PEER_KEYS, tn), lambda i: (0, 0, i))
    big_shape = jax.ShapeDtypeStruct((PEER_HEADS, PEER_KEYS, n), F32)
    c3 = lambda i: (0, 0, 0)
    return pl.pallas_call(
        _topk_body,
        grid=(n // tn,),
        in_specs=[pl.BlockSpec((tn, PEER_HEADS * 2 * PEER_KEYS), lambda i: (i, 0)),
                  pl.BlockSpec((PEER_HEADS, PEER_KEYS, PEER_KEYS), c3),
                  pl.BlockSpec((PEER_HEADS, PEER_KEYS, PEER_KEYS), c3)],
        out_specs=[big, big, big, pl.BlockSpec((PEER_HEADS, PEER_KEYS, tn), lambda i: (0, 0, _rotated_block(i)))],
        out_shape=[big_shape, big_shape, big_shape, big_shape],
        compiler_params=_cparams(("arbitrary",)),
        name="peer_topk",
    )(q, k1, k2)


GELU_C0 = 0.7978845608028654
GELU_C1 = 0.044715


PEER_PIECE = 256
PEER_VGROUP = 256


def _dense_body(h2_ref, u_ref, vt_ref, thr_ref, s2_ref, e1_ref, e2_ref, x1_ref, mod_ref,
                o_ref, acc, *piece_scratch):
    j = pl.program_id(1)

    @pl.when(j == 0)
    def _():
        acc[...] = jnp.zeros_like(acc)

    n_pieces = u_ref.shape[0] // PEER_PIECE
    hs, gs = piece_scratch[:n_pieces], piece_scratch[n_pieces:]
    tn = hs[0].shape[1]
    n_col = tn // 128
    h2 = h2_ref[...]

    def pre_activations(p):
        rows = slice(p * PEER_PIECE, (p + 1) * PEER_PIECE)
        hs[p][...] = _dot_nt(u_ref[rows, :], h2)

    pre_activations(0)
    for p in range(n_pieces):
        if p + 1 < n_pieces:
            pre_activations(p + 1)
        halves = PEER_PIECE // PEER_KEYS
        for sub in range(2 * n_col):
            keys = slice((sub % 2) * 64, (sub % 2) * 64 + 64)
            c = sub // 2
            cols = slice(c * 128, c * 128 + 128)
            e2_cols = slice(((c + 1) % n_col) * 128, ((c + 1) % n_col) * 128 + 128)
            n_grp = 64 // 8
            w = [[None] * n_grp for _ in range(halves)]
            for h in range(PEER_HEADS):
                thr8 = [jnp.broadcast_to(thr_ref[h, p * halves + half:p * halves + half + 1, cols], (8, 128))
                        for half in range(halves)]
                e18 = [jnp.broadcast_to(e1_ref[h, p * halves + half:p * halves + half + 1, cols], (8, 128))
                       for half in range(halves)]
                for g in range(n_grp):
                    grp = slice(keys.start + 8 * g, keys.start + 8 * g + 8)
                    s2g = s2_ref[h, grp, cols]
                    e2g = e2_ref[h, grp, e2_cols]
                    for half in range(halves):
                        term = jnp.where(s2g >= thr8[half], e18[half] * e2g, 0.0)
                        w[half][g] = term if w[half][g] is None else w[half][g] + term
            for half in range(halves):
                rows = slice(half * PEER_KEYS + keys.start, half * PEER_KEYS + keys.stop)
                x = hs[p][rows, cols]
                inner = x * (GELU_C0 + (GELU_C0 * GELU_C1) * (x * x))
                wh = jnp.concatenate(w[half], axis=0)
                gs[p][rows, cols] = (wh * x * (1.0 + jnp.tanh(inner))).astype(BF16)
        prows = slice(p * PEER_PIECE, (p + 1) * PEER_PIECE)
        acc[...] += _dot(vt_ref[:, prows], gs[p][...])

    @pl.when(j == pl.num_programs(1) - 1)
    def _():
        gate2 = mod_ref[:, 5 * D_MODEL:6 * D_MODEL]
        o_ref[...] = x1_ref[...] + gate2 * acc[...].T


def _peer_dense(h2, u_all, vt_all, layer, thr, s2, e1, e2, x1, mod):
    n = h2.shape[0]
    tn, te = PEER_TN, PEER_TE
    rows_per_step = te // PEER_KEYS
    tpg = COND_GROUP // tn
    tok = lambda i, j: (i, 0)
    key_rows = pl.BlockSpec((PEER_HEADS, rows_per_step, tn), lambda i, j: (0, j, i))
    key_all = pl.BlockSpec((PEER_HEADS, PEER_KEYS, tn), lambda i, j: (0, 0, i))
    return pl.pallas_call(
        _dense_body,
        grid=(n // tn, N_EXPERTS // te),
        in_specs=[pl.BlockSpec((tn, D_MODEL), tok),
                  pl.BlockSpec((None, te, D_MODEL), lambda i, j: (layer, j, 0)),
                  pl.BlockSpec((None, D_MODEL, te), lambda i, j: (layer, 0, j)),
                  key_rows, key_all, key_rows, key_all,
                  pl.BlockSpec((tn, D_MODEL), tok),
                  pl.BlockSpec((None, 1, 6 * D_MODEL), lambda i, j: (i // tpg, 0, 0))],
        out_specs=pl.BlockSpec((tn, D_MODEL), tok),
        out_shape=jax.ShapeDtypeStruct((n, D_MODEL), F32),
        scratch_shapes=([pltpu.VMEM((D_MODEL, tn), F32)]
                        + [pltpu.VMEM((PEER_PIECE, tn), F32)] * (te // PEER_PIECE)
                        + [pltpu.VMEM((PEER_PIECE, tn), BF16)] * (te // PEER_PIECE)),
        compiler_params=_cparams(("arbitrary", "arbitrary")),
        name="peer_dense",
    )(h2, u_all, vt_all, thr, s2, e1, e2, x1, mod)


def _in_proj_columns():
    src = np.full((IN_PAD,), -1, np.int64)

    def put(dst, start, width):
        src[dst:dst + width] = np.arange(start, start + width)

    put(0, 0, 768)
    put(768, 768, 128)
    put(896, 896, 64)
    put(1024, 960, 128)
    b0 = A_W
    put(b0, 1088, 192)
    put(b0 + 256, 1280, 128)
    put(b0 + 384 + 64, 1408, 32)
    put(b0 + 512 + 64, 1408 + 16, 16)
    put(b0 + 512 + 80, 1408, 16)
    c0 = A_W + B_W
    for slot, head in enumerate((0, 2, 1, 3)):
        put(c0 + 64 * slot, 1440 + 64 * head, 64)
    put(c0 + 256, 1696, 256)
    put(A_W + B_W + C_W, 1952, 768)
    return src


def _swap_tail(w):
    return jnp.concatenate([w[..., :64], w[..., 80:96], w[..., 64:80]], axis=-1)


def _pad_last(w, width):
    return jnp.pad(w, [(0, 0)] * (w.ndim - 1) + [(0, width - w.shape[-1])])


STACKED_WEIGHTS = ("w_in", "w_out", "wq", "u", "vt")


def _layer_weights(p):
    depth = p["w_in"].shape[0]
    src = _in_proj_columns()
    runs, start = [], 0
    for i in range(1, IN_PAD + 1):
        if i == IN_PAD or (src[i] != src[i - 1] + 1 if src[i - 1] >= 0 else src[i] >= 0):
            runs.append((start, i))
            start = i
    w_bf = p["w_in"].astype(BF16)
    pieces = [w_bf[:, :, src[a]:src[a] + (b - a)] if src[a] >= 0
              else jnp.zeros((depth, D_MODEL, b - a), BF16) for a, b in runs]
    w_in = jnp.concatenate(pieces, axis=2)

    wuq = p["mla_wuq"].reshape(depth, MLA_Q_RANK, 4, MLA_QK_DIM).transpose(0, 2, 1, 3)
    wq_plain = jnp.pad(wuq, ((0, 0), (0, 0), (0, 256 - MLA_Q_RANK), (0, 128 - MLA_QK_DIM)))
    wq_swap = jnp.pad(_swap_tail(wuq), ((0, 0), (0, 0), (0, 256 - MLA_Q_RANK), (0, 128 - MLA_QK_DIM)))
    wukv = p["mla_wukv"].reshape(depth, 128, 4, 128)
    wk = _pad_last(wukv[..., :64].transpose(0, 2, 1, 3), 128)
    wv_heads = wukv[..., 64:].transpose(0, 2, 1, 3)
    wv = jnp.concatenate([wv_heads[:, 0::2], wv_heads[:, 1::2]], axis=-1)

    def gain2(g):
        return jnp.stack([_pad_last(g, 128), _pad_last(_swap_tail(g), 128)], axis=1)

    w_out = p["w_out"]
    oc = w_out[:, 512:768].reshape(depth, 4, 64, D_MODEL)[:, jnp.asarray([0, 2, 1, 3])].reshape(depth, 256, D_MODEL)
    w_out = jnp.concatenate([w_out[:, :512], oc, w_out[:, 768:]], axis=1).astype(BF16)

    wu_pad = jnp.zeros((depth, 2, 128, GROUP_W), F32)
    wu_pad = wu_pad.at[:, 0, 0:64].set(p["rw_wu"][:, 0]).at[:, 1, 64:128].set(p["rw_wu"][:, 1])
    au_pad = jnp.zeros((depth, 2, 128, GROUP_W), F32)
    au_pad = au_pad.at[:, 0, 0:32].set(p["rw_au"][:, 0]).at[:, 1, 32:64].set(p["rw_au"][:, 1])

    return {
        "w_in": w_in,
        "norm1_g": p["norm1_g"][:, None, :],
        "norm2_g": p["norm2_g"][:, None, :],
        "w_out": w_out,
        "w0": p["rw_w0"], "a0": p["rw_a0"],
        "wu": wu_pad.astype(BF16), "au": au_pad.astype(BF16),
        "kkp": p["rw_kk"][:, None, :], "ka": p["rw_ka"][:, None, :],
        "gu": p["rw_gu"].astype(BF16),
        "rk": p["rw_rk"].reshape(depth, 1, GROUP_W),
        "gn": p["rw_gn"][:, None, :],
        "qa_g": _pad_last(p["mla_qa_g"], 256)[:, None, :],
        "wqq": jnp.concatenate([wq_plain, wq_swap], axis=1).astype(BF16),
        "qn_g2": gain2(p["mla_qn_g"]),
        "kva_g": p["mla_kva_g"][:, None, :],
        "wk": wk.astype(BF16), "wv": jnp.swapaxes(wv, -1, -2).astype(BF16),
        "kn_g2": gain2(p["mla_kn_g"]),
        "gq_g": jnp.tile(p["gqa_qn_g"], (1, 2))[:, None, :],
        "gk_g": jnp.tile(p["gqa_kn_g"], (1, 2))[:, None, :],
        "conv_w": p["conv_w"], "conv_b": p["conv_b"][:, None, :],
        "wq": p["peer_wq"].astype(BF16),
        "k1": p["peer_k1"].astype(BF16), "k2": p["peer_k2"].astype(BF16),
        "u": p["peer_u"].astype(BF16),
        "vt": jnp.swapaxes(p["peer_v"], 1, 2).astype(BF16),
    }


def _rope_tables():
    t = jnp.arange(LAT_LEN, dtype=F32)
    grid_row = jnp.floor(t / GRID_W)
    grid_col = t - grid_row * GRID_W

    def angles(rot_dim):
        n_freq = rot_dim // 4
        freqs = ROPE_THETA ** (-jnp.arange(n_freq, dtype=F32) / n_freq)
        ang = jnp.concatenate([grid_row[:, None] * freqs, grid_col[:, None] * freqs], axis=-1)
        return jnp.cos(ang), jnp.sin(ang)

    cm, sm = angles(32)
    ones64 = jnp.ones((LAT_LEN, 64), F32)
    zeros64 = jnp.zeros((LAT_LEN, 64), F32)
    cosm = jnp.concatenate([ones64, cm, cm, ones64[:, :32]], axis=-1)
    sinm = jnp.concatenate([zeros64, -sm, sm, zeros64[:, :32]], axis=-1)
    cg, sg = angles(64)
    cosg = jnp.tile(jnp.concatenate([cg, cg], axis=-1), (1, 2))
    sing = jnp.tile(jnp.concatenate([-sg, sg], axis=-1), (1, 2))
    ident = jnp.ones((TILE, 128), F32)
    zero = jnp.zeros((TILE, 128), F32)
    return {"cosm": jnp.concatenate([ident, cosm]), "sinm": jnp.concatenate([zero, sinm]),
            "cosg": jnp.concatenate([ident, cosg]), "sing": jnp.concatenate([zero, sing])}


def _states_to_pairs(s):
    lead = s.shape[:-3]
    s = s.reshape(lead + (2, 2, 64, 64))
    z = jnp.zeros(lead + (2, 64, 64), s.dtype)
    top = jnp.concatenate([s[..., 0, :, :], z], axis=-1)
    bot = jnp.concatenate([z, s[..., 1, :, :]], axis=-1)
    return jnp.concatenate([top, bot], axis=-2)


def _pairs_to_states(sp):
    lead = sp.shape[:-3]
    a = sp[..., 0:64, 0:64]
    b = sp[..., 64:128, 64:128]
    return jnp.stack([a, b], axis=-3).reshape(lead + (4, 64, 64))


def _conv_halos(d_grp):
    n_tiles = d_grp.shape[0] // TILE
    tiles = d_grp.reshape(n_tiles, TILE, D_W)
    first_rows = tiles[:, 0, :]
    last_rows = tiles[:, TILE - 1, :]
    zero = jnp.zeros((1, D_W), d_grp.dtype)
    prev = jnp.concatenate([zero, last_rows[:-1]], axis=0)
    nxt = jnp.concatenate([first_rows[1:], zero], axis=0)
    idx = np.arange(n_tiles)
    n_ctx_tiles = N_CTX_TOK // TILE
    per_seq = LAT_LEN // TILE
    lat_pos = (idx - n_ctx_tiles) % per_seq
    seq_start = np.where(idx < n_ctx_tiles, True, lat_pos == 0)
    seq_end = np.where(idx < n_ctx_tiles, True, lat_pos == per_seq - 1)
    prev = jnp.where(jnp.asarray(seq_start)[:, None], 0.0, prev)
    nxt = jnp.where(jnp.asarray(seq_end)[:, None], 0.0, nxt)
    return prev[:, None, :], nxt[:, None, :]


def kernel(x_prompt, x_sample, state_rwkv, cache_mla_ckv, cache_mla_krope, cache_gqa_k, cache_gqa_v, c, c_ctx, norm1_g, norm2_g, w_mod, b_mod, w_in, w_out, rw_w0, rw_wu, rw_a0, rw_au, rw_gu, rw_kk, rw_ka, rw_rk, rw_gn, mla_qa_g, mla_wuq, mla_kva_g, mla_wukv, mla_qn_g, mla_kn_g, gqa_qn_g, gqa_kn_g, conv_w, conv_b, peer_wq, peer_k1, peer_k2, peer_u, peer_v):
    depth = w_in.shape[0]
    params = dict(norm1_g=norm1_g, norm2_g=norm2_g, w_in=w_in, w_out=w_out, rw_w0=rw_w0, rw_wu=rw_wu,
                  rw_a0=rw_a0, rw_au=rw_au, rw_gu=rw_gu, rw_kk=rw_kk, rw_ka=rw_ka, rw_rk=rw_rk, rw_gn=rw_gn,
                  mla_qa_g=mla_qa_g, mla_wuq=mla_wuq, mla_kva_g=mla_kva_g, mla_wukv=mla_wukv,
                  mla_qn_g=mla_qn_g, mla_kn_g=mla_kn_g, gqa_qn_g=gqa_qn_g, gqa_kn_g=gqa_kn_g,
                  conv_w=conv_w, conv_b=conv_b, peer_wq=peer_wq, peer_k1=peer_k1, peer_k2=peer_k2,
                  peer_u=peer_u, peer_v=peer_v)
    lw_all = _layer_weights(params)
    tabs = _rope_tables()

    cvecs = jnp.concatenate([c_ctx[None, :], c, jnp.zeros((8 - 1 - N_LAT_SEQ, D_MODEL), F32)], axis=0)
    mods = _modulation(cvecs, w_mod, b_mod).reshape(depth, 8, 1, 6 * D_MODEL)

    x = jnp.concatenate([x_prompt.reshape(N_CTX_TOK, D_MODEL), x_sample.reshape(N_LAT_TOK, D_MODEL)], axis=0)

    s0_lat = _states_to_pairs(state_rwkv)
    s0_all = jnp.concatenate([jnp.zeros((N_CTX_SEQ,) + s0_lat.shape[1:], F32), s0_lat], axis=0)

    krope_placed = jnp.pad(cache_mla_krope, ((0, 0), (0, 0), (0, 0), (64, 32)))
    kctx_m, vctx_m = _mla_ctx_kv(cache_mla_ckv, krope_placed, lw_all["wk"], lw_all["wv"], lw_all["kn_g2"])
    past = cache_gqa_k.shape[2]
    kctx_g = cache_gqa_k.reshape(N_LAT_SEQ, depth, past, 128)
    vctx_g = jnp.swapaxes(cache_gqa_v.reshape(N_LAT_SEQ, depth, past, 128), -1, -2)

    ident = lambda p: p
    st_a, st_ckv, st_kr, st_k, st_v = [], [], [], [], []
    for l in range(depth):
        lw = {name: w[l] for name, w in lw_all.items() if name not in STACKED_WEIGHTS}
        mod = mods[l]
        a_grp, b_grp, c_grp, d_grp = _in_proj(x, mod, lw["norm1_g"], lw_all["w_in"], l)

        yf, yb, s_fin = _rwkv_scan(a_grp, s0_all[:, l], lw["w0"], lw["wu"], lw["a0"], lw["au"], lw["kkp"], lw["ka"])

        qm, km, vm, ckvn, qg, kg, vg, kgn = _attn_prep(b_grp, c_grp, tabs, lw)
        ob_ctx = _attention(qm, km, km, vm, lambda p: 2 * p, lambda p: 2 * p + 1, lambda p: 2 * p,
                            lambda p: 2 * p + 1, ident, N_CTX_SEQ, CTX_LEN, 0)
        ob_lat = _attention(qm, km, km, vm, lambda p: 2 * p, lambda p: 2 * p + 1, lambda p: 2 * p,
                            lambda p: 2 * p + 1, ident, N_LAT_SEQ, LAT_LEN, N_CTX_TOK,
                            ctx=(kctx_m[l], kctx_m[l], vctx_m[l]))
        oc_ctx = _attention(qg, kg, kg, vg, ident, lambda p: p + 2, ident, ident, ident,
                            N_CTX_SEQ, CTX_LEN, 0)
        oc_lat = _attention(qg, kg, kg, vg, ident, lambda p: p + 2, ident, ident, ident,
                            N_LAT_SEQ, LAT_LEN, N_CTX_TOK,
                            ctx=(kctx_g[:, l], kctx_g[:, l], vctx_g[:, l]))

        d_prev, d_next = _conv_halos(d_grp)
        x1, h2, q = _out_proj(x, mod, a_grp, yf, yb, ob_ctx, ob_lat, oc_ctx, oc_lat, d_grp, d_prev, d_next, lw,
                              lw_all["w_out"], lw_all["wq"], l)

        thr, s2, e1, e2 = _peer_topk(q, lw["k1"], lw["k2"])
        x = _peer_dense(h2, lw_all["u"], lw_all["vt"], l, thr, s2, e1, e2, x1, mod)

        st_a.append(_pairs_to_states(s_fin[:N_CTX_SEQ]))
        st_ckv.append(ckvn[:N_CTX_TOK].reshape(N_CTX_SEQ, CTX_LEN, 128))
        st_kr.append(b_grp[:N_CTX_TOK, 448:480].reshape(N_CTX_SEQ, CTX_LEN, 32))
        st_k.append(kgn[:N_CTX_TOK].reshape(N_CTX_SEQ, CTX_LEN, 2, HEAD_DIM))
        st_v.append(c_grp[:N_CTX_TOK, 384:512].reshape(N_CTX_SEQ, CTX_LEN, 2, HEAD_DIM))

    y_prompt = x[:N_CTX_TOK].reshape(N_CTX_SEQ, CTX_LEN, D_MODEL)
    y_sample = x[N_CTX_TOK:].reshape(N_LAT_SEQ, LAT_LEN, D_MODEL)
    return (y_prompt, y_sample, jnp.stack(st_a, axis=1), jnp.stack(st_ckv, axis=1), jnp.stack(st_kr, axis=1),
            jnp.stack(st_k, axis=1), jnp.stack(st_v, axis=1))
```

```python
import functools

import numpy as np
import jax
import jax.numpy as jnp
from jax import lax
from jax.experimental import pallas as pl
from jax.experimental.pallas import tpu as pltpu

F32 = jnp.float32
BF16 = jnp.bfloat16

D_MODEL = 1024
N_CTX_SEQ = 16
CTX_LEN = 256
N_LAT_SEQ = 2
LAT_LEN = 4096
GRID_W = 64
ROPE_THETA = 10000.0
EPS = 1e-6
GROUP_W = 256
HEAD_DIM = 64
RW_DECAY_SCALE = 0.6065306597
MLA_QK_DIM = 96
MLA_Q_RANK = 192
PEER_HEADS = 8
PEER_KEYS = 128
PEER_TOPK = 16
N_EXPERTS = PEER_KEYS * PEER_KEYS

N_CTX_TOK = N_CTX_SEQ * CTX_LEN
N_LAT_TOK = N_LAT_SEQ * LAT_LEN
N_TOK = N_CTX_TOK + N_LAT_TOK
COND_GROUP = 4096

A_W, B_W, C_W, D_W = 1152, 640, 512, 768
IN_PAD = A_W + B_W + C_W + D_W

SCAN_CHUNK = 128
TILE = 256
IN_TILE = 512
PEER_TN = 512
PEER_TE = 2048
TOPK_TN = 128
ATT_TQ = 256
ATT_CHUNK = 512

VMEM_LIMIT = 56 * 1024 * 1024


def _cparams(sem):
    return pltpu.CompilerParams(dimension_semantics=sem, vmem_limit_bytes=VMEM_LIMIT)


def _dot(a, b, precision=None):
    return jnp.dot(a, b, preferred_element_type=F32, precision=precision)


def _dot_nt(a, b, precision=None):
    return lax.dot_general(a, b, (((1,), (1,)), ((), ())), preferred_element_type=F32,
                           precision=precision)


def _dot_tn(a, b, precision=None):
    return lax.dot_general(a, b, (((0,), (0,)), ((), ())), preferred_element_type=F32,
                           precision=precision)


def _rms_rows(x, g):
    return x * lax.rsqrt(jnp.mean(x * x, axis=-1, keepdims=True) + EPS) * g


def _group_sums(x, group_mat):
    hi = x.astype(BF16)
    lo = (x - hi.astype(F32)).astype(BF16)
    gm = group_mat.astype(BF16)
    return _dot(hi, gm) + _dot(lo, gm)


def _group_matrix(n, group, value):
    r = lax.broadcasted_iota(jnp.int32, (n, n), 0) // group
    c = lax.broadcasted_iota(jnp.int32, (n, n), 1) // group
    return jnp.where(r == c, value, 0.0).astype(F32)


def _mod_body(c_ref, w_ref, b_ref, o_ref):
    c = c_ref[...]
    s = c * jax.nn.sigmoid(c)
    o_ref[...] = _dot(s.astype(BF16), w_ref[...].astype(BF16)) + b_ref[...]


def _modulation(cvecs, w_mod, b_mod):
    depth = w_mod.shape[0]
    tn = 1536
    return pl.pallas_call(
        _mod_body,
        grid=(depth, 6 * D_MODEL // tn),
        in_specs=[pl.BlockSpec((8, D_MODEL), lambda l, j: (0, 0)),
                  pl.BlockSpec((None, D_MODEL, tn), lambda l, j: (l, 0, j)),
                  pl.BlockSpec((None, 1, tn), lambda l, j: (l, 0, j))],
        out_specs=pl.BlockSpec((None, 8, tn), lambda l, j: (l, 0, j)),
        out_shape=jax.ShapeDtypeStruct((depth, 8, 6 * D_MODEL), F32),
        compiler_params=_cparams(("arbitrary", "arbitrary")),
        name="adaln_mod",
    )(cvecs, w_mod, b_mod.reshape(depth, 1, 6 * D_MODEL))


def _in_body(x_ref, mod_ref, g_ref, w_ref, a_ref, b_ref, c_ref, d_ref):
    mod = mod_ref[...]
    shift = mod[:, 0:D_MODEL]
    scale = mod[:, D_MODEL:2 * D_MODEL]
    h = _rms_rows(x_ref[...], g_ref[...]) * (1.0 + scale) + shift
    y = _dot(h.astype(BF16), w_ref[...])
    a_ref[...] = y[:, 0:A_W]
    b_ref[...] = y[:, A_W:A_W + B_W]
    c_ref[...] = y[:, A_W + B_W:A_W + B_W + C_W]
    d_ref[...] = y[:, A_W + B_W + C_W:IN_PAD]


def _in_proj(x, mod, norm_g, w_pad_all, layer):
    n = x.shape[0]
    tpg = COND_GROUP // IN_TILE
    row = lambda i: (i, 0)
    return pl.pallas_call(
        _in_body,
        grid=(n // IN_TILE,),
        in_specs=[pl.BlockSpec((IN_TILE, D_MODEL), row),
                  pl.BlockSpec((None, 1, 6 * D_MODEL), lambda i: (i // tpg, 0, 0)),
                  pl.BlockSpec((1, D_MODEL), lambda i: (0, 0)),
                  pl.BlockSpec((None, D_MODEL, IN_PAD), lambda i: (layer, 0, 0))],
        out_specs=[pl.BlockSpec((IN_TILE, A_W), row), pl.BlockSpec((IN_TILE, B_W), row),
                   pl.BlockSpec((IN_TILE, C_W), row), pl.BlockSpec((IN_TILE, D_W), row)],
        out_shape=[jax.ShapeDtypeStruct((n, A_W), F32), jax.ShapeDtypeStruct((n, B_W), F32),
                   jax.ShapeDtypeStruct((n, C_W), F32), jax.ShapeDtypeStruct((n, D_W), F32)],
        compiler_params=_cparams(("arbitrary",)),
        name="in_proj",
    )(x, mod, norm_g, w_pad_all)


def _chunk_cumsum(x, reverse):
    c = x.shape[0]
    rows = lax.broadcasted_iota(jnp.int32, x.shape, 0)
    sh = 1
    while sh < c:
        if reverse:
            x = x + jnp.where(rows < c - sh, pltpu.roll(x, c - sh, 0), 0.0)
        else:
            x = x + jnp.where(rows >= sh, pltpu.roll(x, sh, 0), 0.0)
        sh *= 2
    return x


def _scan_body(fb_ref, bb_ref, first_ref, last_ref, seq_ref,
               af_ref, ab_ref, s0_ref, w0_ref, wu_ref, a0_ref, au_ref, kkp_ref, ka_ref,
               yf_ref, yb_ref, sfin_ref, s_scr):
    step = pl.program_id(0)

    @pl.when(first_ref[step] == 1)
    def _():
        s_scr[...] = s0_ref[...]

    c = SCAN_CHUNK
    row = lax.broadcasted_iota(jnp.int32, (c, c), 0)
    col = lax.broadcasted_iota(jnp.int32, (c, c), 1)
    lane_c = lax.broadcasted_iota(jnp.int32, (c, 128), 1) < HEAD_DIM
    lane_2c = lax.broadcasted_iota(jnp.int32, (2 * c, 128), 1) < HEAD_DIM
    same_head = _group_matrix(128, HEAD_DIM, 1.0)
    kkp = kkp_ref[...]
    ka = ka_ref[...]

    def per_head(lo, hi):
        return jnp.where(lane_c, lo, hi)

    groups = []
    for d in range(2):
        x_ref = af_ref if d == 0 else ab_ref
        if d == 0:
            strict, incl = col < row, col <= row
        else:
            strict, incl = col > row, col >= row
        r = x_ref[:, 0:256]
        k = x_ref[:, 256:512]
        v = x_ref[:, 512:768]
        wd = x_ref[:, 768:896]
        ad = x_ref[:, 896:1024]
        wlog = -RW_DECAY_SCALE * jax.nn.sigmoid(
            w0_ref[d:d + 1, :] + _dot(jnp.tanh(wd).astype(BF16), wu_ref[d]))
        a = jax.nn.sigmoid(a0_ref[d:d + 1, :] + _dot(ad.astype(BF16), au_ref[d]))
        kd = k * (1.0 + (a - 1.0) * ka)
        kkf = k * kkp
        cum = _chunk_cumsum(wlog, reverse=(d == 1))
        tot = cum[c - 1:c, :] if d == 0 else cum[0:1, :]
        p_inc = jnp.exp(cum)
        p_exc = jnp.exp(cum - wlog)
        p_inv = jnp.exp(-cum)
        p_tot = jnp.exp(tot)
        for p in range(2):
            sl = slice(128 * p, 128 * p + 128)
            groups.append(dict(d=d, p=p, sl=sl, strict=strict, incl=incl, kk_raw=kkf[:, sl],
                               a=a[:, sl], kd=kd[:, sl], r=r[:, sl], vb=v[:, sl].astype(BF16),
                               p_inc=p_inc[:, sl], p_exc=p_exc[:, sl], p_inv=p_inv[:, sl], pt=p_tot[:, sl]))

    for g in groups:
        g["ss"] = _group_sums(g["kk_raw"] * g["kk_raw"], same_head)
    for g in groups:
        kk = g["kk_raw"] * lax.rsqrt(g["ss"] + EPS)
        g["a_p"] = (-kk * g["p_exc"]).astype(BF16)
        g["r_p"] = (g["r"] * g["p_inc"]).astype(BF16)
        g["b_i"] = kk * g["a"] * g["p_inv"]
        g["k_i"] = g["kd"] * g["p_inv"]
        g["lhs"] = jnp.concatenate([g["a_p"], g["r_p"]], axis=0)
        g["rhs"] = jnp.concatenate([g["b_i"], g["k_i"]], axis=0).astype(BF16)
        g["sb"] = s_scr[g["d"], g["p"]].astype(BF16)
    zero_b = jnp.zeros((), BF16)
    for g in groups:
        g["gram"] = (_dot_nt(jnp.where(lane_2c, g["lhs"], zero_b), g["rhs"]),
                     _dot_nt(jnp.where(lane_2c, zero_b, g["lhs"]), g["rhs"]))
        g["a_s"] = _dot_nt(g["a_p"], g["sb"])
        g["r_s"] = _dot_nt(g["r_p"], g["sb"])
    for g in groups:
        st, inc = g["strict"], g["incl"]
        g["pow"] = [jnp.where(st, gm[0:c, 0:c], 0.0).astype(BF16) for gm in g["gram"]]
        g["dm"] = [jnp.where(st, gm[0:c, c:2 * c], 0.0).astype(BF16) for gm in g["gram"]]
        g["et"] = [jnp.where(inc, gm[c:2 * c, 0:c], 0.0).astype(BF16) for gm in g["gram"]]
        g["ft"] = [jnp.where(inc, gm[c:2 * c, c:2 * c], 0.0).astype(BF16) for gm in g["gram"]]
    for g in groups:
        g["z"] = g["a_s"] + per_head(_dot(g["dm"][0], g["vb"]), _dot(g["dm"][1], g["vb"]))
    span = 1
    while span < c:
        for g in groups:
            zb = g["z"].astype(BF16)
            g["z"] = g["z"] + per_head(_dot(g["pow"][0], zb), _dot(g["pow"][1], zb))
        span *= 2
        if span < c:
            for g in groups:
                g["pow"] = [_dot(m, m).astype(BF16) for m in g["pow"]]
    for g in groups:
        zb = g["z"].astype(BF16)
        g["zb"] = zb
        y = g["r_s"] + per_head(_dot(g["et"][0], zb) + _dot(g["ft"][0], g["vb"]),
                                _dot(g["et"][1], zb) + _dot(g["ft"][1], g["vb"]))
        y_ref = yf_ref if g["d"] == 0 else yb_ref
        y_ref[:, g["sl"]] = y
    for g in groups:
        pt = g["pt"]
        upd = (_dot_tn(g["zb"], (g["b_i"] * pt).astype(BF16))
               + _dot_tn(g["vb"], (g["k_i"] * pt).astype(BF16)))
        s_scr[g["d"], g["p"]] = s_scr[g["d"], g["p"]] * pt + same_head * upd

    @pl.when(last_ref[step] == 1)
    def _():
        sfin_ref[...] = s_scr[...]


def _scan_tables():
    c = SCAN_CHUNK
    fb, bb, first, last, seq = [], [], [], [], []
    base = 0
    sid = 0
    for nseq, length in ((N_CTX_SEQ, CTX_LEN), (N_LAT_SEQ, LAT_LEN)):
        nc = length // c
        for _ in range(nseq):
            for j in range(nc):
                fb.append(base + j)
                bb.append(base + nc - 1 - j)
                first.append(int(j == 0))
                last.append(int(j == nc - 1))
                seq.append(sid)
            base += nc
            sid += 1
    return tuple(np.asarray(t, np.int32) for t in (fb, bb, first, last, seq))


def _rwkv_scan(a_grp, s0_pairs, w0, wu_pad, a0, au_pad, kkp, ka):
    tables = _scan_tables()
    n_steps = tables[0].shape[0]
    n_seq = s0_pairs.shape[0]
    n = a_grp.shape[0]
    c = SCAN_CHUNK
    const2 = lambda s, fb, bb, fi, la, sq: (0, 0)
    const3 = lambda s, fb, bb, fi, la, sq: (0, 0, 0)
    grid_spec = pltpu.PrefetchScalarGridSpec(
        num_scalar_prefetch=5,
        grid=(n_steps,),
        in_specs=[pl.BlockSpec((c, A_W), lambda s, fb, bb, fi, la, sq: (fb[s], 0)),
                  pl.BlockSpec((c, A_W), lambda s, fb, bb, fi, la, sq: (bb[s], 0)),
                  pl.BlockSpec((None, 2, 2, 128, 128), lambda s, fb, bb, fi, la, sq: (sq[s], 0, 0, 0, 0)),
                  pl.BlockSpec((2, GROUP_W), const2),
                  pl.BlockSpec((2, 128, GROUP_W), const3),
                  pl.BlockSpec((2, GROUP_W), const2),
                  pl.BlockSpec((2, 128, GROUP_W), const3),
                  pl.BlockSpec((1, GROUP_W), const2),
                  pl.BlockSpec((1, GROUP_W), const2)],
        out_specs=[pl.BlockSpec((c, GROUP_W), lambda s, fb, bb, fi, la, sq: (fb[s], 0)),
                   pl.BlockSpec((c, GROUP_W), lambda s, fb, bb, fi, la, sq: (bb[s], 0)),
                   pl.BlockSpec((None, 2, 2, 128, 128), lambda s, fb, bb, fi, la, sq: (sq[s], 0, 0, 0, 0))],
        scratch_shapes=[pltpu.VMEM((2, 2, 128, 128), F32)],
    )
    return pl.pallas_call(
        _scan_body,
        grid_spec=grid_spec,
        out_shape=[jax.ShapeDtypeStruct((n, GROUP_W), F32), jax.ShapeDtypeStruct((n, GROUP_W), F32),
                   jax.ShapeDtypeStruct((n_seq, 2, 2, 128, 128), F32)],
        compiler_params=_cparams(("arbitrary",)),
        name="rwkv_scan",
    )(*[jnp.asarray(t) for t in tables], a_grp, a_grp, s0_pairs, w0, wu_pad, a0, au_pad, kkp, ka)


LOG2_E = 1.4426950408889634
MLA_SCALE = MLA_QK_DIM ** -0.5 * LOG2_E
GQA_SCALE = HEAD_DIM ** -0.5 * LOG2_E


def _mla_keys(ckv_b, rope_slot, rope_slot_sw, wk_ref, kng_ref, cosm, sinm, k_out_ref):
    g = kng_ref[0:1, :]
    g_sw = kng_ref[1:2, :]
    for h in range(4):
        nope = _dot(ckv_b, wk_ref[h])
        kr = nope + rope_slot
        rs = lax.rsqrt(jnp.sum(kr * kr, axis=-1, keepdims=True) * (1.0 / MLA_QK_DIM) + EPS)
        if cosm is None:
            k_out_ref[h] = (kr * rs * g).astype(BF16)
        else:
            ks = nope + rope_slot_sw
            k_out_ref[h] = ((kr * rs * g) * cosm + (ks * rs * g_sw) * sinm).astype(BF16)


def _prep_body(b_ref, c_ref, cosm_ref, sinm_ref, cosg_ref, sing_ref,
               qag_ref, wqq_ref, qng_ref, kvag_ref, wk_ref, wv_ref, kng_ref, gqg_ref, gkg_ref,
               qm_ref, km_ref, vm_ref, ckvn_ref, qg_ref, kg_ref, vg_ref, kgn_ref):
    cosm = cosm_ref[...]
    sinm = sinm_ref[...]
    qc = b_ref[:, 0:256]
    qn = qc * lax.rsqrt(jnp.sum(qc * qc, axis=-1, keepdims=True) * (1.0 / MLA_Q_RANK) + EPS) * qag_ref[...]
    qnb = qn.astype(BF16)
    g = qng_ref[0:1, :]
    g_sw = qng_ref[1:2, :]
    for h in range(4):
        qr = _dot(qnb, wqq_ref[h])
        qs = _dot(qnb, wqq_ref[4 + h])
        rs = lax.rsqrt(jnp.sum(qr * qr, axis=-1, keepdims=True) * (1.0 / MLA_QK_DIM) + EPS)
        qm_ref[h] = (((qr * rs * g) * cosm + (qs * rs * g_sw) * sinm) * MLA_SCALE).astype(BF16)
    ckv = _rms_rows(b_ref[:, 256:384], kvag_ref[...])
    ckvn_ref[...] = ckv
    cb = ckv.astype(BF16)
    _mla_keys(cb, b_ref[:, 384:512], b_ref[:, 512:640], wk_ref, kng_ref, cosm, sinm, km_ref)
    for p in range(2):
        vm_ref[p] = _dot_nt(wv_ref[p], cb).astype(BF16)

    tm = c_ref.shape[0]
    avg = _group_matrix(128, HEAD_DIM, 1.0 / HEAD_DIM)
    lane = lax.broadcasted_iota(jnp.int32, (tm, 128), 1)
    first_half = (lane % HEAD_DIM) < (HEAD_DIM // 2)
    low = lane < HEAD_DIM
    cosg = cosg_ref[...]
    sing = sing_ref[...]

    def rotate(xn):
        swapped = jnp.where(first_half, pltpu.roll(xn, 128 - HEAD_DIM // 2, 1), pltpu.roll(xn, HEAD_DIM // 2, 1))
        return xn * cosg + swapped * sing

    for blk in range(2):
        x = c_ref[:, 128 * blk:128 * blk + 128]
        xn = x * lax.rsqrt(_group_sums(x * x, avg) + EPS) * gqg_ref[...]
        xr = rotate(xn) * GQA_SCALE
        qg_ref[blk] = jnp.where(low, xr, 0.0).astype(BF16)
        qg_ref[blk + 2] = jnp.where(low, 0.0, xr).astype(BF16)
    xk = c_ref[:, 256:384]
    kn = xk * lax.rsqrt(_group_sums(xk * xk, avg) + EPS) * gkg_ref[...]
    kgn_ref[...] = kn
    kg_ref[...] = rotate(kn).astype(BF16)
    vg_ref[...] = c_ref[:, 384:512].T.astype(BF16)


def _attn_prep(b_grp, c_grp, tabs, lw):
    n = b_grp.shape[0]
    n_ctx_tiles = N_CTX_TOK // TILE
    lat_tiles = LAT_LEN // TILE
    row = lambda i: (i, 0)
    hrow = lambda i: (0, i, 0)
    tab = lambda i: (jnp.where(i < n_ctx_tiles, 0, 1 + (i - n_ctx_tiles) % lat_tiles), 0)
    c2 = lambda i: (0, 0)
    c3 = lambda i: (0, 0, 0)
    return pl.pallas_call(
        _prep_body,
        grid=(n // TILE,),
        in_specs=[pl.BlockSpec((TILE, B_W), row), pl.BlockSpec((TILE, C_W), row),
                  pl.BlockSpec((TILE, 128), tab), pl.BlockSpec((TILE, 128), tab),
                  pl.BlockSpec((TILE, 128), tab), pl.BlockSpec((TILE, 128), tab),
                  pl.BlockSpec((1, 256), c2), pl.BlockSpec((8, 256, 128), c3), pl.BlockSpec((2, 128), c2),
                  pl.BlockSpec((1, 128), c2), pl.BlockSpec((4, 128, 128), c3), pl.BlockSpec((2, 128, 128), c3),
                  pl.BlockSpec((2, 128), c2), pl.BlockSpec((1, 128), c2), pl.BlockSpec((1, 128), c2)],
        out_specs=[pl.BlockSpec((4, TILE, 128), hrow), pl.BlockSpec((4, TILE, 128), hrow),
                   pl.BlockSpec((2, 128, TILE), lambda i: (0, 0, i)), pl.BlockSpec((TILE, 128), row),
                   pl.BlockSpec((4, TILE, 128), hrow), pl.BlockSpec((TILE, 128), row),
                   pl.BlockSpec((128, TILE), lambda i: (0, i)), pl.BlockSpec((TILE, 128), row)],
        out_shape=[jax.ShapeDtypeStruct((4, n, 128), BF16), jax.ShapeDtypeStruct((4, n, 128), BF16),
                   jax.ShapeDtypeStruct((2, 128, n), BF16), jax.ShapeDtypeStruct((n, 128), F32),
                   jax.ShapeDtypeStruct((4, n, 128), BF16), jax.ShapeDtypeStruct((n, 128), BF16),
                   jax.ShapeDtypeStruct((128, n), BF16), jax.ShapeDtypeStruct((n, 128), F32)],
        compiler_params=_cparams(("arbitrary",)),
        name="attn_prep",
    )(b_grp, c_grp, tabs["cosm"], tabs["sinm"], tabs["cosg"], tabs["sing"],
      lw["qa_g"], lw["wqq"], lw["qn_g2"], lw["kva_g"], lw["wk"], lw["wv"], lw["kn_g2"], lw["gq_g"], lw["gk_g"])


def _ctxkv_body(ckv_ref, krp_ref, wk_ref, wv_ref, kng_ref, k_ref, v_ref):
    cb = ckv_ref[...].astype(BF16)
    _mla_keys(cb, krp_ref[...], None, wk_ref, kng_ref, None, None, k_ref)
    for p in range(2):
        v_ref[p] = _dot_nt(wv_ref[p], cb).astype(BF16)


def _mla_ctx_kv(cache_ckv, cache_krope_placed, wk, wv, kn_g2):
    nb, depth, plen, _ = cache_ckv.shape
    return pl.pallas_call(
        _ctxkv_body,
        grid=(depth, nb),
        in_specs=[pl.BlockSpec((None, None, plen, 128), lambda l, b: (b, l, 0, 0)),
                  pl.BlockSpec((None, None, plen, 128), lambda l, b: (b, l, 0, 0)),
                  pl.BlockSpec((None, 4, 128, 128), lambda l, b: (l, 0, 0, 0)),
                  pl.BlockSpec((None, 2, 128, 128), lambda l, b: (l, 0, 0, 0)),
                  pl.BlockSpec((None, 2, 128), lambda l, b: (l, 0, 0))],
        out_specs=[pl.BlockSpec((None, 4, None, plen, 128), lambda l, b: (l, 0, b, 0, 0)),
                   pl.BlockSpec((None, 2, None, 128, plen), lambda l, b: (l, 0, b, 0, 0))],
        out_shape=[jax.ShapeDtypeStruct((depth, 4, nb, plen, 128), BF16),
                   jax.ShapeDtypeStruct((depth, 2, nb, 128, plen), BF16)],
        compiler_params=_cparams(("arbitrary", "arbitrary")),
        name="mla_ctx_kv",
    )(cache_ckv, cache_krope_placed, wk, wv, kn_g2)


def _attn_body(*refs, has_ctx):
    if has_ctx:
        qa_ref, qb_ref, ka_ref, kb_ref, vt_ref, kca_ref, kcb_ref, vct_ref, o_ref, sa_scr, sb_scr = refs
    else:
        qa_ref, qb_ref, ka_ref, kb_ref, vt_ref, o_ref, sa_scr, sb_scr = refs
        kca_ref = kcb_ref = vct_ref = None
    tk = ka_ref.shape[0]
    chunk = min(ATT_CHUNK, tk)
    q = (qa_ref[...], qb_ref[...])
    k_refs = (ka_ref, kb_ref)
    kc_refs = (kca_ref, kcb_ref)
    s_scr = (sa_scr, sb_scr)
    vrows = (slice(0, HEAD_DIM), slice(HEAD_DIM, 2 * HEAD_DIM))
    spans = [("new", c * chunk, chunk) for c in range(tk // chunk)]
    if has_ctx:
        spans = [("ctx", 0, kca_ref.shape[0])] + spans

    def scratch_rows(kind, start, size):
        base = tk if kind == "ctx" else 0
        return slice(base + start, base + start + size)

    m = [None, None]
    den = [None, None]
    acc = [None, None]

    def score_pass(hd, span):
        kind, start, size = span
        if kind == "ctx":
            keys = kc_refs[hd][...].astype(BF16)
        else:
            keys = k_refs[hd][start:start + size, :]
        s = _dot_nt(keys, q[hd])
        s_scr[hd][scratch_rows(kind, start, size), :] = s
        cm = jnp.max(s, axis=0, keepdims=True)
        m[hd] = cm if m[hd] is None else jnp.maximum(m[hd], cm)

    def value_pass(hd, span):
        kind, start, size = span
        e = jnp.exp2(s_scr[hd][scratch_rows(kind, start, size), :] - m[hd])
        if kind == "ctx":
            vals = vct_ref[vrows[hd], :].astype(BF16)
        else:
            vals = vt_ref[vrows[hd], start:start + size]
        part = _dot(vals, e.astype(BF16))
        rs = jnp.sum(e, axis=0, keepdims=True)
        den[hd] = rs if den[hd] is None else den[hd] + rs
        acc[hd] = part if acc[hd] is None else acc[hd] + part

    for span in spans:
        for hd in range(2):
            score_pass(hd, span)
    for span in spans:
        for hd in range(2):
            value_pass(hd, span)
    out_t = jnp.concatenate([acc[0] / den[0], acc[1] / den[1]], axis=0)
    o_ref[...] = out_t.T.astype(o_ref.dtype)


def _attention(q, ka, kb, v, head_a, head_b, k_head_a, k_head_b, v_idx, n_seq, seq_len, tok0, ctx=None):
    tq = min(ATT_TQ, seq_len)
    nq = seq_len // tq
    seq0 = tok0 // seq_len
    q0 = tok0 // tq

    def qmap(hsel):
        return lambda b, p, i: (hsel(p), q0 + b * nq + i, 0)

    def kmap(hsel, arr):
        if arr.ndim == 3:
            return lambda b, p, i: (hsel(p), seq0 + b, 0)
        return lambda b, p, i: (seq0 + b, 0)

    def kspec(arr, hsel):
        if arr.ndim == 3:
            return pl.BlockSpec((None, seq_len, 128), kmap(hsel, arr))
        return pl.BlockSpec((seq_len, 128), kmap(hsel, arr))

    if v.ndim == 3:
        vspec = pl.BlockSpec((None, 128, seq_len), lambda b, p, i: (v_idx(p), 0, seq0 + b))
    else:
        vspec = pl.BlockSpec((128, seq_len), lambda b, p, i: (0, seq0 + b))
    in_specs = [pl.BlockSpec((None, tq, 128), qmap(head_a)), pl.BlockSpec((None, tq, 128), qmap(head_b)),
                kspec(ka, k_head_a), kspec(kb, k_head_b), vspec]
    args = [q, q, ka, kb, v]
    if ctx is not None:
        kca, kcb, vc = ctx
        past = vc.shape[-1]

        def cspec(arr, hsel, shape):
            if arr.ndim == 4:
                return pl.BlockSpec((None, None) + shape, lambda b, p, i: (hsel(p), b, 0, 0))
            return pl.BlockSpec((None,) + shape, lambda b, p, i: (b, 0, 0))

        in_specs += [cspec(kca, k_head_a, (past, 128)), cspec(kcb, k_head_b, (past, 128)),
                     cspec(vc, v_idx, (128, past))]
        args += [kca, kcb, vc]
    return pl.pallas_call(
        functools.partial(_attn_body, has_ctx=ctx is not None),
        grid=(n_seq, 2, nq),
        in_specs=in_specs,
        out_specs=pl.BlockSpec((tq, 128), lambda b, p, i: (b * nq + i, p)),
        out_shape=jax.ShapeDtypeStruct((n_seq * seq_len, 256), BF16),
        scratch_shapes=[pltpu.VMEM((seq_len + (0 if ctx is None else ctx[2].shape[-1]), tq), F32)] * 2,
        compiler_params=_cparams(("arbitrary", "arbitrary", "arbitrary")),
        name="attention_ctx" if ctx is None else "attention_lat",
    )(*args)


def _out_body(x_ref, mod_ref, a_ref, yf_ref, yb_ref, obc_ref, obl_ref, occ_ref, ocl_ref,
              d_ref, dprev_ref, dnext_ref, wo_ref, gu_ref, rk_ref, gn_ref, cw_ref, cb_ref, n2_ref, wq_ref,
              x1_ref, h2_ref, q_ref, *, n_ctx_tiles):
    i = pl.program_id(0)
    is_ctx = i < n_ctx_tiles
    tm = x_ref.shape[0]
    mod = mod_ref[...]
    gate1 = mod[:, 2 * D_MODEL:3 * D_MODEL]
    shift2 = mod[:, 3 * D_MODEL:4 * D_MODEL]
    scale2 = mod[:, 4 * D_MODEL:5 * D_MODEL]

    r = a_ref[:, 0:256]
    k = a_ref[:, 256:512]
    v = a_ref[:, 512:768]
    gd = a_ref[:, 1024:1152]
    y = yf_ref[...] + yb_ref[...]
    avg = _group_matrix(GROUP_W, HEAD_DIM, 1.0 / HEAD_DIM)
    ones = _group_matrix(GROUP_W, HEAD_DIM, 1.0)
    yn = y * lax.rsqrt(_group_sums(y * y, avg) + EPS) * gn_ref[...]
    bonus = _group_sums(r * k * rk_ref[...], ones) * v
    gate = _dot(jax.nn.sigmoid(gd).astype(BF16), gu_ref[...])
    o_a = (yn + bonus) * gate

    o_b = jnp.where(is_ctx, obc_ref[...], obl_ref[...])
    o_c = jnp.where(is_ctx, occ_ref[...], ocl_ref[...])

    u = d_ref[:, 512:768] * d_ref[:, 0:256]
    u_prev = dprev_ref[:, 512:768] * dprev_ref[:, 0:256]
    u_next = dnext_ref[:, 512:768] * dnext_ref[:, 0:256]
    rows = lax.broadcasted_iota(jnp.int32, (tm, GROUP_W), 0)
    up = jnp.where(rows == 0, u_prev, pltpu.roll(u, 1, 0))
    un = jnp.where(rows == tm - 1, u_next, pltpu.roll(u, tm - 1, 0))
    conv = up * cw_ref[0:1, :] + u * cw_ref[1:2, :] + un * cw_ref[2:3, :] + cb_ref[...]
    o_d = d_ref[:, 256:512] * conv

    mix_in = jnp.concatenate([o_a.astype(BF16), o_b, o_c, o_d.astype(BF16)], axis=1)
    x1 = x_ref[...] + gate1 * _dot(mix_in, wo_ref[...])
    x1_ref[...] = x1
    h2 = (_rms_rows(x1, n2_ref[...]) * (1.0 + scale2) + shift2).astype(BF16)
    h2_ref[...] = h2
    q_ref[...] = _dot(h2, wq_ref[...]).astype(q_ref.dtype)


def _out_proj(x, mod, a_grp, yf, yb, ob_ctx, ob_lat, oc_ctx, oc_lat, d_grp, d_prev, d_next, lw, w_out_all, wq_all,
              layer):
    n = x.shape[0]
    n_tiles = n // TILE
    n_ctx_tiles = N_CTX_TOK // TILE
    tpg = COND_GROUP // TILE
    row = lambda i: (i, 0)
    ctx_row = lambda i: (jnp.minimum(i, n_ctx_tiles - 1), 0)
    lat_row = lambda i: (jnp.maximum(i - n_ctx_tiles, 0), 0)
    halo = lambda i: (i, 0, 0)
    c2 = lambda i: (0, 0)
    nq = PEER_HEADS * 2 * PEER_KEYS
    return pl.pallas_call(
        functools.partial(_out_body, n_ctx_tiles=n_ctx_tiles),
        grid=(n_tiles,),
        in_specs=[pl.BlockSpec((TILE, D_MODEL), row),
                  pl.BlockSpec((None, 1, 6 * D_MODEL), lambda i: (i // tpg, 0, 0)),
                  pl.BlockSpec((TILE, A_W), row),
                  pl.BlockSpec((TILE, GROUP_W), row), pl.BlockSpec((TILE, GROUP_W), row),
                  pl.BlockSpec((TILE, GROUP_W), ctx_row), pl.BlockSpec((TILE, GROUP_W), lat_row),
                  pl.BlockSpec((TILE, GROUP_W), ctx_row), pl.BlockSpec((TILE, GROUP_W), lat_row),
                  pl.BlockSpec((TILE, D_W), row),
                  pl.BlockSpec((None, 1, D_W), halo), pl.BlockSpec((None, 1, D_W), halo),
                  pl.BlockSpec((None, D_MODEL, D_MODEL), lambda i: (layer, 0, 0)), pl.BlockSpec((128, GROUP_W), c2),
                  pl.BlockSpec((1, GROUP_W), c2), pl.BlockSpec((1, GROUP_W), c2),
                  pl.BlockSpec((3, GROUP_W), c2), pl.BlockSpec((1, GROUP_W), c2),
                  pl.BlockSpec((1, D_MODEL), c2), pl.BlockSpec((None, D_MODEL, nq), lambda i: (layer, 0, 0))],
        out_specs=[pl.BlockSpec((TILE, D_MODEL), row), pl.BlockSpec((TILE, D_MODEL), row),
                   pl.BlockSpec((TILE, nq), row)],
        out_shape=[jax.ShapeDtypeStruct((n, D_MODEL), F32), jax.ShapeDtypeStruct((n, D_MODEL), BF16),
                   jax.ShapeDtypeStruct((n, nq), BF16)],
        compiler_params=_cparams(("arbitrary",)),
        name="out_proj",
    )(x, mod, a_grp, yf, yb, ob_ctx, ob_lat, oc_ctx, oc_lat, d_grp, d_prev, d_next,
      w_out_all, lw["gu"], lw["rk"], lw["gn"], lw["conv_w"], lw["conv_b"], lw["norm2_g"], wq_all)


def _exchange(a, b):
    if a is None:
        return b, None
    if b is None:
        return a, None
    return jnp.maximum(a, b), jnp.minimum(a, b)


def _sort16_desc(xs):
    xs = list(xs)
    k = 2
    while k <= 16:
        j = k // 2
        while j >= 1:
            for i in range(16):
                partner = i ^ j
                if partner > i:
                    hi, lo = _exchange(xs[i], xs[partner])
                    xs[i], xs[partner] = (hi, lo) if (i & k) == 0 else (lo, hi)
            j //= 2
        k *= 2
    return xs


def _bitonic_merge_desc(xs):
    xs = list(xs)
    j = 8
    while j >= 1:
        for i in range(16):
            partner = i ^ j
            if partner > i:
                xs[i], xs[partner] = _exchange(xs[i], xs[partner])
        j //= 2
    return xs


def _top16_of_rows(xs):
    ys = _sort16_desc(xs)
    for shift in (4, 2, 1):
        zs = [None if y is None else pltpu.roll(y, shift, 0) for y in ys]
        ts = [_exchange(ys[i], zs[15 - i])[0] for i in range(16)]
        ys = _bitonic_merge_desc(ts)
    return ys


def _topk_body(q_ref, k1_ref, k2_ref, thr_ref, s2_ref, e1_ref, e2_ref):
    tn = q_ref.shape[0]
    sub = lax.broadcasted_iota(jnp.int32, (8, tn), 0)

    def spread(vals):
        out = vals[7]
        for s in range(6, -1, -1):
            out = jnp.where(sub == s, vals[s], out)
        return out

    for h in range(PEER_HEADS):
        qa = q_ref[:, (2 * h) * PEER_KEYS:(2 * h + 1) * PEER_KEYS].astype(BF16)
        qb = q_ref[:, (2 * h + 1) * PEER_KEYS:(2 * h + 2) * PEER_KEYS].astype(BF16)
        s1 = _dot_nt(k1_ref[h], qa)
        s2 = _dot_nt(k2_ref[h], qb)
        v1 = _top16_of_rows([s1[8 * i:8 * i + 8, :] for i in range(16)])
        v2 = _top16_of_rows([s2[8 * i:8 * i + 8, :] for i in range(16)])
        v2_lo, v2_hi, v1_hi = spread(v2[0:8]), spread(v2[8:16]), spread(v1[8:16])
        cands = ([v1[0] + v2_lo, v1[0] + v2_hi] + [v1[a] + v2_lo for a in range(1, 8)]
                 + [v1_hi + v2[0]] + [None] * 6)
        best = _top16_of_rows(cands)
        tau = best[PEER_TOPK - 1][0:1, :]
        zsum = jnp.exp(best[0] - best[0])
        for kth in range(1, PEER_TOPK):
            zsum = zsum + jnp.exp(best[kth] - best[0])
        thr = jnp.full(s1.shape, jnp.inf, F32)
        for b in range(PEER_TOPK):
            vb = v2[b][0:1, :]
            thr = jnp.where(s1 + vb >= tau, vb, thr)
        thr_ref[h] = thr
        s2_ref[h] = s2
        e1_ref[h] = jnp.exp(s1 - v1[0][0:1, :])
        e2_ref[h] = jnp.exp(s2 - v2[0][0:1, :]) * (0.5 / zsum[0:1, :])


def _rotated_block(i):
    per_tile = PEER_TN // TOPK_TN
    return (i // per_tile) * per_tile + (i + 1) % per_tile


def _peer_topk(q, k1, k2):
    n = q.shape[0]
    tn = TOPK_TN
    big = pl.BlockSpec((PEER_HEADS, PEER_KEYS, tn), lambda i: (0, 0, i))
    big_shape = jax.ShapeDtypeStruct((PEER_HEADS, PEER_KEYS, n), F32)
    c3 = lambda i: (0, 0, 0)
    return pl.pallas_call(
        _topk_body,
        grid=(n // tn,),
        in_specs=[pl.BlockSpec((tn, PEER_HEADS * 2 * PEER_KEYS), lambda i: (i, 0)),
                  pl.BlockSpec((PEER_HEADS, PEER_KEYS, PEER_KEYS), c3),
                  pl.BlockSpec((PEER_HEADS, PEER_KEYS, PEER_KEYS), c3)],
        out_specs=[big, big, big, pl.BlockSpec((PEER_HEADS, PEER_KEYS, tn), lambda i: (0, 0, _rotated_block(i)))],
        out_shape=[big_shape, big_shape, big_shape, big_shape],
        compiler_params=_cparams(("arbitrary",)),
        name="peer_topk",
    )(q, k1, k2)


GELU_C0 = 0.7978845608028654
GELU_C1 = 0.044715


PEER_PIECE = 256
GATE_KEYS = 64
VALUE_PIECES = 1


def _dense_body(h2_ref, u_ref, vt_ref, thr_ref, s2_ref, e1_ref, e2_ref, x1_ref, mod_ref,
                o_ref, acc, *piece_scratch):
    j = pl.program_id(1)

    @pl.when(j == 0)
    def _():
        acc[...] = jnp.zeros_like(acc)

    n_pieces = u_ref.shape[0] // PEER_PIECE
    hs, gs = piece_scratch[:n_pieces], piece_scratch[n_pieces:]
    tn = hs[0].shape[1]
    n_col = tn // 128
    h2 = h2_ref[...]

    def pre_activations(p):
        rows = slice(p * PEER_PIECE, (p + 1) * PEER_PIECE)
        hs[p][...] = _dot_nt(u_ref[rows, :], h2)

    pre_activations(0)
    for p in range(n_pieces):
        if p + 1 < n_pieces:
            pre_activations(p + 1)
        halves = PEER_PIECE // PEER_KEYS
        per_col = PEER_KEYS // GATE_KEYS
        for sub in range(per_col * n_col):
            keys = slice((sub % per_col) * GATE_KEYS, (sub % per_col) * GATE_KEYS + GATE_KEYS)
            c = sub // per_col
            cols = slice(c * 128, c * 128 + 128)
            e2_cols = slice(((c + 1) % n_col) * 128, ((c + 1) % n_col) * 128 + 128)
            n_grp = GATE_KEYS // 8
            w = [[None] * n_grp for _ in range(halves)]
            for h in range(PEER_HEADS):
                thr8 = [jnp.broadcast_to(thr_ref[h, p * halves + half:p * halves + half + 1, cols], (8, 128))
                        for half in range(halves)]
                e18 = [jnp.broadcast_to(e1_ref[h, p * halves + half:p * halves + half + 1, cols], (8, 128))
                       for half in range(halves)]
                for g in range(n_grp):
                    grp = slice(keys.start + 8 * g, keys.start + 8 * g + 8)
                    s2g = s2_ref[h, grp, cols]
                    e2g = e2_ref[h, grp, e2_cols]
                    for half in range(halves):
                        term = jnp.where(s2g >= thr8[half], e18[half] * e2g, 0.0)
                        w[half][g] = term if w[half][g] is None else w[half][g] + term
            for half in range(halves):
                rows = slice(half * PEER_KEYS + keys.start, half * PEER_KEYS + keys.stop)
                x = hs[p][rows, cols]
                inner = x * (GELU_C0 + (GELU_C0 * GELU_C1) * (x * x))
                wh = jnp.concatenate(w[half], axis=0)
                grows = slice((p % VALUE_PIECES) * PEER_PIECE + rows.start,
                              (p % VALUE_PIECES) * PEER_PIECE + rows.stop)
                gs[p // VALUE_PIECES][grows, cols] = (wh * x * (1.0 + jnp.tanh(inner))).astype(BF16)
        if (p + 1) % VALUE_PIECES == 0:
            grp_i = p // VALUE_PIECES
            vrows = slice(grp_i * VALUE_PIECES * PEER_PIECE, (grp_i + 1) * VALUE_PIECES * PEER_PIECE)
            acc[...] += _dot(vt_ref[:, vrows], gs[grp_i][...])

    @pl.when(j == pl.num_programs(1) - 1)
    def _():
        gate2 = mod_ref[:, 5 * D_MODEL:6 * D_MODEL]
        o_ref[...] = x1_ref[...] + gate2 * acc[...].T


def _peer_dense(h2, u_all, vt_all, layer, thr, s2, e1, e2, x1, mod):
    n = h2.shape[0]
    tn, te = PEER_TN, PEER_TE
    rows_per_step = te // PEER_KEYS
    tpg = COND_GROUP // tn
    tok = lambda i, j: (i, 0)
    key_rows = pl.BlockSpec((PEER_HEADS, rows_per_step, tn), lambda i, j: (0, j, i))
    key_all = pl.BlockSpec((PEER_HEADS, PEER_KEYS, tn), lambda i, j: (0, 0, i))
    return pl.pallas_call(
        _dense_body,
        grid=(n // tn, N_EXPERTS // te),
        in_specs=[pl.BlockSpec((tn, D_MODEL), tok),
                  pl.BlockSpec((None, te, D_MODEL), lambda i, j: (layer, j, 0)),
                  pl.BlockSpec((None, D_MODEL, te), lambda i, j: (layer, 0, j)),
                  key_rows, key_all, key_rows, key_all,
                  pl.BlockSpec((tn, D_MODEL), tok),
                  pl.BlockSpec((None, 1, 6 * D_MODEL), lambda i, j: (i // tpg, 0, 0))],
        out_specs=pl.BlockSpec((tn, D_MODEL), tok),
        out_shape=jax.ShapeDtypeStruct((n, D_MODEL), F32),
        scratch_shapes=([pltpu.VMEM((D_MODEL, tn), F32)]
                        + [pltpu.VMEM((PEER_PIECE, tn), F32)] * (te // PEER_PIECE)
                        + [pltpu.VMEM((VALUE_PIECES * PEER_PIECE, tn), BF16)] * (te // (VALUE_PIECES * PEER_PIECE))),
        compiler_params=_cparams(("arbitrary", "arbitrary")),
        name="peer_dense",
    )(h2, u_all, vt_all, thr, s2, e1, e2, x1, mod)


def _in_proj_columns():
    src = np.full((IN_PAD,), -1, np.int64)

    def put(dst, start, width):
        src[dst:dst + width] = np.arange(start, start + width)

    put(0, 0, 768)
    put(768, 768, 128)
    put(896, 896, 64)
    put(1024, 960, 128)
    b0 = A_W
    put(b0, 1088, 192)
    put(b0 + 256, 1280, 128)
    put(b0 + 384 + 64, 1408, 32)
    put(b0 + 512 + 64, 1408 + 16, 16)
    put(b0 + 512 + 80, 1408, 16)
    c0 = A_W + B_W
    for slot, head in enumerate((0, 2, 1, 3)):
        put(c0 + 64 * slot, 1440 + 64 * head, 64)
    put(c0 + 256, 1696, 256)
    put(A_W + B_W + C_W, 1952, 768)
    return src


def _swap_tail(w):
    return jnp.concatenate([w[..., :64], w[..., 80:96], w[..., 64:80]], axis=-1)


def _pad_last(w, width):
    return jnp.pad(w, [(0, 0)] * (w.ndim - 1) + [(0, width - w.shape[-1])])


STACKED_WEIGHTS = ("w_in", "w_out", "wq", "u", "vt")


def _layer_weights(p):
    depth = p["w_in"].shape[0]
    src = _in_proj_columns()
    runs, start = [], 0
    for i in range(1, IN_PAD + 1):
        if i == IN_PAD or (src[i] != src[i - 1] + 1 if src[i - 1] >= 0 else src[i] >= 0):
            runs.append((start, i))
            start = i
    w_bf = p["w_in"].astype(BF16)
    pieces = [w_bf[:, :, src[a]:src[a] + (b - a)] if src[a] >= 0
              else jnp.zeros((depth, D_MODEL, b - a), BF16) for a, b in runs]
    w_in = jnp.concatenate(pieces, axis=2)

    wuq = p["mla_wuq"].reshape(depth, MLA_Q_RANK, 4, MLA_QK_DIM).transpose(0, 2, 1, 3)
    wq_plain = jnp.pad(wuq, ((0, 0), (0, 0), (0, 256 - MLA_Q_RANK), (0, 128 - MLA_QK_DIM)))
    wq_swap = jnp.pad(_swap_tail(wuq), ((0, 0), (0, 0), (0, 256 - MLA_Q_RANK), (0, 128 - MLA_QK_DIM)))
    wukv = p["mla_wukv"].reshape(depth, 128, 4, 128)
    wk = _pad_last(wukv[..., :64].transpose(0, 2, 1, 3), 128)
    wv_heads = wukv[..., 64:].transpose(0, 2, 1, 3)
    wv = jnp.concatenate([wv_heads[:, 0::2], wv_heads[:, 1::2]], axis=-1)

    def gain2(g):
        return jnp.stack([_pad_last(g, 128), _pad_last(_swap_tail(g), 128)], axis=1)

    w_out = p["w_out"]
    oc = w_out[:, 512:768].reshape(depth, 4, 64, D_MODEL)[:, jnp.asarray([0, 2, 1, 3])].reshape(depth, 256, D_MODEL)
    w_out = jnp.concatenate([w_out[:, :512], oc, w_out[:, 768:]], axis=1).astype(BF16)

    wu_pad = jnp.zeros((depth, 2, 128, GROUP_W), F32)
    wu_pad = wu_pad.at[:, 0, 0:64].set(p["rw_wu"][:, 0]).at[:, 1, 64:128].set(p["rw_wu"][:, 1])
    au_pad = jnp.zeros((depth, 2, 128, GROUP_W), F32)
    au_pad = au_pad.at[:, 0, 0:32].set(p["rw_au"][:, 0]).at[:, 1, 32:64].set(p["rw_au"][:, 1])

    return {
        "w_in": w_in,
        "norm1_g": p["norm1_g"][:, None, :],
        "norm2_g": p["norm2_g"][:, None, :],
        "w_out": w_out,
        "w0": p["rw_w0"], "a0": p["rw_a0"],
        "wu": wu_pad.astype(BF16), "au": au_pad.astype(BF16),
        "kkp": p["rw_kk"][:, None, :], "ka": p["rw_ka"][:, None, :],
        "gu": p["rw_gu"].astype(BF16),
        "rk": p["rw_rk"].reshape(depth, 1, GROUP_W),
        "gn": p["rw_gn"][:, None, :],
        "qa_g": _pad_last(p["mla_qa_g"], 256)[:, None, :],
        "wqq": jnp.concatenate([wq_plain, wq_swap], axis=1).astype(BF16),
        "qn_g2": gain2(p["mla_qn_g"]),
        "kva_g": p["mla_kva_g"][:, None, :],
        "wk": wk.astype(BF16), "wv": jnp.swapaxes(wv, -1, -2).astype(BF16),
        "kn_g2": gain2(p["mla_kn_g"]),
        "gq_g": jnp.tile(p["gqa_qn_g"], (1, 2))[:, None, :],
        "gk_g": jnp.tile(p["gqa_kn_g"], (1, 2))[:, None, :],
        "conv_w": p["conv_w"], "conv_b": p["conv_b"][:, None, :],
        "wq": p["peer_wq"].astype(BF16),
        "k1": p["peer_k1"].astype(BF16), "k2": p["peer_k2"].astype(BF16),
        "u": p["peer_u"].astype(BF16),
        "vt": jnp.swapaxes(p["peer_v"], 1, 2).astype(BF16),
    }


def _rope_tables():
    t = jnp.arange(LAT_LEN, dtype=F32)
    grid_row = jnp.floor(t / GRID_W)
    grid_col = t - grid_row * GRID_W

    def angles(rot_dim):
        n_freq = rot_dim // 4
        freqs = ROPE_THETA ** (-jnp.arange(n_freq, dtype=F32) / n_freq)
        ang = jnp.concatenate([grid_row[:, None] * freqs, grid_col[:, None] * freqs], axis=-1)
        return jnp.cos(ang), jnp.sin(ang)

    cm, sm = angles(32)
    ones64 = jnp.ones((LAT_LEN, 64), F32)
    zeros64 = jnp.zeros((LAT_LEN, 64), F32)
    cosm = jnp.concatenate([ones64, cm, cm, ones64[:, :32]], axis=-1)
    sinm = jnp.concatenate([zeros64, -sm, sm, zeros64[:, :32]], axis=-1)
    cg, sg = angles(64)
    cosg = jnp.tile(jnp.concatenate([cg, cg], axis=-1), (1, 2))
    sing = jnp.tile(jnp.concatenate([-sg, sg], axis=-1), (1, 2))
    ident = jnp.ones((TILE, 128), F32)
    zero = jnp.zeros((TILE, 128), F32)
    return {"cosm": jnp.concatenate([ident, cosm]), "sinm": jnp.concatenate([zero, sinm]),
            "cosg": jnp.concatenate([ident, cosg]), "sing": jnp.concatenate([zero, sing])}


def _states_to_pairs(s):
    lead = s.shape[:-3]
    s = s.reshape(lead + (2, 2, 64, 64))
    z = jnp.zeros(lead + (2, 64, 64), s.dtype)
    top = jnp.concatenate([s[..., 0, :, :], z], axis=-1)
    bot = jnp.concatenate([z, s[..., 1, :, :]], axis=-1)
    return jnp.concatenate([top, bot], axis=-2)


def _pairs_to_states(sp):
    lead = sp.shape[:-3]
    a = sp[..., 0:64, 0:64]
    b = sp[..., 64:128, 64:128]
    return jnp.stack([a, b], axis=-3).reshape(lead + (4, 64, 64))


def _conv_halos(d_grp):
    n_tiles = d_grp.shape[0] // TILE
    tiles = d_grp.reshape(n_tiles, TILE, D_W)
    first_rows = tiles[:, 0, :]
    last_rows = tiles[:, TILE - 1, :]
    zero = jnp.zeros((1, D_W), d_grp.dtype)
    prev = jnp.concatenate([zero, last_rows[:-1]], axis=0)
    nxt = jnp.concatenate([first_rows[1:], zero], axis=0)
    idx = np.arange(n_tiles)
    n_ctx_tiles = N_CTX_TOK // TILE
    per_seq = LAT_LEN // TILE
    lat_pos = (idx - n_ctx_tiles) % per_seq
    seq_start = np.where(idx < n_ctx_tiles, True, lat_pos == 0)
    seq_end = np.where(idx < n_ctx_tiles, True, lat_pos == per_seq - 1)
    prev = jnp.where(jnp.asarray(seq_start)[:, None], 0.0, prev)
    nxt = jnp.where(jnp.asarray(seq_end)[:, None], 0.0, nxt)
    return prev[:, None, :], nxt[:, None, :]


def kernel(x_prompt, x_sample, state_rwkv, cache_mla_ckv, cache_mla_krope, cache_gqa_k, cache_gqa_v, c, c_ctx, norm1_g, norm2_g, w_mod, b_mod, w_in, w_out, rw_w0, rw_wu, rw_a0, rw_au, rw_gu, rw_kk, rw_ka, rw_rk, rw_gn, mla_qa_g, mla_wuq, mla_kva_g, mla_wukv, mla_qn_g, mla_kn_g, gqa_qn_g, gqa_kn_g, conv_w, conv_b, peer_wq, peer_k1, peer_k2, peer_u, peer_v):
    depth = w_in.shape[0]
    params = dict(norm1_g=norm1_g, norm2_g=norm2_g, w_in=w_in, w_out=w_out, rw_w0=rw_w0, rw_wu=rw_wu,
                  rw_a0=rw_a0, rw_au=rw_au, rw_gu=rw_gu, rw_kk=rw_kk, rw_ka=rw_ka, rw_rk=rw_rk, rw_gn=rw_gn,
                  mla_qa_g=mla_qa_g, mla_wuq=mla_wuq, mla_kva_g=mla_kva_g, mla_wukv=mla_wukv,
                  mla_qn_g=mla_qn_g, mla_kn_g=mla_kn_g, gqa_qn_g=gqa_qn_g, gqa_kn_g=gqa_kn_g,
                  conv_w=conv_w, conv_b=conv_b, peer_wq=peer_wq, peer_k1=peer_k1, peer_k2=peer_k2,
                  peer_u=peer_u, peer_v=peer_v)
    lw_all = _layer_weights(params)
    tabs = _rope_tables()

    cvecs = jnp.concatenate([c_ctx[None, :], c, jnp.zeros((8 - 1 - N_LAT_SEQ, D_MODEL), F32)], axis=0)
    mods = _modulation(cvecs, w_mod, b_mod).reshape(depth, 8, 1, 6 * D_MODEL)

    x = jnp.concatenate([x_prompt.reshape(N_CTX_TOK, D_MODEL), x_sample.reshape(N_LAT_TOK, D_MODEL)], axis=0)

    s0_lat = _states_to_pairs(state_rwkv)
    s0_all = jnp.concatenate([jnp.zeros((N_CTX_SEQ,) + s0_lat.shape[1:], F32), s0_lat], axis=0)

    krope_placed = jnp.pad(cache_mla_krope, ((0, 0), (0, 0), (0, 0), (64, 32)))
    kctx_m, vctx_m = _mla_ctx_kv(cache_mla_ckv, krope_placed, lw_all["wk"], lw_all["wv"], lw_all["kn_g2"])
    past = cache_gqa_k.shape[2]
    kctx_g = cache_gqa_k.reshape(N_LAT_SEQ, depth, past, 128)
    vctx_g = jnp.swapaxes(cache_gqa_v.reshape(N_LAT_SEQ, depth, past, 128), -1, -2)

    ident = lambda p: p
    st_a, st_ckv, st_kr, st_k, st_v = [], [], [], [], []
    for l in range(depth):
        lw = {name: w[l] for name, w in lw_all.items() if name not in STACKED_WEIGHTS}
        mod = mods[l]
        a_grp, b_grp, c_grp, d_grp = _in_proj(x, mod, lw["norm1_g"], lw_all["w_in"], l)

        yf, yb, s_fin = _rwkv_scan(a_grp, s0_all[:, l], lw["w0"], lw["wu"], lw["a0"], lw["au"], lw["kkp"], lw["ka"])

        qm, km, vm, ckvn, qg, kg, vg, kgn = _attn_prep(b_grp, c_grp, tabs, lw)
        ob_ctx = _attention(qm, km, km, vm, lambda p: 2 * p, lambda p: 2 * p + 1, lambda p: 2 * p,
                            lambda p: 2 * p + 1, ident, N_CTX_SEQ, CTX_LEN, 0)
        ob_lat = _attention(qm, km, km, vm, lambda p: 2 * p, lambda p: 2 * p + 1, lambda p: 2 * p,
                            lambda p: 2 * p + 1, ident, N_LAT_SEQ, LAT_LEN, N_CTX_TOK,
                            ctx=(kctx_m[l], kctx_m[l], vctx_m[l]))
        oc_ctx = _attention(qg, kg, kg, vg, ident, lambda p: p + 2, ident, ident, ident,
                            N_CTX_SEQ, CTX_LEN, 0)
        oc_lat = _attention(qg, kg, kg, vg, ident, lambda p: p + 2, ident, ident, ident,
                            N_LAT_SEQ, LAT_LEN, N_CTX_TOK,
                            ctx=(kctx_g[:, l], kctx_g[:, l], vctx_g[:, l]))

        d_prev, d_next = _conv_halos(d_grp)
        x1, h2, q = _out_proj(x, mod, a_grp, yf, yb, ob_ctx, ob_lat, oc_ctx, oc_lat, d_grp, d_prev, d_next, lw,
                              lw_all["w_out"], lw_all["wq"], l)

        thr, s2, e1, e2 = _peer_topk(q, lw["k1"], lw["k2"])
        x = _peer_dense(h2, lw_all["u"], lw_all["vt"], l, thr, s2, e1, e2, x1, mod)

        st_a.append(_pairs_to_states(s_fin[:N_CTX_SEQ]))
        st_ckv.append(ckvn[:N_CTX_TOK].reshape(N_CTX_SEQ, CTX_LEN, 128))
        st_kr.append(b_grp[:N_CTX_TOK, 448:480].reshape(N_CTX_SEQ, CTX_LEN, 32))
        st_k.append(kgn[:N_CTX_TOK].reshape(N_CTX_SEQ, CTX_LEN, 2, HEAD_DIM))
        st_v.append(c_grp[:N_CTX_TOK, 384:512].reshape(N_CTX_SEQ, CTX_LEN, 2, HEAD_DIM))

    y_prompt = x[:N_CTX_TOK].reshape(N_CTX_SEQ, CTX_LEN, D_MODEL)
    y_sample = x[N_CTX_TOK:].reshape(N_LAT_SEQ, LAT_LEN, D_MODEL)
    return (y_prompt, y_sample, jnp.stack(st_a, axis=1), jnp.stack(st_ckv, axis=1), jnp.stack(st_kr, axis=1),
            jnp.stack(st_k, axis=1), jnp.stack(st_v, axis=1))
```

```python
import functools

import numpy as np
import jax
import jax.numpy as jnp
from jax import lax
from jax.experimental import pallas as pl
from jax.experimental.pallas import tpu as pltpu

F32 = jnp.float32
BF16 = jnp.bfloat16

D_MODEL = 1024
N_CTX_SEQ = 16
CTX_LEN = 256
N_LAT_SEQ = 2
LAT_LEN = 4096
GRID_W = 64
ROPE_THETA = 10000.0
EPS = 1e-6
GROUP_W = 256
HEAD_DIM = 64
RW_DECAY_SCALE = 0.6065306597
MLA_QK_DIM = 96
MLA_Q_RANK = 192
PEER_HEADS = 8
PEER_KEYS = 128
PEER_TOPK = 16
N_EXPERTS = PEER_KEYS * PEER_KEYS

N_CTX_TOK = N_CTX_SEQ * CTX_LEN
N_LAT_TOK = N_LAT_SEQ * LAT_LEN
N_TOK = N_CTX_TOK + N_LAT_TOK
COND_GROUP = 4096

A_W, B_W, C_W, D_W = 1152, 640, 512, 768
IN_PAD = A_W + B_W + C_W + D_W

SCAN_CHUNK = 128
TILE = 256
IN_TILE = 512
PEER_TN = 512
PEER_TE = 2048
TOPK_TN = 128
ATT_TQ = 512
ATT_CHUNK = 512

VMEM_LIMIT = 56 * 1024 * 1024


def _cparams(sem):
    return pltpu.CompilerParams(dimension_semantics=sem, vmem_limit_bytes=VMEM_LIMIT)


def _dot(a, b, precision=None):
    return jnp.dot(a, b, preferred_element_type=F32, precision=precision)


def _dot_nt(a, b, precision=None):
    return lax.dot_general(a, b, (((1,), (1,)), ((), ())), preferred_element_type=F32,
                           precision=precision)


def _dot_tn(a, b, precision=None):
    return lax.dot_general(a, b, (((0,), (0,)), ((), ())), preferred_element_type=F32,
                           precision=precision)


def _rms_rows(x, g):
    return x * lax.rsqrt(jnp.mean(x * x, axis=-1, keepdims=True) + EPS) * g


def _group_sums(x, group_mat):
    hi = x.astype(BF16)
    lo = (x - hi.astype(F32)).astype(BF16)
    gm = group_mat.astype(BF16)
    return _dot(hi, gm) + _dot(lo, gm)


def _group_matrix(n, group, value):
    r = lax.broadcasted_iota(jnp.int32, (n, n), 0) // group
    c = lax.broadcasted_iota(jnp.int32, (n, n), 1) // group
    return jnp.where(r == c, value, 0.0).astype(F32)


def _mod_body(c_ref, w_ref, b_ref, o_ref):
    c = c_ref[...]
    s = c * jax.nn.sigmoid(c)
    o_ref[...] = _dot(s.astype(BF16), w_ref[...].astype(BF16)) + b_ref[...]


def _modulation(cvecs, w_mod, b_mod):
    depth = w_mod.shape[0]
    tn = 1536
    return pl.pallas_call(
        _mod_body,
        grid=(depth, 6 * D_MODEL // tn),
        in_specs=[pl.BlockSpec((8, D_MODEL), lambda l, j: (0, 0)),
                  pl.BlockSpec((None, D_MODEL, tn), lambda l, j: (l, 0, j)),
                  pl.BlockSpec((None, 1, tn), lambda l, j: (l, 0, j))],
        out_specs=pl.BlockSpec((None, 8, tn), lambda l, j: (l, 0, j)),
        out_shape=jax.ShapeDtypeStruct((depth, 8, 6 * D_MODEL), F32),
        compiler_params=_cparams(("arbitrary", "arbitrary")),
        name="adaln_mod",
    )(cvecs, w_mod, b_mod.reshape(depth, 1, 6 * D_MODEL))


def _in_body(x_ref, mod_ref, g_ref, w_ref, a_ref, b_ref, c_ref, d_ref):
    mod = mod_ref[...]
    shift = mod[:, 0:D_MODEL]
    scale = mod[:, D_MODEL:2 * D_MODEL]
    h = _rms_rows(x_ref[...], g_ref[...]) * (1.0 + scale) + shift
    y = _dot(h.astype(BF16), w_ref[...])
    a_ref[...] = y[:, 0:A_W]
    b_ref[...] = y[:, A_W:A_W + B_W]
    c_ref[...] = y[:, A_W + B_W:A_W + B_W + C_W]
    d_ref[...] = y[:, A_W + B_W + C_W:IN_PAD]


def _in_proj(x, mod, norm_g, w_pad_all, layer):
    n = x.shape[0]
    tpg = COND_GROUP // IN_TILE
    row = lambda i: (i, 0)
    return pl.pallas_call(
        _in_body,
        grid=(n // IN_TILE,),
        in_specs=[pl.BlockSpec((IN_TILE, D_MODEL), row),
                  pl.BlockSpec((None, 1, 6 * D_MODEL), lambda i: (i // tpg, 0, 0)),
                  pl.BlockSpec((1, D_MODEL), lambda i: (0, 0)),
                  pl.BlockSpec((None, D_MODEL, IN_PAD), lambda i: (layer, 0, 0))],
        out_specs=[pl.BlockSpec((IN_TILE, A_W), row), pl.BlockSpec((IN_TILE, B_W), row),
                   pl.BlockSpec((IN_TILE, C_W), row), pl.BlockSpec((IN_TILE, D_W), row)],
        out_shape=[jax.ShapeDtypeStruct((n, A_W), F32), jax.ShapeDtypeStruct((n, B_W), F32),
                   jax.ShapeDtypeStruct((n, C_W), F32), jax.ShapeDtypeStruct((n, D_W), F32)],
        compiler_params=_cparams(("arbitrary",)),
        name="in_proj",
    )(x, mod, norm_g, w_pad_all)


def _chunk_cumsum(x, reverse):
    c = x.shape[0]
    rows = lax.broadcasted_iota(jnp.int32, x.shape, 0)
    sh = 1
    while sh < c:
        if reverse:
            x = x + jnp.where(rows < c - sh, pltpu.roll(x, c - sh, 0), 0.0)
        else:
            x = x + jnp.where(rows >= sh, pltpu.roll(x, sh, 0), 0.0)
        sh *= 2
    return x


def _scan_body(fb_ref, bb_ref, first_ref, last_ref, seq_ref,
               af_ref, ab_ref, s0_ref, w0_ref, wu_ref, a0_ref, au_ref, kkp_ref, ka_ref,
               yf_ref, yb_ref, sfin_ref, s_scr):
    step = pl.program_id(0)

    @pl.when(first_ref[step] == 1)
    def _():
        s_scr[...] = s0_ref[...]

    c = SCAN_CHUNK
    row = lax.broadcasted_iota(jnp.int32, (c, c), 0)
    col = lax.broadcasted_iota(jnp.int32, (c, c), 1)
    lane_c = lax.broadcasted_iota(jnp.int32, (c, 128), 1) < HEAD_DIM
    lane_2c = lax.broadcasted_iota(jnp.int32, (2 * c, 128), 1) < HEAD_DIM
    same_head = _group_matrix(128, HEAD_DIM, 1.0)
    kkp = kkp_ref[...]
    ka = ka_ref[...]

    def per_head(lo, hi):
        return jnp.where(lane_c, lo, hi)

    groups = []
    for d in range(2):
        x_ref = af_ref if d == 0 else ab_ref
        if d == 0:
            strict, incl = col < row, col <= row
        else:
            strict, incl = col > row, col >= row
        r = x_ref[:, 0:256]
        k = x_ref[:, 256:512]
        v = x_ref[:, 512:768]
        wd = x_ref[:, 768:896]
        ad = x_ref[:, 896:1024]
        wlog = -RW_DECAY_SCALE * jax.nn.sigmoid(
            w0_ref[d:d + 1, :] + _dot(jnp.tanh(wd).astype(BF16), wu_ref[d]))
        a = jax.nn.sigmoid(a0_ref[d:d + 1, :] + _dot(ad.astype(BF16), au_ref[d]))
        kd = k * (1.0 + (a - 1.0) * ka)
        kkf = k * kkp
        cum = _chunk_cumsum(wlog, reverse=(d == 1))
        tot = cum[c - 1:c, :] if d == 0 else cum[0:1, :]
        p_inc = jnp.exp(cum)
        p_exc = jnp.exp(cum - wlog)
        p_inv = jnp.exp(-cum)
        p_tot = jnp.exp(tot)
        for p in range(2):
            sl = slice(128 * p, 128 * p + 128)
            groups.append(dict(d=d, p=p, sl=sl, strict=strict, incl=incl, kk_raw=kkf[:, sl],
                               a=a[:, sl], kd=kd[:, sl], r=r[:, sl], vb=v[:, sl].astype(BF16),
                               p_inc=p_inc[:, sl], p_exc=p_exc[:, sl], p_inv=p_inv[:, sl], pt=p_tot[:, sl]))

    for g in groups:
        g["ss"] = _group_sums(g["kk_raw"] * g["kk_raw"], same_head)
    for g in groups:
        kk = g["kk_raw"] * lax.rsqrt(g["ss"] + EPS)
        g["a_p"] = (-kk * g["p_exc"]).astype(BF16)
        g["r_p"] = (g["r"] * g["p_inc"]).astype(BF16)
        g["b_i"] = kk * g["a"] * g["p_inv"]
        g["k_i"] = g["kd"] * g["p_inv"]
        g["lhs"] = jnp.concatenate([g["a_p"], g["r_p"]], axis=0)
        g["rhs"] = jnp.concatenate([g["b_i"], g["k_i"]], axis=0).astype(BF16)
        g["sb"] = s_scr[g["d"], g["p"]].astype(BF16)
    zero_b = jnp.zeros((), BF16)
    for g in groups:
        g["gram"] = (_dot_nt(jnp.where(lane_2c, g["lhs"], zero_b), g["rhs"]),
                     _dot_nt(jnp.where(lane_2c, zero_b, g["lhs"]), g["rhs"]))
        g["a_s"] = _dot_nt(g["a_p"], g["sb"])
        g["r_s"] = _dot_nt(g["r_p"], g["sb"])
    for g in groups:
        st, inc = g["strict"], g["incl"]
        g["pow"] = [jnp.where(st, gm[0:c, 0:c], 0.0).astype(BF16) for gm in g["gram"]]
        g["dm"] = [jnp.where(st, gm[0:c, c:2 * c], 0.0).astype(BF16) for gm in g["gram"]]
        g["et"] = [jnp.where(inc, gm[c:2 * c, 0:c], 0.0).astype(BF16) for gm in g["gram"]]
        g["ft"] = [jnp.where(inc, gm[c:2 * c, c:2 * c], 0.0).astype(BF16) for gm in g["gram"]]
    for g in groups:
        g["z"] = g["a_s"] + per_head(_dot(g["dm"][0], g["vb"]), _dot(g["dm"][1], g["vb"]))
    span = 1
    while span < c:
        for g in groups:
            zb = g["z"].astype(BF16)
            g["z"] = g["z"] + per_head(_dot(g["pow"][0], zb), _dot(g["pow"][1], zb))
        span *= 2
        if span < c:
            for g in groups:
                g["pow"] = [_dot(m, m).astype(BF16) for m in g["pow"]]
    for g in groups:
        zb = g["z"].astype(BF16)
        g["zb"] = zb
        y = g["r_s"] + per_head(_dot(g["et"][0], zb) + _dot(g["ft"][0], g["vb"]),
                                _dot(g["et"][1], zb) + _dot(g["ft"][1], g["vb"]))
        y_ref = yf_ref if g["d"] == 0 else yb_ref
        y_ref[:, g["sl"]] = y
    for g in groups:
        pt = g["pt"]
        upd = (_dot_tn(g["zb"], (g["b_i"] * pt).astype(BF16))
               + _dot_tn(g["vb"], (g["k_i"] * pt).astype(BF16)))
        s_scr[g["d"], g["p"]] = s_scr[g["d"], g["p"]] * pt + same_head * upd

    @pl.when(last_ref[step] == 1)
    def _():
        sfin_ref[...] = s_scr[...]


def _scan_tables():
    c = SCAN_CHUNK
    fb, bb, first, last, seq = [], [], [], [], []
    base = 0
    sid = 0
    for nseq, length in ((N_CTX_SEQ, CTX_LEN), (N_LAT_SEQ, LAT_LEN)):
        nc = length // c
        for _ in range(nseq):
            for j in range(nc):
                fb.append(base + j)
                bb.append(base + nc - 1 - j)
                first.append(int(j == 0))
                last.append(int(j == nc - 1))
                seq.append(sid)
            base += nc
            sid += 1
    return tuple(np.asarray(t, np.int32) for t in (fb, bb, first, last, seq))


def _rwkv_scan(a_grp, s0_pairs, w0, wu_pad, a0, au_pad, kkp, ka):
    tables = _scan_tables()
    n_steps = tables[0].shape[0]
    n_seq = s0_pairs.shape[0]
    n = a_grp.shape[0]
    c = SCAN_CHUNK
    const2 = lambda s, fb, bb, fi, la, sq: (0, 0)
    const3 = lambda s, fb, bb, fi, la, sq: (0, 0, 0)
    grid_spec = pltpu.PrefetchScalarGridSpec(
        num_scalar_prefetch=5,
        grid=(n_steps,),
        in_specs=[pl.BlockSpec((c, A_W), lambda s, fb, bb, fi, la, sq: (fb[s], 0)),
                  pl.BlockSpec((c, A_W), lambda s, fb, bb, fi, la, sq: (bb[s], 0)),
                  pl.BlockSpec((None, 2, 2, 128, 128), lambda s, fb, bb, fi, la, sq: (sq[s], 0, 0, 0, 0)),
                  pl.BlockSpec((2, GROUP_W), const2),
                  pl.BlockSpec((2, 128, GROUP_W), const3),
                  pl.BlockSpec((2, GROUP_W), const2),
                  pl.BlockSpec((2, 128, GROUP_W), const3),
                  pl.BlockSpec((1, GROUP_W), const2),
                  pl.BlockSpec((1, GROUP_W), const2)],
        out_specs=[pl.BlockSpec((c, GROUP_W), lambda s, fb, bb, fi, la, sq: (fb[s], 0)),
                   pl.BlockSpec((c, GROUP_W), lambda s, fb, bb, fi, la, sq: (bb[s], 0)),
                   pl.BlockSpec((None, 2, 2, 128, 128), lambda s, fb, bb, fi, la, sq: (sq[s], 0, 0, 0, 0))],
        scratch_shapes=[pltpu.VMEM((2, 2, 128, 128), F32)],
    )
    return pl.pallas_call(
        _scan_body,
        grid_spec=grid_spec,
        out_shape=[jax.ShapeDtypeStruct((n, GROUP_W), F32), jax.ShapeDtypeStruct((n, GROUP_W), F32),
                   jax.ShapeDtypeStruct((n_seq, 2, 2, 128, 128), F32)],
        compiler_params=_cparams(("arbitrary",)),
        name="rwkv_scan",
    )(*[jnp.asarray(t) for t in tables], a_grp, a_grp, s0_pairs, w0, wu_pad, a0, au_pad, kkp, ka)


LOG2_E = 1.4426950408889634
MLA_SCALE = MLA_QK_DIM ** -0.5 * LOG2_E
GQA_SCALE = HEAD_DIM ** -0.5 * LOG2_E


def _mla_keys(ckv_b, rope_slot, rope_slot_sw, wk_ref, kng_ref, cosm, sinm, k_out_ref):
    g = kng_ref[0:1, :]
    g_sw = kng_ref[1:2, :]
    for h in range(4):
        nope = _dot(ckv_b, wk_ref[h])
        kr = nope + rope_slot
        rs = lax.rsqrt(jnp.sum(kr * kr, axis=-1, keepdims=True) * (1.0 / MLA_QK_DIM) + EPS)
        if cosm is None:
            k_out_ref[h] = (kr * rs * g).astype(BF16)
        else:
            ks = nope + rope_slot_sw
            k_out_ref[h] = ((kr * rs * g) * cosm + (ks * rs * g_sw) * sinm).astype(BF16)


def _prep_body(b_ref, c_ref, cosm_ref, sinm_ref, cosg_ref, sing_ref,
               qag_ref, wqq_ref, qng_ref, kvag_ref, wk_ref, wv_ref, kng_ref, gqg_ref, gkg_ref,
               qm_ref, km_ref, vm_ref, ckvn_ref, qg_ref, kg_ref, vg_ref, kgn_ref):
    cosm = cosm_ref[...]
    sinm = sinm_ref[...]
    qc = b_ref[:, 0:256]
    qn = qc * lax.rsqrt(jnp.sum(qc * qc, axis=-1, keepdims=True) * (1.0 / MLA_Q_RANK) + EPS) * qag_ref[...]
    qnb = qn.astype(BF16)
    g = qng_ref[0:1, :]
    g_sw = qng_ref[1:2, :]
    for h in range(4):
        qr = _dot(qnb, wqq_ref[h])
        qs = _dot(qnb, wqq_ref[4 + h])
        rs = lax.rsqrt(jnp.sum(qr * qr, axis=-1, keepdims=True) * (1.0 / MLA_QK_DIM) + EPS)
        qm_ref[h] = (((qr * rs * g) * cosm + (qs * rs * g_sw) * sinm) * MLA_SCALE).astype(BF16)
    ckv = _rms_rows(b_ref[:, 256:384], kvag_ref[...])
    ckvn_ref[...] = ckv
    cb = ckv.astype(BF16)
    _mla_keys(cb, b_ref[:, 384:512], b_ref[:, 512:640], wk_ref, kng_ref, cosm, sinm, km_ref)
    for p in range(2):
        vm_ref[p] = _dot_nt(wv_ref[p], cb).astype(BF16)

    tm = c_ref.shape[0]
    avg = _group_matrix(128, HEAD_DIM, 1.0 / HEAD_DIM)
    lane = lax.broadcasted_iota(jnp.int32, (tm, 128), 1)
    first_half = (lane % HEAD_DIM) < (HEAD_DIM // 2)
    low = lane < HEAD_DIM
    cosg = cosg_ref[...]
    sing = sing_ref[...]

    def rotate(xn):
        swapped = jnp.where(first_half, pltpu.roll(xn, 128 - HEAD_DIM // 2, 1), pltpu.roll(xn, HEAD_DIM // 2, 1))
        return xn * cosg + swapped * sing

    for blk in range(2):
        x = c_ref[:, 128 * blk:128 * blk + 128]
        xn = x * lax.rsqrt(_group_sums(x * x, avg) + EPS) * gqg_ref[...]
        xr = rotate(xn) * GQA_SCALE
        qg_ref[blk] = jnp.where(low, xr, 0.0).astype(BF16)
        qg_ref[blk + 2] = jnp.where(low, 0.0, xr).astype(BF16)
    xk = c_ref[:, 256:384]
    kn = xk * lax.rsqrt(_group_sums(xk * xk, avg) + EPS) * gkg_ref[...]
    kgn_ref[...] = kn
    kg_ref[...] = rotate(kn).astype(BF16)
    vg_ref[...] = c_ref[:, 384:512].T.astype(BF16)


def _attn_prep(b_grp, c_grp, tabs, lw):
    n = b_grp.shape[0]
    n_ctx_tiles = N_CTX_TOK // TILE
    lat_tiles = LAT_LEN // TILE
    row = lambda i: (i, 0)
    hrow = lambda i: (0, i, 0)
    tab = lambda i: (jnp.where(i < n_ctx_tiles, 0, 1 + (i - n_ctx_tiles) % lat_tiles), 0)
    c2 = lambda i: (0, 0)
    c3 = lambda i: (0, 0, 0)
    return pl.pallas_call(
        _prep_body,
        grid=(n // TILE,),
        in_specs=[pl.BlockSpec((TILE, B_W), row), pl.BlockSpec((TILE, C_W), row),
                  pl.BlockSpec((TILE, 128), tab), pl.BlockSpec((TILE, 128), tab),
                  pl.BlockSpec((TILE, 128), tab), pl.BlockSpec((TILE, 128), tab),
                  pl.BlockSpec((1, 256), c2), pl.BlockSpec((8, 256, 128), c3), pl.BlockSpec((2, 128), c2),
                  pl.BlockSpec((1, 128), c2), pl.BlockSpec((4, 128, 128), c3), pl.BlockSpec((2, 128, 128), c3),
                  pl.BlockSpec((2, 128), c2), pl.BlockSpec((1, 128), c2), pl.BlockSpec((1, 128), c2)],
        out_specs=[pl.BlockSpec((4, TILE, 128), hrow), pl.BlockSpec((4, TILE, 128), hrow),
                   pl.BlockSpec((2, 128, TILE), lambda i: (0, 0, i)), pl.BlockSpec((TILE, 128), row),
                   pl.BlockSpec((4, TILE, 128), hrow), pl.BlockSpec((TILE, 128), row),
                   pl.BlockSpec((128, TILE), lambda i: (0, i)), pl.BlockSpec((TILE, 128), row)],
        out_shape=[jax.ShapeDtypeStruct((4, n, 128), BF16), jax.ShapeDtypeStruct((4, n, 128), BF16),
                   jax.ShapeDtypeStruct((2, 128, n), BF16), jax.ShapeDtypeStruct((n, 128), F32),
                   jax.ShapeDtypeStruct((4, n, 128), BF16), jax.ShapeDtypeStruct((n, 128), BF16),
                   jax.ShapeDtypeStruct((128, n), BF16), jax.ShapeDtypeStruct((n, 128), F32)],
        compiler_params=_cparams(("arbitrary",)),
        name="attn_prep",
    )(b_grp, c_grp, tabs["cosm"], tabs["sinm"], tabs["cosg"], tabs["sing"],
      lw["qa_g"], lw["wqq"], lw["qn_g2"], lw["kva_g"], lw["wk"], lw["wv"], lw["kn_g2"], lw["gq_g"], lw["gk_g"])


def _ctxkv_body(ckv_ref, krp_ref, wk_ref, wv_ref, kng_ref, k_ref, v_ref):
    cb = ckv_ref[...].astype(BF16)
    _mla_keys(cb, krp_ref[...], None, wk_ref, kng_ref, None, None, k_ref)
    for p in range(2):
        v_ref[p] = _dot_nt(wv_ref[p], cb).astype(BF16)


def _mla_ctx_kv(cache_ckv, cache_krope_placed, wk, wv, kn_g2):
    nb, depth, plen, _ = cache_ckv.shape
    return pl.pallas_call(
        _ctxkv_body,
        grid=(depth, nb),
        in_specs=[pl.BlockSpec((None, None, plen, 128), lambda l, b: (b, l, 0, 0)),
                  pl.BlockSpec((None, None, plen, 128), lambda l, b: (b, l, 0, 0)),
                  pl.BlockSpec((None, 4, 128, 128), lambda l, b: (l, 0, 0, 0)),
                  pl.BlockSpec((None, 2, 128, 128), lambda l, b: (l, 0, 0, 0)),
                  pl.BlockSpec((None, 2, 128), lambda l, b: (l, 0, 0))],
        out_specs=[pl.BlockSpec((None, 4, None, plen, 128), lambda l, b: (l, 0, b, 0, 0)),
                   pl.BlockSpec((None, 2, None, 128, plen), lambda l, b: (l, 0, b, 0, 0))],
        out_shape=[jax.ShapeDtypeStruct((depth, 4, nb, plen, 128), BF16),
                   jax.ShapeDtypeStruct((depth, 2, nb, 128, plen), BF16)],
        compiler_params=_cparams(("arbitrary", "arbitrary")),
        name="mla_ctx_kv",
    )(cache_ckv, cache_krope_placed, wk, wv, kn_g2)


def _attn_body(*refs, has_ctx):
    if has_ctx:
        qa_ref, qb_ref, ka_ref, kb_ref, vt_ref, kca_ref, kcb_ref, vct_ref, o_ref, sa_scr, sb_scr = refs
    else:
        qa_ref, qb_ref, ka_ref, kb_ref, vt_ref, o_ref, sa_scr, sb_scr = refs
        kca_ref = kcb_ref = vct_ref = None
    tk = ka_ref.shape[0]
    chunk = min(ATT_CHUNK, tk)
    q = (qa_ref[...], qb_ref[...])
    k_refs = (ka_ref, kb_ref)
    kc_refs = (kca_ref, kcb_ref)
    s_scr = (sa_scr, sb_scr)
    vrows = (slice(0, HEAD_DIM), slice(HEAD_DIM, 2 * HEAD_DIM))
    spans = [("new", c * chunk, chunk) for c in range(tk // chunk)]
    if has_ctx:
        spans = [("ctx", 0, kca_ref.shape[0])] + spans

    def scratch_rows(kind, start, size):
        base = tk if kind == "ctx" else 0
        return slice(base + start, base + start + size)

    m = [None, None]
    den = [None, None]
    acc = [None, None]

    def score_pass(hd, span):
        kind, start, size = span
        if kind == "ctx":
            keys = kc_refs[hd][...].astype(BF16)
        else:
            keys = k_refs[hd][start:start + size, :]
        s = _dot_nt(keys, q[hd])
        s_scr[hd][scratch_rows(kind, start, size), :] = s
        cm = jnp.max(s, axis=0, keepdims=True)
        m[hd] = cm if m[hd] is None else jnp.maximum(m[hd], cm)

    def value_pass(hd, span):
        kind, start, size = span
        e = jnp.exp2(s_scr[hd][scratch_rows(kind, start, size), :] - m[hd])
        if kind == "ctx":
            vals = vct_ref[vrows[hd], :].astype(BF16)
        else:
            vals = vt_ref[vrows[hd], start:start + size]
        part = _dot(vals, e.astype(BF16))
        rs = jnp.sum(e, axis=0, keepdims=True)
        den[hd] = rs if den[hd] is None else den[hd] + rs
        acc[hd] = part if acc[hd] is None else acc[hd] + part

    for span in spans:
        for hd in range(2):
            score_pass(hd, span)
    for span in spans:
        for hd in range(2):
            value_pass(hd, span)
    out_t = jnp.concatenate([acc[0] / den[0], acc[1] / den[1]], axis=0)
    o_ref[...] = out_t.T.astype(o_ref.dtype)


def _attention(q, ka, kb, v, head_a, head_b, k_head_a, k_head_b, v_idx, n_seq, seq_len, tok0, ctx=None):
    tq = min(ATT_TQ, seq_len)
    nq = seq_len // tq
    seq0 = tok0 // seq_len
    q0 = tok0 // tq

    def qmap(hsel):
        return lambda b, p, i: (hsel(p), q0 + b * nq + i, 0)

    def kmap(hsel, arr):
        if arr.ndim == 3:
            return lambda b, p, i: (hsel(p), seq0 + b, 0)
        return lambda b, p, i: (seq0 + b, 0)

    def kspec(arr, hsel):
        if arr.ndim == 3:
            return pl.BlockSpec((None, seq_len, 128), kmap(hsel, arr))
        return pl.BlockSpec((seq_len, 128), kmap(hsel, arr))

    if v.ndim == 3:
        vspec = pl.BlockSpec((None, 128, seq_len), lambda b, p, i: (v_idx(p), 0, seq0 + b))
    else:
        vspec = pl.BlockSpec((128, seq_len), lambda b, p, i: (0, seq0 + b))
    in_specs = [pl.BlockSpec((None, tq, 128), qmap(head_a)), pl.BlockSpec((None, tq, 128), qmap(head_b)),
                kspec(ka, k_head_a), kspec(kb, k_head_b), vspec]
    args = [q, q, ka, kb, v]
    if ctx is not None:
        kca, kcb, vc = ctx
        past = vc.shape[-1]

        def cspec(arr, hsel, shape):
            if arr.ndim == 4:
                return pl.BlockSpec((None, None) + shape, lambda b, p, i: (hsel(p), b, 0, 0))
            return pl.BlockSpec((None,) + shape, lambda b, p, i: (b, 0, 0))

        in_specs += [cspec(kca, k_head_a, (past, 128)), cspec(kcb, k_head_b, (past, 128)),
                     cspec(vc, v_idx, (128, past))]
        args += [kca, kcb, vc]
    return pl.pallas_call(
        functools.partial(_attn_body, has_ctx=ctx is not None),
        grid=(n_seq, 2, nq),
        in_specs=in_specs,
        out_specs=pl.BlockSpec((tq, 128), lambda b, p, i: (b * nq + i, p)),
        out_shape=jax.ShapeDtypeStruct((n_seq * seq_len, 256), BF16),
        scratch_shapes=[pltpu.VMEM((seq_len + (0 if ctx is None else ctx[2].shape[-1]), tq), F32)] * 2,
        compiler_params=_cparams(("arbitrary", "arbitrary", "arbitrary")),
        name="attention_ctx" if ctx is None else "attention_lat",
    )(*args)


def _out_body(x_ref, mod_ref, a_ref, yf_ref, yb_ref, obc_ref, obl_ref, occ_ref, ocl_ref,
              d_ref, dprev_ref, dnext_ref, wo_ref, gu_ref, rk_ref, gn_ref, cw_ref, cb_ref, n2_ref, wq_ref,
              x1_ref, h2_ref, q_ref, *, n_ctx_tiles):
    i = pl.program_id(0)
    is_ctx = i < n_ctx_tiles
    tm = x_ref.shape[0]
    mod = mod_ref[...]
    gate1 = mod[:, 2 * D_MODEL:3 * D_MODEL]
    shift2 = mod[:, 3 * D_MODEL:4 * D_MODEL]
    scale2 = mod[:, 4 * D_MODEL:5 * D_MODEL]

    r = a_ref[:, 0:256]
    k = a_ref[:, 256:512]
    v = a_ref[:, 512:768]
    gd = a_ref[:, 1024:1152]
    y = yf_ref[...] + yb_ref[...]
    avg = _group_matrix(GROUP_W, HEAD_DIM, 1.0 / HEAD_DIM)
    ones = _group_matrix(GROUP_W, HEAD_DIM, 1.0)
    yn = y * lax.rsqrt(_group_sums(y * y, avg) + EPS) * gn_ref[...]
    bonus = _group_sums(r * k * rk_ref[...], ones) * v
    gate = _dot(jax.nn.sigmoid(gd).astype(BF16), gu_ref[...])
    o_a = (yn + bonus) * gate

    o_b = jnp.where(is_ctx, obc_ref[...], obl_ref[...])
    o_c = jnp.where(is_ctx, occ_ref[...], ocl_ref[...])

    u = d_ref[:, 512:768] * d_ref[:, 0:256]
    u_prev = dprev_ref[:, 512:768] * dprev_ref[:, 0:256]
    u_next = dnext_ref[:, 512:768] * dnext_ref[:, 0:256]
    rows = lax.broadcasted_iota(jnp.int32, (tm, GROUP_W), 0)
    up = jnp.where(rows == 0, u_prev, pltpu.roll(u, 1, 0))
    un = jnp.where(rows == tm - 1, u_next, pltpu.roll(u, tm - 1, 0))
    conv = up * cw_ref[0:1, :] + u * cw_ref[1:2, :] + un * cw_ref[2:3, :] + cb_ref[...]
    o_d = d_ref[:, 256:512] * conv

    mix_in = jnp.concatenate([o_a.astype(BF16), o_b, o_c, o_d.astype(BF16)], axis=1)
    x1 = x_ref[...] + gate1 * _dot(mix_in, wo_ref[...])
    x1_ref[...] = x1
    h2 = (_rms_rows(x1, n2_ref[...]) * (1.0 + scale2) + shift2).astype(BF16)
    h2_ref[...] = h2
    q_ref[...] = _dot(h2, wq_ref[...]).astype(q_ref.dtype)


def _out_proj(x, mod, a_grp, yf, yb, ob_ctx, ob_lat, oc_ctx, oc_lat, d_grp, d_prev, d_next, lw, w_out_all, wq_all,
              layer):
    n = x.shape[0]
    n_tiles = n // TILE
    n_ctx_tiles = N_CTX_TOK // TILE
    tpg = COND_GROUP // TILE
    row = lambda i: (i, 0)
    ctx_row = lambda i: (jnp.minimum(i, n_ctx_tiles - 1), 0)
    lat_row = lambda i: (jnp.maximum(i - n_ctx_tiles, 0), 0)
    halo = lambda i: (i, 0, 0)
    c2 = lambda i: (0, 0)
    nq = PEER_HEADS * 2 * PEER_KEYS
    return pl.pallas_call(
        functools.partial(_out_body, n_ctx_tiles=n_ctx_tiles),
        grid=(n_tiles,),
        in_specs=[pl.BlockSpec((TILE, D_MODEL), row),
                  pl.BlockSpec((None, 1, 6 * D_MODEL), lambda i: (i // tpg, 0, 0)),
                  pl.BlockSpec((TILE, A_W), row),
                  pl.BlockSpec((TILE, GROUP_W), row), pl.BlockSpec((TILE, GROUP_W), row),
                  pl.BlockSpec((TILE, GROUP_W), ctx_row), pl.BlockSpec((TILE, GROUP_W), lat_row),
                  pl.BlockSpec((TILE, GROUP_W), ctx_row), pl.BlockSpec((TILE, GROUP_W), lat_row),
                  pl.BlockSpec((TILE, D_W), row),
                  pl.BlockSpec((None, 1, D_W), halo), pl.BlockSpec((None, 1, D_W), halo),
                  pl.BlockSpec((None, D_MODEL, D_MODEL), lambda i: (layer, 0, 0)), pl.BlockSpec((128, GROUP_W), c2),
                  pl.BlockSpec((1, GROUP_W), c2), pl.BlockSpec((1, GROUP_W), c2),
                  pl.BlockSpec((3, GROUP_W), c2), pl.BlockSpec((1, GROUP_W), c2),
                  pl.BlockSpec((1, D_MODEL), c2), pl.BlockSpec((None, D_MODEL, nq), lambda i: (layer, 0, 0))],
        out_specs=[pl.BlockSpec((TILE, D_MODEL), row), pl.BlockSpec((TILE, D_MODEL), row),
                   pl.BlockSpec((TILE, nq), row)],
        out_shape=[jax.ShapeDtypeStruct((n, D_MODEL), F32), jax.ShapeDtypeStruct((n, D_MODEL), BF16),
                   jax.ShapeDtypeStruct((n, nq), BF16)],
        compiler_params=_cparams(("arbitrary",)),
        name="out_proj",
    )(x, mod, a_grp, yf, yb, ob_ctx, ob_lat, oc_ctx, oc_lat, d_grp, d_prev, d_next,
      w_out_all, lw["gu"], lw["rk"], lw["gn"], lw["conv_w"], lw["conv_b"], lw["norm2_g"], wq_all)


def _exchange(a, b):
    if a is None:
        return b, None
    if b is None:
        return a, None
    return jnp.maximum(a, b), jnp.minimum(a, b)


def _sort16_desc(xs):
    xs = list(xs)
    k = 2
    while k <= 16:
        j = k // 2
        while j >= 1:
            for i in range(16):
                partner = i ^ j
                if partner > i:
                    hi, lo = _exchange(xs[i], xs[partner])
                    xs[i], xs[partner] = (hi, lo) if (i & k) == 0 else (lo, hi)
            j //= 2
        k *= 2
    return xs


def _bitonic_merge_desc(xs):
    xs = list(xs)
    j = 8
    while j >= 1:
        for i in range(16):
            partner = i ^ j
            if partner > i:
                xs[i], xs[partner] = _exchange(xs[i], xs[partner])
        j //= 2
    return xs


def _top16_of_rows(xs):
    ys = _sort16_desc(xs)
    for shift in (4, 2, 1):
        zs = [None if y is None else pltpu.roll(y, shift, 0) for y in ys]
        ts = [_exchange(ys[i], zs[15 - i])[0] for i in range(16)]
        ys = _bitonic_merge_desc(ts)
    return ys


def _topk_body(q_ref, k1_ref, k2_ref, thr_ref, s2_ref, e1_ref, e2_ref):
    tn = q_ref.shape[0]
    sub = lax.broadcasted_iota(jnp.int32, (8, tn), 0)

    def spread(vals):
        out = vals[7]
        for s in range(6, -1, -1):
            out = jnp.where(sub == s, vals[s], out)
        return out

    for h in range(PEER_HEADS):
        qa = q_ref[:, (2 * h) * PEER_KEYS:(2 * h + 1) * PEER_KEYS].astype(BF16)
        qb = q_ref[:, (2 * h + 1) * PEER_KEYS:(2 * h + 2) * PEER_KEYS].astype(BF16)
        s1 = _dot_nt(k1_ref[h], qa)
        s2 = _dot_nt(k2_ref[h], qb)
        v1 = _top16_of_rows([s1[8 * i:8 * i + 8, :] for i in range(16)])
        v2 = _top16_of_rows([s2[8 * i:8 * i + 8, :] for i in range(16)])
        v2_lo, v2_hi, v1_hi = spread(v2[0:8]), spread(v2[8:16]), spread(v1[8:16])
        cands = ([v1[0] + v2_lo, v1[0] + v2_hi] + [v1[a] + v2_lo for a in range(1, 8)]
                 + [v1_hi + v2[0]] + [None] * 6)
        best = _top16_of_rows(cands)
        tau = best[PEER_TOPK - 1][0:1, :]
        zsum = jnp.exp(best[0] - best[0])
        for kth in range(1, PEER_TOPK):
            zsum = zsum + jnp.exp(best[kth] - best[0])
        thr = jnp.full(s1.shape, jnp.inf, F32)
        for b in range(PEER_TOPK):
            vb = v2[b][0:1, :]
            thr = jnp.where(s1 + vb >= tau, vb, thr)
        thr_ref[h] = thr
        s2_ref[h] = s2
        e1_ref[h] = jnp.exp(s1 - v1[0][0:1, :])
        e2_ref[h] = jnp.exp(s2 - v2[0][0:1, :]) * (0.5 / zsum[0:1, :])


def _rotated_block(i):
    per_tile = PEER_TN // TOPK_TN
    return (i // per_tile) * per_tile + (i + 1) % per_tile


def _peer_topk(q, k1, k2):
    n = q.shape[0]
    tn = TOPK_TN
    big = pl.BlockSpec((PEER_HEADS, PEER_KEYS, tn), lambda i: (0, 0, i))
    big_shape = jax.ShapeDtypeStruct((PEER_HEADS, PEER_KEYS, n), F32)
    c3 = lambda i: (0, 0, 0)
    return pl.pallas_call(
        _topk_body,
        grid=(n // tn,),
        in_specs=[pl.BlockSpec((tn, PEER_HEADS * 2 * PEER_KEYS), lambda i: (i, 0)),
                  pl.BlockSpec((PEER_HEADS, PEER_KEYS, PEER_KEYS), c3),
                  pl.BlockSpec((PEER_HEADS, PEER_KEYS, PEER_KEYS), c3)],
        out_specs=[big, big, big, pl.BlockSpec((PEER_HEADS, PEER_KEYS, tn), lambda i: (0, 0, _rotated_block(i)))],
        out_shape=[big_shape, big_shape, big_shape, big_shape],
        compiler_params=_cparams(("arbitrary",)),
        name="peer_topk",
    )(q, k1, k2)


GELU_C0 = 0.7978845608028654
GELU_C1 = 0.044715


PEER_PIECE = 256
GATE_KEYS = 64


def _dense_body(h2_ref, u_ref, vt_ref, thr_ref, s2_ref, e1_ref, e2_ref, x1_ref, mod_ref,
                o_ref, acc, gs, *hs):
    j = pl.program_id(1)

    @pl.when(j == 0)
    def _():
        acc[...] = jnp.zeros_like(acc)

    n_pieces = len(hs)
    tn = hs[0].shape[1]
    n_col = tn // 128
    h2 = h2_ref[...]

    def pre_activations(p):
        rows = slice(p * PEER_PIECE, (p + 1) * PEER_PIECE)
        hs[p][...] = _dot_nt(u_ref[rows, :], h2)

    pre_activations(0)
    for p in range(n_pieces):
        if p + 1 < n_pieces:
            pre_activations(p + 1)
        halves = PEER_PIECE // PEER_KEYS
        per_col = PEER_KEYS // GATE_KEYS
        for sub in range(per_col * n_col):
            keys = slice((sub % per_col) * GATE_KEYS, (sub % per_col) * GATE_KEYS + GATE_KEYS)
            c = sub // per_col
            cols = slice(c * 128, c * 128 + 128)
            e2_cols = slice(((c + 1) % n_col) * 128, ((c + 1) % n_col) * 128 + 128)
            n_grp = GATE_KEYS // 8
            w = [[None] * n_grp for _ in range(halves)]
            for h in range(PEER_HEADS):
                thr8 = [jnp.broadcast_to(thr_ref[h, p * halves + half:p * halves + half + 1, cols], (8, 128))
                        for half in range(halves)]
                e18 = [jnp.broadcast_to(e1_ref[h, p * halves + half:p * halves + half + 1, cols], (8, 128))
                       for half in range(halves)]
                for g in range(n_grp):
                    grp = slice(keys.start + 8 * g, keys.start + 8 * g + 8)
                    s2g = s2_ref[h, grp, cols]
                    e2g = e2_ref[h, grp, e2_cols]
                    for half in range(halves):
                        term = jnp.where(s2g >= thr8[half], e18[half] * e2g, 0.0)
                        w[half][g] = term if w[half][g] is None else w[half][g] + term
            for half in range(halves):
                rows = slice(half * PEER_KEYS + keys.start, half * PEER_KEYS + keys.stop)
                x = hs[p][rows, cols]
                inner = x * (GELU_C0 + (GELU_C0 * GELU_C1) * (x * x))
                wh = jnp.concatenate(w[half], axis=0)
                grows = slice(p * PEER_PIECE + rows.start, p * PEER_PIECE + rows.stop)
                gs[grows, cols] = (wh * x * (1.0 + jnp.tanh(inner))).astype(BF16)
    acc[...] += _dot(vt_ref[...], gs[...])

    @pl.when(j == pl.num_programs(1) - 1)
    def _():
        gate2 = mod_ref[:, 5 * D_MODEL:6 * D_MODEL]
        o_ref[...] = x1_ref[...] + gate2 * acc[...].T


def _peer_dense(h2, u_all, vt_all, layer, thr, s2, e1, e2, x1, mod):
    n = h2.shape[0]
    tn, te = PEER_TN, PEER_TE
    rows_per_step = te // PEER_KEYS
    tpg = COND_GROUP // tn
    tok = lambda i, j: (i, 0)
    key_rows = pl.BlockSpec((PEER_HEADS, rows_per_step, tn), lambda i, j: (0, j, i))
    key_all = pl.BlockSpec((PEER_HEADS, PEER_KEYS, tn), lambda i, j: (0, 0, i))
    return pl.pallas_call(
        _dense_body,
        grid=(n // tn, N_EXPERTS // te),
        in_specs=[pl.BlockSpec((tn, D_MODEL), tok),
                  pl.BlockSpec((None, te, D_MODEL), lambda i, j: (layer, j, 0)),
                  pl.BlockSpec((None, D_MODEL, te), lambda i, j: (layer, 0, j)),
                  key_rows, key_all, key_rows, key_all,
                  pl.BlockSpec((tn, D_MODEL), tok),
                  pl.BlockSpec((None, 1, 6 * D_MODEL), lambda i, j: (i // tpg, 0, 0))],
        out_specs=pl.BlockSpec((tn, D_MODEL), tok),
        out_shape=jax.ShapeDtypeStruct((n, D_MODEL), F32),
        scratch_shapes=([pltpu.VMEM((D_MODEL, tn), F32), pltpu.VMEM((te, tn), BF16)]
                        + [pltpu.VMEM((PEER_PIECE, tn), F32)] * (te // PEER_PIECE)),
        compiler_params=_cparams(("arbitrary", "arbitrary")),
        name="peer_dense",
    )(h2, u_all, vt_all, thr, s2, e1, e2, x1, mod)


def _in_proj_columns():
    src = np.full((IN_PAD,), -1, np.int64)

    def put(dst, start, width):
        src[dst:dst + width] = np.arange(start, start + width)

    put(0, 0, 768)
    put(768, 768, 128)
    put(896, 896, 64)
    put(1024, 960, 128)
    b0 = A_W
    put(b0, 1088, 192)
    put(b0 + 256, 1280, 128)
    put(b0 + 384 + 64, 1408, 32)
    put(b0 + 512 + 64, 1408 + 16, 16)
    put(b0 + 512 + 80, 1408, 16)
    c0 = A_W + B_W
    for slot, head in enumerate((0, 2, 1, 3)):
        put(c0 + 64 * slot, 1440 + 64 * head, 64)
    put(c0 + 256, 1696, 256)
    put(A_W + B_W + C_W, 1952, 768)
    return src


def _swap_tail(w):
    return jnp.concatenate([w[..., :64], w[..., 80:96], w[..., 64:80]], axis=-1)


def _pad_last(w, width):
    return jnp.pad(w, [(0, 0)] * (w.ndim - 1) + [(0, width - w.shape[-1])])


STACKED_WEIGHTS = ("w_in", "w_out", "wq", "u", "vt")


def _layer_weights(p):
    depth = p["w_in"].shape[0]
    src = _in_proj_columns()
    runs, start = [], 0
    for i in range(1, IN_PAD + 1):
        if i == IN_PAD or (src[i] != src[i - 1] + 1 if src[i - 1] >= 0 else src[i] >= 0):
            runs.append((start, i))
            start = i
    w_bf = p["w_in"].astype(BF16)
    pieces = [w_bf[:, :, src[a]:src[a] + (b - a)] if src[a] >= 0
              else jnp.zeros((depth, D_MODEL, b - a), BF16) for a, b in runs]
    w_in = jnp.concatenate(pieces, axis=2)

    wuq = p["mla_wuq"].reshape(depth, MLA_Q_RANK, 4, MLA_QK_DIM).transpose(0, 2, 1, 3)
    wq_plain = jnp.pad(wuq, ((0, 0), (0, 0), (0, 256 - MLA_Q_RANK), (0, 128 - MLA_QK_DIM)))
    wq_swap = jnp.pad(_swap_tail(wuq), ((0, 0), (0, 0), (0, 256 - MLA_Q_RANK), (0, 128 - MLA_QK_DIM)))
    wukv = p["mla_wukv"].reshape(depth, 128, 4, 128)
    wk = _pad_last(wukv[..., :64].transpose(0, 2, 1, 3), 128)
    wv_heads = wukv[..., 64:].transpose(0, 2, 1, 3)
    wv = jnp.concatenate([wv_heads[:, 0::2], wv_heads[:, 1::2]], axis=-1)

    def gain2(g):
        return jnp.stack([_pad_last(g, 128), _pad_last(_swap_tail(g), 128)], axis=1)

    w_out = p["w_out"]
    oc = w_out[:, 512:768].reshape(depth, 4, 64, D_MODEL)[:, jnp.asarray([0, 2, 1, 3])].reshape(depth, 256, D_MODEL)
    w_out = jnp.concatenate([w_out[:, :512], oc, w_out[:, 768:]], axis=1).astype(BF16)

    wu_pad = jnp.zeros((depth, 2, 128, GROUP_W), F32)
    wu_pad = wu_pad.at[:, 0, 0:64].set(p["rw_wu"][:, 0]).at[:, 1, 64:128].set(p["rw_wu"][:, 1])
    au_pad = jnp.zeros((depth, 2, 128, GROUP_W), F32)
    au_pad = au_pad.at[:, 0, 0:32].set(p["rw_au"][:, 0]).at[:, 1, 32:64].set(p["rw_au"][:, 1])

    return {
        "w_in": w_in,
        "norm1_g": p["norm1_g"][:, None, :],
        "norm2_g": p["norm2_g"][:, None, :],
        "w_out": w_out,
        "w0": p["rw_w0"], "a0": p["rw_a0"],
        "wu": wu_pad.astype(BF16), "au": au_pad.astype(BF16),
        "kkp": p["rw_kk"][:, None, :], "ka": p["rw_ka"][:, None, :],
        "gu": p["rw_gu"].astype(BF16),
        "rk": p["rw_rk"].reshape(depth, 1, GROUP_W),
        "gn": p["rw_gn"][:, None, :],
        "qa_g": _pad_last(p["mla_qa_g"], 256)[:, None, :],
        "wqq": jnp.concatenate([wq_plain, wq_swap], axis=1).astype(BF16),
        "qn_g2": gain2(p["mla_qn_g"]),
        "kva_g": p["mla_kva_g"][:, None, :],
        "wk": wk.astype(BF16), "wv": jnp.swapaxes(wv, -1, -2).astype(BF16),
        "kn_g2": gain2(p["mla_kn_g"]),
        "gq_g": jnp.tile(p["gqa_qn_g"], (1, 2))[:, None, :],
        "gk_g": jnp.tile(p["gqa_kn_g"], (1, 2))[:, None, :],
        "conv_w": p["conv_w"], "conv_b": p["conv_b"][:, None, :],
        "wq": p["peer_wq"].astype(BF16),
        "k1": p["peer_k1"].astype(BF16), "k2": p["peer_k2"].astype(BF16),
        "u": p["peer_u"].astype(BF16),
        "vt": jnp.swapaxes(p["peer_v"], 1, 2).astype(BF16),
    }


def _rope_tables():
    t = jnp.arange(LAT_LEN, dtype=F32)
    grid_row = jnp.floor(t / GRID_W)
    grid_col = t - grid_row * GRID_W

    def angles(rot_dim):
        n_freq = rot_dim // 4
        freqs = ROPE_THETA ** (-jnp.arange(n_freq, dtype=F32) / n_freq)
        ang = jnp.concatenate([grid_row[:, None] * freqs, grid_col[:, None] * freqs], axis=-1)
        return jnp.cos(ang), jnp.sin(ang)

    cm, sm = angles(32)
    ones64 = jnp.ones((LAT_LEN, 64), F32)
    zeros64 = jnp.zeros((LAT_LEN, 64), F32)
    cosm = jnp.concatenate([ones64, cm, cm, ones64[:, :32]], axis=-1)
    sinm = jnp.concatenate([zeros64, -sm, sm, zeros64[:, :32]], axis=-1)
    cg, sg = angles(64)
    cosg = jnp.tile(jnp.concatenate([cg, cg], axis=-1), (1, 2))
    sing = jnp.tile(jnp.concatenate([-sg, sg], axis=-1), (1, 2))
    ident = jnp.ones((TILE, 128), F32)
    zero = jnp.zeros((TILE, 128), F32)
    return {"cosm": jnp.concatenate([ident, cosm]), "sinm": jnp.concatenate([zero, sinm]),
            "cosg": jnp.concatenate([ident, cosg]), "sing": jnp.concatenate([zero, sing])}


def _states_to_pairs(s):
    lead = s.shape[:-3]
    s = s.reshape(lead + (2, 2, 64, 64))
    z = jnp.zeros(lead + (2, 64, 64), s.dtype)
    top = jnp.concatenate([s[..., 0, :, :], z], axis=-1)
    bot = jnp.concatenate([z, s[..., 1, :, :]], axis=-1)
    return jnp.concatenate([top, bot], axis=-2)


def _pairs_to_states(sp):
    lead = sp.shape[:-3]
    a = sp[..., 0:64, 0:64]
    b = sp[..., 64:128, 64:128]
    return jnp.stack([a, b], axis=-3).reshape(lead + (4, 64, 64))


def _conv_halos(d_grp):
    n_tiles = d_grp.shape[0] // TILE
    tiles = d_grp.reshape(n_tiles, TILE, D_W)
    first_rows = tiles[:, 0, :]
    last_rows = tiles[:, TILE - 1, :]
    zero = jnp.zeros((1, D_W), d_grp.dtype)
    prev = jnp.concatenate([zero, last_rows[:-1]], axis=0)
    nxt = jnp.concatenate([first_rows[1:], zero], axis=0)
    idx = np.arange(n_tiles)
    n_ctx_tiles = N_CTX_TOK // TILE
    per_seq = LAT_LEN // TILE
    lat_pos = (idx - n_ctx_tiles) % per_seq
    seq_start = np.where(idx < n_ctx_tiles, True, lat_pos == 0)
    seq_end = np.where(idx < n_ctx_tiles, True, lat_pos == per_seq - 1)
    prev = jnp.where(jnp.asarray(seq_start)[:, None], 0.0, prev)
    nxt = jnp.where(jnp.asarray(seq_end)[:, None], 0.0, nxt)
    return prev[:, None, :], nxt[:, None, :]


def kernel(x_prompt, x_sample, state_rwkv, cache_mla_ckv, cache_mla_krope, cache_gqa_k, cache_gqa_v, c, c_ctx, norm1_g, norm2_g, w_mod, b_mod, w_in, w_out, rw_w0, rw_wu, rw_a0, rw_au, rw_gu, rw_kk, rw_ka, rw_rk, rw_gn, mla_qa_g, mla_wuq, mla_kva_g, mla_wukv, mla_qn_g, mla_kn_g, gqa_qn_g, gqa_kn_g, conv_w, conv_b, peer_wq, peer_k1, peer_k2, peer_u, peer_v):
    depth = w_in.shape[0]
    params = dict(norm1_g=norm1_g, norm2_g=norm2_g, w_in=w_in, w_out=w_out, rw_w0=rw_w0, rw_wu=rw_wu,
                  rw_a0=rw_a0, rw_au=rw_au, rw_gu=rw_gu, rw_kk=rw_kk, rw_ka=rw_ka, rw_rk=rw_rk, rw_gn=rw_gn,
                  mla_qa_g=mla_qa_g, mla_wuq=mla_wuq, mla_kva_g=mla_kva_g, mla_wukv=mla_wukv,
                  mla_qn_g=mla_qn_g, mla_kn_g=mla_kn_g, gqa_qn_g=gqa_qn_g, gqa_kn_g=gqa_kn_g,
                  conv_w=conv_w, conv_b=conv_b, peer_wq=peer_wq, peer_k1=peer_k1, peer_k2=peer_k2,
                  peer_u=peer_u, peer_v=peer_v)
    lw_all = _layer_weights(params)
    tabs = _rope_tables()

    cvecs = jnp.concatenate([c_ctx[None, :], c, jnp.zeros((8 - 1 - N_LAT_SEQ, D_MODEL), F32)], axis=0)
    mods = _modulation(cvecs, w_mod, b_mod).reshape(depth, 8, 1, 6 * D_MODEL)

    x = jnp.concatenate([x_prompt.reshape(N_CTX_TOK, D_MODEL), x_sample.reshape(N_LAT_TOK, D_MODEL)], axis=0)

    s0_lat = _states_to_pairs(state_rwkv)
    s0_all = jnp.concatenate([jnp.zeros((N_CTX_SEQ,) + s0_lat.shape[1:], F32), s0_lat], axis=0)

    krope_placed = jnp.pad(cache_mla_krope, ((0, 0), (0, 0), (0, 0), (64, 32)))
    kctx_m, vctx_m = _mla_ctx_kv(cache_mla_ckv, krope_placed, lw_all["wk"], lw_all["wv"], lw_all["kn_g2"])
    past = cache_gqa_k.shape[2]
    kctx_g = cache_gqa_k.reshape(N_LAT_SEQ, depth, past, 128)
    vctx_g = jnp.swapaxes(cache_gqa_v.reshape(N_LAT_SEQ, depth, past, 128), -1, -2)

    ident = lambda p: p
    st_a, st_ckv, st_kr, st_k, st_v = [], [], [], [], []
    for l in range(depth):
        lw = {name: w[l] for name, w in lw_all.items() if name not in STACKED_WEIGHTS}
        mod = mods[l]
        a_grp, b_grp, c_grp, d_grp = _in_proj(x, mod, lw["norm1_g"], lw_all["w_in"], l)

        yf, yb, s_fin = _rwkv_scan(a_grp, s0_all[:, l], lw["w0"], lw["wu"], lw["a0"], lw["au"], lw["kkp"], lw["ka"])

        qm, km, vm, ckvn, qg, kg, vg, kgn = _attn_prep(b_grp, c_grp, tabs, lw)
        ob_ctx = _attention(qm, km, km, vm, lambda p: 2 * p, lambda p: 2 * p + 1, lambda p: 2 * p,
                            lambda p: 2 * p + 1, ident, N_CTX_SEQ, CTX_LEN, 0)
        ob_lat = _attention(qm, km, km, vm, lambda p: 2 * p, lambda p: 2 * p + 1, lambda p: 2 * p,
                            lambda p: 2 * p + 1, ident, N_LAT_SEQ, LAT_LEN, N_CTX_TOK,
                            ctx=(kctx_m[l], kctx_m[l], vctx_m[l]))
        oc_ctx = _attention(qg, kg, kg, vg, ident, lambda p: p + 2, ident, ident, ident,
                            N_CTX_SEQ, CTX_LEN, 0)
        oc_lat = _attention(qg, kg, kg, vg, ident, lambda p: p + 2, ident, ident, ident,
                            N_LAT_SEQ, LAT_LEN, N_CTX_TOK,
                            ctx=(kctx_g[:, l], kctx_g[:, l], vctx_g[:, l]))

        d_prev, d_next = _conv_halos(d_grp)
        x1, h2, q = _out_proj(x, mod, a_grp, yf, yb, ob_ctx, ob_lat, oc_ctx, oc_lat, d_grp, d_prev, d_next, lw,
                              lw_all["w_out"], lw_all["wq"], l)

        thr, s2, e1, e2 = _peer_topk(q, lw["k1"], lw["k2"])
        x = _peer_dense(h2, lw_all["u"], lw_all["vt"], l, thr, s2, e1, e2, x1, mod)

        st_a.append(_pairs_to_states(s_fin[:N_CTX_SEQ]))
        st_ckv.append(ckvn[:N_CTX_TOK].reshape(N_CTX_SEQ, CTX_LEN, 128))
        st_kr.append(b_grp[:N_CTX_TOK, 448:480].reshape(N_CTX_SEQ, CTX_LEN, 32))
        st_k.append(kgn[:N_CTX_TOK].reshape(N_CTX_SEQ, CTX_LEN, 2, HEAD_DIM))
        st_v.append(c_grp[:N_CTX_TOK, 384:512].reshape(N_CTX_SEQ, CTX_LEN, 2, HEAD_DIM))

    y_prompt = x[:N_CTX_TOK].reshape(N_CTX_SEQ, CTX_LEN, D_MODEL)
    y_sample = x[N_CTX_TOK:].reshape(N_LAT_SEQ, LAT_LEN, D_MODEL)
    return (y_prompt, y_sample, jnp.stack(st_a, axis=1), jnp.stack(st_ckv, axis=1), jnp.stack(st_kr, axis=1),
            jnp.stack(st_k, axis=1), jnp.stack(st_v, axis=1))
```

```python
import functools

import numpy as np
import jax
import jax.numpy as jnp
from jax import lax
from jax.experimental import pallas as pl
from jax.experimental.pallas import tpu as pltpu

F32 = jnp.float32
BF16 = jnp.bfloat16

D_MODEL = 1024
N_CTX_SEQ = 16
CTX_LEN = 256
N_LAT_SEQ = 2
LAT_LEN = 4096
GRID_W = 64
ROPE_THETA = 10000.0
EPS = 1e-6
GROUP_W = 256
HEAD_DIM = 64
RW_DECAY_SCALE = 0.6065306597
MLA_QK_DIM = 96
MLA_Q_RANK = 192
PEER_HEADS = 8
PEER_KEYS = 128
PEER_TOPK = 16
N_EXPERTS = PEER_KEYS * PEER_KEYS

N_CTX_TOK = N_CTX_SEQ * CTX_LEN
N_LAT_TOK = N_LAT_SEQ * LAT_LEN
N_TOK = N_CTX_TOK + N_LAT_TOK
COND_GROUP = 4096

A_W, B_W, C_W, D_W = 1152, 640, 512, 768
IN_PAD = A_W + B_W + C_W + D_W

SCAN_CHUNK = 128
TILE = 256
IN_TILE = 512
PEER_TN = 512
PEER_TE = 2048
TOPK_TN = 128
ATT_TQ = 512
ATT_CHUNK = 1024

VMEM_LIMIT = 56 * 1024 * 1024


def _cparams(sem):
    return pltpu.CompilerParams(dimension_semantics=sem, vmem_limit_bytes=VMEM_LIMIT)


def _dot(a, b, precision=None):
    return jnp.dot(a, b, preferred_element_type=F32, precision=precision)


def _dot_nt(a, b, precision=None):
    return lax.dot_general(a, b, (((1,), (1,)), ((), ())), preferred_element_type=F32,
                           precision=precision)


def _dot_tn(a, b, precision=None):
    return lax.dot_general(a, b, (((0,), (0,)), ((), ())), preferred_element_type=F32,
                           precision=precision)


def _rms_rows(x, g):
    return x * lax.rsqrt(jnp.mean(x * x, axis=-1, keepdims=True) + EPS) * g


def _group_sums(x, group_mat):
    hi = x.astype(BF16)
    lo = (x - hi.astype(F32)).astype(BF16)
    gm = group_mat.astype(BF16)
    return _dot(hi, gm) + _dot(lo, gm)


def _group_matrix(n, group, value):
    r = lax.broadcasted_iota(jnp.int32, (n, n), 0) // group
    c = lax.broadcasted_iota(jnp.int32, (n, n), 1) // group
    return jnp.where(r == c, value, 0.0).astype(F32)


def _mod_body(c_ref, w_ref, b_ref, o_ref):
    c = c_ref[...]
    s = c * jax.nn.sigmoid(c)
    o_ref[...] = _dot(s.astype(BF16), w_ref[...].astype(BF16)) + b_ref[...]


def _modulation(cvecs, w_mod, b_mod):
    depth = w_mod.shape[0]
    tn = 1536
    return pl.pallas_call(
        _mod_body,
        grid=(depth, 6 * D_MODEL // tn),
        in_specs=[pl.BlockSpec((8, D_MODEL), lambda l, j: (0, 0)),
                  pl.BlockSpec((None, D_MODEL, tn), lambda l, j: (l, 0, j)),
                  pl.BlockSpec((None, 1, tn), lambda l, j: (l, 0, j))],
        out_specs=pl.BlockSpec((None, 8, tn), lambda l, j: (l, 0, j)),
        out_shape=jax.ShapeDtypeStruct((depth, 8, 6 * D_MODEL), F32),
        compiler_params=_cparams(("arbitrary", "arbitrary")),
        name="adaln_mod",
    )(cvecs, w_mod, b_mod.reshape(depth, 1, 6 * D_MODEL))


def _in_body(x_ref, mod_ref, g_ref, w_ref, a_ref, b_ref, c_ref, d_ref):
    mod = mod_ref[...]
    shift = mod[:, 0:D_MODEL]
    scale = mod[:, D_MODEL:2 * D_MODEL]
    h = _rms_rows(x_ref[...], g_ref[...]) * (1.0 + scale) + shift
    y = _dot(h.astype(BF16), w_ref[...])
    a_ref[...] = y[:, 0:A_W]
    b_ref[...] = y[:, A_W:A_W + B_W]
    c_ref[...] = y[:, A_W + B_W:A_W + B_W + C_W]
    d_ref[...] = y[:, A_W + B_W + C_W:IN_PAD]


def _in_proj(x, mod, norm_g, w_pad_all, layer):
    n = x.shape[0]
    tpg = COND_GROUP // IN_TILE
    row = lambda i: (i, 0)
    return pl.pallas_call(
        _in_body,
        grid=(n // IN_TILE,),
        in_specs=[pl.BlockSpec((IN_TILE, D_MODEL), row),
                  pl.BlockSpec((None, 1, 6 * D_MODEL), lambda i: (i // tpg, 0, 0)),
                  pl.BlockSpec((1, D_MODEL), lambda i: (0, 0)),
                  pl.BlockSpec((None, D_MODEL, IN_PAD), lambda i: (layer, 0, 0))],
        out_specs=[pl.BlockSpec((IN_TILE, A_W), row), pl.BlockSpec((IN_TILE, B_W), row),
                   pl.BlockSpec((IN_TILE, C_W), row), pl.BlockSpec((IN_TILE, D_W), row)],
        out_shape=[jax.ShapeDtypeStruct((n, A_W), F32), jax.ShapeDtypeStruct((n, B_W), F32),
                   jax.ShapeDtypeStruct((n, C_W), F32), jax.ShapeDtypeStruct((n, D_W), F32)],
        compiler_params=_cparams(("arbitrary",)),
        name="in_proj",
    )(x, mod, norm_g, w_pad_all)


def _chunk_cumsum(x, reverse):
    c = x.shape[0]
    rows = lax.broadcasted_iota(jnp.int32, x.shape, 0)
    sh = 1
    while sh < c:
        if reverse:
            x = x + jnp.where(rows < c - sh, pltpu.roll(x, c - sh, 0), 0.0)
        else:
            x = x + jnp.where(rows >= sh, pltpu.roll(x, sh, 0), 0.0)
        sh *= 2
    return x


def _scan_body(fb_ref, bb_ref, first_ref, last_ref, seq_ref,
               af_ref, ab_ref, s0_ref, w0_ref, wu_ref, a0_ref, au_ref, kkp_ref, ka_ref,
               yf_ref, yb_ref, sfin_ref, s_scr):
    step = pl.program_id(0)

    @pl.when(first_ref[step] == 1)
    def _():
        s_scr[...] = s0_ref[...]

    c = SCAN_CHUNK
    row = lax.broadcasted_iota(jnp.int32, (c, c), 0)
    col = lax.broadcasted_iota(jnp.int32, (c, c), 1)
    lane_c = lax.broadcasted_iota(jnp.int32, (c, 128), 1) < HEAD_DIM
    lane_2c = lax.broadcasted_iota(jnp.int32, (2 * c, 128), 1) < HEAD_DIM
    same_head = _group_matrix(128, HEAD_DIM, 1.0)
    kkp = kkp_ref[...]
    ka = ka_ref[...]

    def per_head(lo, hi):
        return jnp.where(lane_c, lo, hi)

    groups = []
    for d in range(2):
        x_ref = af_ref if d == 0 else ab_ref
        if d == 0:
            strict, incl = col < row, col <= row
        else:
            strict, incl = col > row, col >= row
        r = x_ref[:, 0:256]
        k = x_ref[:, 256:512]
        v = x_ref[:, 512:768]
        wd = x_ref[:, 768:896]
        ad = x_ref[:, 896:1024]
        wlog = -RW_DECAY_SCALE * jax.nn.sigmoid(
            w0_ref[d:d + 1, :] + _dot(jnp.tanh(wd).astype(BF16), wu_ref[d]))
        a = jax.nn.sigmoid(a0_ref[d:d + 1, :] + _dot(ad.astype(BF16), au_ref[d]))
        kd = k * (1.0 + (a - 1.0) * ka)
        kkf = k * kkp
        cum = _chunk_cumsum(wlog, reverse=(d == 1))
        tot = cum[c - 1:c, :] if d == 0 else cum[0:1, :]
        p_inc = jnp.exp(cum)
        p_exc = jnp.exp(cum - wlog)
        p_inv = jnp.exp(-cum)
        p_tot = jnp.exp(tot)
        for p in range(2):
            sl = slice(128 * p, 128 * p + 128)
            groups.append(dict(d=d, p=p, sl=sl, strict=strict, incl=incl, kk_raw=kkf[:, sl],
                               a=a[:, sl], kd=kd[:, sl], r=r[:, sl], vb=v[:, sl].astype(BF16),
                               p_inc=p_inc[:, sl], p_exc=p_exc[:, sl], p_inv=p_inv[:, sl], pt=p_tot[:, sl]))

    for g in groups:
        g["ss"] = _group_sums(g["kk_raw"] * g["kk_raw"], same_head)
    for g in groups:
        kk = g["kk_raw"] * lax.rsqrt(g["ss"] + EPS)
        g["a_p"] = (-kk * g["p_exc"]).astype(BF16)
        g["r_p"] = (g["r"] * g["p_inc"]).astype(BF16)
        g["b_i"] = kk * g["a"] * g["p_inv"]
        g["k_i"] = g["kd"] * g["p_inv"]
        g["lhs"] = jnp.concatenate([g["a_p"], g["r_p"]], axis=0)
        g["rhs"] = jnp.concatenate([g["b_i"], g["k_i"]], axis=0).astype(BF16)
        g["sb"] = s_scr[g["d"], g["p"]].astype(BF16)
    zero_b = jnp.zeros((), BF16)
    for g in groups:
        g["gram"] = (_dot_nt(jnp.where(lane_2c, g["lhs"], zero_b), g["rhs"]),
                     _dot_nt(jnp.where(lane_2c, zero_b, g["lhs"]), g["rhs"]))
        g["a_s"] = _dot_nt(g["a_p"], g["sb"])
        g["r_s"] = _dot_nt(g["r_p"], g["sb"])
    for g in groups:
        st, inc = g["strict"], g["incl"]
        g["pow"] = [jnp.where(st, gm[0:c, 0:c], 0.0).astype(BF16) for gm in g["gram"]]
        g["dm"] = [jnp.where(st, gm[0:c, c:2 * c], 0.0).astype(BF16) for gm in g["gram"]]
        g["et"] = [jnp.where(inc, gm[c:2 * c, 0:c], 0.0).astype(BF16) for gm in g["gram"]]
        g["ft"] = [jnp.where(inc, gm[c:2 * c, c:2 * c], 0.0).astype(BF16) for gm in g["gram"]]
    for g in groups:
        g["z"] = g["a_s"] + per_head(_dot(g["dm"][0], g["vb"]), _dot(g["dm"][1], g["vb"]))
    span = 1
    while span < c:
        for g in groups:
            zb = g["z"].astype(BF16)
            g["z"] = g["z"] + per_head(_dot(g["pow"][0], zb), _dot(g["pow"][1], zb))
        span *= 2
        if span < c:
            for g in groups:
                g["pow"] = [_dot(m, m).astype(BF16) for m in g["pow"]]
    for g in groups:
        zb = g["z"].astype(BF16)
        g["zb"] = zb
        y = g["r_s"] + per_head(_dot(g["et"][0], zb) + _dot(g["ft"][0], g["vb"]),
                                _dot(g["et"][1], zb) + _dot(g["ft"][1], g["vb"]))
        y_ref = yf_ref if g["d"] == 0 else yb_ref
        y_ref[:, g["sl"]] = y
    for g in groups:
        pt = g["pt"]
        upd = (_dot_tn(g["zb"], (g["b_i"] * pt).astype(BF16))
               + _dot_tn(g["vb"], (g["k_i"] * pt).astype(BF16)))
        s_scr[g["d"], g["p"]] = s_scr[g["d"], g["p"]] * pt + same_head * upd

    @pl.when(last_ref[step] == 1)
    def _():
        sfin_ref[...] = s_scr[...]


def _scan_tables():
    c = SCAN_CHUNK
    fb, bb, first, last, seq = [], [], [], [], []
    base = 0
    sid = 0
    for nseq, length in ((N_CTX_SEQ, CTX_LEN), (N_LAT_SEQ, LAT_LEN)):
        nc = length // c
        for _ in range(nseq):
            for j in range(nc):
                fb.append(base + j)
                bb.append(base + nc - 1 - j)
                first.append(int(j == 0))
                last.append(int(j == nc - 1))
                seq.append(sid)
            base += nc
            sid += 1
    return tuple(np.asarray(t, np.int32) for t in (fb, bb, first, last, seq))


def _rwkv_scan(a_grp, s0_pairs, w0, wu_pad, a0, au_pad, kkp, ka):
    tables = _scan_tables()
    n_steps = tables[0].shape[0]
    n_seq = s0_pairs.shape[0]
    n = a_grp.shape[0]
    c = SCAN_CHUNK
    const2 = lambda s, fb, bb, fi, la, sq: (0, 0)
    const3 = lambda s, fb, bb, fi, la, sq: (0, 0, 0)
    grid_spec = pltpu.PrefetchScalarGridSpec(
        num_scalar_prefetch=5,
        grid=(n_steps,),
        in_specs=[pl.BlockSpec((c, A_W), lambda s, fb, bb, fi, la, sq: (fb[s], 0)),
                  pl.BlockSpec((c, A_W), lambda s, fb, bb, fi, la, sq: (bb[s], 0)),
                  pl.BlockSpec((None, 2, 2, 128, 128), lambda s, fb, bb, fi, la, sq: (sq[s], 0, 0, 0, 0)),
                  pl.BlockSpec((2, GROUP_W), const2),
                  pl.BlockSpec((2, 128, GROUP_W), const3),
                  pl.BlockSpec((2, GROUP_W), const2),
                  pl.BlockSpec((2, 128, GROUP_W), const3),
                  pl.BlockSpec((1, GROUP_W), const2),
                  pl.BlockSpec((1, GROUP_W), const2)],
        out_specs=[pl.BlockSpec((c, GROUP_W), lambda s, fb, bb, fi, la, sq: (fb[s], 0)),
                   pl.BlockSpec((c, GROUP_W), lambda s, fb, bb, fi, la, sq: (bb[s], 0)),
                   pl.BlockSpec((None, 2, 2, 128, 128), lambda s, fb, bb, fi, la, sq: (sq[s], 0, 0, 0, 0))],
        scratch_shapes=[pltpu.VMEM((2, 2, 128, 128), F32)],
    )
    return pl.pallas_call(
        _scan_body,
        grid_spec=grid_spec,
        out_shape=[jax.ShapeDtypeStruct((n, GROUP_W), F32), jax.ShapeDtypeStruct((n, GROUP_W), F32),
                   jax.ShapeDtypeStruct((n_seq, 2, 2, 128, 128), F32)],
        compiler_params=_cparams(("arbitrary",)),
        name="rwkv_scan",
    )(*[jnp.asarray(t) for t in tables], a_grp, a_grp, s0_pairs, w0, wu_pad, a0, au_pad, kkp, ka)


LOG2_E = 1.4426950408889634
MLA_SCALE = MLA_QK_DIM ** -0.5 * LOG2_E
GQA_SCALE = HEAD_DIM ** -0.5 * LOG2_E


def _mla_keys(ckv_b, rope_slot, rope_slot_sw, wk_ref, kng_ref, cosm, sinm, k_out_ref):
    g = kng_ref[0:1, :]
    g_sw = kng_ref[1:2, :]
    for h in range(4):
        nope = _dot(ckv_b, wk_ref[h])
        kr = nope + rope_slot
        rs = lax.rsqrt(jnp.sum(kr * kr, axis=-1, keepdims=True) * (1.0 / MLA_QK_DIM) + EPS)
        if cosm is None:
            k_out_ref[h] = (kr * rs * g).astype(BF16)
        else:
            ks = nope + rope_slot_sw
            k_out_ref[h] = ((kr * rs * g) * cosm + (ks * rs * g_sw) * sinm).astype(BF16)


def _prep_body(b_ref, c_ref, cosm_ref, sinm_ref, cosg_ref, sing_ref,
               qag_ref, wqq_ref, qng_ref, kvag_ref, wk_ref, wv_ref, kng_ref, gqg_ref, gkg_ref,
               qm_ref, km_ref, vm_ref, ckvn_ref, qg_ref, kg_ref, vg_ref, kgn_ref):
    cosm = cosm_ref[...]
    sinm = sinm_ref[...]
    qc = b_ref[:, 0:256]
    qn = qc * lax.rsqrt(jnp.sum(qc * qc, axis=-1, keepdims=True) * (1.0 / MLA_Q_RANK) + EPS) * qag_ref[...]
    qnb = qn.astype(BF16)
    g = qng_ref[0:1, :]
    g_sw = qng_ref[1:2, :]
    for h in range(4):
        qr = _dot(qnb, wqq_ref[h])
        qs = _dot(qnb, wqq_ref[4 + h])
        rs = lax.rsqrt(jnp.sum(qr * qr, axis=-1, keepdims=True) * (1.0 / MLA_QK_DIM) + EPS)
        qm_ref[h] = (((qr * rs * g) * cosm + (qs * rs * g_sw) * sinm) * MLA_SCALE).astype(BF16)
    ckv = _rms_rows(b_ref[:, 256:384], kvag_ref[...])
    ckvn_ref[...] = ckv
    cb = ckv.astype(BF16)
    _mla_keys(cb, b_ref[:, 384:512], b_ref[:, 512:640], wk_ref, kng_ref, cosm, sinm, km_ref)
    for p in range(2):
        vm_ref[p] = _dot_nt(wv_ref[p], cb).astype(BF16)

    tm = c_ref.shape[0]
    avg = _group_matrix(128, HEAD_DIM, 1.0 / HEAD_DIM)
    lane = lax.broadcasted_iota(jnp.int32, (tm, 128), 1)
    first_half = (lane % HEAD_DIM) < (HEAD_DIM // 2)
    low = lane < HEAD_DIM
    cosg = cosg_ref[...]
    sing = sing_ref[...]

    def rotate(xn):
        swapped = jnp.where(first_half, pltpu.roll(xn, 128 - HEAD_DIM // 2, 1), pltpu.roll(xn, HEAD_DIM // 2, 1))
        return xn * cosg + swapped * sing

    for blk in range(2):
        x = c_ref[:, 128 * blk:128 * blk + 128]
        xn = x * lax.rsqrt(_group_sums(x * x, avg) + EPS) * gqg_ref[...]
        xr = rotate(xn) * GQA_SCALE
        qg_ref[blk] = jnp.where(low, xr, 0.0).astype(BF16)
        qg_ref[blk + 2] = jnp.where(low, 0.0, xr).astype(BF16)
    xk = c_ref[:, 256:384]
    kn = xk * lax.rsqrt(_group_sums(xk * xk, avg) + EPS) * gkg_ref[...]
    kgn_ref[...] = kn
    kg_ref[...] = rotate(kn).astype(BF16)
    vg_ref[...] = c_ref[:, 384:512].T.astype(BF16)


def _attn_prep(b_grp, c_grp, tabs, lw):
    n = b_grp.shape[0]
    n_ctx_tiles = N_CTX_TOK // TILE
    lat_tiles = LAT_LEN // TILE
    row = lambda i: (i, 0)
    hrow = lambda i: (0, i, 0)
    tab = lambda i: (jnp.where(i < n_ctx_tiles, 0, 1 + (i - n_ctx_tiles) % lat_tiles), 0)
    c2 = lambda i: (0, 0)
    c3 = lambda i: (0, 0, 0)
    return pl.pallas_call(
        _prep_body,
        grid=(n // TILE,),
        in_specs=[pl.BlockSpec((TILE, B_W), row), pl.BlockSpec((TILE, C_W), row),
                  pl.BlockSpec((TILE, 128), tab), pl.BlockSpec((TILE, 128), tab),
                  pl.BlockSpec((TILE, 128), tab), pl.BlockSpec((TILE, 128), tab),
                  pl.BlockSpec((1, 256), c2), pl.BlockSpec((8, 256, 128), c3), pl.BlockSpec((2, 128), c2),
                  pl.BlockSpec((1, 128), c2), pl.BlockSpec((4, 128, 128), c3), pl.BlockSpec((2, 128, 128), c3),
                  pl.BlockSpec((2, 128), c2), pl.BlockSpec((1, 128), c2), pl.BlockSpec((1, 128), c2)],
        out_specs=[pl.BlockSpec((4, TILE, 128), hrow), pl.BlockSpec((4, TILE, 128), hrow),
                   pl.BlockSpec((2, 128, TILE), lambda i: (0, 0, i)), pl.BlockSpec((TILE, 128), row),
                   pl.BlockSpec((4, TILE, 128), hrow), pl.BlockSpec((TILE, 128), row),
                   pl.BlockSpec((128, TILE), lambda i: (0, i)), pl.BlockSpec((TILE, 128), row)],
        out_shape=[jax.ShapeDtypeStruct((4, n, 128), BF16), jax.ShapeDtypeStruct((4, n, 128), BF16),
                   jax.ShapeDtypeStruct((2, 128, n), BF16), jax.ShapeDtypeStruct((n, 128), F32),
                   jax.ShapeDtypeStruct((4, n, 128), BF16), jax.ShapeDtypeStruct((n, 128), BF16),
                   jax.ShapeDtypeStruct((128, n), BF16), jax.ShapeDtypeStruct((n, 128), F32)],
        compiler_params=_cparams(("arbitrary",)),
        name="attn_prep",
    )(b_grp, c_grp, tabs["cosm"], tabs["sinm"], tabs["cosg"], tabs["sing"],
      lw["qa_g"], lw["wqq"], lw["qn_g2"], lw["kva_g"], lw["wk"], lw["wv"], lw["kn_g2"], lw["gq_g"], lw["gk_g"])


def _ctxkv_body(ckv_ref, krp_ref, wk_ref, wv_ref, kng_ref, k_ref, v_ref):
    cb = ckv_ref[...].astype(BF16)
    _mla_keys(cb, krp_ref[...], None, wk_ref, kng_ref, None, None, k_ref)
    for p in range(2):
        v_ref[p] = _dot_nt(wv_ref[p], cb).astype(BF16)


def _mla_ctx_kv(cache_ckv, cache_krope_placed, wk, wv, kn_g2):
    nb, depth, plen, _ = cache_ckv.shape
    return pl.pallas_call(
        _ctxkv_body,
        grid=(depth, nb),
        in_specs=[pl.BlockSpec((None, None, plen, 128), lambda l, b: (b, l, 0, 0)),
                  pl.BlockSpec((None, None, plen, 128), lambda l, b: (b, l, 0, 0)),
                  pl.BlockSpec((None, 4, 128, 128), lambda l, b: (l, 0, 0, 0)),
                  pl.BlockSpec((None, 2, 128, 128), lambda l, b: (l, 0, 0, 0)),
                  pl.BlockSpec((None, 2, 128), lambda l, b: (l, 0, 0))],
        out_specs=[pl.BlockSpec((None, 4, None, plen, 128), lambda l, b: (l, 0, b, 0, 0)),
                   pl.BlockSpec((None, 2, None, 128, plen), lambda l, b: (l, 0, b, 0, 0))],
        out_shape=[jax.ShapeDtypeStruct((depth, 4, nb, plen, 128), BF16),
                   jax.ShapeDtypeStruct((depth, 2, nb, 128, plen), BF16)],
        compiler_params=_cparams(("arbitrary", "arbitrary")),
        name="mla_ctx_kv",
    )(cache_ckv, cache_krope_placed, wk, wv, kn_g2)


def _attn_body(*refs, has_ctx):
    if has_ctx:
        qa_ref, qb_ref, ka_ref, kb_ref, vt_ref, kca_ref, kcb_ref, vct_ref, o_ref, sa_scr, sb_scr = refs
    else:
        qa_ref, qb_ref, ka_ref, kb_ref, vt_ref, o_ref, sa_scr, sb_scr = refs
        kca_ref = kcb_ref = vct_ref = None
    tk = ka_ref.shape[0]
    chunk = min(ATT_CHUNK, tk)
    q = (qa_ref[...], qb_ref[...])
    k_refs = (ka_ref, kb_ref)
    kc_refs = (kca_ref, kcb_ref)
    s_scr = (sa_scr, sb_scr)
    vrows = (slice(0, HEAD_DIM), slice(HEAD_DIM, 2 * HEAD_DIM))
    spans = [("new", c * chunk, chunk) for c in range(tk // chunk)]
    if has_ctx:
        spans = [("ctx", 0, kca_ref.shape[0])] + spans

    def scratch_rows(kind, start, size):
        base = tk if kind == "ctx" else 0
        return slice(base + start, base + start + size)

    m = [None, None]
    den = [None, None]
    acc = [None, None]

    def score_pass(hd, span):
        kind, start, size = span
        if kind == "ctx":
            keys = kc_refs[hd][...].astype(BF16)
        else:
            keys = k_refs[hd][start:start + size, :]
        s = _dot_nt(keys, q[hd])
        s_scr[hd][scratch_rows(kind, start, size), :] = s
        cm = jnp.max(s, axis=0, keepdims=True)
        m[hd] = cm if m[hd] is None else jnp.maximum(m[hd], cm)

    def value_pass(hd, span):
        kind, start, size = span
        e = jnp.exp2(s_scr[hd][scratch_rows(kind, start, size), :] - m[hd])
        if kind == "ctx":
            vals = vct_ref[vrows[hd], :].astype(BF16)
        else:
            vals = vt_ref[vrows[hd], start:start + size]
        part = _dot(vals, e.astype(BF16))
        rs = jnp.sum(e, axis=0, keepdims=True)
        den[hd] = rs if den[hd] is None else den[hd] + rs
        acc[hd] = part if acc[hd] is None else acc[hd] + part

    for span in spans:
        for hd in range(2):
            score_pass(hd, span)
    for span in spans:
        for hd in range(2):
            value_pass(hd, span)
    out_t = jnp.concatenate([acc[0] / den[0], acc[1] / den[1]], axis=0)
    o_ref[...] = out_t.T.astype(o_ref.dtype)


def _attention(q, ka, kb, v, head_a, head_b, k_head_a, k_head_b, v_idx, n_seq, seq_len, tok0, ctx=None):
    tq = min(ATT_TQ, seq_len)
    nq = seq_len // tq
    seq0 = tok0 // seq_len
    q0 = tok0 // tq

    def qmap(hsel):
        return lambda b, p, i: (hsel(p), q0 + b * nq + i, 0)

    def kmap(hsel, arr):
        if arr.ndim == 3:
            return lambda b, p, i: (hsel(p), seq0 + b, 0)
        return lambda b, p, i: (seq0 + b, 0)

    def kspec(arr, hsel):
        if arr.ndim == 3:
            return pl.BlockSpec((None, seq_len, 128), kmap(hsel, arr))
        return pl.BlockSpec((seq_len, 128), kmap(hsel, arr))

    if v.ndim == 3:
        vspec = pl.BlockSpec((None, 128, seq_len), lambda b, p, i: (v_idx(p), 0, seq0 + b))
    else:
        vspec = pl.BlockSpec((128, seq_len), lambda b, p, i: (0, seq0 + b))
    in_specs = [pl.BlockSpec((None, tq, 128), qmap(head_a)), pl.BlockSpec((None, tq, 128), qmap(head_b)),
                kspec(ka, k_head_a), kspec(kb, k_head_b), vspec]
    args = [q, q, ka, kb, v]
    if ctx is not None:
        kca, kcb, vc = ctx
        past = vc.shape[-1]

        def cspec(arr, hsel, shape):
            if arr.ndim == 4:
                return pl.BlockSpec((None, None) + shape, lambda b, p, i: (hsel(p), b, 0, 0))
            return pl.BlockSpec((None,) + shape, lambda b, p, i: (b, 0, 0))

        in_specs += [cspec(kca, k_head_a, (past, 128)), cspec(kcb, k_head_b, (past, 128)),
                     cspec(vc, v_idx, (128, past))]
        args += [kca, kcb, vc]
    return pl.pallas_call(
        functools.partial(_attn_body, has_ctx=ctx is not None),
        grid=(n_seq, 2, nq),
        in_specs=in_specs,
        out_specs=pl.BlockSpec((tq, 128), lambda b, p, i: (b * nq + i, p)),
        out_shape=jax.ShapeDtypeStruct((n_seq * seq_len, 256), BF16),
        scratch_shapes=[pltpu.VMEM((seq_len + (0 if ctx is None else ctx[2].shape[-1]), tq), F32)] * 2,
        compiler_params=_cparams(("arbitrary", "arbitrary", "arbitrary")),
        name="attention_ctx" if ctx is None else "attention_lat",
    )(*args)


def _out_body(x_ref, mod_ref, a_ref, yf_ref, yb_ref, obc_ref, obl_ref, occ_ref, ocl_ref,
              d_ref, dprev_ref, dnext_ref, wo_ref, gu_ref, rk_ref, gn_ref, cw_ref, cb_ref, n2_ref, wq_ref,
              x1_ref, h2_ref, q_ref, *, n_ctx_tiles):
    i = pl.program_id(0)
    is_ctx = i < n_ctx_tiles
    tm = x_ref.shape[0]
    mod = mod_ref[...]
    gate1 = mod[:, 2 * D_MODEL:3 * D_MODEL]
    shift2 = mod[:, 3 * D_MODEL:4 * D_MODEL]
    scale2 = mod[:, 4 * D_MODEL:5 * D_MODEL]

    r = a_ref[:, 0:256]
    k = a_ref[:, 256:512]
    v = a_ref[:, 512:768]
    gd = a_ref[:, 1024:1152]
    y = yf_ref[...] + yb_ref[...]
    avg = _group_matrix(GROUP_W, HEAD_DIM, 1.0 / HEAD_DIM)
    ones = _group_matrix(GROUP_W, HEAD_DIM, 1.0)
    yn = y * lax.rsqrt(_group_sums(y * y, avg) + EPS) * gn_ref[...]
    bonus = _group_sums(r * k * rk_ref[...], ones) * v
    gate = _dot(jax.nn.sigmoid(gd).astype(BF16), gu_ref[...])
    o_a = (yn + bonus) * gate

    o_b = jnp.where(is_ctx, obc_ref[...], obl_ref[...])
    o_c = jnp.where(is_ctx, occ_ref[...], ocl_ref[...])

    u = d_ref[:, 512:768] * d_ref[:, 0:256]
    u_prev = dprev_ref[:, 512:768] * dprev_ref[:, 0:256]
    u_next = dnext_ref[:, 512:768] * dnext_ref[:, 0:256]
    rows = lax.broadcasted_iota(jnp.int32, (tm, GROUP_W), 0)
    up = jnp.where(rows == 0, u_prev, pltpu.roll(u, 1, 0))
    un = jnp.where(rows == tm - 1, u_next, pltpu.roll(u, tm - 1, 0))
    conv = up * cw_ref[0:1, :] + u * cw_ref[1:2, :] + un * cw_ref[2:3, :] + cb_ref[...]
    o_d = d_ref[:, 256:512] * conv

    mix_in = jnp.concatenate([o_a.astype(BF16), o_b, o_c, o_d.astype(BF16)], axis=1)
    x1 = x_ref[...] + gate1 * _dot(mix_in, wo_ref[...])
    x1_ref[...] = x1
    h2 = (_rms_rows(x1, n2_ref[...]) * (1.0 + scale2) + shift2).astype(BF16)
    h2_ref[...] = h2
    q_ref[...] = _dot(h2, wq_ref[...]).astype(q_ref.dtype)


def _out_proj(x, mod, a_grp, yf, yb, ob_ctx, ob_lat, oc_ctx, oc_lat, d_grp, d_prev, d_next, lw, w_out_all, wq_all,
              layer):
    n = x.shape[0]
    n_tiles = n // TILE
    n_ctx_tiles = N_CTX_TOK // TILE
    tpg = COND_GROUP // TILE
    row = lambda i: (i, 0)
    ctx_row = lambda i: (jnp.minimum(i, n_ctx_tiles - 1), 0)
    lat_row = lambda i: (jnp.maximum(i - n_ctx_tiles, 0), 0)
    halo = lambda i: (i, 0, 0)
    c2 = lambda i: (0, 0)
    nq = PEER_HEADS * 2 * PEER_KEYS
    return pl.pallas_call(
        functools.partial(_out_body, n_ctx_tiles=n_ctx_tiles),
        grid=(n_tiles,),
        in_specs=[pl.BlockSpec((TILE, D_MODEL), row),
                  pl.BlockSpec((None, 1, 6 * D_MODEL), lambda i: (i // tpg, 0, 0)),
                  pl.BlockSpec((TILE, A_W), row),
                  pl.BlockSpec((TILE, GROUP_W), row), pl.BlockSpec((TILE, GROUP_W), row),
                  pl.BlockSpec((TILE, GROUP_W), ctx_row), pl.BlockSpec((TILE, GROUP_W), lat_row),
                  pl.BlockSpec((TILE, GROUP_W), ctx_row), pl.BlockSpec((TILE, GROUP_W), lat_row),
                  pl.BlockSpec((TILE, D_W), row),
                  pl.BlockSpec((None, 1, D_W), halo), pl.BlockSpec((None, 1, D_W), halo),
                  pl.BlockSpec((None, D_MODEL, D_MODEL), lambda i: (layer, 0, 0)), pl.BlockSpec((128, GROUP_W), c2),
                  pl.BlockSpec((1, GROUP_W), c2), pl.BlockSpec((1, GROUP_W), c2),
                  pl.BlockSpec((3, GROUP_W), c2), pl.BlockSpec((1, GROUP_W), c2),
                  pl.BlockSpec((1, D_MODEL), c2), pl.BlockSpec((None, D_MODEL, nq), lambda i: (layer, 0, 0))],
        out_specs=[pl.BlockSpec((TILE, D_MODEL), row), pl.BlockSpec((TILE, D_MODEL), row),
                   pl.BlockSpec((TILE, nq), row)],
        out_shape=[jax.ShapeDtypeStruct((n, D_MODEL), F32), jax.ShapeDtypeStruct((n, D_MODEL), BF16),
                   jax.ShapeDtypeStruct((n, nq), BF16)],
        compiler_params=_cparams(("arbitrary",)),
        name="out_proj",
    )(x, mod, a_grp, yf, yb, ob_ctx, ob_lat, oc_ctx, oc_lat, d_grp, d_prev, d_next,
      w_out_all, lw["gu"], lw["rk"], lw["gn"], lw["conv_w"], lw["conv_b"], lw["norm2_g"], wq_all)


def _exchange(a, b):
    if a is None:
        return b, None
    if b is None:
        return a, None
    return jnp.maximum(a, b), jnp.minimum(a, b)


def _sort16_desc(xs):
    xs = list(xs)
    k = 2
    while k <= 16:
        j = k // 2
        while j >= 1:
            for i in range(16):
                partner = i ^ j
                if partner > i:
                    hi, lo = _exchange(xs[i], xs[partner])
                    xs[i], xs[partner] = (hi, lo) if (i & k) == 0 else (lo, hi)
            j //= 2
        k *= 2
    return xs


def _bitonic_merge_desc(xs):
    xs = list(xs)
    j = 8
    while j >= 1:
        for i in range(16):
            partner = i ^ j
            if partner > i:
                xs[i], xs[partner] = _exchange(xs[i], xs[partner])
        j //= 2
    return xs


def _top16_of_rows(xs):
    ys = _sort16_desc(xs)
    for shift in (4, 2, 1):
        zs = [None if y is None else pltpu.roll(y, shift, 0) for y in ys]
        ts = [_exchange(ys[i], zs[15 - i])[0] for i in range(16)]
        ys = _bitonic_merge_desc(ts)
    return ys


def _topk_body(q_ref, k1_ref, k2_ref, thr_ref, s2_ref, e1_ref, e2_ref):
    tn = q_ref.shape[0]
    sub = lax.broadcasted_iota(jnp.int32, (8, tn), 0)

    def spread(vals):
        out = vals[7]
        for s in range(6, -1, -1):
            out = jnp.where(sub == s, vals[s], out)
        return out

    for h in range(PEER_HEADS):
        qa = q_ref[:, (2 * h) * PEER_KEYS:(2 * h + 1) * PEER_KEYS].astype(BF16)
        qb = q_ref[:, (2 * h + 1) * PEER_KEYS:(2 * h + 2) * PEER_KEYS].astype(BF16)
        s1 = _dot_nt(k1_ref[h], qa)
        s2 = _dot_nt(k2_ref[h], qb)
        v1 = _top16_of_rows([s1[8 * i:8 * i + 8, :] for i in range(16)])
        v2 = _top16_of_rows([s2[8 * i:8 * i + 8, :] for i in range(16)])
        v2_lo, v2_hi, v1_hi = spread(v2[0:8]), spread(v2[8:16]), spread(v1[8:16])
        cands = ([v1[0] + v2_lo, v1[0] + v2_hi] + [v1[a] + v2_lo for a in range(1, 8)]
                 + [v1_hi + v2[0]] + [None] * 6)
        best = _top16_of_rows(cands)
        tau = best[PEER_TOPK - 1][0:1, :]
        zsum = jnp.exp(best[0] - best[0])
        for kth in range(1, PEER_TOPK):
            zsum = zsum + jnp.exp(best[kth] - best[0])
        thr = jnp.full(s1.shape, jnp.inf, F32)
        for b in range(PEER_TOPK):
            vb = v2[b][0:1, :]
            thr = jnp.where(s1 + vb >= tau, vb, thr)
        thr_ref[h] = thr
        s2_ref[h] = s2
        e1_ref[h] = jnp.exp(s1 - v1[0][0:1, :])
        e2_ref[h] = jnp.exp(s2 - v2[0][0:1, :]) * (0.5 / zsum[0:1, :])


def _rotated_block(i):
    per_tile = PEER_TN // TOPK_TN
    return (i // per_tile) * per_tile + (i + 1) % per_tile


def _peer_topk(q, k1, k2):
    n = q.shape[0]
    tn = TOPK_TN
    big = pl.BlockSpec((PEER_HEADS, PEER_KEYS, tn), lambda i: (0, 0, i))
    big_shape = jax.ShapeDtypeStruct((PEER_HEADS, PEER_KEYS, n), F32)
    c3 = lambda i: (0, 0, 0)
    return pl.pallas_call(
        _topk_body,
        grid=(n // tn,),
        in_specs=[pl.BlockSpec((tn, PEER_HEADS * 2 * PEER_KEYS), lambda i: (i, 0)),
                  pl.BlockSpec((PEER_HEADS, PEER_KEYS, PEER_KEYS), c3),
                  pl.BlockSpec((PEER_HEADS, PEER_KEYS, PEER_KEYS), c3)],
        out_specs=[big, big, big, pl.BlockSpec((PEER_HEADS, PEER_KEYS, tn), lambda i: (0, 0, _rotated_block(i)))],
        out_shape=[big_shape, big_shape, big_shape, big_shape],
        compiler_params=_cparams(("arbitrary",)),
        name="peer_topk",
    )(q, k1, k2)


GELU_C0 = 0.7978845608028654
GELU_C1 = 0.044715


PEER_PIECE = 256
GATE_KEYS = 64


def _dense_body(h2_ref, u_ref, vt_ref, thr_ref, s2_ref, e1_ref, e2_ref, x1_ref, mod_ref,
                o_ref, acc, gs, *hs):
    j = pl.program_id(1)

    @pl.when(j == 0)
    def _():
        acc[...] = jnp.zeros_like(acc)

    n_pieces = len(hs)
    tn = hs[0].shape[1]
    n_col = tn // 128
    h2 = h2_ref[...]

    def pre_activations(p):
        rows = slice(p * PEER_PIECE, (p + 1) * PEER_PIECE)
        hs[p][...] = _dot_nt(u_ref[rows, :], h2)

    pre_activations(0)
    for p in range(n_pieces):
        if p + 1 < n_pieces:
            pre_activations(p + 1)
        halves = PEER_PIECE // PEER_KEYS
        per_col = PEER_KEYS // GATE_KEYS
        for sub in range(per_col * n_col):
            keys = slice((sub % per_col) * GATE_KEYS, (sub % per_col) * GATE_KEYS + GATE_KEYS)
            c = sub // per_col
            cols = slice(c * 128, c * 128 + 128)
            e2_cols = slice(((c + 1) % n_col) * 128, ((c + 1) % n_col) * 128 + 128)
            n_grp = GATE_KEYS // 8
            w = [[None] * n_grp for _ in range(halves)]
            for h in range(PEER_HEADS):
                thr8 = [jnp.broadcast_to(thr_ref[h, p * halves + half:p * halves + half + 1, cols], (8, 128))
                        for half in range(halves)]
                e18 = [jnp.broadcast_to(e1_ref[h, p * halves + half:p * halves + half + 1, cols], (8, 128))
                       for half in range(halves)]
                for g in range(n_grp):
                    grp = slice(keys.start + 8 * g, keys.start + 8 * g + 8)
                    s2g = s2_ref[h, grp, cols]
                    e2g = e2_ref[h, grp, e2_cols]
                    for half in range(halves):
                        term = jnp.where(s2g >= thr8[half], e18[half] * e2g, 0.0)
                        w[half][g] = term if w[half][g] is None else w[half][g] + term
            for half in range(halves):
                rows = slice(half * PEER_KEYS + keys.start, half * PEER_KEYS + keys.stop)
                x = hs[p][rows, cols]
                inner = x * (GELU_C0 + (GELU_C0 * GELU_C1) * (x * x))
                wh = jnp.concatenate(w[half], axis=0)
                grows = slice(p * PEER_PIECE + rows.start, p * PEER_PIECE + rows.stop)
                gs[grows, cols] = (wh * x * (1.0 + jnp.tanh(inner))).astype(BF16)
    acc[...] += _dot(vt_ref[...], gs[...])

    @pl.when(j == pl.num_programs(1) - 1)
    def _():
        gate2 = mod_ref[:, 5 * D_MODEL:6 * D_MODEL]
        o_ref[...] = x1_ref[...] + gate2 * acc[...].T


def _peer_dense(h2, u_all, vt_all, layer, thr, s2, e1, e2, x1, mod):
    n = h2.shape[0]
    tn, te = PEER_TN, PEER_TE
    rows_per_step = te // PEER_KEYS
    tpg = COND_GROUP // tn
    tok = lambda i, j: (i, 0)
    key_rows = pl.BlockSpec((PEER_HEADS, rows_per_step, tn), lambda i, j: (0, j, i))
    key_all = pl.BlockSpec((PEER_HEADS, PEER_KEYS, tn), lambda i, j: (0, 0, i))
    return pl.pallas_call(
        _dense_body,
        grid=(n // tn, N_EXPERTS // te),
        in_specs=[pl.BlockSpec((tn, D_MODEL), tok),
                  pl.BlockSpec((None, te, D_MODEL), lambda i, j: (layer, j, 0)),
                  pl.BlockSpec((None, D_MODEL, te), lambda i, j: (layer, 0, j)),
                  key_rows, key_all, key_rows, key_all,
                  pl.BlockSpec((tn, D_MODEL), tok),
                  pl.BlockSpec((None, 1, 6 * D_MODEL), lambda i, j: (i // tpg, 0, 0))],
        out_specs=pl.BlockSpec((tn, D_MODEL), tok),
        out_shape=jax.ShapeDtypeStruct((n, D_MODEL), F32),
        scratch_shapes=([pltpu.VMEM((D_MODEL, tn), F32), pltpu.VMEM((te, tn), BF16)]
                        + [pltpu.VMEM((PEER_PIECE, tn), F32)] * (te // PEER_PIECE)),
        compiler_params=_cparams(("arbitrary", "arbitrary")),
        name="peer_dense",
    )(h2, u_all, vt_all, thr, s2, e1, e2, x1, mod)


def _in_proj_columns():
    src = np.full((IN_PAD,), -1, np.int64)

    def put(dst, start, width):
        src[dst:dst + width] = np.arange(start, start + width)

    put(0, 0, 768)
    put(768, 768, 128)
    put(896, 896, 64)
    put(1024, 960, 128)
    b0 = A_W
    put(b0, 1088, 192)
    put(b0 + 256, 1280, 128)
    put(b0 + 384 + 64, 1408, 32)
    put(b0 + 512 + 64, 1408 + 16, 16)
    put(b0 + 512 + 80, 1408, 16)
    c0 = A_W + B_W
    for slot, head in enumerate((0, 2, 1, 3)):
        put(c0 + 64 * slot, 1440 + 64 * head, 64)
    put(c0 + 256, 1696, 256)
    put(A_W + B_W + C_W, 1952, 768)
    return src


def _swap_tail(w):
    return jnp.concatenate([w[..., :64], w[..., 80:96], w[..., 64:80]], axis=-1)


def _pad_last(w, width):
    return jnp.pad(w, [(0, 0)] * (w.ndim - 1) + [(0, width - w.shape[-1])])


STACKED_WEIGHTS = ("w_in", "w_out", "wq", "u", "vt")


def _layer_weights(p):
    depth = p["w_in"].shape[0]
    src = _in_proj_columns()
    runs, start = [], 0
    for i in range(1, IN_PAD + 1):
        if i == IN_PAD or (src[i] != src[i - 1] + 1 if src[i - 1] >= 0 else src[i] >= 0):
            runs.append((start, i))
            start = i
    w_bf = p["w_in"].astype(BF16)
    pieces = [w_bf[:, :, src[a]:src[a] + (b - a)] if src[a] >= 0
              else jnp.zeros((depth, D_MODEL, b - a), BF16) for a, b in runs]
    w_in = jnp.concatenate(pieces, axis=2)

    wuq = p["mla_wuq"].reshape(depth, MLA_Q_RANK, 4, MLA_QK_DIM).transpose(0, 2, 1, 3)
    wq_plain = jnp.pad(wuq, ((0, 0), (0, 0), (0, 256 - MLA_Q_RANK), (0, 128 - MLA_QK_DIM)))
    wq_swap = jnp.pad(_swap_tail(wuq), ((0, 0), (0, 0), (0, 256 - MLA_Q_RANK), (0, 128 - MLA_QK_DIM)))
    wukv = p["mla_wukv"].reshape(depth, 128, 4, 128)
    wk = _pad_last(wukv[..., :64].transpose(0, 2, 1, 3), 128)
    wv_heads = wukv[..., 64:].transpose(0, 2, 1, 3)
    wv = jnp.concatenate([wv_heads[:, 0::2], wv_heads[:, 1::2]], axis=-1)

    def gain2(g):
        return jnp.stack([_pad_last(g, 128), _pad_last(_swap_tail(g), 128)], axis=1)

    w_out = p["w_out"]
    oc = w_out[:, 512:768].reshape(depth, 4, 64, D_MODEL)[:, jnp.asarray([0, 2, 1, 3])].reshape(depth, 256, D_MODEL)
    w_out = jnp.concatenate([w_out[:, :512], oc, w_out[:, 768:]], axis=1).astype(BF16)

    wu_pad = jnp.zeros((depth, 2, 128, GROUP_W), F32)
    wu_pad = wu_pad.at[:, 0, 0:64].set(p["rw_wu"][:, 0]).at[:, 1, 64:128].set(p["rw_wu"][:, 1])
    au_pad = jnp.zeros((depth, 2, 128, GROUP_W), F32)
    au_pad = au_pad.at[:, 0, 0:32].set(p["rw_au"][:, 0]).at[:, 1, 32:64].set(p["rw_au"][:, 1])

    return {
        "w_in": w_in,
        "norm1_g": p["norm1_g"][:, None, :],
        "norm2_g": p["norm2_g"][:, None, :],
        "w_out": w_out,
        "w0": p["rw_w0"], "a0": p["rw_a0"],
        "wu": wu_pad.astype(BF16), "au": au_pad.astype(BF16),
        "kkp": p["rw_kk"][:, None, :], "ka": p["rw_ka"][:, None, :],
        "gu": p["rw_gu"].astype(BF16),
        "rk": p["rw_rk"].reshape(depth, 1, GROUP_W),
        "gn": p["rw_gn"][:, None, :],
        "qa_g": _pad_last(p["mla_qa_g"], 256)[:, None, :],
        "wqq": jnp.concatenate([wq_plain, wq_swap], axis=1).astype(BF16),
        "qn_g2": gain2(p["mla_qn_g"]),
        "kva_g": p["mla_kva_g"][:, None, :],
        "wk": wk.astype(BF16), "wv": jnp.swapaxes(wv, -1, -2).astype(BF16),
        "kn_g2": gain2(p["mla_kn_g"]),
        "gq_g": jnp.tile(p["gqa_qn_g"], (1, 2))[:, None, :],
        "gk_g": jnp.tile(p["gqa_kn_g"], (1, 2))[:, None, :],
        "conv_w": p["conv_w"], "conv_b": p["conv_b"][:, None, :],
        "wq": p["peer_wq"].astype(BF16),
        "k1": p["peer_k1"].astype(BF16), "k2": p["peer_k2"].astype(BF16),
        "u": p["peer_u"].astype(BF16),
        "vt": jnp.swapaxes(p["peer_v"], 1, 2).astype(BF16),
    }


def _rope_tables():
    t = jnp.arange(LAT_LEN, dtype=F32)
    grid_row = jnp.floor(t / GRID_W)
    grid_col = t - grid_row * GRID_W

    def angles(rot_dim):
        n_freq = rot_dim // 4
        freqs = ROPE_THETA ** (-jnp.arange(n_freq, dtype=F32) / n_freq)
        ang = jnp.concatenate([grid_row[:, None] * freqs, grid_col[:, None] * freqs], axis=-1)
        return jnp.cos(ang), jnp.sin(ang)

    cm, sm = angles(32)
    ones64 = jnp.ones((LAT_LEN, 64), F32)
    zeros64 = jnp.zeros((LAT_LEN, 64), F32)
    cosm = jnp.concatenate([ones64, cm, cm, ones64[:, :32]], axis=-1)
    sinm = jnp.concatenate([zeros64, -sm, sm, zeros64[:, :32]], axis=-1)
    cg, sg = angles(64)
    cosg = jnp.tile(jnp.concatenate([cg, cg], axis=-1), (1, 2))
    sing = jnp.tile(jnp.concatenate([-sg, sg], axis=-1), (1, 2))
    ident = jnp.ones((TILE, 128), F32)
    zero = jnp.zeros((TILE, 128), F32)
    return {"cosm": jnp.concatenate([ident, cosm]), "sinm": jnp.concatenate([zero, sinm]),
            "cosg": jnp.concatenate([ident, cosg]), "sing": jnp.concatenate([zero, sing])}


def _states_to_pairs(s):
    lead = s.shape[:-3]
    s = s.reshape(lead + (2, 2, 64, 64))
    z = jnp.zeros(lead + (2, 64, 64), s.dtype)
    top = jnp.concatenate([s[..., 0, :, :], z], axis=-1)
    bot = jnp.concatenate([z, s[..., 1, :, :]], axis=-1)
    return jnp.concatenate([top, bot], axis=-2)


def _pairs_to_states(sp):
    lead = sp.shape[:-3]
    a = sp[..., 0:64, 0:64]
    b = sp[..., 64:128, 64:128]
    return jnp.stack([a, b], axis=-3).reshape(lead + (4, 64, 64))


def _conv_halos(d_grp):
    n_tiles = d_grp.shape[0] // TILE
    tiles = d_grp.reshape(n_tiles, TILE, D_W)
    first_rows = tiles[:, 0, :]
    last_rows = tiles[:, TILE - 1, :]
    zero = jnp.zeros((1, D_W), d_grp.dtype)
    prev = jnp.concatenate([zero, last_rows[:-1]], axis=0)
    nxt = jnp.concatenate([first_rows[1:], zero], axis=0)
    idx = np.arange(n_tiles)
    n_ctx_tiles = N_CTX_TOK // TILE
    per_seq = LAT_LEN // TILE
    lat_pos = (idx - n_ctx_tiles) % per_seq
    seq_start = np.where(idx < n_ctx_tiles, True, lat_pos == 0)
    seq_end = np.where(idx < n_ctx_tiles, True, lat_pos == per_seq - 1)
    prev = jnp.where(jnp.asarray(seq_start)[:, None], 0.0, prev)
    nxt = jnp.where(jnp.asarray(seq_end)[:, None], 0.0, nxt)
    return prev[:, None, :], nxt[:, None, :]


def kernel(x_prompt, x_sample, state_rwkv, cache_mla_ckv, cache_mla_krope, cache_gqa_k, cache_gqa_v, c, c_ctx, norm1_g, norm2_g, w_mod, b_mod, w_in, w_out, rw_w0, rw_wu, rw_a0, rw_au, rw_gu, rw_kk, rw_ka, rw_rk, rw_gn, mla_qa_g, mla_wuq, mla_kva_g, mla_wukv, mla_qn_g, mla_kn_g, gqa_qn_g, gqa_kn_g, conv_w, conv_b, peer_wq, peer_k1, peer_k2, peer_u, peer_v):
    depth = w_in.shape[0]
    params = dict(norm1_g=norm1_g, norm2_g=norm2_g, w_in=w_in, w_out=w_out, rw_w0=rw_w0, rw_wu=rw_wu,
                  rw_a0=rw_a0, rw_au=rw_au, rw_gu=rw_gu, rw_kk=rw_kk, rw_ka=rw_ka, rw_rk=rw_rk, rw_gn=rw_gn,
                  mla_qa_g=mla_qa_g, mla_wuq=mla_wuq, mla_kva_g=mla_kva_g, mla_wukv=mla_wukv,
                  mla_qn_g=mla_qn_g, mla_kn_g=mla_kn_g, gqa_qn_g=gqa_qn_g, gqa_kn_g=gqa_kn_g,
                  conv_w=conv_w, conv_b=conv_b, peer_wq=peer_wq, peer_k1=peer_k1, peer_k2=peer_k2,
                  peer_u=peer_u, peer_v=peer_v)
    lw_all = _layer_weights(params)
    tabs = _rope_tables()

    cvecs = jnp.concatenate([c_ctx[None, :], c, jnp.zeros((8 - 1 - N_LAT_SEQ, D_MODEL), F32)], axis=0)
    mods = _modulation(cvecs, w_mod, b_mod).reshape(depth, 8, 1, 6 * D_MODEL)

    x = jnp.concatenate([x_prompt.reshape(N_CTX_TOK, D_MODEL), x_sample.reshape(N_LAT_TOK, D_MODEL)], axis=0)

    s0_lat = _states_to_pairs(state_rwkv)
    s0_all = jnp.concatenate([jnp.zeros((N_CTX_SEQ,) + s0_lat.shape[1:], F32), s0_lat], axis=0)

    krope_placed = jnp.pad(cache_mla_krope, ((0, 0), (0, 0), (0, 0), (64, 32)))
    kctx_m, vctx_m = _mla_ctx_kv(cache_mla_ckv, krope_placed, lw_all["wk"], lw_all["wv"], lw_all["kn_g2"])
    past = cache_gqa_k.shape[2]
    kctx_g = cache_gqa_k.reshape(N_LAT_SEQ, depth, past, 128)
    vctx_g = jnp.swapaxes(cache_gqa_v.reshape(N_LAT_SEQ, depth, past, 128), -1, -2)

    ident = lambda p: p
    st_a, st_ckv, st_kr, st_k, st_v = [], [], [], [], []
    for l in range(depth):
        lw = {name: w[l] for name, w in lw_all.items() if name not in STACKED_WEIGHTS}
        mod = mods[l]
        a_grp, b_grp, c_grp, d_grp = _in_proj(x, mod, lw["norm1_g"], lw_all["w_in"], l)

        yf, yb, s_fin = _rwkv_scan(a_grp, s0_all[:, l], lw["w0"], lw["wu"], lw["a0"], lw["au"], lw["kkp"], lw["ka"])

        qm, km, vm, ckvn, qg, kg, vg, kgn = _attn_prep(b_grp, c_grp, tabs, lw)
        ob_ctx = _attention(qm, km, km, vm, lambda p: 2 * p, lambda p: 2 * p + 1, lambda p: 2 * p,
                            lambda p: 2 * p + 1, ident, N_CTX_SEQ, CTX_LEN, 0)
        ob_lat = _attention(qm, km, km, vm, lambda p: 2 * p, lambda p: 2 * p + 1, lambda p: 2 * p,
                            lambda p: 2 * p + 1, ident, N_LAT_SEQ, LAT_LEN, N_CTX_TOK,
                            ctx=(kctx_m[l], kctx_m[l], vctx_m[l]))
        oc_ctx = _attention(qg, kg, kg, vg, ident, lambda p: p + 2, ident, ident, ident,
                            N_CTX_SEQ, CTX_LEN, 0)
        oc_lat = _attention(qg, kg, kg, vg, ident, lambda p: p + 2, ident, ident, ident,
                            N_LAT_SEQ, LAT_LEN, N_CTX_TOK,
                            ctx=(kctx_g[:, l], kctx_g[:, l], vctx_g[:, l]))

        d_prev, d_next = _conv_halos(d_grp)
        x1, h2, q = _out_proj(x, mod, a_grp, yf, yb, ob_ctx, ob_lat, oc_ctx, oc_lat, d_grp, d_prev, d_next, lw,
                              lw_all["w_out"], lw_all["wq"], l)

        thr, s2, e1, e2 = _peer_topk(q, lw["k1"], lw["k2"])
        x = _peer_dense(h2, lw_all["u"], lw_all["vt"], l, thr, s2, e1, e2, x1, mod)

        st_a.append(_pairs_to_states(s_fin[:N_CTX_SEQ]))
        st_ckv.append(ckvn[:N_CTX_TOK].reshape(N_CTX_SEQ, CTX_LEN, 128))
        st_kr.append(b_grp[:N_CTX_TOK, 448:480].reshape(N_CTX_SEQ, CTX_LEN, 32))
        st_k.append(kgn[:N_CTX_TOK].reshape(N_CTX_SEQ, CTX_LEN, 2, HEAD_DIM))
        st_v.append(c_grp[:N_CTX_TOK, 384:512].reshape(N_CTX_SEQ, CTX_LEN, 2, HEAD_DIM))

    y_prompt = x[:N_CTX_TOK].reshape(N_CTX_SEQ, CTX_LEN, D_MODEL)
    y_sample = x[N_CTX_TOK:].reshape(N_LAT_SEQ, LAT_LEN, D_MODEL)
    return (y_prompt, y_sample, jnp.stack(st_a, axis=1), jnp.stack(st_ckv, axis=1), jnp.stack(st_kr, axis=1),
            jnp.stack(st_k, axis=1), jnp.stack(st_v, axis=1))
```

```python
import functools

import numpy as np
import jax
import jax.numpy as jnp
from jax import lax
from jax.experimental import pallas as pl
from jax.experimental.pallas import tpu as pltpu

F32 = jnp.float32
BF16 = jnp.bfloat16

D_MODEL = 1024
N_CTX_SEQ = 16
CTX_LEN = 256
N_LAT_SEQ = 2
LAT_LEN = 4096
GRID_W = 64
ROPE_THETA = 10000.0
EPS = 1e-6
GROUP_W = 256
HEAD_DIM = 64
RW_DECAY_SCALE = 0.6065306597
MLA_QK_DIM = 96
MLA_Q_RANK = 192
PEER_HEADS = 8
PEER_KEYS = 128
PEER_TOPK = 16
N_EXPERTS = PEER_KEYS * PEER_KEYS

N_CTX_TOK = N_CTX_SEQ * CTX_LEN
N_LAT_TOK = N_LAT_SEQ * LAT_LEN
N_TOK = N_CTX_TOK + N_LAT_TOK
COND_GROUP = 4096

A_W, B_W, C_W, D_W = 1152, 640, 512, 768
IN_PAD = A_W + B_W + C_W + D_W

SCAN_CHUNK = 128
TILE = 256
IN_TILE = 512
PEER_TN = 512
PEER_TE = 2048
TOPK_TN = 128
ATT_TQ = 512
ATT_CHUNK = 1024

VMEM_LIMIT = 56 * 1024 * 1024


def _cparams(sem):
    return pltpu.CompilerParams(dimension_semantics=sem, vmem_limit_bytes=VMEM_LIMIT)


def _dot(a, b, precision=None):
    return jnp.dot(a, b, preferred_element_type=F32, precision=precision)


def _dot_nt(a, b, precision=None):
    return lax.dot_general(a, b, (((1,), (1,)), ((), ())), preferred_element_type=F32,
                           precision=precision)


def _dot_tn(a, b, precision=None):
    return lax.dot_general(a, b, (((0,), (0,)), ((), ())), preferred_element_type=F32,
                           precision=precision)


def _rms_rows(x, g):
    return x * lax.rsqrt(jnp.mean(x * x, axis=-1, keepdims=True) + EPS) * g


def _group_sums(x, group_mat):
    hi = x.astype(BF16)
    lo = (x - hi.astype(F32)).astype(BF16)
    gm = group_mat.astype(BF16)
    return _dot(hi, gm) + _dot(lo, gm)


def _group_matrix(n, group, value):
    r = lax.broadcasted_iota(jnp.int32, (n, n), 0) // group
    c = lax.broadcasted_iota(jnp.int32, (n, n), 1) // group
    return jnp.where(r == c, value, 0.0).astype(F32)


def _mod_body(c_ref, w_ref, b_ref, o_ref):
    c = c_ref[...]
    s = c * jax.nn.sigmoid(c)
    o_ref[...] = _dot(s.astype(BF16), w_ref[...].astype(BF16)) + b_ref[...]


def _modulation(cvecs, w_mod, b_mod):
    depth = w_mod.shape[0]
    tn = 1536
    return pl.pallas_call(
        _mod_body,
        grid=(depth, 6 * D_MODEL // tn),
        in_specs=[pl.BlockSpec((8, D_MODEL), lambda l, j: (0, 0)),
                  pl.BlockSpec((None, D_MODEL, tn), lambda l, j: (l, 0, j)),
                  pl.BlockSpec((None, 1, tn), lambda l, j: (l, 0, j))],
        out_specs=pl.BlockSpec((None, 8, tn), lambda l, j: (l, 0, j)),
        out_shape=jax.ShapeDtypeStruct((depth, 8, 6 * D_MODEL), F32),
        compiler_params=_cparams(("arbitrary", "arbitrary")),
        name="adaln_mod",
    )(cvecs, w_mod, b_mod.reshape(depth, 1, 6 * D_MODEL))


def _in_body(x_ref, mod_ref, g_ref, w_ref, *rest):
    prep_in, (a_ref, b_ref, c_ref, d_ref), prep_out = rest[:13], rest[13:17], rest[17:]
    mod = mod_ref[...]
    shift = mod[:, 0:D_MODEL]
    scale = mod[:, D_MODEL:2 * D_MODEL]
    h = _rms_rows(x_ref[...], g_ref[...]) * (1.0 + scale) + shift
    y = _dot(h.astype(BF16), w_ref[...])
    a_ref[...] = y[:, 0:A_W]
    b_ref[...] = y[:, A_W:A_W + B_W]
    c_ref[...] = y[:, A_W + B_W:A_W + B_W + C_W]
    d_ref[...] = y[:, A_W + B_W + C_W:IN_PAD]
    _prep_body(b_ref, c_ref, *prep_in, *prep_out)


def _in_proj(x, mod, norm_g, w_pad_all, layer, tabs, lw):
    n = x.shape[0]
    tpg = COND_GROUP // IN_TILE
    n_ctx_tiles = N_CTX_TOK // IN_TILE
    lat_tiles = LAT_LEN // IN_TILE
    row = lambda i: (i, 0)
    hrow = lambda i: (0, i, 0)
    tab = lambda i: (jnp.where(i < n_ctx_tiles, 0, 1 + (i - n_ctx_tiles) % lat_tiles), 0)
    c2 = lambda i: (0, 0)
    c3 = lambda i: (0, 0, 0)
    return pl.pallas_call(
        _in_body,
        grid=(n // IN_TILE,),
        in_specs=[pl.BlockSpec((IN_TILE, D_MODEL), row),
                  pl.BlockSpec((None, 1, 6 * D_MODEL), lambda i: (i // tpg, 0, 0)),
                  pl.BlockSpec((1, D_MODEL), c2),
                  pl.BlockSpec((None, D_MODEL, IN_PAD), lambda i: (layer, 0, 0)),
                  pl.BlockSpec((IN_TILE, 128), tab), pl.BlockSpec((IN_TILE, 128), tab),
                  pl.BlockSpec((IN_TILE, 128), tab), pl.BlockSpec((IN_TILE, 128), tab),
                  pl.BlockSpec((1, 256), c2), pl.BlockSpec((8, 256, 128), c3), pl.BlockSpec((2, 128), c2),
                  pl.BlockSpec((1, 128), c2), pl.BlockSpec((4, 128, 128), c3), pl.BlockSpec((2, 128, 128), c3),
                  pl.BlockSpec((2, 128), c2), pl.BlockSpec((1, 128), c2), pl.BlockSpec((1, 128), c2)],
        out_specs=[pl.BlockSpec((IN_TILE, A_W), row), pl.BlockSpec((IN_TILE, B_W), row),
                   pl.BlockSpec((IN_TILE, C_W), row), pl.BlockSpec((IN_TILE, D_W), row),
                   pl.BlockSpec((4, IN_TILE, 128), hrow), pl.BlockSpec((4, IN_TILE, 128), hrow),
                   pl.BlockSpec((2, 128, IN_TILE), lambda i: (0, 0, i)), pl.BlockSpec((IN_TILE, 128), row),
                   pl.BlockSpec((4, IN_TILE, 128), hrow), pl.BlockSpec((IN_TILE, 128), row),
                   pl.BlockSpec((128, IN_TILE), lambda i: (0, i)), pl.BlockSpec((IN_TILE, 128), row)],
        out_shape=[jax.ShapeDtypeStruct((n, A_W), F32), jax.ShapeDtypeStruct((n, B_W), F32),
                   jax.ShapeDtypeStruct((n, C_W), F32), jax.ShapeDtypeStruct((n, D_W), F32),
                   jax.ShapeDtypeStruct((4, n, 128), BF16), jax.ShapeDtypeStruct((4, n, 128), BF16),
                   jax.ShapeDtypeStruct((2, 128, n), BF16), jax.ShapeDtypeStruct((n, 128), F32),
                   jax.ShapeDtypeStruct((4, n, 128), BF16), jax.ShapeDtypeStruct((n, 128), BF16),
                   jax.ShapeDtypeStruct((128, n), BF16), jax.ShapeDtypeStruct((n, 128), F32)],
        compiler_params=_cparams(("arbitrary",)),
        name="in_proj",
    )(x, mod, norm_g, w_pad_all, tabs["cosm"], tabs["sinm"], tabs["cosg"], tabs["sing"],
      lw["qa_g"], lw["wqq"], lw["qn_g2"], lw["kva_g"], lw["wk"], lw["wv"], lw["kn_g2"], lw["gq_g"], lw["gk_g"])


def _chunk_cumsum(x, reverse):
    c = x.shape[0]
    rows = lax.broadcasted_iota(jnp.int32, x.shape, 0)
    sh = 1
    while sh < c:
        if reverse:
            x = x + jnp.where(rows < c - sh, pltpu.roll(x, c - sh, 0), 0.0)
        else:
            x = x + jnp.where(rows >= sh, pltpu.roll(x, sh, 0), 0.0)
        sh *= 2
    return x


def _scan_body(fb_ref, bb_ref, first_ref, last_ref, seq_ref,
               af_ref, ab_ref, s0_ref, w0_ref, wu_ref, a0_ref, au_ref, kkp_ref, ka_ref,
               yf_ref, yb_ref, sfin_ref, s_scr):
    step = pl.program_id(0)

    @pl.when(first_ref[step] == 1)
    def _():
        s_scr[...] = s0_ref[...]

    c = SCAN_CHUNK
    row = lax.broadcasted_iota(jnp.int32, (c, c), 0)
    col = lax.broadcasted_iota(jnp.int32, (c, c), 1)
    lane_c = lax.broadcasted_iota(jnp.int32, (c, 128), 1) < HEAD_DIM
    lane_2c = lax.broadcasted_iota(jnp.int32, (2 * c, 128), 1) < HEAD_DIM
    same_head = _group_matrix(128, HEAD_DIM, 1.0)
    kkp = kkp_ref[...]
    ka = ka_ref[...]

    def per_head(lo, hi):
        return jnp.where(lane_c, lo, hi)

    groups = []
    for d in range(2):
        x_ref = af_ref if d == 0 else ab_ref
        if d == 0:
            strict, incl = col < row, col <= row
        else:
            strict, incl = col > row, col >= row
        r = x_ref[:, 0:256]
        k = x_ref[:, 256:512]
        v = x_ref[:, 512:768]
        wd = x_ref[:, 768:896]
        ad = x_ref[:, 896:1024]
        wlog = -RW_DECAY_SCALE * jax.nn.sigmoid(
            w0_ref[d:d + 1, :] + _dot(jnp.tanh(wd).astype(BF16), wu_ref[d]))
        a = jax.nn.sigmoid(a0_ref[d:d + 1, :] + _dot(ad.astype(BF16), au_ref[d]))
        kd = k * (1.0 + (a - 1.0) * ka)
        kkf = k * kkp
        cum = _chunk_cumsum(wlog, reverse=(d == 1))
        tot = cum[c - 1:c, :] if d == 0 else cum[0:1, :]
        p_inc = jnp.exp(cum)
        p_exc = jnp.exp(cum - wlog)
        p_inv = jnp.exp(-cum)
        p_tot = jnp.exp(tot)
        for p in range(2):
            sl = slice(128 * p, 128 * p + 128)
            groups.append(dict(d=d, p=p, sl=sl, strict=strict, incl=incl, kk_raw=kkf[:, sl],
                               a=a[:, sl], kd=kd[:, sl], r=r[:, sl], vb=v[:, sl].astype(BF16),
                               p_inc=p_inc[:, sl], p_exc=p_exc[:, sl], p_inv=p_inv[:, sl], pt=p_tot[:, sl]))

    for g in groups:
        g["ss"] = _group_sums(g["kk_raw"] * g["kk_raw"], same_head)
    for g in groups:
        kk = g["kk_raw"] * lax.rsqrt(g["ss"] + EPS)
        g["a_p"] = (-kk * g["p_exc"]).astype(BF16)
        g["r_p"] = (g["r"] * g["p_inc"]).astype(BF16)
        g["b_i"] = kk * g["a"] * g["p_inv"]
        g["k_i"] = g["kd"] * g["p_inv"]
        g["lhs"] = jnp.concatenate([g["a_p"], g["r_p"]], axis=0)
        g["rhs"] = jnp.concatenate([g["b_i"], g["k_i"]], axis=0).astype(BF16)
        g["sb"] = s_scr[g["d"], g["p"]].astype(BF16)
    zero_b = jnp.zeros((), BF16)
    for g in groups:
        g["gram"] = (_dot_nt(jnp.where(lane_2c, g["lhs"], zero_b), g["rhs"]),
                     _dot_nt(jnp.where(lane_2c, zero_b, g["lhs"]), g["rhs"]))
        g["a_s"] = _dot_nt(g["a_p"], g["sb"])
        g["r_s"] = _dot_nt(g["r_p"], g["sb"])
    for g in groups:
        st, inc = g["strict"], g["incl"]
        g["pow"] = [jnp.where(st, gm[0:c, 0:c], 0.0).astype(BF16) for gm in g["gram"]]
        g["dm"] = [jnp.where(st, gm[0:c, c:2 * c], 0.0).astype(BF16) for gm in g["gram"]]
        g["et"] = [jnp.where(inc, gm[c:2 * c, 0:c], 0.0).astype(BF16) for gm in g["gram"]]
        g["ft"] = [jnp.where(inc, gm[c:2 * c, c:2 * c], 0.0).astype(BF16) for gm in g["gram"]]
    for g in groups:
        g["z"] = g["a_s"] + per_head(_dot(g["dm"][0], g["vb"]), _dot(g["dm"][1], g["vb"]))
    span = 1
    while span < c:
        for g in groups:
            zb = g["z"].astype(BF16)
            g["z"] = g["z"] + per_head(_dot(g["pow"][0], zb), _dot(g["pow"][1], zb))
        span *= 2
        if span < c:
            for g in groups:
                g["pow"] = [_dot(m, m).astype(BF16) for m in g["pow"]]
    for g in groups:
        zb = g["z"].astype(BF16)
        g["zb"] = zb
        y = g["r_s"] + per_head(_dot(g["et"][0], zb) + _dot(g["ft"][0], g["vb"]),
                                _dot(g["et"][1], zb) + _dot(g["ft"][1], g["vb"]))
        y_ref = yf_ref if g["d"] == 0 else yb_ref
        y_ref[:, g["sl"]] = y
    for g in groups:
        pt = g["pt"]
        upd = (_dot_tn(g["zb"], (g["b_i"] * pt).astype(BF16))
               + _dot_tn(g["vb"], (g["k_i"] * pt).astype(BF16)))
        s_scr[g["d"], g["p"]] = s_scr[g["d"], g["p"]] * pt + same_head * upd

    @pl.when(last_ref[step] == 1)
    def _():
        sfin_ref[...] = s_scr[...]


def _scan_tables():
    c = SCAN_CHUNK
    fb, bb, first, last, seq = [], [], [], [], []
    base = 0
    sid = 0
    for nseq, length in ((N_CTX_SEQ, CTX_LEN), (N_LAT_SEQ, LAT_LEN)):
        nc = length // c
        for _ in range(nseq):
            for j in range(nc):
                fb.append(base + j)
                bb.append(base + nc - 1 - j)
                first.append(int(j == 0))
                last.append(int(j == nc - 1))
                seq.append(sid)
            base += nc
            sid += 1
    return tuple(np.asarray(t, np.int32) for t in (fb, bb, first, last, seq))


def _rwkv_scan(a_grp, s0_pairs, w0, wu_pad, a0, au_pad, kkp, ka):
    tables = _scan_tables()
    n_steps = tables[0].shape[0]
    n_seq = s0_pairs.shape[0]
    n = a_grp.shape[0]
    c = SCAN_CHUNK
    const2 = lambda s, fb, bb, fi, la, sq: (0, 0)
    const3 = lambda s, fb, bb, fi, la, sq: (0, 0, 0)
    grid_spec = pltpu.PrefetchScalarGridSpec(
        num_scalar_prefetch=5,
        grid=(n_steps,),
        in_specs=[pl.BlockSpec((c, A_W), lambda s, fb, bb, fi, la, sq: (fb[s], 0)),
                  pl.BlockSpec((c, A_W), lambda s, fb, bb, fi, la, sq: (bb[s], 0)),
                  pl.BlockSpec((None, 2, 2, 128, 128), lambda s, fb, bb, fi, la, sq: (sq[s], 0, 0, 0, 0)),
                  pl.BlockSpec((2, GROUP_W), const2),
                  pl.BlockSpec((2, 128, GROUP_W), const3),
                  pl.BlockSpec((2, GROUP_W), const2),
                  pl.BlockSpec((2, 128, GROUP_W), const3),
                  pl.BlockSpec((1, GROUP_W), const2),
                  pl.BlockSpec((1, GROUP_W), const2)],
        out_specs=[pl.BlockSpec((c, GROUP_W), lambda s, fb, bb, fi, la, sq: (fb[s], 0)),
                   pl.BlockSpec((c, GROUP_W), lambda s, fb, bb, fi, la, sq: (bb[s], 0)),
                   pl.BlockSpec((None, 2, 2, 128, 128), lambda s, fb, bb, fi, la, sq: (sq[s], 0, 0, 0, 0))],
        scratch_shapes=[pltpu.VMEM((2, 2, 128, 128), F32)],
    )
    return pl.pallas_call(
        _scan_body,
        grid_spec=grid_spec,
        out_shape=[jax.ShapeDtypeStruct((n, GROUP_W), F32), jax.ShapeDtypeStruct((n, GROUP_W), F32),
                   jax.ShapeDtypeStruct((n_seq, 2, 2, 128, 128), F32)],
        compiler_params=_cparams(("arbitrary",)),
        name="rwkv_scan",
    )(*[jnp.asarray(t) for t in tables], a_grp, a_grp, s0_pairs, w0, wu_pad, a0, au_pad, kkp, ka)


LOG2_E = 1.4426950408889634
MLA_SCALE = MLA_QK_DIM ** -0.5 * LOG2_E
GQA_SCALE = HEAD_DIM ** -0.5 * LOG2_E


def _mla_keys(ckv_b, rope_slot, rope_slot_sw, wk_ref, kng_ref, cosm, sinm, k_out_ref):
    g = kng_ref[0:1, :]
    g_sw = kng_ref[1:2, :]
    for h in range(4):
        nope = _dot(ckv_b, wk_ref[h])
        kr = nope + rope_slot
        rs = lax.rsqrt(jnp.sum(kr * kr, axis=-1, keepdims=True) * (1.0 / MLA_QK_DIM) + EPS)
        if cosm is None:
            k_out_ref[h] = (kr * rs * g).astype(BF16)
        else:
            ks = nope + rope_slot_sw
            k_out_ref[h] = ((kr * rs * g) * cosm + (ks * rs * g_sw) * sinm).astype(BF16)


def _prep_body(b_ref, c_ref, cosm_ref, sinm_ref, cosg_ref, sing_ref,
               qag_ref, wqq_ref, qng_ref, kvag_ref, wk_ref, wv_ref, kng_ref, gqg_ref, gkg_ref,
               qm_ref, km_ref, vm_ref, ckvn_ref, qg_ref, kg_ref, vg_ref, kgn_ref):
    cosm = cosm_ref[...]
    sinm = sinm_ref[...]
    qc = b_ref[:, 0:256]
    qn = qc * lax.rsqrt(jnp.sum(qc * qc, axis=-1, keepdims=True) * (1.0 / MLA_Q_RANK) + EPS) * qag_ref[...]
    qnb = qn.astype(BF16)
    g = qng_ref[0:1, :]
    g_sw = qng_ref[1:2, :]
    for h in range(4):
        qr = _dot(qnb, wqq_ref[h])
        qs = _dot(qnb, wqq_ref[4 + h])
        rs = lax.rsqrt(jnp.sum(qr * qr, axis=-1, keepdims=True) * (1.0 / MLA_QK_DIM) + EPS)
        qm_ref[h] = (((qr * rs * g) * cosm + (qs * rs * g_sw) * sinm) * MLA_SCALE).astype(BF16)
    ckv = _rms_rows(b_ref[:, 256:384], kvag_ref[...])
    ckvn_ref[...] = ckv
    cb = ckv.astype(BF16)
    _mla_keys(cb, b_ref[:, 384:512], b_ref[:, 512:640], wk_ref, kng_ref, cosm, sinm, km_ref)
    for p in range(2):
        vm_ref[p] = _dot_nt(wv_ref[p], cb).astype(BF16)

    tm = c_ref.shape[0]
    avg = _group_matrix(128, HEAD_DIM, 1.0 / HEAD_DIM)
    lane = lax.broadcasted_iota(jnp.int32, (tm, 128), 1)
    first_half = (lane % HEAD_DIM) < (HEAD_DIM // 2)
    low = lane < HEAD_DIM
    cosg = cosg_ref[...]
    sing = sing_ref[...]

    def rotate(xn):
        swapped = jnp.where(first_half, pltpu.roll(xn, 128 - HEAD_DIM // 2, 1), pltpu.roll(xn, HEAD_DIM // 2, 1))
        return xn * cosg + swapped * sing

    for blk in range(2):
        x = c_ref[:, 128 * blk:128 * blk + 128]
        xn = x * lax.rsqrt(_group_sums(x * x, avg) + EPS) * gqg_ref[...]
        xr = rotate(xn) * GQA_SCALE
        qg_ref[blk] = jnp.where(low, xr, 0.0).astype(BF16)
        qg_ref[blk + 2] = jnp.where(low, 0.0, xr).astype(BF16)
    xk = c_ref[:, 256:384]
    kn = xk * lax.rsqrt(_group_sums(xk * xk, avg) + EPS) * gkg_ref[...]
    kgn_ref[...] = kn
    kg_ref[...] = rotate(kn).astype(BF16)
    vg_ref[...] = c_ref[:, 384:512].T.astype(BF16)


def _ctxkv_body(ckv_ref, krp_ref, wk_ref, wv_ref, kng_ref, k_ref, v_ref):
    cb = ckv_ref[...].astype(BF16)
    _mla_keys(cb, krp_ref[...], None, wk_ref, kng_ref, None, None, k_ref)
    for p in range(2):
        v_ref[p] = _dot_nt(wv_ref[p], cb).astype(BF16)


def _mla_ctx_kv(cache_ckv, cache_krope_placed, wk, wv, kn_g2):
    nb, depth, plen, _ = cache_ckv.shape
    return pl.pallas_call(
        _ctxkv_body,
        grid=(depth, nb),
        in_specs=[pl.BlockSpec((None, None, plen, 128), lambda l, b: (b, l, 0, 0)),
                  pl.BlockSpec((None, None, plen, 128), lambda l, b: (b, l, 0, 0)),
                  pl.BlockSpec((None, 4, 128, 128), lambda l, b: (l, 0, 0, 0)),
                  pl.BlockSpec((None, 2, 128, 128), lambda l, b: (l, 0, 0, 0)),
                  pl.BlockSpec((None, 2, 128), lambda l, b: (l, 0, 0))],
        out_specs=[pl.BlockSpec((None, 4, None, plen, 128), lambda l, b: (l, 0, b, 0, 0)),
                   pl.BlockSpec((None, 2, None, 128, plen), lambda l, b: (l, 0, b, 0, 0))],
        out_shape=[jax.ShapeDtypeStruct((depth, 4, nb, plen, 128), BF16),
                   jax.ShapeDtypeStruct((depth, 2, nb, 128, plen), BF16)],
        compiler_params=_cparams(("arbitrary", "arbitrary")),
        name="mla_ctx_kv",
    )(cache_ckv, cache_krope_placed, wk, wv, kn_g2)


def _attn_body(*refs, has_ctx):
    if has_ctx:
        qa_ref, qb_ref, ka_ref, kb_ref, vt_ref, kca_ref, kcb_ref, vct_ref, o_ref, sa_scr, sb_scr = refs
    else:
        qa_ref, qb_ref, ka_ref, kb_ref, vt_ref, o_ref, sa_scr, sb_scr = refs
        kca_ref = kcb_ref = vct_ref = None
    tk = ka_ref.shape[0]
    chunk = min(ATT_CHUNK, tk)
    q = (qa_ref[...], qb_ref[...])
    k_refs = (ka_ref, kb_ref)
    kc_refs = (kca_ref, kcb_ref)
    s_scr = (sa_scr, sb_scr)
    vrows = (slice(0, HEAD_DIM), slice(HEAD_DIM, 2 * HEAD_DIM))
    spans = [("new", c * chunk, chunk) for c in range(tk // chunk)]
    if has_ctx:
        spans = [("ctx", 0, kca_ref.shape[0])] + spans

    def scratch_rows(kind, start, size):
        base = tk if kind == "ctx" else 0
        return slice(base + start, base + start + size)

    m = [None, None]
    den = [None, None]
    acc = [None, None]

    def score_pass(hd, span):
        kind, start, size = span
        if kind == "ctx":
            keys = kc_refs[hd][...].astype(BF16)
        else:
            keys = k_refs[hd][start:start + size, :]
        s = _dot_nt(keys, q[hd])
        s_scr[hd][scratch_rows(kind, start, size), :] = s
        cm = jnp.max(s, axis=0, keepdims=True)
        m[hd] = cm if m[hd] is None else jnp.maximum(m[hd], cm)

    def value_pass(hd, span):
        kind, start, size = span
        e = jnp.exp2(s_scr[hd][scratch_rows(kind, start, size), :] - m[hd])
        if kind == "ctx":
            vals = vct_ref[vrows[hd], :].astype(BF16)
        else:
            vals = vt_ref[vrows[hd], start:start + size]
        part = _dot(vals, e.astype(BF16))
        rs = jnp.sum(e, axis=0, keepdims=True)
        den[hd] = rs if den[hd] is None else den[hd] + rs
        acc[hd] = part if acc[hd] is None else acc[hd] + part

    for span in spans:
        for hd in range(2):
            score_pass(hd, span)
    for span in spans:
        for hd in range(2):
            value_pass(hd, span)
    out_t = jnp.concatenate([acc[0] / den[0], acc[1] / den[1]], axis=0)
    o_ref[...] = out_t.T.astype(o_ref.dtype)


def _attention(q, ka, kb, v, head_a, head_b, k_head_a, k_head_b, v_idx, n_seq, seq_len, tok0, ctx=None):
    tq = min(ATT_TQ, seq_len)
    nq = seq_len // tq
    seq0 = tok0 // seq_len
    q0 = tok0 // tq

    def qmap(hsel):
        return lambda b, p, i: (hsel(p), q0 + b * nq + i, 0)

    def kmap(hsel, arr):
        if arr.ndim == 3:
            return lambda b, p, i: (hsel(p), seq0 + b, 0)
        return lambda b, p, i: (seq0 + b, 0)

    def kspec(arr, hsel):
        if arr.ndim == 3:
            return pl.BlockSpec((None, seq_len, 128), kmap(hsel, arr))
        return pl.BlockSpec((seq_len, 128), kmap(hsel, arr))

    if v.ndim == 3:
        vspec = pl.BlockSpec((None, 128, seq_len), lambda b, p, i: (v_idx(p), 0, seq0 + b))
    else:
        vspec = pl.BlockSpec((128, seq_len), lambda b, p, i: (0, seq0 + b))
    in_specs = [pl.BlockSpec((None, tq, 128), qmap(head_a)), pl.BlockSpec((None, tq, 128), qmap(head_b)),
                kspec(ka, k_head_a), kspec(kb, k_head_b), vspec]
    args = [q, q, ka, kb, v]
    if ctx is not None:
        kca, kcb, vc = ctx
        past = vc.shape[-1]

        def cspec(arr, hsel, shape):
            if arr.ndim == 4:
                return pl.BlockSpec((None, None) + shape, lambda b, p, i: (hsel(p), b, 0, 0))
            return pl.BlockSpec((None,) + shape, lambda b, p, i: (b, 0, 0))

        in_specs += [cspec(kca, k_head_a, (past, 128)), cspec(kcb, k_head_b, (past, 128)),
                     cspec(vc, v_idx, (128, past))]
        args += [kca, kcb, vc]
    return pl.pallas_call(
        functools.partial(_attn_body, has_ctx=ctx is not None),
        grid=(n_seq, 2, nq),
        in_specs=in_specs,
        out_specs=pl.BlockSpec((tq, 128), lambda b, p, i: (b * nq + i, p)),
        out_shape=jax.ShapeDtypeStruct((n_seq * seq_len, 256), BF16),
        scratch_shapes=[pltpu.VMEM((seq_len + (0 if ctx is None else ctx[2].shape[-1]), tq), F32)] * 2,
        compiler_params=_cparams(("arbitrary", "arbitrary", "arbitrary")),
        name="attention_ctx" if ctx is None else "attention_lat",
    )(*args)


def _out_body(x_ref, mod_ref, a_ref, yf_ref, yb_ref, obc_ref, obl_ref, occ_ref, ocl_ref,
              d_ref, dprev_ref, dnext_ref, wo_ref, gu_ref, rk_ref, gn_ref, cw_ref, cb_ref, n2_ref, wq_ref,
              x1_ref, h2_ref, q_ref, *, n_ctx_tiles):
    i = pl.program_id(0)
    is_ctx = i < n_ctx_tiles
    tm = x_ref.shape[0]
    mod = mod_ref[...]
    gate1 = mod[:, 2 * D_MODEL:3 * D_MODEL]
    shift2 = mod[:, 3 * D_MODEL:4 * D_MODEL]
    scale2 = mod[:, 4 * D_MODEL:5 * D_MODEL]

    r = a_ref[:, 0:256]
    k = a_ref[:, 256:512]
    v = a_ref[:, 512:768]
    gd = a_ref[:, 1024:1152]
    y = yf_ref[...] + yb_ref[...]
    avg = _group_matrix(GROUP_W, HEAD_DIM, 1.0 / HEAD_DIM)
    ones = _group_matrix(GROUP_W, HEAD_DIM, 1.0)
    yn = y * lax.rsqrt(_group_sums(y * y, avg) + EPS) * gn_ref[...]
    bonus = _group_sums(r * k * rk_ref[...], ones) * v
    gate = _dot(jax.nn.sigmoid(gd).astype(BF16), gu_ref[...])
    o_a = (yn + bonus) * gate

    o_b = jnp.where(is_ctx, obc_ref[...], obl_ref[...])
    o_c = jnp.where(is_ctx, occ_ref[...], ocl_ref[...])

    u = d_ref[:, 512:768] * d_ref[:, 0:256]
    u_prev = dprev_ref[:, 512:768] * dprev_ref[:, 0:256]
    u_next = dnext_ref[:, 512:768] * dnext_ref[:, 0:256]
    rows = lax.broadcasted_iota(jnp.int32, (tm, GROUP_W), 0)
    up = jnp.where(rows == 0, u_prev, pltpu.roll(u, 1, 0))
    un = jnp.where(rows == tm - 1, u_next, pltpu.roll(u, tm - 1, 0))
    conv = up * cw_ref[0:1, :] + u * cw_ref[1:2, :] + un * cw_ref[2:3, :] + cb_ref[...]
    o_d = d_ref[:, 256:512] * conv

    mix_in = jnp.concatenate([o_a.astype(BF16), o_b, o_c, o_d.astype(BF16)], axis=1)
    x1 = x_ref[...] + gate1 * _dot(mix_in, wo_ref[...])
    x1_ref[...] = x1
    h2 = (_rms_rows(x1, n2_ref[...]) * (1.0 + scale2) + shift2).astype(BF16)
    h2_ref[...] = h2
    q_ref[...] = _dot(h2, wq_ref[...]).astype(q_ref.dtype)


def _out_proj(x, mod, a_grp, yf, yb, ob_ctx, ob_lat, oc_ctx, oc_lat, d_grp, d_prev, d_next, lw, w_out_all, wq_all,
              layer):
    n = x.shape[0]
    n_tiles = n // TILE
    n_ctx_tiles = N_CTX_TOK // TILE
    tpg = COND_GROUP // TILE
    row = lambda i: (i, 0)
    ctx_row = lambda i: (jnp.minimum(i, n_ctx_tiles - 1), 0)
    lat_row = lambda i: (jnp.maximum(i - n_ctx_tiles, 0), 0)
    halo = lambda i: (i, 0, 0)
    c2 = lambda i: (0, 0)
    nq = PEER_HEADS * 2 * PEER_KEYS
    return pl.pallas_call(
        functools.partial(_out_body, n_ctx_tiles=n_ctx_tiles),
        grid=(n_tiles,),
        in_specs=[pl.BlockSpec((TILE, D_MODEL), row),
                  pl.BlockSpec((None, 1, 6 * D_MODEL), lambda i: (i // tpg, 0, 0)),
                  pl.BlockSpec((TILE, A_W), row),
                  pl.BlockSpec((TILE, GROUP_W), row), pl.BlockSpec((TILE, GROUP_W), row),
                  pl.BlockSpec((TILE, GROUP_W), ctx_row), pl.BlockSpec((TILE, GROUP_W), lat_row),
                  pl.BlockSpec((TILE, GROUP_W), ctx_row), pl.BlockSpec((TILE, GROUP_W), lat_row),
                  pl.BlockSpec((TILE, D_W), row),
                  pl.BlockSpec((None, 1, D_W), halo), pl.BlockSpec((None, 1, D_W), halo),
                  pl.BlockSpec((None, D_MODEL, D_MODEL), lambda i: (layer, 0, 0)), pl.BlockSpec((128, GROUP_W), c2),
                  pl.BlockSpec((1, GROUP_W), c2), pl.BlockSpec((1, GROUP_W), c2),
                  pl.BlockSpec((3, GROUP_W), c2), pl.BlockSpec((1, GROUP_W), c2),
                  pl.BlockSpec((1, D_MODEL), c2), pl.BlockSpec((None, D_MODEL, nq), lambda i: (layer, 0, 0))],
        out_specs=[pl.BlockSpec((TILE, D_MODEL), row), pl.BlockSpec((TILE, D_MODEL), row),
                   pl.BlockSpec((TILE, nq), row)],
        out_shape=[jax.ShapeDtypeStruct((n, D_MODEL), F32), jax.ShapeDtypeStruct((n, D_MODEL), BF16),
                   jax.ShapeDtypeStruct((n, nq), BF16)],
        compiler_params=_cparams(("arbitrary",)),
        name="out_proj",
    )(x, mod, a_grp, yf, yb, ob_ctx, ob_lat, oc_ctx, oc_lat, d_grp, d_prev, d_next,
      w_out_all, lw["gu"], lw["rk"], lw["gn"], lw["conv_w"], lw["conv_b"], lw["norm2_g"], wq_all)


def _exchange(a, b):
    if a is None:
        return b, None
    if b is None:
        return a, None
    return jnp.maximum(a, b), jnp.minimum(a, b)


def _sort16_desc(xs):
    xs = list(xs)
    k = 2
    while k <= 16:
        j = k // 2
        while j >= 1:
            for i in range(16):
                partner = i ^ j
                if partner > i:
                    hi, lo = _exchange(xs[i], xs[partner])
                    xs[i], xs[partner] = (hi, lo) if (i & k) == 0 else (lo, hi)
            j //= 2
        k *= 2
    return xs


def _bitonic_merge_desc(xs):
    xs = list(xs)
    j = 8
    while j >= 1:
        for i in range(16):
            partner = i ^ j
            if partner > i:
                xs[i], xs[partner] = _exchange(xs[i], xs[partner])
        j //= 2
    return xs


def _top16_of_rows(xs):
    ys = _sort16_desc(xs)
    for shift in (4, 2, 1):
        zs = [None if y is None else pltpu.roll(y, shift, 0) for y in ys]
        ts = [_exchange(ys[i], zs[15 - i])[0] for i in range(16)]
        ys = _bitonic_merge_desc(ts)
    return ys


def _topk_body(q_ref, k1_ref, k2_ref, thr_ref, s2_ref, e1_ref, e2_ref):
    tn = q_ref.shape[0]
    sub = lax.broadcasted_iota(jnp.int32, (8, tn), 0)

    def spread(vals):
        out = vals[7]
        for s in range(6, -1, -1):
            out = jnp.where(sub == s, vals[s], out)
        return out

    for h in range(PEER_HEADS):
        qa = q_ref[:, (2 * h) * PEER_KEYS:(2 * h + 1) * PEER_KEYS].astype(BF16)
        qb = q_ref[:, (2 * h + 1) * PEER_KEYS:(2 * h + 2) * PEER_KEYS].astype(BF16)
        s1 = _dot_nt(k1_ref[h], qa)
        s2 = _dot_nt(k2_ref[h], qb)
        v1 = _top16_of_rows([s1[8 * i:8 * i + 8, :] for i in range(16)])
        v2 = _top16_of_rows([s2[8 * i:8 * i + 8, :] for i in range(16)])
        v2_lo, v2_hi, v1_hi = spread(v2[0:8]), spread(v2[8:16]), spread(v1[8:16])
        cands = ([v1[0] + v2_lo, v1[0] + v2_hi] + [v1[a] + v2_lo for a in range(1, 8)]
                 + [v1_hi + v2[0]] + [None] * 6)
        best = _top16_of_rows(cands)
        tau = best[PEER_TOPK - 1][0:1, :]
        zsum = jnp.exp(best[0] - best[0])
        for kth in range(1, PEER_TOPK):
            zsum = zsum + jnp.exp(best[kth] - best[0])
        thr = jnp.full(s1.shape, jnp.inf, F32)
        for b in range(PEER_TOPK):
            vb = v2[b][0:1, :]
            thr = jnp.where(s1 + vb >= tau, vb, thr)
        thr_ref[h] = thr
        s2_ref[h] = s2
        e1_ref[h] = jnp.exp(s1 - v1[0][0:1, :])
        e2_ref[h] = jnp.exp(s2 - v2[0][0:1, :]) * (0.5 / zsum[0:1, :])


def _rotated_block(i):
    per_tile = PEER_TN // TOPK_TN
    return (i // per_tile) * per_tile + (i + 1) % per_tile


def _peer_topk(q, k1, k2):
    n = q.shape[0]
    tn = TOPK_TN
    big = pl.BlockSpec((PEER_HEADS, PEER_KEYS, tn), lambda i: (0, 0, i))
    big_shape = jax.ShapeDtypeStruct((PEER_HEADS, PEER_KEYS, n), F32)
    c3 = lambda i: (0, 0, 0)
    return pl.pallas_call(
        _topk_body,
        grid=(n // tn,),
        in_specs=[pl.BlockSpec((tn, PEER_HEADS * 2 * PEER_KEYS), lambda i: (i, 0)),
                  pl.BlockSpec((PEER_HEADS, PEER_KEYS, PEER_KEYS), c3),
                  pl.BlockSpec((PEER_HEADS, PEER_KEYS, PEER_KEYS), c3)],
        out_specs=[big, big, big, pl.BlockSpec((PEER_HEADS, PEER_KEYS, tn), lambda i: (0, 0, _rotated_block(i)))],
        out_shape=[big_shape, big_shape, big_shape, big_shape],
        compiler_params=_cparams(("arbitrary",)),
        name="peer_topk",
    )(q, k1, k2)


GELU_C0 = 0.7978845608028654
GELU_C1 = 0.044715


PEER_PIECE = 256
GATE_KEYS = 64


def _dense_body(h2_ref, u_ref, vt_ref, thr_ref, s2_ref, e1_ref, e2_ref, x1_ref, mod_ref,
                o_ref, acc, gs, *hs):
    j = pl.program_id(1)

    @pl.when(j == 0)
    def _():
        acc[...] = jnp.zeros_like(acc)

    n_pieces = len(hs)
    tn = hs[0].shape[1]
    n_col = tn // 128
    h2 = h2_ref[...]

    def pre_activations(p):
        rows = slice(p * PEER_PIECE, (p + 1) * PEER_PIECE)
        hs[p][...] = _dot_nt(u_ref[rows, :], h2)

    pre_activations(0)
    for p in range(n_pieces):
        if p + 1 < n_pieces:
            pre_activations(p + 1)
        halves = PEER_PIECE // PEER_KEYS
        per_col = PEER_KEYS // GATE_KEYS
        for sub in range(per_col * n_col):
            keys = slice((sub % per_col) * GATE_KEYS, (sub % per_col) * GATE_KEYS + GATE_KEYS)
            c = sub // per_col
            cols = slice(c * 128, c * 128 + 128)
            e2_cols = slice(((c + 1) % n_col) * 128, ((c + 1) % n_col) * 128 + 128)
            n_grp = GATE_KEYS // 8
            w = [[None] * n_grp for _ in range(halves)]
            for h in range(PEER_HEADS):
                thr8 = [jnp.broadcast_to(thr_ref[h, p * halves + half:p * halves + half + 1, cols], (8, 128))
                        for half in range(halves)]
                e18 = [jnp.broadcast_to(e1_ref[h, p * halves + half:p * halves + half + 1, cols], (8, 128))
                       for half in range(halves)]
                for g in range(n_grp):
                    grp = slice(keys.start + 8 * g, keys.start + 8 * g + 8)
                    s2g = s2_ref[h, grp, cols]
                    e2g = e2_ref[h, grp, e2_cols]
                    for half in range(halves):
                        term = jnp.where(s2g >= thr8[half], e18[half] * e2g, 0.0)
                        w[half][g] = term if w[half][g] is None else w[half][g] + term
            for half in range(halves):
                rows = slice(half * PEER_KEYS + keys.start, half * PEER_KEYS + keys.stop)
                x = hs[p][rows, cols]
                inner = x * (GELU_C0 + (GELU_C0 * GELU_C1) * (x * x))
                wh = jnp.concatenate(w[half], axis=0)
                grows = slice(p * PEER_PIECE + rows.start, p * PEER_PIECE + rows.stop)
                gs[grows, cols] = (wh * x * (1.0 + jnp.tanh(inner))).astype(BF16)
    acc[...] += _dot(vt_ref[...], gs[...])

    @pl.when(j == pl.num_programs(1) - 1)
    def _():
        gate2 = mod_ref[:, 5 * D_MODEL:6 * D_MODEL]
        o_ref[...] = x1_ref[...] + gate2 * acc[...].T


def _peer_dense(h2, u_all, vt_all, layer, thr, s2, e1, e2, x1, mod):
    n = h2.shape[0]
    tn, te = PEER_TN, PEER_TE
    rows_per_step = te // PEER_KEYS
    tpg = COND_GROUP // tn
    tok = lambda i, j: (i, 0)
    key_rows = pl.BlockSpec((PEER_HEADS, rows_per_step, tn), lambda i, j: (0, j, i))
    key_all = pl.BlockSpec((PEER_HEADS, PEER_KEYS, tn), lambda i, j: (0, 0, i))
    return pl.pallas_call(
        _dense_body,
        grid=(n // tn, N_EXPERTS // te),
        in_specs=[pl.BlockSpec((tn, D_MODEL), tok),
                  pl.BlockSpec((None, te, D_MODEL), lambda i, j: (layer, j, 0)),
                  pl.BlockSpec((None, D_MODEL, te), lambda i, j: (layer, 0, j)),
                  key_rows, key_all, key_rows, key_all,
                  pl.BlockSpec((tn, D_MODEL), tok),
                  pl.BlockSpec((None, 1, 6 * D_MODEL), lambda i, j: (i // tpg, 0, 0))],
        out_specs=pl.BlockSpec((tn, D_MODEL), tok),
        out_shape=jax.ShapeDtypeStruct((n, D_MODEL), F32),
        scratch_shapes=([pltpu.VMEM((D_MODEL, tn), F32), pltpu.VMEM((te, tn), BF16)]
                        + [pltpu.VMEM((PEER_PIECE, tn), F32)] * (te // PEER_PIECE)),
        compiler_params=_cparams(("arbitrary", "arbitrary")),
        name="peer_dense",
    )(h2, u_all, vt_all, thr, s2, e1, e2, x1, mod)


def _in_proj_columns():
    src = np.full((IN_PAD,), -1, np.int64)

    def put(dst, start, width):
        src[dst:dst + width] = np.arange(start, start + width)

    put(0, 0, 768)
    put(768, 768, 128)
    put(896, 896, 64)
    put(1024, 960, 128)
    b0 = A_W
    put(b0, 1088, 192)
    put(b0 + 256, 1280, 128)
    put(b0 + 384 + 64, 1408, 32)
    put(b0 + 512 + 64, 1408 + 16, 16)
    put(b0 + 512 + 80, 1408, 16)
    c0 = A_W + B_W
    for slot, head in enumerate((0, 2, 1, 3)):
        put(c0 + 64 * slot, 1440 + 64 * head, 64)
    put(c0 + 256, 1696, 256)
    put(A_W + B_W + C_W, 1952, 768)
    return src


def _swap_tail(w):
    return jnp.concatenate([w[..., :64], w[..., 80:96], w[..., 64:80]], axis=-1)


def _pad_last(w, width):
    return jnp.pad(w, [(0, 0)] * (w.ndim - 1) + [(0, width - w.shape[-1])])


STACKED_WEIGHTS = ("w_in", "w_out", "wq", "u", "vt")


def _layer_weights(p):
    depth = p["w_in"].shape[0]
    src = _in_proj_columns()
    runs, start = [], 0
    for i in range(1, IN_PAD + 1):
        if i == IN_PAD or (src[i] != src[i - 1] + 1 if src[i - 1] >= 0 else src[i] >= 0):
            runs.append((start, i))
            start = i
    w_bf = p["w_in"].astype(BF16)
    pieces = [w_bf[:, :, src[a]:src[a] + (b - a)] if src[a] >= 0
              else jnp.zeros((depth, D_MODEL, b - a), BF16) for a, b in runs]
    w_in = jnp.concatenate(pieces, axis=2)

    wuq = p["mla_wuq"].reshape(depth, MLA_Q_RANK, 4, MLA_QK_DIM).transpose(0, 2, 1, 3)
    wq_plain = jnp.pad(wuq, ((0, 0), (0, 0), (0, 256 - MLA_Q_RANK), (0, 128 - MLA_QK_DIM)))
    wq_swap = jnp.pad(_swap_tail(wuq), ((0, 0), (0, 0), (0, 256 - MLA_Q_RANK), (0, 128 - MLA_QK_DIM)))
    wukv = p["mla_wukv"].reshape(depth, 128, 4, 128)
    wk = _pad_last(wukv[..., :64].transpose(0, 2, 1, 3), 128)
    wv_heads = wukv[..., 64:].transpose(0, 2, 1, 3)
    wv = jnp.concatenate([wv_heads[:, 0::2], wv_heads[:, 1::2]], axis=-1)

    def gain2(g):
        return jnp.stack([_pad_last(g, 128), _pad_last(_swap_tail(g), 128)], axis=1)

    w_out = p["w_out"]
    oc = w_out[:, 512:768].reshape(depth, 4, 64, D_MODEL)[:, jnp.asarray([0, 2, 1, 3])].reshape(depth, 256, D_MODEL)
    w_out = jnp.concatenate([w_out[:, :512], oc, w_out[:, 768:]], axis=1).astype(BF16)

    wu_pad = jnp.zeros((depth, 2, 128, GROUP_W), F32)
    wu_pad = wu_pad.at[:, 0, 0:64].set(p["rw_wu"][:, 0]).at[:, 1, 64:128].set(p["rw_wu"][:, 1])
    au_pad = jnp.zeros((depth, 2, 128, GROUP_W), F32)
    au_pad = au_pad.at[:, 0, 0:32].set(p["rw_au"][:, 0]).at[:, 1, 32:64].set(p["rw_au"][:, 1])

    return {
        "w_in": w_in,
        "norm1_g": p["norm1_g"][:, None, :],
        "norm2_g": p["norm2_g"][:, None, :],
        "w_out": w_out,
        "w0": p["rw_w0"], "a0": p["rw_a0"],
        "wu": wu_pad.astype(BF16), "au": au_pad.astype(BF16),
        "kkp": p["rw_kk"][:, None, :], "ka": p["rw_ka"][:, None, :],
        "gu": p["rw_gu"].astype(BF16),
        "rk": p["rw_rk"].reshape(depth, 1, GROUP_W),
        "gn": p["rw_gn"][:, None, :],
        "qa_g": _pad_last(p["mla_qa_g"], 256)[:, None, :],
        "wqq": jnp.concatenate([wq_plain, wq_swap], axis=1).astype(BF16),
        "qn_g2": gain2(p["mla_qn_g"]),
        "kva_g": p["mla_kva_g"][:, None, :],
        "wk": wk.astype(BF16), "wv": jnp.swapaxes(wv, -1, -2).astype(BF16),
        "kn_g2": gain2(p["mla_kn_g"]),
        "gq_g": jnp.tile(p["gqa_qn_g"], (1, 2))[:, None, :],
        "gk_g": jnp.tile(p["gqa_kn_g"], (1, 2))[:, None, :],
        "conv_w": p["conv_w"], "conv_b": p["conv_b"][:, None, :],
        "wq": p["peer_wq"].astype(BF16),
        "k1": p["peer_k1"].astype(BF16), "k2": p["peer_k2"].astype(BF16),
        "u": p["peer_u"].astype(BF16),
        "vt": jnp.swapaxes(p["peer_v"], 1, 2).astype(BF16),
    }


def _rope_tables():
    t = jnp.arange(LAT_LEN, dtype=F32)
    grid_row = jnp.floor(t / GRID_W)
    grid_col = t - grid_row * GRID_W

    def angles(rot_dim):
        n_freq = rot_dim // 4
        freqs = ROPE_THETA ** (-jnp.arange(n_freq, dtype=F32) / n_freq)
        ang = jnp.concatenate([grid_row[:, None] * freqs, grid_col[:, None] * freqs], axis=-1)
        return jnp.cos(ang), jnp.sin(ang)

    cm, sm = angles(32)
    ones64 = jnp.ones((LAT_LEN, 64), F32)
    zeros64 = jnp.zeros((LAT_LEN, 64), F32)
    cosm = jnp.concatenate([ones64, cm, cm, ones64[:, :32]], axis=-1)
    sinm = jnp.concatenate([zeros64, -sm, sm, zeros64[:, :32]], axis=-1)
    cg, sg = angles(64)
    cosg = jnp.tile(jnp.concatenate([cg, cg], axis=-1), (1, 2))
    sing = jnp.tile(jnp.concatenate([-sg, sg], axis=-1), (1, 2))
    ident = jnp.ones((IN_TILE, 128), F32)
    zero = jnp.zeros((IN_TILE, 128), F32)
    return {"cosm": jnp.concatenate([ident, cosm]), "sinm": jnp.concatenate([zero, sinm]),
            "cosg": jnp.concatenate([ident, cosg]), "sing": jnp.concatenate([zero, sing])}


def _states_to_pairs(s):
    lead = s.shape[:-3]
    s = s.reshape(lead + (2, 2, 64, 64))
    z = jnp.zeros(lead + (2, 64, 64), s.dtype)
    top = jnp.concatenate([s[..., 0, :, :], z], axis=-1)
    bot = jnp.concatenate([z, s[..., 1, :, :]], axis=-1)
    return jnp.concatenate([top, bot], axis=-2)


def _pairs_to_states(sp):
    lead = sp.shape[:-3]
    a = sp[..., 0:64, 0:64]
    b = sp[..., 64:128, 64:128]
    return jnp.stack([a, b], axis=-3).reshape(lead + (4, 64, 64))


def _conv_halos(d_grp):
    n_tiles = d_grp.shape[0] // TILE
    tiles = d_grp.reshape(n_tiles, TILE, D_W)
    first_rows = tiles[:, 0, :]
    last_rows = tiles[:, TILE - 1, :]
    zero = jnp.zeros((1, D_W), d_grp.dtype)
    prev = jnp.concatenate([zero, last_rows[:-1]], axis=0)
    nxt = jnp.concatenate([first_rows[1:], zero], axis=0)
    idx = np.arange(n_tiles)
    n_ctx_tiles = N_CTX_TOK // TILE
    per_seq = LAT_LEN // TILE
    lat_pos = (idx - n_ctx_tiles) % per_seq
    seq_start = np.where(idx < n_ctx_tiles, True, lat_pos == 0)
    seq_end = np.where(idx < n_ctx_tiles, True, lat_pos == per_seq - 1)
    prev = jnp.where(jnp.asarray(seq_start)[:, None], 0.0, prev)
    nxt = jnp.where(jnp.asarray(seq_end)[:, None], 0.0, nxt)
    return prev[:, None, :], nxt[:, None, :]


def kernel(x_prompt, x_sample, state_rwkv, cache_mla_ckv, cache_mla_krope, cache_gqa_k, cache_gqa_v, c, c_ctx, norm1_g, norm2_g, w_mod, b_mod, w_in, w_out, rw_w0, rw_wu, rw_a0, rw_au, rw_gu, rw_kk, rw_ka, rw_rk, rw_gn, mla_qa_g, mla_wuq, mla_kva_g, mla_wukv, mla_qn_g, mla_kn_g, gqa_qn_g, gqa_kn_g, conv_w, conv_b, peer_wq, peer_k1, peer_k2, peer_u, peer_v):
    depth = w_in.shape[0]
    params = dict(norm1_g=norm1_g, norm2_g=norm2_g, w_in=w_in, w_out=w_out, rw_w0=rw_w0, rw_wu=rw_wu,
                  rw_a0=rw_a0, rw_au=rw_au, rw_gu=rw_gu, rw_kk=rw_kk, rw_ka=rw_ka, rw_rk=rw_rk, rw_gn=rw_gn,
                  mla_qa_g=mla_qa_g, mla_wuq=mla_wuq, mla_kva_g=mla_kva_g, mla_wukv=mla_wukv,
                  mla_qn_g=mla_qn_g, mla_kn_g=mla_kn_g, gqa_qn_g=gqa_qn_g, gqa_kn_g=gqa_kn_g,
                  conv_w=conv_w, conv_b=conv_b, peer_wq=peer_wq, peer_k1=peer_k1, peer_k2=peer_k2,
                  peer_u=peer_u, peer_v=peer_v)
    lw_all = _layer_weights(params)
    tabs = _rope_tables()

    cvecs = jnp.concatenate([c_ctx[None, :], c, jnp.zeros((8 - 1 - N_LAT_SEQ, D_MODEL), F32)], axis=0)
    mods = _modulation(cvecs, w_mod, b_mod).reshape(depth, 8, 1, 6 * D_MODEL)

    x = jnp.concatenate([x_prompt.reshape(N_CTX_TOK, D_MODEL), x_sample.reshape(N_LAT_TOK, D_MODEL)], axis=0)

    s0_lat = _states_to_pairs(state_rwkv)
    s0_all = jnp.concatenate([jnp.zeros((N_CTX_SEQ,) + s0_lat.shape[1:], F32), s0_lat], axis=0)

    krope_placed = jnp.pad(cache_mla_krope, ((0, 0), (0, 0), (0, 0), (64, 32)))
    kctx_m, vctx_m = _mla_ctx_kv(cache_mla_ckv, krope_placed, lw_all["wk"], lw_all["wv"], lw_all["kn_g2"])
    past = cache_gqa_k.shape[2]
    kctx_g = cache_gqa_k.reshape(N_LAT_SEQ, depth, past, 128)
    vctx_g = jnp.swapaxes(cache_gqa_v.reshape(N_LAT_SEQ, depth, past, 128), -1, -2)

    ident = lambda p: p
    st_a, st_ckv, st_kr, st_k, st_v = [], [], [], [], []
    for l in range(depth):
        lw = {name: w[l] for name, w in lw_all.items() if name not in STACKED_WEIGHTS}
        mod = mods[l]
        a_grp, b_grp, c_grp, d_grp, qm, km, vm, ckvn, qg, kg, vg, kgn = _in_proj(
            x, mod, lw["norm1_g"], lw_all["w_in"], l, tabs, lw)

        yf, yb, s_fin = _rwkv_scan(a_grp, s0_all[:, l], lw["w0"], lw["wu"], lw["a0"], lw["au"], lw["kkp"], lw["ka"])

        ob_ctx = _attention(qm, km, km, vm, lambda p: 2 * p, lambda p: 2 * p + 1, lambda p: 2 * p,
                            lambda p: 2 * p + 1, ident, N_CTX_SEQ, CTX_LEN, 0)
        ob_lat = _attention(qm, km, km, vm, lambda p: 2 * p, lambda p: 2 * p + 1, lambda p: 2 * p,
                            lambda p: 2 * p + 1, ident, N_LAT_SEQ, LAT_LEN, N_CTX_TOK,
                            ctx=(kctx_m[l], kctx_m[l], vctx_m[l]))
        oc_ctx = _attention(qg, kg, kg, vg, ident, lambda p: p + 2, ident, ident, ident,
                            N_CTX_SEQ, CTX_LEN, 0)
        oc_lat = _attention(qg, kg, kg, vg, ident, lambda p: p + 2, ident, ident, ident,
                            N_LAT_SEQ, LAT_LEN, N_CTX_TOK,
                            ctx=(kctx_g[:, l], kctx_g[:, l], vctx_g[:, l]))

        d_prev, d_next = _conv_halos(d_grp)
        x1, h2, q = _out_proj(x, mod, a_grp, yf, yb, ob_ctx, ob_lat, oc_ctx, oc_lat, d_grp, d_prev, d_next, lw,
                              lw_all["w_out"], lw_all["wq"], l)

        thr, s2, e1, e2 = _peer_topk(q, lw["k1"], lw["k2"])
        x = _peer_dense(h2, lw_all["u"], lw_all["vt"], l, thr, s2, e1, e2, x1, mod)

        st_a.append(_pairs_to_states(s_fin[:N_CTX_SEQ]))
        st_ckv.append(ckvn[:N_CTX_TOK].reshape(N_CTX_SEQ, CTX_LEN, 128))
        st_kr.append(b_grp[:N_CTX_TOK, 448:480].reshape(N_CTX_SEQ, CTX_LEN, 32))
        st_k.append(kgn[:N_CTX_TOK].reshape(N_CTX_SEQ, CTX_LEN, 2, HEAD_DIM))
        st_v.append(c_grp[:N_CTX_TOK, 384:512].reshape(N_CTX_SEQ, CTX_LEN, 2, HEAD_DIM))

    y_prompt = x[:N_CTX_TOK].reshape(N_CTX_SEQ, CTX_LEN, D_MODEL)
    y_sample = x[N_CTX_TOK:].reshape(N_LAT_SEQ, LAT_LEN, D_MODEL)
    return (y_prompt, y_sample, jnp.stack(st_a, axis=1), jnp.stack(st_ckv, axis=1), jnp.stack(st_kr, axis=1),
            jnp.stack(st_k, axis=1), jnp.stack(st_v, axis=1))
```

```python
import functools

import numpy as np
import jax
import jax.numpy as jnp
from jax import lax
from jax.experimental import pallas as pl
from jax.experimental.pallas import tpu as pltpu

F32 = jnp.float32
BF16 = jnp.bfloat16

D_MODEL = 1024
N_CTX_SEQ = 16
CTX_LEN = 256
N_LAT_SEQ = 2
LAT_LEN = 4096
GRID_W = 64
ROPE_THETA = 10000.0
EPS = 1e-6
GROUP_W = 256
HEAD_DIM = 64
RW_DECAY_SCALE = 0.6065306597
MLA_QK_DIM = 96
MLA_Q_RANK = 192
PEER_HEADS = 8
PEER_KEYS = 128
PEER_TOPK = 16
N_EXPERTS = PEER_KEYS * PEER_KEYS

N_CTX_TOK = N_CTX_SEQ * CTX_LEN
N_LAT_TOK = N_LAT_SEQ * LAT_LEN
N_TOK = N_CTX_TOK + N_LAT_TOK
COND_GROUP = 4096

A_W, B_W, C_W, D_W = 1152, 640, 512, 768
IN_PAD = A_W + B_W + C_W + D_W

SCAN_CHUNK = 128
TILE = 256
IN_TILE = 512
PEER_TN = 512
PEER_TE = 2048
TOPK_TN = 128
ATT_TQ = 512
ATT_CHUNK = 1024

VMEM_LIMIT = 56 * 1024 * 1024


def _cparams(sem):
    return pltpu.CompilerParams(dimension_semantics=sem, vmem_limit_bytes=VMEM_LIMIT)


def _dot(a, b, precision=None):
    return jnp.dot(a, b, preferred_element_type=F32, precision=precision)


def _dot_nt(a, b, precision=None):
    return lax.dot_general(a, b, (((1,), (1,)), ((), ())), preferred_element_type=F32,
                           precision=precision)


def _dot_tn(a, b, precision=None):
    return lax.dot_general(a, b, (((0,), (0,)), ((), ())), preferred_element_type=F32,
                           precision=precision)


def _rms_rows(x, g):
    return x * lax.rsqrt(jnp.mean(x * x, axis=-1, keepdims=True) + EPS) * g


def _group_sums(x, group_mat):
    hi = x.astype(BF16)
    lo = (x - hi.astype(F32)).astype(BF16)
    gm = group_mat.astype(BF16)
    return _dot(hi, gm) + _dot(lo, gm)


def _group_matrix(n, group, value):
    r = lax.broadcasted_iota(jnp.int32, (n, n), 0) // group
    c = lax.broadcasted_iota(jnp.int32, (n, n), 1) // group
    return jnp.where(r == c, value, 0.0).astype(F32)


def _mod_body(c_ref, w_ref, b_ref, o_ref):
    c = c_ref[...]
    s = c * jax.nn.sigmoid(c)
    o_ref[...] = _dot(s.astype(BF16), w_ref[...].astype(BF16)) + b_ref[...]


def _modulation(cvecs, w_mod, b_mod):
    depth = w_mod.shape[0]
    tn = 1536
    return pl.pallas_call(
        _mod_body,
        grid=(depth, 6 * D_MODEL // tn),
        in_specs=[pl.BlockSpec((8, D_MODEL), lambda l, j: (0, 0)),
                  pl.BlockSpec((None, D_MODEL, tn), lambda l, j: (l, 0, j)),
                  pl.BlockSpec((None, 1, tn), lambda l, j: (l, 0, j))],
        out_specs=pl.BlockSpec((None, 8, tn), lambda l, j: (l, 0, j)),
        out_shape=jax.ShapeDtypeStruct((depth, 8, 6 * D_MODEL), F32),
        compiler_params=_cparams(("arbitrary", "arbitrary")),
        name="adaln_mod",
    )(cvecs, w_mod, b_mod.reshape(depth, 1, 6 * D_MODEL))


def _in_body(x_ref, mod_ref, g_ref, w_ref, *rest):
    prep_in, (a_ref, b_ref, c_ref, d_ref), prep_out = rest[:13], rest[13:17], rest[17:]
    mod = mod_ref[...]
    shift = mod[:, 0:D_MODEL]
    scale = mod[:, D_MODEL:2 * D_MODEL]
    h = _rms_rows(x_ref[...], g_ref[...]) * (1.0 + scale) + shift
    y = _dot(h.astype(BF16), w_ref[...])
    a_ref[...] = y[:, 0:A_W]
    b_ref[...] = y[:, A_W:A_W + B_W]
    c_ref[...] = y[:, A_W + B_W:A_W + B_W + C_W]
    d_ref[...] = y[:, A_W + B_W + C_W:IN_PAD]
    _prep_body(b_ref, c_ref, *prep_in, *prep_out)


def _in_proj(x, mod, norm_g, w_pad_all, layer, tabs, lw):
    n = x.shape[0]
    tpg = COND_GROUP // IN_TILE
    n_ctx_tiles = N_CTX_TOK // IN_TILE
    lat_tiles = LAT_LEN // IN_TILE
    row = lambda i: (i, 0)
    hrow = lambda i: (0, i, 0)
    tab = lambda i: (jnp.where(i < n_ctx_tiles, 0, 1 + (i - n_ctx_tiles) % lat_tiles), 0)
    c2 = lambda i: (0, 0)
    c3 = lambda i: (0, 0, 0)
    return pl.pallas_call(
        _in_body,
        grid=(n // IN_TILE,),
        in_specs=[pl.BlockSpec((IN_TILE, D_MODEL), row),
                  pl.BlockSpec((None, 1, 6 * D_MODEL), lambda i: (i // tpg, 0, 0)),
                  pl.BlockSpec((1, D_MODEL), c2),
                  pl.BlockSpec((None, D_MODEL, IN_PAD), lambda i: (layer, 0, 0)),
                  pl.BlockSpec((IN_TILE, 128), tab), pl.BlockSpec((IN_TILE, 128), tab),
                  pl.BlockSpec((IN_TILE, 128), tab), pl.BlockSpec((IN_TILE, 128), tab),
                  pl.BlockSpec((1, 256), c2), pl.BlockSpec((8, 256, 128), c3), pl.BlockSpec((2, 128), c2),
                  pl.BlockSpec((1, 128), c2), pl.BlockSpec((4, 128, 128), c3), pl.BlockSpec((2, 128, 128), c3),
                  pl.BlockSpec((2, 128), c2), pl.BlockSpec((1, 128), c2), pl.BlockSpec((1, 128), c2)],
        out_specs=[pl.BlockSpec((IN_TILE, A_W), row), pl.BlockSpec((IN_TILE, B_W), row),
                   pl.BlockSpec((IN_TILE, C_W), row), pl.BlockSpec((IN_TILE, D_W), row),
                   pl.BlockSpec((4, IN_TILE, 128), hrow), pl.BlockSpec((4, IN_TILE, 128), hrow),
                   pl.BlockSpec((2, 128, IN_TILE), lambda i: (0, 0, i)), pl.BlockSpec((IN_TILE, 128), row),
                   pl.BlockSpec((4, IN_TILE, 128), hrow), pl.BlockSpec((IN_TILE, 128), row),
                   pl.BlockSpec((128, IN_TILE), lambda i: (0, i)), pl.BlockSpec((IN_TILE, 128), row)],
        out_shape=[jax.ShapeDtypeStruct((n, A_W), F32), jax.ShapeDtypeStruct((n, B_W), F32),
                   jax.ShapeDtypeStruct((n, C_W), F32), jax.ShapeDtypeStruct((n, D_W), F32),
                   jax.ShapeDtypeStruct((4, n, 128), BF16), jax.ShapeDtypeStruct((4, n, 128), BF16),
                   jax.ShapeDtypeStruct((2, 128, n), BF16), jax.ShapeDtypeStruct((n, 128), F32),
                   jax.ShapeDtypeStruct((4, n, 128), BF16), jax.ShapeDtypeStruct((n, 128), BF16),
                   jax.ShapeDtypeStruct((128, n), BF16), jax.ShapeDtypeStruct((n, 128), F32)],
        compiler_params=_cparams(("arbitrary",)),
        name="in_proj",
    )(x, mod, norm_g, w_pad_all, tabs["cosm"], tabs["sinm"], tabs["cosg"], tabs["sing"],
      lw["qa_g"], lw["wqq"], lw["qn_g2"], lw["kva_g"], lw["wk"], lw["wv"], lw["kn_g2"], lw["gq_g"], lw["gk_g"])


def _chunk_cumsum(x, reverse):
    c = x.shape[0]
    rows = lax.broadcasted_iota(jnp.int32, x.shape, 0)
    sh = 1
    while sh < c:
        if reverse:
            x = x + jnp.where(rows < c - sh, pltpu.roll(x, c - sh, 0), 0.0)
        else:
            x = x + jnp.where(rows >= sh, pltpu.roll(x, sh, 0), 0.0)
        sh *= 2
    return x


def _scan_body(fa_ref, ba_ref, fb_ref, bb_ref, fo_ref, bo_ref, first_ref, last_ref, seq_ref,
               af0_ref, ab0_ref, af1_ref, ab1_ref, s0a_ref, s0b_ref,
               w0_ref, wu_ref, a0_ref, au_ref, kkp_ref, ka_ref,
               yf_ref, yb_ref, sfina_ref, sfinb_ref, s_scr):
    step = pl.program_id(0)

    @pl.when(first_ref[step] == 1)
    def _():
        s_scr[0] = s0a_ref[...]
        s_scr[1] = s0b_ref[...]

    c = SCAN_CHUNK
    row = lax.broadcasted_iota(jnp.int32, (c, c), 0)
    col = lax.broadcasted_iota(jnp.int32, (c, c), 1)
    lane_c = lax.broadcasted_iota(jnp.int32, (c, 128), 1) < HEAD_DIM
    lane_2c = lax.broadcasted_iota(jnp.int32, (2 * c, 128), 1) < HEAD_DIM
    same_head = _group_matrix(128, HEAD_DIM, 1.0)
    kkp = kkp_ref[...]
    ka = ka_ref[...]

    def per_head(lo, hi):
        return jnp.where(lane_c, lo, hi)

    groups = []
    for q, d in ((0, 0), (0, 1), (1, 0), (1, 1)):
        x_ref = ((af0_ref, ab0_ref), (af1_ref, ab1_ref))[q][d]
        if d == 0:
            strict, incl = col < row, col <= row
        else:
            strict, incl = col > row, col >= row
        r = x_ref[:, 0:256]
        k = x_ref[:, 256:512]
        v = x_ref[:, 512:768]
        wd = x_ref[:, 768:896]
        ad = x_ref[:, 896:1024]
        wlog = -RW_DECAY_SCALE * jax.nn.sigmoid(
            w0_ref[d:d + 1, :] + _dot(jnp.tanh(wd).astype(BF16), wu_ref[d]))
        a = jax.nn.sigmoid(a0_ref[d:d + 1, :] + _dot(ad.astype(BF16), au_ref[d]))
        kd = k * (1.0 + (a - 1.0) * ka)
        kkf = k * kkp
        cum = _chunk_cumsum(wlog, reverse=(d == 1))
        tot = cum[c - 1:c, :] if d == 0 else cum[0:1, :]
        p_inc = jnp.exp(cum)
        p_exc = jnp.exp(cum - wlog)
        p_inv = jnp.exp(-cum)
        p_tot = jnp.exp(tot)
        for p in range(2):
            sl = slice(128 * p, 128 * p + 128)
            groups.append(dict(q=q, d=d, p=p, sl=sl, strict=strict, incl=incl, kk_raw=kkf[:, sl],
                               a=a[:, sl], kd=kd[:, sl], r=r[:, sl], vb=v[:, sl].astype(BF16),
                               p_inc=p_inc[:, sl], p_exc=p_exc[:, sl], p_inv=p_inv[:, sl], pt=p_tot[:, sl]))

    for g in groups:
        g["ss"] = _group_sums(g["kk_raw"] * g["kk_raw"], same_head)
    for g in groups:
        kk = g["kk_raw"] * lax.rsqrt(g["ss"] + EPS)
        g["a_p"] = (-kk * g["p_exc"]).astype(BF16)
        g["r_p"] = (g["r"] * g["p_inc"]).astype(BF16)
        g["b_i"] = kk * g["a"] * g["p_inv"]
        g["k_i"] = g["kd"] * g["p_inv"]
        g["lhs"] = jnp.concatenate([g["a_p"], g["r_p"]], axis=0)
        g["rhs"] = jnp.concatenate([g["b_i"], g["k_i"]], axis=0).astype(BF16)
        g["sb"] = s_scr[g["q"], g["d"], g["p"]].astype(BF16)
    zero_b = jnp.zeros((), BF16)
    for g in groups:
        g["gram"] = (_dot_nt(jnp.where(lane_2c, g["lhs"], zero_b), g["rhs"]),
                     _dot_nt(jnp.where(lane_2c, zero_b, g["lhs"]), g["rhs"]))
        g["a_s"] = _dot_nt(g["a_p"], g["sb"])
        g["r_s"] = _dot_nt(g["r_p"], g["sb"])
    for g in groups:
        st, inc = g["strict"], g["incl"]
        g["pow"] = [jnp.where(st, gm[0:c, 0:c], 0.0).astype(BF16) for gm in g["gram"]]
        g["dm"] = [jnp.where(st, gm[0:c, c:2 * c], 0.0).astype(BF16) for gm in g["gram"]]
        g["et"] = [jnp.where(inc, gm[c:2 * c, 0:c], 0.0).astype(BF16) for gm in g["gram"]]
        g["ft"] = [jnp.where(inc, gm[c:2 * c, c:2 * c], 0.0).astype(BF16) for gm in g["gram"]]
    for g in groups:
        g["z"] = g["a_s"] + per_head(_dot(g["dm"][0], g["vb"]), _dot(g["dm"][1], g["vb"]))
    span = 1
    while span < c:
        for g in groups:
            zb = g["z"].astype(BF16)
            g["z"] = g["z"] + per_head(_dot(g["pow"][0], zb), _dot(g["pow"][1], zb))
        span *= 2
        if span < c:
            for g in groups:
                g["pow"] = [_dot(m, m).astype(BF16) for m in g["pow"]]
    for g in groups:
        zb = g["z"].astype(BF16)
        g["zb"] = zb
        y = g["r_s"] + per_head(_dot(g["et"][0], zb) + _dot(g["ft"][0], g["vb"]),
                                _dot(g["et"][1], zb) + _dot(g["ft"][1], g["vb"]))
        y_ref = yf_ref if g["d"] == 0 else yb_ref
        y_ref[g["q"], :, g["sl"]] = y
    for g in groups:
        pt = g["pt"]
        upd = (_dot_tn(g["zb"], (g["b_i"] * pt).astype(BF16))
               + _dot_tn(g["vb"], (g["k_i"] * pt).astype(BF16)))
        slot = (g["q"], g["d"], g["p"])
        s_scr[slot] = s_scr[slot] * pt + same_head * upd

    @pl.when(last_ref[step] == 1)
    def _():
        sfina_ref[...] = s_scr[0]
        sfinb_ref[...] = s_scr[1]


def _scan_tables():
    c = SCAN_CHUNK
    fa, ba, fb, bb, fo, bo, first, last, seq = [], [], [], [], [], [], [], [], []
    base = out_base = slot_base = 0
    for nseq, length in ((N_CTX_SEQ, CTX_LEN), (N_LAT_SEQ, LAT_LEN)):
        nc = length // c
        half = nseq // 2
        for s in range(half):
            for j in range(nc):
                fa.append(base + s * nc + j)
                ba.append(base + s * nc + nc - 1 - j)
                fb.append(base + (s + half) * nc + j)
                bb.append(base + (s + half) * nc + nc - 1 - j)
                fo.append(out_base + s * nc + j)
                bo.append(out_base + s * nc + nc - 1 - j)
                first.append(int(j == 0))
                last.append(int(j == nc - 1))
                seq.append(slot_base + s)
        base += nseq * nc
        out_base += half * nc
        slot_base += half
    return tuple(np.asarray(t, np.int32) for t in (fa, ba, fb, bb, fo, bo, first, last, seq))


def _scan_halves(per_seq):
    h_ctx, h_lat = N_CTX_SEQ // 2, N_LAT_SEQ // 2
    ctx, lat = per_seq[:N_CTX_SEQ], per_seq[N_CTX_SEQ:]
    return (jnp.concatenate([ctx[:h_ctx], lat[:h_lat]]), jnp.concatenate([ctx[h_ctx:], lat[h_lat:]]))


def _rwkv_scan(a_grp, s0_pairs, w0, wu_pad, a0, au_pad, kkp, ka):
    tables = _scan_tables()
    n_steps = tables[0].shape[0]
    s0_a, s0_b = _scan_halves(s0_pairs)
    n_slots = s0_a.shape[0]
    n = a_grp.shape[0]
    c = SCAN_CHUNK
    const2 = lambda s, *t: (0, 0)
    const3 = lambda s, *t: (0, 0, 0)
    chunk = lambda k: pl.BlockSpec((c, A_W), lambda s, *t: (t[k][s], 0))
    state = pl.BlockSpec((None, 2, 2, 128, 128), lambda s, *t: (t[8][s], 0, 0, 0, 0))
    grid_spec = pltpu.PrefetchScalarGridSpec(
        num_scalar_prefetch=9,
        grid=(n_steps,),
        in_specs=[chunk(0), chunk(1), chunk(2), chunk(3), state, state,
                  pl.BlockSpec((2, GROUP_W), const2),
                  pl.BlockSpec((2, 128, GROUP_W), const3),
                  pl.BlockSpec((2, GROUP_W), const2),
                  pl.BlockSpec((2, 128, GROUP_W), const3),
                  pl.BlockSpec((1, GROUP_W), const2),
                  pl.BlockSpec((1, GROUP_W), const2)],
        out_specs=[pl.BlockSpec((2, c, GROUP_W), lambda s, *t: (0, t[4][s], 0)),
                   pl.BlockSpec((2, c, GROUP_W), lambda s, *t: (0, t[5][s], 0)),
                   state, state],
        scratch_shapes=[pltpu.VMEM((2, 2, 2, 128, 128), F32)],
    )
    yf, yb, sfin_a, sfin_b = pl.pallas_call(
        _scan_body,
        grid_spec=grid_spec,
        out_shape=[jax.ShapeDtypeStruct((2, n // 2, GROUP_W), F32), jax.ShapeDtypeStruct((2, n // 2, GROUP_W), F32),
                   jax.ShapeDtypeStruct((n_slots, 2, 2, 128, 128), F32),
                   jax.ShapeDtypeStruct((n_slots, 2, 2, 128, 128), F32)],
        compiler_params=_cparams(("arbitrary",)),
        name="rwkv_scan",
    )(*[jnp.asarray(t) for t in tables], a_grp, a_grp, a_grp, a_grp, s0_a, s0_b, w0, wu_pad, a0, au_pad, kkp, ka)
    h_ctx = N_CTX_SEQ // 2
    s_fin = jnp.concatenate([sfin_a[:h_ctx], sfin_b[:h_ctx], sfin_a[h_ctx:], sfin_b[h_ctx:]])
    return yf, yb, s_fin


LOG2_E = 1.4426950408889634
MLA_SCALE = MLA_QK_DIM ** -0.5 * LOG2_E
GQA_SCALE = HEAD_DIM ** -0.5 * LOG2_E


def _mla_keys(ckv_b, rope_slot, rope_slot_sw, wk_ref, kng_ref, cosm, sinm, k_out_ref):
    g = kng_ref[0:1, :]
    g_sw = kng_ref[1:2, :]
    for h in range(4):
        nope = _dot(ckv_b, wk_ref[h])
        kr = nope + rope_slot
        rs = lax.rsqrt(jnp.sum(kr * kr, axis=-1, keepdims=True) * (1.0 / MLA_QK_DIM) + EPS)
        if cosm is None:
            k_out_ref[h] = (kr * rs * g).astype(BF16)
        else:
            ks = nope + rope_slot_sw
            k_out_ref[h] = ((kr * rs * g) * cosm + (ks * rs * g_sw) * sinm).astype(BF16)


def _prep_body(b_ref, c_ref, cosm_ref, sinm_ref, cosg_ref, sing_ref,
               qag_ref, wqq_ref, qng_ref, kvag_ref, wk_ref, wv_ref, kng_ref, gqg_ref, gkg_ref,
               qm_ref, km_ref, vm_ref, ckvn_ref, qg_ref, kg_ref, vg_ref, kgn_ref):
    cosm = cosm_ref[...]
    sinm = sinm_ref[...]
    qc = b_ref[:, 0:256]
    qn = qc * lax.rsqrt(jnp.sum(qc * qc, axis=-1, keepdims=True) * (1.0 / MLA_Q_RANK) + EPS) * qag_ref[...]
    qnb = qn.astype(BF16)
    g = qng_ref[0:1, :]
    g_sw = qng_ref[1:2, :]
    for h in range(4):
        qr = _dot(qnb, wqq_ref[h])
        qs = _dot(qnb, wqq_ref[4 + h])
        rs = lax.rsqrt(jnp.sum(qr * qr, axis=-1, keepdims=True) * (1.0 / MLA_QK_DIM) + EPS)
        qm_ref[h] = (((qr * rs * g) * cosm + (qs * rs * g_sw) * sinm) * MLA_SCALE).astype(BF16)
    ckv = _rms_rows(b_ref[:, 256:384], kvag_ref[...])
    ckvn_ref[...] = ckv
    cb = ckv.astype(BF16)
    _mla_keys(cb, b_ref[:, 384:512], b_ref[:, 512:640], wk_ref, kng_ref, cosm, sinm, km_ref)
    for p in range(2):
        vm_ref[p] = _dot_nt(wv_ref[p], cb).astype(BF16)

    tm = c_ref.shape[0]
    avg = _group_matrix(128, HEAD_DIM, 1.0 / HEAD_DIM)
    lane = lax.broadcasted_iota(jnp.int32, (tm, 128), 1)
    first_half = (lane % HEAD_DIM) < (HEAD_DIM // 2)
    low = lane < HEAD_DIM
    cosg = cosg_ref[...]
    sing = sing_ref[...]

    def rotate(xn):
        swapped = jnp.where(first_half, pltpu.roll(xn, 128 - HEAD_DIM // 2, 1), pltpu.roll(xn, HEAD_DIM // 2, 1))
        return xn * cosg + swapped * sing

    for blk in range(2):
        x = c_ref[:, 128 * blk:128 * blk + 128]
        xn = x * lax.rsqrt(_group_sums(x * x, avg) + EPS) * gqg_ref[...]
        xr = rotate(xn) * GQA_SCALE
        qg_ref[blk] = jnp.where(low, xr, 0.0).astype(BF16)
        qg_ref[blk + 2] = jnp.where(low, 0.0, xr).astype(BF16)
    xk = c_ref[:, 256:384]
    kn = xk * lax.rsqrt(_group_sums(xk * xk, avg) + EPS) * gkg_ref[...]
    kgn_ref[...] = kn
    kg_ref[...] = rotate(kn).astype(BF16)
    vg_ref[...] = c_ref[:, 384:512].T.astype(BF16)


def _ctxkv_body(ckv_ref, krp_ref, wk_ref, wv_ref, kng_ref, k_ref, v_ref):
    cb = ckv_ref[...].astype(BF16)
    _mla_keys(cb, krp_ref[...], None, wk_ref, kng_ref, None, None, k_ref)
    for p in range(2):
        v_ref[p] = _dot_nt(wv_ref[p], cb).astype(BF16)


def _mla_ctx_kv(cache_ckv, cache_krope_placed, wk, wv, kn_g2):
    nb, depth, plen, _ = cache_ckv.shape
    return pl.pallas_call(
        _ctxkv_body,
        grid=(depth, nb),
        in_specs=[pl.BlockSpec((None, None, plen, 128), lambda l, b: (b, l, 0, 0)),
                  pl.BlockSpec((None, None, plen, 128), lambda l, b: (b, l, 0, 0)),
                  pl.BlockSpec((None, 4, 128, 128), lambda l, b: (l, 0, 0, 0)),
                  pl.BlockSpec((None, 2, 128, 128), lambda l, b: (l, 0, 0, 0)),
                  pl.BlockSpec((None, 2, 128), lambda l, b: (l, 0, 0))],
        out_specs=[pl.BlockSpec((None, 4, None, plen, 128), lambda l, b: (l, 0, b, 0, 0)),
                   pl.BlockSpec((None, 2, None, 128, plen), lambda l, b: (l, 0, b, 0, 0))],
        out_shape=[jax.ShapeDtypeStruct((depth, 4, nb, plen, 128), BF16),
                   jax.ShapeDtypeStruct((depth, 2, nb, 128, plen), BF16)],
        compiler_params=_cparams(("arbitrary", "arbitrary")),
        name="mla_ctx_kv",
    )(cache_ckv, cache_krope_placed, wk, wv, kn_g2)


def _attn_body(*refs, has_ctx):
    if has_ctx:
        qa_ref, qb_ref, ka_ref, kb_ref, vt_ref, kca_ref, kcb_ref, vct_ref, o_ref, sa_scr, sb_scr = refs
    else:
        qa_ref, qb_ref, ka_ref, kb_ref, vt_ref, o_ref, sa_scr, sb_scr = refs
        kca_ref = kcb_ref = vct_ref = None
    tk = ka_ref.shape[0]
    chunk = min(ATT_CHUNK, tk)
    q = (qa_ref[...], qb_ref[...])
    k_refs = (ka_ref, kb_ref)
    kc_refs = (kca_ref, kcb_ref)
    s_scr = (sa_scr, sb_scr)
    vrows = (slice(0, HEAD_DIM), slice(HEAD_DIM, 2 * HEAD_DIM))
    spans = [("new", c * chunk, chunk) for c in range(tk // chunk)]
    if has_ctx:
        spans = [("ctx", 0, kca_ref.shape[0])] + spans

    def scratch_rows(kind, start, size):
        base = tk if kind == "ctx" else 0
        return slice(base + start, base + start + size)

    m = [None, None]
    den = [None, None]
    acc = [None, None]

    def score_pass(hd, span):
        kind, start, size = span
        if kind == "ctx":
            keys = kc_refs[hd][...].astype(BF16)
        else:
            keys = k_refs[hd][start:start + size, :]
        s = _dot_nt(keys, q[hd])
        s_scr[hd][scratch_rows(kind, start, size), :] = s
        cm = jnp.max(s, axis=0, keepdims=True)
        m[hd] = cm if m[hd] is None else jnp.maximum(m[hd], cm)

    def value_pass(hd, span):
        kind, start, size = span
        e = jnp.exp2(s_scr[hd][scratch_rows(kind, start, size), :] - m[hd])
        if kind == "ctx":
            vals = vct_ref[vrows[hd], :].astype(BF16)
        else:
            vals = vt_ref[vrows[hd], start:start + size]
        part = _dot(vals, e.astype(BF16))
        rs = jnp.sum(e, axis=0, keepdims=True)
        den[hd] = rs if den[hd] is None else den[hd] + rs
        acc[hd] = part if acc[hd] is None else acc[hd] + part

    for span in spans:
        for hd in range(2):
            score_pass(hd, span)
    for span in spans:
        for hd in range(2):
            value_pass(hd, span)
    out_t = jnp.concatenate([acc[0] / den[0], acc[1] / den[1]], axis=0)
    o_ref[...] = out_t.T.astype(o_ref.dtype)


def _attention(q, ka, kb, v, head_a, head_b, k_head_a, k_head_b, v_idx, n_seq, seq_len, tok0, ctx=None):
    tq = min(ATT_TQ, seq_len)
    nq = seq_len // tq
    seq0 = tok0 // seq_len
    q0 = tok0 // tq

    def qmap(hsel):
        return lambda b, p, i: (hsel(p), q0 + b * nq + i, 0)

    def kmap(hsel, arr):
        if arr.ndim == 3:
            return lambda b, p, i: (hsel(p), seq0 + b, 0)
        return lambda b, p, i: (seq0 + b, 0)

    def kspec(arr, hsel):
        if arr.ndim == 3:
            return pl.BlockSpec((None, seq_len, 128), kmap(hsel, arr))
        return pl.BlockSpec((seq_len, 128), kmap(hsel, arr))

    if v.ndim == 3:
        vspec = pl.BlockSpec((None, 128, seq_len), lambda b, p, i: (v_idx(p), 0, seq0 + b))
    else:
        vspec = pl.BlockSpec((128, seq_len), lambda b, p, i: (0, seq0 + b))
    in_specs = [pl.BlockSpec((None, tq, 128), qmap(head_a)), pl.BlockSpec((None, tq, 128), qmap(head_b)),
                kspec(ka, k_head_a), kspec(kb, k_head_b), vspec]
    args = [q, q, ka, kb, v]
    if ctx is not None:
        kca, kcb, vc = ctx
        past = vc.shape[-1]

        def cspec(arr, hsel, shape):
            if arr.ndim == 4:
                return pl.BlockSpec((None, None) + shape, lambda b, p, i: (hsel(p), b, 0, 0))
            return pl.BlockSpec((None,) + shape, lambda b, p, i: (b, 0, 0))

        in_specs += [cspec(kca, k_head_a, (past, 128)), cspec(kcb, k_head_b, (past, 128)),
                     cspec(vc, v_idx, (128, past))]
        args += [kca, kcb, vc]
    return pl.pallas_call(
        functools.partial(_attn_body, has_ctx=ctx is not None),
        grid=(n_seq, 2, nq),
        in_specs=in_specs,
        out_specs=pl.BlockSpec((tq, 128), lambda b, p, i: (b * nq + i, p)),
        out_shape=jax.ShapeDtypeStruct((n_seq * seq_len, 256), BF16),
        scratch_shapes=[pltpu.VMEM((seq_len + (0 if ctx is None else ctx[2].shape[-1]), tq), F32)] * 2,
        compiler_params=_cparams(("arbitrary", "arbitrary", "arbitrary")),
        name="attention_ctx" if ctx is None else "attention_lat",
    )(*args)


def _out_body(x_ref, mod_ref, a_ref, yf_ref, yb_ref, obc_ref, obl_ref, occ_ref, ocl_ref,
              d_ref, dprev_ref, dnext_ref, wo_ref, gu_ref, rk_ref, gn_ref, cw_ref, cb_ref, n2_ref, wq_ref,
              x1_ref, h2_ref, q_ref, *, n_ctx_tiles):
    i = pl.program_id(0)
    is_ctx = i < n_ctx_tiles
    tm = x_ref.shape[0]
    mod = mod_ref[...]
    gate1 = mod[:, 2 * D_MODEL:3 * D_MODEL]
    shift2 = mod[:, 3 * D_MODEL:4 * D_MODEL]
    scale2 = mod[:, 4 * D_MODEL:5 * D_MODEL]

    r = a_ref[:, 0:256]
    k = a_ref[:, 256:512]
    v = a_ref[:, 512:768]
    gd = a_ref[:, 1024:1152]
    y = yf_ref[...] + yb_ref[...]
    avg = _group_matrix(GROUP_W, HEAD_DIM, 1.0 / HEAD_DIM)
    ones = _group_matrix(GROUP_W, HEAD_DIM, 1.0)
    yn = y * lax.rsqrt(_group_sums(y * y, avg) + EPS) * gn_ref[...]
    bonus = _group_sums(r * k * rk_ref[...], ones) * v
    gate = _dot(jax.nn.sigmoid(gd).astype(BF16), gu_ref[...])
    o_a = (yn + bonus) * gate

    o_b = jnp.where(is_ctx, obc_ref[...], obl_ref[...])
    o_c = jnp.where(is_ctx, occ_ref[...], ocl_ref[...])

    u = d_ref[:, 512:768] * d_ref[:, 0:256]
    u_prev = dprev_ref[:, 512:768] * dprev_ref[:, 0:256]
    u_next = dnext_ref[:, 512:768] * dnext_ref[:, 0:256]
    rows = lax.broadcasted_iota(jnp.int32, (tm, GROUP_W), 0)
    up = jnp.where(rows == 0, u_prev, pltpu.roll(u, 1, 0))
    un = jnp.where(rows == tm - 1, u_next, pltpu.roll(u, tm - 1, 0))
    conv = up * cw_ref[0:1, :] + u * cw_ref[1:2, :] + un * cw_ref[2:3, :] + cb_ref[...]
    o_d = d_ref[:, 256:512] * conv

    mix_in = jnp.concatenate([o_a.astype(BF16), o_b, o_c, o_d.astype(BF16)], axis=1)
    x1 = x_ref[...] + gate1 * _dot(mix_in, wo_ref[...])
    x1_ref[...] = x1
    h2 = (_rms_rows(x1, n2_ref[...]) * (1.0 + scale2) + shift2).astype(BF16)
    h2_ref[...] = h2
    q_ref[...] = _dot(h2, wq_ref[...]).astype(q_ref.dtype)


def _out_proj(x, mod, a_grp, yf, yb, ob_ctx, ob_lat, oc_ctx, oc_lat, d_grp, d_prev, d_next, lw, w_out_all, wq_all,
              layer):
    n = x.shape[0]
    n_tiles = n // TILE
    n_ctx_tiles = N_CTX_TOK // TILE
    tpg = COND_GROUP // TILE
    row = lambda i: (i, 0)
    ctx_row = lambda i: (jnp.minimum(i, n_ctx_tiles - 1), 0)
    lat_row = lambda i: (jnp.maximum(i - n_ctx_tiles, 0), 0)
    half_ctx = n_ctx_tiles // 2
    half_lat = (n_tiles - n_ctx_tiles) // 2
    scan_row = lambda i: (jnp.where(i < n_ctx_tiles, i // half_ctx, (i - n_ctx_tiles) // half_lat),
                          jnp.where(i < n_ctx_tiles, i % half_ctx, half_ctx + (i - n_ctx_tiles) % half_lat), 0)
    halo = lambda i: (i, 0, 0)
    c2 = lambda i: (0, 0)
    nq = PEER_HEADS * 2 * PEER_KEYS
    return pl.pallas_call(
        functools.partial(_out_body, n_ctx_tiles=n_ctx_tiles),
        grid=(n_tiles,),
        in_specs=[pl.BlockSpec((TILE, D_MODEL), row),
                  pl.BlockSpec((None, 1, 6 * D_MODEL), lambda i: (i // tpg, 0, 0)),
                  pl.BlockSpec((TILE, A_W), row),
                  pl.BlockSpec((None, TILE, GROUP_W), scan_row), pl.BlockSpec((None, TILE, GROUP_W), scan_row),
                  pl.BlockSpec((TILE, GROUP_W), ctx_row), pl.BlockSpec((TILE, GROUP_W), lat_row),
                  pl.BlockSpec((TILE, GROUP_W), ctx_row), pl.BlockSpec((TILE, GROUP_W), lat_row),
                  pl.BlockSpec((TILE, D_W), row),
                  pl.BlockSpec((None, 1, D_W), halo), pl.BlockSpec((None, 1, D_W), halo),
                  pl.BlockSpec((None, D_MODEL, D_MODEL), lambda i: (layer, 0, 0)), pl.BlockSpec((128, GROUP_W), c2),
                  pl.BlockSpec((1, GROUP_W), c2), pl.BlockSpec((1, GROUP_W), c2),
                  pl.BlockSpec((3, GROUP_W), c2), pl.BlockSpec((1, GROUP_W), c2),
                  pl.BlockSpec((1, D_MODEL), c2), pl.BlockSpec((None, D_MODEL, nq), lambda i: (layer, 0, 0))],
        out_specs=[pl.BlockSpec((TILE, D_MODEL), row), pl.BlockSpec((TILE, D_MODEL), row),
                   pl.BlockSpec((TILE, nq), row)],
        out_shape=[jax.ShapeDtypeStruct((n, D_MODEL), F32), jax.ShapeDtypeStruct((n, D_MODEL), BF16),
                   jax.ShapeDtypeStruct((n, nq), BF16)],
        compiler_params=_cparams(("arbitrary",)),
        name="out_proj",
    )(x, mod, a_grp, yf, yb, ob_ctx, ob_lat, oc_ctx, oc_lat, d_grp, d_prev, d_next,
      w_out_all, lw["gu"], lw["rk"], lw["gn"], lw["conv_w"], lw["conv_b"], lw["norm2_g"], wq_all)


def _exchange(a, b):
    if a is None:
        return b, None
    if b is None:
        return a, None
    return jnp.maximum(a, b), jnp.minimum(a, b)


def _sort16_desc(xs):
    xs = list(xs)
    k = 2
    while k <= 16:
        j = k // 2
        while j >= 1:
            for i in range(16):
                partner = i ^ j
                if partner > i:
                    hi, lo = _exchange(xs[i], xs[partner])
                    xs[i], xs[partner] = (hi, lo) if (i & k) == 0 else (lo, hi)
            j //= 2
        k *= 2
    return xs


def _bitonic_merge_desc(xs):
    xs = list(xs)
    j = 8
    while j >= 1:
        for i in range(16):
            partner = i ^ j
            if partner > i:
                xs[i], xs[partner] = _exchange(xs[i], xs[partner])
        j //= 2
    return xs


def _top16_of_rows(xs):
    ys = _sort16_desc(xs)
    for shift in (4, 2, 1):
        zs = [None if y is None else pltpu.roll(y, shift, 0) for y in ys]
        ts = [_exchange(ys[i], zs[15 - i])[0] for i in range(16)]
        ys = _bitonic_merge_desc(ts)
    return ys


def _topk_body(q_ref, k1_ref, k2_ref, thr_ref, s2_ref, e1_ref, e2_ref):
    tn = q_ref.shape[0]
    sub = lax.broadcasted_iota(jnp.int32, (8, tn), 0)

    def spread(vals):
        out = vals[7]
        for s in range(6, -1, -1):
            out = jnp.where(sub == s, vals[s], out)
        return out

    for h in range(PEER_HEADS):
        qa = q_ref[:, (2 * h) * PEER_KEYS:(2 * h + 1) * PEER_KEYS].astype(BF16)
        qb = q_ref[:, (2 * h + 1) * PEER_KEYS:(2 * h + 2) * PEER_KEYS].astype(BF16)
        s1 = _dot_nt(k1_ref[h], qa)
        s2 = _dot_nt(k2_ref[h], qb)
        v1 = _top16_of_rows([s1[8 * i:8 * i + 8, :] for i in range(16)])
        v2 = _top16_of_rows([s2[8 * i:8 * i + 8, :] for i in range(16)])
        v2_lo, v2_hi, v1_hi = spread(v2[0:8]), spread(v2[8:16]), spread(v1[8:16])
        cands = ([v1[0] + v2_lo, v1[0] + v2_hi] + [v1[a] + v2_lo for a in range(1, 8)]
                 + [v1_hi + v2[0]] + [None] * 6)
        best = _top16_of_rows(cands)
        tau = best[PEER_TOPK - 1][0:1, :]
        zsum = jnp.exp(best[0] - best[0])
        for kth in range(1, PEER_TOPK):
            zsum = zsum + jnp.exp(best[kth] - best[0])
        thr = jnp.full(s1.shape, jnp.inf, F32)
        for b in range(PEER_TOPK):
            vb = v2[b][0:1, :]
            thr = jnp.where(s1 + vb >= tau, vb, thr)
        thr_ref[h] = thr
        s2_ref[h] = s2
        e1_ref[h] = jnp.exp(s1 - v1[0][0:1, :])
        e2_ref[h] = jnp.exp(s2 - v2[0][0:1, :]) * (0.5 / zsum[0:1, :])


def _rotated_block(i):
    per_tile = PEER_TN // TOPK_TN
    return (i // per_tile) * per_tile + (i + 1) % per_tile


def _peer_topk(q, k1, k2):
    n = q.shape[0]
    tn = TOPK_TN
    big = pl.BlockSpec((PEER_HEADS, PEER_KEYS, tn), lambda i: (0, 0, i))
    big_shape = jax.ShapeDtypeStruct((PEER_HEADS, PEER_KEYS, n), F32)
    c3 = lambda i: (0, 0, 0)
    return pl.pallas_call(
        _topk_body,
        grid=(n // tn,),
        in_specs=[pl.BlockSpec((tn, PEER_HEADS * 2 * PEER_KEYS), lambda i: (i, 0)),
                  pl.BlockSpec((PEER_HEADS, PEER_KEYS, PEER_KEYS), c3),
                  pl.BlockSpec((PEER_HEADS, PEER_KEYS, PEER_KEYS), c3)],
        out_specs=[big, big, big, pl.BlockSpec((PEER_HEADS, PEER_KEYS, tn), lambda i: (0, 0, _rotated_block(i)))],
        out_shape=[big_shape, big_shape, big_shape, big_shape],
        compiler_params=_cparams(("arbitrary",)),
        name="peer_topk",
    )(q, k1, k2)


GELU_C0 = 0.7978845608028654
GELU_C1 = 0.044715


PEER_PIECE = 256
GATE_KEYS = 64


def _dense_body(h2_ref, u_ref, vt_ref, thr_ref, s2_ref, e1_ref, e2_ref, x1_ref, mod_ref,
                o_ref, acc, gs, *hs):
    j = pl.program_id(1)

    @pl.when(j == 0)
    def _():
        acc[...] = jnp.zeros_like(acc)

    n_pieces = len(hs)
    tn = hs[0].shape[1]
    n_col = tn // 128
    h2 = h2_ref[...]

    def pre_activations(p):
        rows = slice(p * PEER_PIECE, (p + 1) * PEER_PIECE)
        hs[p][...] = _dot_nt(u_ref[rows, :], h2)

    pre_activations(0)
    for p in range(n_pieces):
        if p + 1 < n_pieces:
            pre_activations(p + 1)
        halves = PEER_PIECE // PEER_KEYS
        per_col = PEER_KEYS // GATE_KEYS
        for sub in range(per_col * n_col):
            keys = slice((sub % per_col) * GATE_KEYS, (sub % per_col) * GATE_KEYS + GATE_KEYS)
            c = sub // per_col
            cols = slice(c * 128, c * 128 + 128)
            e2_cols = slice(((c + 1) % n_col) * 128, ((c + 1) % n_col) * 128 + 128)
            n_grp = GATE_KEYS // 8
            w = [[None] * n_grp for _ in range(halves)]
            for h in range(PEER_HEADS):
                thr8 = [jnp.broadcast_to(thr_ref[h, p * halves + half:p * halves + half + 1, cols], (8, 128))
                        for half in range(halves)]
                e18 = [jnp.broadcast_to(e1_ref[h, p * halves + half:p * halves + half + 1, cols], (8, 128))
                       for half in range(halves)]
                for g in range(n_grp):
                    grp = slice(keys.start + 8 * g, keys.start + 8 * g + 8)
                    s2g = s2_ref[h, grp, cols]
                    e2g = e2_ref[h, grp, e2_cols]
                    for half in range(halves):
                        term = jnp.where(s2g >= thr8[half], e18[half] * e2g, 0.0)
                        w[half][g] = term if w[half][g] is None else w[half][g] + term
            for half in range(halves):
                rows = slice(half * PEER_KEYS + keys.start, half * PEER_KEYS + keys.stop)
                x = hs[p][rows, cols]
                inner = x * (GELU_C0 + (GELU_C0 * GELU_C1) * (x * x))
                wh = jnp.concatenate(w[half], axis=0)
                grows = slice(p * PEER_PIECE + rows.start, p * PEER_PIECE + rows.stop)
                gs[grows, cols] = (wh * x * (1.0 + jnp.tanh(inner))).astype(BF16)
    acc[...] += _dot(vt_ref[...], gs[...])

    @pl.when(j == pl.num_programs(1) - 1)
    def _():
        gate2 = mod_ref[:, 5 * D_MODEL:6 * D_MODEL]
        o_ref[...] = x1_ref[...] + gate2 * acc[...].T


def _peer_dense(h2, u_all, vt_all, layer, thr, s2, e1, e2, x1, mod):
    n = h2.shape[0]
    tn, te = PEER_TN, PEER_TE
    rows_per_step = te // PEER_KEYS
    tpg = COND_GROUP // tn
    tok = lambda i, j: (i, 0)
    key_rows = pl.BlockSpec((PEER_HEADS, rows_per_step, tn), lambda i, j: (0, j, i))
    key_all = pl.BlockSpec((PEER_HEADS, PEER_KEYS, tn), lambda i, j: (0, 0, i))
    return pl.pallas_call(
        _dense_body,
        grid=(n // tn, N_EXPERTS // te),
        in_specs=[pl.BlockSpec((tn, D_MODEL), tok),
                  pl.BlockSpec((None, te, D_MODEL), lambda i, j: (layer, j, 0)),
                  pl.BlockSpec((None, D_MODEL, te), lambda i, j: (layer, 0, j)),
                  key_rows, key_all, key_rows, key_all,
                  pl.BlockSpec((tn, D_MODEL), tok),
                  pl.BlockSpec((None, 1, 6 * D_MODEL), lambda i, j: (i // tpg, 0, 0))],
        out_specs=pl.BlockSpec((tn, D_MODEL), tok),
        out_shape=jax.ShapeDtypeStruct((n, D_MODEL), F32),
        scratch_shapes=([pltpu.VMEM((D_MODEL, tn), F32), pltpu.VMEM((te, tn), BF16)]
                        + [pltpu.VMEM((PEER_PIECE, tn), F32)] * (te // PEER_PIECE)),
        compiler_params=_cparams(("arbitrary", "arbitrary")),
        name="peer_dense",
    )(h2, u_all, vt_all, thr, s2, e1, e2, x1, mod)


def _in_proj_columns():
    src = np.full((IN_PAD,), -1, np.int64)

    def put(dst, start, width):
        src[dst:dst + width] = np.arange(start, start + width)

    put(0, 0, 768)
    put(768, 768, 128)
    put(896, 896, 64)
    put(1024, 960, 128)
    b0 = A_W
    put(b0, 1088, 192)
    put(b0 + 256, 1280, 128)
    put(b0 + 384 + 64, 1408, 32)
    put(b0 + 512 + 64, 1408 + 16, 16)
    put(b0 + 512 + 80, 1408, 16)
    c0 = A_W + B_W
    for slot, head in enumerate((0, 2, 1, 3)):
        put(c0 + 64 * slot, 1440 + 64 * head, 64)
    put(c0 + 256, 1696, 256)
    put(A_W + B_W + C_W, 1952, 768)
    return src


def _swap_tail(w):
    return jnp.concatenate([w[..., :64], w[..., 80:96], w[..., 64:80]], axis=-1)


def _pad_last(w, width):
    return jnp.pad(w, [(0, 0)] * (w.ndim - 1) + [(0, width - w.shape[-1])])


STACKED_WEIGHTS = ("w_in", "w_out", "wq", "u", "vt")


def _layer_weights(p):
    depth = p["w_in"].shape[0]
    src = _in_proj_columns()
    runs, start = [], 0
    for i in range(1, IN_PAD + 1):
        if i == IN_PAD or (src[i] != src[i - 1] + 1 if src[i - 1] >= 0 else src[i] >= 0):
            runs.append((start, i))
            start = i
    w_bf = p["w_in"].astype(BF16)
    pieces = [w_bf[:, :, src[a]:src[a] + (b - a)] if src[a] >= 0
              else jnp.zeros((depth, D_MODEL, b - a), BF16) for a, b in runs]
    w_in = jnp.concatenate(pieces, axis=2)

    wuq = p["mla_wuq"].reshape(depth, MLA_Q_RANK, 4, MLA_QK_DIM).transpose(0, 2, 1, 3)
    wq_plain = jnp.pad(wuq, ((0, 0), (0, 0), (0, 256 - MLA_Q_RANK), (0, 128 - MLA_QK_DIM)))
    wq_swap = jnp.pad(_swap_tail(wuq), ((0, 0), (0, 0), (0, 256 - MLA_Q_RANK), (0, 128 - MLA_QK_DIM)))
    wukv = p["mla_wukv"].reshape(depth, 128, 4, 128)
    wk = _pad_last(wukv[..., :64].transpose(0, 2, 1, 3), 128)
    wv_heads = wukv[..., 64:].transpose(0, 2, 1, 3)
    wv = jnp.concatenate([wv_heads[:, 0::2], wv_heads[:, 1::2]], axis=-1)

    def gain2(g):
        return jnp.stack([_pad_last(g, 128), _pad_last(_swap_tail(g), 128)], axis=1)

    w_out = p["w_out"]
    oc = w_out[:, 512:768].reshape(depth, 4, 64, D_MODEL)[:, jnp.asarray([0, 2, 1, 3])].reshape(depth, 256, D_MODEL)
    w_out = jnp.concatenate([w_out[:, :512], oc, w_out[:, 768:]], axis=1).astype(BF16)

    wu_pad = jnp.zeros((depth, 2, 128, GROUP_W), F32)
    wu_pad = wu_pad.at[:, 0, 0:64].set(p["rw_wu"][:, 0]).at[:, 1, 64:128].set(p["rw_wu"][:, 1])
    au_pad = jnp.zeros((depth, 2, 128, GROUP_W), F32)
    au_pad = au_pad.at[:, 0, 0:32].set(p["rw_au"][:, 0]).at[:, 1, 32:64].set(p["rw_au"][:, 1])

    return {
        "w_in": w_in,
        "norm1_g": p["norm1_g"][:, None, :],
        "norm2_g": p["norm2_g"][:, None, :],
        "w_out": w_out,
        "w0": p["rw_w0"], "a0": p["rw_a0"],
        "wu": wu_pad.astype(BF16), "au": au_pad.astype(BF16),
        "kkp": p["rw_kk"][:, None, :], "ka": p["rw_ka"][:, None, :],
        "gu": p["rw_gu"].astype(BF16),
        "rk": p["rw_rk"].reshape(depth, 1, GROUP_W),
        "gn": p["rw_gn"][:, None, :],
        "qa_g": _pad_last(p["mla_qa_g"], 256)[:, None, :],
        "wqq": jnp.concatenate([wq_plain, wq_swap], axis=1).astype(BF16),
        "qn_g2": gain2(p["mla_qn_g"]),
        "kva_g": p["mla_kva_g"][:, None, :],
        "wk": wk.astype(BF16), "wv": jnp.swapaxes(wv, -1, -2).astype(BF16),
        "kn_g2": gain2(p["mla_kn_g"]),
        "gq_g": jnp.tile(p["gqa_qn_g"], (1, 2))[:, None, :],
        "gk_g": jnp.tile(p["gqa_kn_g"], (1, 2))[:, None, :],
        "conv_w": p["conv_w"], "conv_b": p["conv_b"][:, None, :],
        "wq": p["peer_wq"].astype(BF16),
        "k1": p["peer_k1"].astype(BF16), "k2": p["peer_k2"].astype(BF16),
        "u": p["peer_u"].astype(BF16),
        "vt": jnp.swapaxes(p["peer_v"], 1, 2).astype(BF16),
    }


def _rope_tables():
    t = jnp.arange(LAT_LEN, dtype=F32)
    grid_row = jnp.floor(t / GRID_W)
    grid_col = t - grid_row * GRID_W

    def angles(rot_dim):
        n_freq = rot_dim // 4
        freqs = ROPE_THETA ** (-jnp.arange(n_freq, dtype=F32) / n_freq)
        ang = jnp.concatenate([grid_row[:, None] * freqs, grid_col[:, None] * freqs], axis=-1)
        return jnp.cos(ang), jnp.sin(ang)

    cm, sm = angles(32)
    ones64 = jnp.ones((LAT_LEN, 64), F32)
    zeros64 = jnp.zeros((LAT_LEN, 64), F32)
    cosm = jnp.concatenate([ones64, cm, cm, ones64[:, :32]], axis=-1)
    sinm = jnp.concatenate([zeros64, -sm, sm, zeros64[:, :32]], axis=-1)
    cg, sg = angles(64)
    cosg = jnp.tile(jnp.concatenate([cg, cg], axis=-1), (1, 2))
    sing = jnp.tile(jnp.concatenate([-sg, sg], axis=-1), (1, 2))
    ident = jnp.ones((IN_TILE, 128), F32)
    zero = jnp.zeros((IN_TILE, 128), F32)
    return {"cosm": jnp.concatenate([ident, cosm]), "sinm": jnp.concatenate([zero, sinm]),
            "cosg": jnp.concatenate([ident, cosg]), "sing": jnp.concatenate([zero, sing])}


def _states_to_pairs(s):
    lead = s.shape[:-3]
    s = s.reshape(lead + (2, 2, 64, 64))
    z = jnp.zeros(lead + (2, 64, 64), s.dtype)
    top = jnp.concatenate([s[..., 0, :, :], z], axis=-1)
    bot = jnp.concatenate([z, s[..., 1, :, :]], axis=-1)
    return jnp.concatenate([top, bot], axis=-2)


def _pairs_to_states(sp):
    lead = sp.shape[:-3]
    a = sp[..., 0:64, 0:64]
    b = sp[..., 64:128, 64:128]
    return jnp.stack([a, b], axis=-3).reshape(lead + (4, 64, 64))


def _conv_halos(d_grp):
    n_tiles = d_grp.shape[0] // TILE
    tiles = d_grp.reshape(n_tiles, TILE, D_W)
    first_rows = tiles[:, 0, :]
    last_rows = tiles[:, TILE - 1, :]
    zero = jnp.zeros((1, D_W), d_grp.dtype)
    prev = jnp.concatenate([zero, last_rows[:-1]], axis=0)
    nxt = jnp.concatenate([first_rows[1:], zero], axis=0)
    idx = np.arange(n_tiles)
    n_ctx_tiles = N_CTX_TOK // TILE
    per_seq = LAT_LEN // TILE
    lat_pos = (idx - n_ctx_tiles) % per_seq
    seq_start = np.where(idx < n_ctx_tiles, True, lat_pos == 0)
    seq_end = np.where(idx < n_ctx_tiles, True, lat_pos == per_seq - 1)
    prev = jnp.where(jnp.asarray(seq_start)[:, None], 0.0, prev)
    nxt = jnp.where(jnp.asarray(seq_end)[:, None], 0.0, nxt)
    return prev[:, None, :], nxt[:, None, :]


def kernel(x_prompt, x_sample, state_rwkv, cache_mla_ckv, cache_mla_krope, cache_gqa_k, cache_gqa_v, c, c_ctx, norm1_g, norm2_g, w_mod, b_mod, w_in, w_out, rw_w0, rw_wu, rw_a0, rw_au, rw_gu, rw_kk, rw_ka, rw_rk, rw_gn, mla_qa_g, mla_wuq, mla_kva_g, mla_wukv, mla_qn_g, mla_kn_g, gqa_qn_g, gqa_kn_g, conv_w, conv_b, peer_wq, peer_k1, peer_k2, peer_u, peer_v):
    depth = w_in.shape[0]
    params = dict(norm1_g=norm1_g, norm2_g=norm2_g, w_in=w_in, w_out=w_out, rw_w0=rw_w0, rw_wu=rw_wu,
                  rw_a0=rw_a0, rw_au=rw_au, rw_gu=rw_gu, rw_kk=rw_kk, rw_ka=rw_ka, rw_rk=rw_rk, rw_gn=rw_gn,
                  mla_qa_g=mla_qa_g, mla_wuq=mla_wuq, mla_kva_g=mla_kva_g, mla_wukv=mla_wukv,
                  mla_qn_g=mla_qn_g, mla_kn_g=mla_kn_g, gqa_qn_g=gqa_qn_g, gqa_kn_g=gqa_kn_g,
                  conv_w=conv_w, conv_b=conv_b, peer_wq=peer_wq, peer_k1=peer_k1, peer_k2=peer_k2,
                  peer_u=peer_u, peer_v=peer_v)
    lw_all = _layer_weights(params)
    tabs = _rope_tables()

    cvecs = jnp.concatenate([c_ctx[None, :], c, jnp.zeros((8 - 1 - N_LAT_SEQ, D_MODEL), F32)], axis=0)
    mods = _modulation(cvecs, w_mod, b_mod).reshape(depth, 8, 1, 6 * D_MODEL)

    x = jnp.concatenate([x_prompt.reshape(N_CTX_TOK, D_MODEL), x_sample.reshape(N_LAT_TOK, D_MODEL)], axis=0)

    s0_lat = _states_to_pairs(state_rwkv)
    s0_all = jnp.concatenate([jnp.zeros((N_CTX_SEQ,) + s0_lat.shape[1:], F32), s0_lat], axis=0)

    krope_placed = jnp.pad(cache_mla_krope, ((0, 0), (0, 0), (0, 0), (64, 32)))
    kctx_m, vctx_m = _mla_ctx_kv(cache_mla_ckv, krope_placed, lw_all["wk"], lw_all["wv"], lw_all["kn_g2"])
    past = cache_gqa_k.shape[2]
    kctx_g = cache_gqa_k.reshape(N_LAT_SEQ, depth, past, 128)
    vctx_g = jnp.swapaxes(cache_gqa_v.reshape(N_LAT_SEQ, depth, past, 128), -1, -2)

    ident = lambda p: p
    st_a, st_ckv, st_kr, st_k, st_v = [], [], [], [], []
    for l in range(depth):
        lw = {name: w[l] for name, w in lw_all.items() if name not in STACKED_WEIGHTS}
        mod = mods[l]
        a_grp, b_grp, c_grp, d_grp, qm, km, vm, ckvn, qg, kg, vg, kgn = _in_proj(
            x, mod, lw["norm1_g"], lw_all["w_in"], l, tabs, lw)

        yf, yb, s_fin = _rwkv_scan(a_grp, s0_all[:, l], lw["w0"], lw["wu"], lw["a0"], lw["au"], lw["kkp"], lw["ka"])

        ob_ctx = _attention(qm, km, km, vm, lambda p: 2 * p, lambda p: 2 * p + 1, lambda p: 2 * p,
                            lambda p: 2 * p + 1, ident, N_CTX_SEQ, CTX_LEN, 0)
        ob_lat = _attention(qm, km, km, vm, lambda p: 2 * p, lambda p: 2 * p + 1, lambda p: 2 * p,
                            lambda p: 2 * p + 1, ident, N_LAT_SEQ, LAT_LEN, N_CTX_TOK,
                            ctx=(kctx_m[l], kctx_m[l], vctx_m[l]))
        oc_ctx = _attention(qg, kg, kg, vg, ident, lambda p: p + 2, ident, ident, ident,
                            N_CTX_SEQ, CTX_LEN, 0)
        oc_lat = _attention(qg, kg, kg, vg, ident, lambda p: p + 2, ident, ident, ident,
                            N_LAT_SEQ, LAT_LEN, N_CTX_TOK,
                            ctx=(kctx_g[:, l], kctx_g[:, l], vctx_g[:, l]))

        d_prev, d_next = _conv_halos(d_grp)
        x1, h2, q = _out_proj(x, mod, a_grp, yf, yb, ob_ctx, ob_lat, oc_ctx, oc_lat, d_grp, d_prev, d_next, lw,
                              lw_all["w_out"], lw_all["wq"], l)

        thr, s2, e1, e2 = _peer_topk(q, lw["k1"], lw["k2"])
        x = _peer_dense(h2, lw_all["u"], lw_all["vt"], l, thr, s2, e1, e2, x1, mod)

        st_a.append(_pairs_to_states(s_fin[:N_CTX_SEQ]))
        st_ckv.append(ckvn[:N_CTX_TOK].reshape(N_CTX_SEQ, CTX_LEN, 128))
        st_kr.append(b_grp[:N_CTX_TOK, 448:480].reshape(N_CTX_SEQ, CTX_LEN, 32))
        st_k.append(kgn[:N_CTX_TOK].reshape(N_CTX_SEQ, CTX_LEN, 2, HEAD_DIM))
        st_v.append(c_grp[:N_CTX_TOK, 384:512].reshape(N_CTX_SEQ, CTX_LEN, 2, HEAD_DIM))

    y_prompt = x[:N_CTX_TOK].reshape(N_CTX_SEQ, CTX_LEN, D_MODEL)
    y_sample = x[N_CTX_TOK:].reshape(N_LAT_SEQ, LAT_LEN, D_MODEL)
    return (y_prompt, y_sample, jnp.stack(st_a, axis=1), jnp.stack(st_ckv, axis=1), jnp.stack(st_kr, axis=1),
            jnp.stack(st_k, axis=1), jnp.stack(st_v, axis=1))
```
